```python
import math, functools
import jax, jax.numpy as jnp
from jax import lax
import numpy as np

D_MODEL = 1024
BATCH = 2
SEQ = 8192
DEPTH = 2
DEC_BATCH = 128
DEC_SEQ = 1
PAST_LEN = 8192
PAGE_SIZE = 128

F32 = jnp.float32
N_EVEN = (DEPTH + 1) // 2
N_ODD = DEPTH // 2
HEAD_DIM = 64
FOX_HEADS = 8
FOX_WIDTH = FOX_HEADS * HEAD_DIM
Q_BLOCK = 128
M_HEADS = 8
M_HEADDIM = 64
M_INNER = M_HEADS * M_HEADDIM
M_GROUPS = 2
HPG = M_HEADS // M_GROUPS
D_STATE = 128
CONV_W = 4
BC_W = M_GROUPS * D_STATE
CONV_CH = M_INNER + 2 * BC_W
SSD_CHUNK = 128
RMS_EPS = 1e-5
EVEN_SPLITS = (FOX_WIDTH, 2 * FOX_WIDTH, 3 * FOX_WIDTH, 3 * FOX_WIDTH + FOX_HEADS,
               3 * FOX_WIDTH + FOX_HEADS + M_INNER, 3 * FOX_WIDTH + FOX_HEADS + M_INNER + CONV_CH)
EVEN_IN = 3 * FOX_WIDTH + FOX_HEADS + M_INNER + CONV_CH + M_HEADS
EVEN_MIX = FOX_WIDTH + M_INNER
SW_HEADS = 16
SW_KV_HEADS = 4
SW_GROUP = SW_HEADS // SW_KV_HEADS
WINDOW = 128
ROPE_THETA = 10000.0
ODD_MIX = SW_HEADS * HEAD_DIM
KV_W = SW_KV_HEADS * HEAD_DIM
ODD_IN = ODD_MIX + 2 * KV_W
N_EXPERT_GROUPS = 4
EXPERTS_PER_GROUP = 8
N_EXPERTS = N_EXPERT_GROUPS * EXPERTS_PER_GROUP
TOP_K_IN_GROUP = 2
EXPERT_FF = 512
PLE_DIM = 256
LN_EPS = 1e-5
DN_ALPHA = (2 * DEPTH) ** 0.25
DN_BETA = (8 * DEPTH) ** -0.25

kernel_name = 'hybrid_fox_ssd_swa_hmoe_step'


def layer_norm(x, g, b):
    xf = x.astype(F32)
    mu = jnp.mean(xf, -1, keepdims=True)
    var = jnp.mean(jnp.square(xf - mu), -1, keepdims=True)
    return ((xf - mu) * lax.rsqrt(var + LN_EPS) * g.astype(F32) + b.astype(F32)).astype(x.dtype)


def rope(x, pos):
    half = HEAD_DIM // 2
    inv = jnp.exp(-math.log(ROPE_THETA) * jnp.arange(half, dtype=F32) / half)
    ang = pos.astype(F32)[:, None] * inv[None, :]
    cos = jnp.cos(ang)[:, None, :]
    sin = jnp.sin(ang)[:, None, :]
    xf = x.astype(F32)
    x1, x2 = xf[..., :half], xf[..., half:]
    return jnp.concatenate([x1 * cos - x2 * sin, x2 * cos + x1 * sin], -1).astype(x.dtype)


def softmax_with_sink(s, sink):
    sink = jnp.broadcast_to(sink.astype(F32), s.shape[:-1] + (1,))
    p = jax.nn.softmax(jnp.concatenate([s, sink], -1), axis=-1)
    return p[..., :-1]


def fox_prompt(q, k, v, logf):
    Bsz, S = q.shape[:2]
    nb = S // Q_BLOCK
    scale = HEAD_DIM ** -0.5
    kf = k.astype(F32)
    vf = v.astype(F32)
    cumT = jnp.cumsum(logf, axis=1).transpose(0, 2, 1)
    kpos = jnp.arange(S)
    qb = jnp.moveaxis(q.reshape(Bsz, nb, Q_BLOCK, FOX_HEADS, HEAD_DIM), 1, 0)

    def one_block(args):
        qi, bi = args
        qpos = bi * Q_BLOCK + jnp.arange(Q_BLOCK)
        cq = lax.dynamic_slice_in_dim(cumT, bi * Q_BLOCK, Q_BLOCK, axis=2)
        s = jnp.einsum('bqhd,bkhd->bhqk', qi.astype(F32), kf) * scale
        s = s + (cq[..., None] - cumT[:, :, None, :])
        s = jnp.where((kpos[None, :] <= qpos[:, None])[None, None], s, -jnp.inf)
        p = jax.nn.softmax(s, axis=-1)
        return jnp.einsum('bhqk,bkhd->bqhd', p, vf)

    o = lax.map(one_block, (qb, jnp.arange(nb)))
    return jnp.moveaxis(o, 0, 1).reshape(Bsz, S, FOX_HEADS, HEAD_DIM)


def fox_sample(q, k, v, logf, cache_k, cache_v, cache_lf, page_table, layer):
    T = q.shape[1]
    past = page_table.shape[1] * PAGE_SIZE
    qpos = past + jnp.arange(T)
    kpos = jnp.arange(past + T)
    mask = kpos[None, :] <= qpos[:, None]
    scale = HEAD_DIM ** -0.5

    def one_sequence(args):
        qs, kn, vn, ln, pages = args
        kp = cache_k[layer, pages].reshape(past, FOX_HEADS, HEAD_DIM)
        vp = cache_v[layer, pages].reshape(past, FOX_HEADS, HEAD_DIM)
        lp = cache_lf[layer, pages].reshape(past, FOX_HEADS)
        ka = jnp.concatenate([kp.astype(F32), kn.astype(F32)], 0)
        va = jnp.concatenate([vp.astype(F32), vn.astype(F32)], 0)
        cum = jnp.cumsum(jnp.concatenate([lp.astype(F32), ln], 0), axis=0)
        s = jnp.einsum('thd,shd->hts', qs.astype(F32), ka) * scale
        s = s + (cum[past:].T[:, :, None] - cum.T[:, None, :])
        s = jnp.where(mask[None], s, -jnp.inf)
        p = jax.nn.softmax(s, axis=-1)
        return jnp.einsum('hts,shd->thd', p, va)

    return lax.map(one_sequence, (q, k, v, logf, page_table))


def causal_conv(u, ctx, w, b):
    up = jnp.concatenate([ctx.astype(u.dtype), u], axis=1)
    L = u.shape[1]
    y = up[:, 0:L] * w[0]
    for j in range(1, CONV_W):
        y = y + up[:, j:j + L] * w[j]
    return jax.nn.silu(y + b), up[:, -(CONV_W - 1):]


def ssd_chunk(h0, xs, dt, dA, Bm, Cm):
    L = xs.shape[1]
    cum = jnp.cumsum(dA, axis=1)
    tri = jnp.tril(jnp.ones((L, L), bool))
    diff = cum[:, :, None] - cum[:, None, :]
    decay = jnp.exp(jnp.where(tri[None, :, :, None, None], diff, -jnp.inf))
    cb = jnp.einsum('btgn,bsgn->btsg', Cm, Bm)
    w = cb[..., None] * decay * dt[:, None]
    y = jnp.einsum('btsgh,bsghp->btghp', w, xs)
    y = y + jnp.einsum('btgn,bghpn->btghp', Cm, h0) * jnp.exp(cum)[..., None]
    tail = jnp.exp(cum[:, -1:] - cum) * dt
    h_new = h0 * jnp.exp(cum[:, -1])[..., None, None] + jnp.einsum('bsgh,bsghp,bsgn->bghpn', tail, xs, Bm)
    return y, h_new


def ssd_scan(h0, xs, dt, dA, Bm, Cm):
    Bsz, L = xs.shape[:2]
    chunk = L if L <= SSD_CHUNK else SSD_CHUNK
    nc = L // chunk

    def to_chunks(a):
        return jnp.moveaxis(a.reshape((Bsz, nc, chunk) + a.shape[2:]), 1, 0)

    def step(h, inp):
        y, h = ssd_chunk(h, *inp)
        return h, y

    h, ys = lax.scan(step, h0, tuple(to_chunks(a) for a in (xs, dt, dA, Bm, Cm)))
    y = jnp.moveaxis(ys, 0, 1).reshape((Bsz, L) + ys.shape[3:])
    return y, h


def mamba_branch(z, xbc, dtr, conv_ctx, h0, conv_w, conv_b, dt_bias, a_log, d_skip, norm_w):
    Bsz, L = z.shape[:2]
    u, conv_state = causal_conv(xbc, conv_ctx, conv_w, conv_b)
    uf = u.astype(F32)
    xs = uf[..., :M_INNER].reshape(Bsz, L, M_GROUPS, HPG, M_HEADDIM)
    Bm = uf[..., M_INNER:M_INNER + BC_W].reshape(Bsz, L, M_GROUPS, D_STATE)
    Cm = uf[..., M_INNER + BC_W:].reshape(Bsz, L, M_GROUPS, D_STATE)
    dt = jax.nn.softplus(dtr.astype(F32) + dt_bias.astype(F32)).reshape(Bsz, L, M_GROUPS, HPG)
    dA = dt * (-jnp.exp(a_log.astype(F32))).reshape(M_GROUPS, HPG)
    h0g = h0.astype(F32).reshape(Bsz, M_GROUPS, HPG, M_HEADDIM, D_STATE)
    y, hN = ssd_scan(h0g, xs, dt, dA, Bm, Cm)
    y = y + d_skip.astype(F32).reshape(M_GROUPS, HPG, 1) * xs
    y = y.reshape(Bsz, L, M_INNER) * jax.nn.silu(z.astype(F32))
    yg = y.reshape(Bsz, L, M_GROUPS, M_INNER // M_GROUPS)
    yg = yg * lax.rsqrt(jnp.mean(yg * yg, -1, keepdims=True) + RMS_EPS)
    out = (yg.reshape(Bsz, L, M_INNER) * norm_w.astype(F32)).astype(z.dtype)
    return out, conv_state, hN.reshape(Bsz, M_HEADS, M_HEADDIM, D_STATE)


def even_mixer(h, fox_attend, conv_ctx, ssm_h0, w_in, b_fgate, conv_w, conv_b, dt_bias, a_log,
               d_skip, ssm_norm_w, w_out):
    Bsz, L, _ = h.shape
    u = h @ w_in
    q, k, v, fl, z, xbc, dtr = jnp.split(u, EVEN_SPLITS, axis=-1)
    q = q.reshape(Bsz, L, FOX_HEADS, HEAD_DIM)
    k = k.reshape(Bsz, L, FOX_HEADS, HEAD_DIM)
    v = v.reshape(Bsz, L, FOX_HEADS, HEAD_DIM)
    logf = jax.nn.log_sigmoid(fl.astype(F32) + b_fgate.astype(F32))
    a = fox_attend(q, k, v, logf).reshape(Bsz, L, FOX_WIDTH).astype(h.dtype)
    m, conv_state, ssm_state = mamba_branch(z, xbc, dtr, conv_ctx, ssm_h0, conv_w, conv_b,
                                            dt_bias, a_log, d_skip, ssm_norm_w)
    out = jnp.concatenate([a, m], -1) @ w_out
    return out, (k, v, logf, ssm_state, conv_state)


def swa_prompt(q, k, v, sinks, rows):
    Bsz, S = q.shape[:2]
    nb = S // WINDOW
    scale = HEAD_DIM ** -0.5
    qb = q.reshape(Bsz, nb, WINDOW, SW_KV_HEADS, SW_GROUP, HEAD_DIM).astype(F32)
    kb = k.reshape(Bsz, nb, WINDOW, SW_KV_HEADS, HEAD_DIM).astype(F32)
    vb = v.reshape(Bsz, nb, WINDOW, SW_KV_HEADS, HEAD_DIM).astype(F32)
    kk = jnp.concatenate([jnp.concatenate([jnp.zeros_like(kb[:, :1]), kb[:, :-1]], 1), kb], 2)
    vv = jnp.concatenate([jnp.concatenate([jnp.zeros_like(vb[:, :1]), vb[:, :-1]], 1), vb], 2)
    n = jnp.arange(nb)[:, None, None]
    qpos = n * WINDOW + jnp.arange(WINDOW)[None, :, None]
    kpos = (n - 1) * WINDOW + jnp.arange(2 * WINDOW)[None, None, :]
    d = qpos - kpos
    mask = (d >= 0) & (d <= WINDOW) & (kpos >= 0)
    s = jnp.einsum('bnqkgd,bnskd->bnkgqs', qb, kk) * scale
    s = jnp.where(mask[None, :, None, None], s, -jnp.inf)
    p = softmax_with_sink(s, sinks.reshape(SW_KV_HEADS, SW_GROUP)[None, None, :, :, None, None])
    o = jnp.einsum('bnkgqs,bnskd->bnqkgd', p, vv).reshape(Bsz, S, ODD_MIX)
    return o, k[:, -rows:], v[:, -rows:]


def swa_sample(q, k, v, sinks, buf_k, buf_v, past_len):
    Bsz, T = q.shape[:2]
    Lw = buf_k.shape[1]
    scale = HEAD_DIM ** -0.5
    ka = jnp.concatenate([buf_k.astype(k.dtype), k], 1)
    va = jnp.concatenate([buf_v.astype(v.dtype), v], 1)
    kpos = past_len - Lw + jnp.arange(Lw + T)
    qpos = past_len + jnp.arange(T)
    d = qpos[:, None] - kpos[None, :]
    mask = (d >= 0) & (d <= WINDOW)
    qg = q.reshape(Bsz, T, SW_KV_HEADS, SW_GROUP, HEAD_DIM).astype(F32)
    s = jnp.einsum('btkgd,bskd->bkgts', qg, ka.astype(F32)) * scale
    s = jnp.where(mask[None, None, None], s, -jnp.inf)
    p = softmax_with_sink(s, sinks.reshape(SW_KV_HEADS, SW_GROUP)[None, :, :, None, None])
    o = jnp.einsum('bkgts,bskd->btkgd', p, va.astype(F32)).reshape(Bsz, T, ODD_MIX)
    return o, ka[:, -Lw:], va[:, -Lw:]


def odd_mixer(h, pos, attend, w_in, sinks, w_out):
    Bsz, L, _ = h.shape
    u = h @ w_in
    q, k, v = jnp.split(u, (ODD_MIX, ODD_MIX + KV_W), axis=-1)
    q = rope(q.reshape(Bsz, L, SW_HEADS, HEAD_DIM), pos)
    k = rope(k.reshape(Bsz, L, SW_KV_HEADS, HEAD_DIM), pos)
    v = v.reshape(Bsz, L, SW_KV_HEADS, HEAD_DIM)
    o, k_rows, v_rows = attend(q, k, v, sinks)
    return o.astype(h.dtype) @ w_out, (k_rows, v_rows)


def hier_moe(x, w_rg, b_rg, w_re, b_re, w_gate, w_up, w_down):
    N, D = x.shape
    xf = x.astype(F32)
    gl = xf @ w_rg.astype(F32) + b_rg.astype(F32)
    gp = jax.nn.softmax(gl, axis=-1)
    g = jnp.argmax(gl, axis=-1)
    pg = jnp.take_along_axis(gp, g[:, None], axis=-1)
    el = jnp.einsum('nd,gde->nge', xf, w_re.astype(F32)) + b_re.astype(F32)
    el = jnp.take_along_axis(el, g[:, None, None], axis=1)[:, 0]
    top_v, top_i = lax.top_k(el, TOP_K_IN_GROUP)
    wts = jax.nn.softmax(top_v, axis=-1) * pg
    eid = g[:, None] * EXPERTS_PER_GROUP + top_i
    gate = jnp.sum(jax.nn.one_hot(eid, N_EXPERTS, dtype=F32) * wts[..., None], axis=1)

    def expert(acc, inp):
        wg, wu, wd, ge = inp
        hdn = jax.nn.silu(x @ wg) * (x @ wu)
        return acc + ge[:, None] * (hdn @ wd).astype(F32), None

    acc, _ = lax.scan(expert, jnp.zeros((N, D), F32), (w_gate, w_up, w_down, gate.T))
    return acc.astype(x.dtype)


def channel_and_ple(x, m, p, ln_mix_g, ln_mix_b, ln_ffn_g, ln_ffn_b, w_rg, b_rg, w_re, b_re,
                    w_gate, w_up, w_down, w_ple_proj, w_ple_gate, b_ple_gate):
    x = layer_norm(DN_ALPHA * x + m, ln_mix_g, ln_mix_b)
    Bsz, L, D = x.shape
    f = hier_moe(x.reshape(Bsz * L, D), w_rg, b_rg, w_re, b_re, w_gate, w_up, w_down).reshape(Bsz, L, D)
    x = layer_norm(DN_ALPHA * x + f, ln_ffn_g, ln_ffn_b)
    gate = jax.nn.sigmoid((x @ w_ple_gate + b_ple_gate).astype(F32))
    return x + (gate * (p @ w_ple_proj).astype(F32)).astype(x.dtype)


def _normal(key, shape, scale):
    return jax.random.normal(key, shape, F32) * scale


def setup_inputs(seed: int = 0) -> dict:
    key = jax.random.key(seed)
    ks = jax.random.split(key, 40)
    n_pages = PAST_LEN // PAGE_SIZE
    n_used = DEC_BATCH * n_pages
    n_phys = n_used + max(1, n_used // 4)
    win_rows = min(WINDOW, PAST_LEN)
    page_table = jax.random.permutation(ks[0], n_phys)[:n_used].reshape(DEC_BATCH, n_pages).astype(jnp.int32)
    dt0 = jnp.exp(jax.random.uniform(ks[16], (N_EVEN, M_HEADS), F32, math.log(1e-3), math.log(1e-1)))
    a_log = jnp.log(jax.random.uniform(ks[17], (N_EVEN, M_HEADS), F32, 1.0, 16.0))
    return {
        'x_prompt': _normal(ks[1], (BATCH, SEQ, D_MODEL), 1.0),
        'x_sample': _normal(ks[2], (DEC_BATCH, DEC_SEQ, D_MODEL), 1.0),
        'p_prompt': _normal(ks[3], (DEPTH, BATCH, SEQ, PLE_DIM), 1.0),
        'p_sample': _normal(ks[4], (DEPTH, DEC_BATCH, DEC_SEQ, PLE_DIM), 1.0),
        'cache_fox_k': _normal(ks[5], (N_EVEN, n_phys, PAGE_SIZE, FOX_HEADS, HEAD_DIM), 1.0),
        'cache_fox_v': _normal(ks[6], (N_EVEN, n_phys, PAGE_SIZE, FOX_HEADS, HEAD_DIM), 1.0),
        'cache_fox_logf': jax.nn.log_sigmoid(3.0 + _normal(ks[7], (N_EVEN, n_phys, PAGE_SIZE, FOX_HEADS), 1.0)),
        'state_ssm': _normal(ks[8], (N_EVEN, DEC_BATCH, M_HEADS, M_HEADDIM, D_STATE), 0.5),
        'state_conv': _normal(ks[9], (N_EVEN, DEC_BATCH, CONV_W - 1, CONV_CH), 1.0),
        'cache_win_k': _normal(ks[10], (N_ODD, DEC_BATCH, win_rows, SW_KV_HEADS, HEAD_DIM), 1.0),
        'cache_win_v': _normal(ks[11], (N_ODD, DEC_BATCH, win_rows, SW_KV_HEADS, HEAD_DIM), 1.0),
        'page_table': page_table,
        'w_in_even': _normal(ks[12], (N_EVEN, D_MODEL, EVEN_IN), D_MODEL ** -0.5),
        'b_fgate': 3.0 + _normal(ks[13], (N_EVEN, FOX_HEADS), 0.1),
        'conv_w': _normal(ks[14], (N_EVEN, CONV_W, CONV_CH), CONV_W ** -0.5),
        'conv_b': _normal(ks[15], (N_EVEN, CONV_CH), 0.02),
        'dt_bias': dt0 + jnp.log(-jnp.expm1(-dt0)),
        'a_log': a_log,
        'd_skip': 1.0 + _normal(ks[18], (N_EVEN, M_HEADS), 0.05),
        'ssm_norm_w': 1.0 + _normal(ks[19], (N_EVEN, M_INNER), 0.02),
        'w_out_even': _normal(ks[20], (N_EVEN, EVEN_MIX, D_MODEL), EVEN_MIX ** -0.5 * DN_BETA),
        'w_in_odd': _normal(ks[21], (N_ODD, D_MODEL, ODD_IN), D_MODEL ** -0.5),
        'attn_sinks': _normal(ks[22], (N_ODD, SW_HEADS), 0.5),
        'w_out_odd': _normal(ks[23], (N_ODD, ODD_MIX, D_MODEL), ODD_MIX ** -0.5 * DN_BETA),
        'ln_mix_g': 1.0 + _normal(ks[24], (DEPTH, D_MODEL), 0.02),
        'ln_mix_b': _normal(ks[25], (DEPTH, D_MODEL), 0.02),
        'ln_ffn_g': 1.0 + _normal(ks[26], (DEPTH, D_MODEL), 0.02),
        'ln_ffn_b': _normal(ks[27], (DEPTH, D_MODEL), 0.02),
        'w_router_group': _normal(ks[28], (DEPTH, D_MODEL, N_EXPERT_GROUPS), D_MODEL ** -0.5),
        'b_router_group': _normal(ks[29], (DEPTH, N_EXPERT_GROUPS), 0.01),
        'w_router_expert': _normal(ks[30], (DEPTH, N_EXPERT_GROUPS, D_MODEL, EXPERTS_PER_GROUP), D_MODEL ** -0.5),
        'b_router_expert': _normal(ks[31], (DEPTH, N_EXPERT_GROUPS, EXPERTS_PER_GROUP), 0.01),
        'w_exp_gate': _normal(ks[32], (DEPTH, N_EXPERTS, D_MODEL, EXPERT_FF), D_MODEL ** -0.5),
        'w_exp_up': _normal(ks[33], (DEPTH, N_EXPERTS, D_MODEL, EXPERT_FF), D_MODEL ** -0.5),
        'w_exp_down': _normal(ks[34], (DEPTH, N_EXPERTS, EXPERT_FF, D_MODEL), EXPERT_FF ** -0.5 * DN_BETA),
        'w_ple_proj': _normal(ks[35], (DEPTH, PLE_DIM, D_MODEL), PLE_DIM ** -0.5),
        'w_ple_gate': _normal(ks[36], (DEPTH, D_MODEL, D_MODEL), D_MODEL ** -0.5),
        'b_ple_gate': _normal(ks[37], (DEPTH, D_MODEL), 0.02),
    }


def reference(x_prompt, x_sample, p_prompt, p_sample, cache_fox_k, cache_fox_v, cache_fox_logf,
              state_ssm, state_conv, cache_win_k, cache_win_v, page_table,
              w_in_even, b_fgate, conv_w, conv_b, dt_bias, a_log, d_skip, ssm_norm_w, w_out_even,
              w_in_odd, attn_sinks, w_out_odd, ln_mix_g, ln_mix_b, ln_ffn_g, ln_ffn_b,
              w_router_group, b_router_group, w_router_expert, b_router_expert,
              w_exp_gate, w_exp_up, w_exp_down, w_ple_proj, w_ple_gate, b_ple_gate):
    Bp, S, _ = x_prompt.shape
    T = x_sample.shape[1]
    past_len = page_table.shape[1] * PAGE_SIZE
    pos_p = jnp.arange(S, dtype=jnp.int32)
    pos_s = past_len + jnp.arange(T, dtype=jnp.int32)
    zero_conv = jnp.zeros((Bp, CONV_W - 1, CONV_CH), x_prompt.dtype)
    zero_ssm = jnp.zeros((Bp, M_HEADS, M_HEADDIM, D_STATE), F32)
    rows_p = min(WINDOW, S)
    even_p, even_s, odd_p, odd_s = [], [], [], []
    xp, xs = x_prompt, x_sample
    for li in range(DEPTH):
        j = li // 2
        if li % 2 == 0:
            ew = (w_in_even[j], b_fgate[j], conv_w[j], conv_b[j], dt_bias[j], a_log[j], d_skip[j],
                  ssm_norm_w[j], w_out_even[j])
            mp, stp = even_mixer(xp, fox_prompt, zero_conv, zero_ssm, *ew)
            fox_s = functools.partial(fox_sample, cache_k=cache_fox_k, cache_v=cache_fox_v,
                                      cache_lf=cache_fox_logf, page_table=page_table, layer=j)
            ms, sts = even_mixer(xs, fox_s, state_conv[j], state_ssm[j], *ew)
            even_p.append(stp)
            even_s.append(sts)
        else:
            ow = (w_in_odd[j], attn_sinks[j], w_out_odd[j])
            mp, stp = odd_mixer(xp, pos_p, functools.partial(swa_prompt, rows=rows_p), *ow)
            swa_s = functools.partial(swa_sample, buf_k=cache_win_k[j], buf_v=cache_win_v[j], past_len=past_len)
            ms, sts = odd_mixer(xs, pos_s, swa_s, *ow)
            odd_p.append(stp)
            odd_s.append(sts)
        cw = (ln_mix_g[li], ln_mix_b[li], ln_ffn_g[li], ln_ffn_b[li], w_router_group[li], b_router_group[li],
              w_router_expert[li], b_router_expert[li], w_exp_gate[li], w_exp_up[li], w_exp_down[li],
              w_ple_proj[li], w_ple_gate[li], b_ple_gate[li])
        xp = channel_and_ple(xp, mp, p_prompt[li], *cw)
        xs = channel_and_ple(xs, ms, p_sample[li], *cw)
    return (xp, xs,
            jnp.stack([st[0] for st in even_p]), jnp.stack([st[1] for st in even_p]),
            jnp.stack([st[2] for st in even_p]), jnp.stack([st[3] for st in even_p]),
            jnp.stack([st[4] for st in even_p]),
            jnp.stack([st[0] for st in odd_p]), jnp.stack([st[1] for st in odd_p]),
            jnp.stack([st[0] for st in even_s]), jnp.stack([st[1] for st in even_s]),
            jnp.stack([st[2] for st in even_s]), jnp.stack([st[3] for st in even_s]),
            jnp.stack([st[4] for st in even_s]),
            jnp.stack([st[0] for st in odd_s]), jnp.stack([st[1] for st in odd_s]))
```

```python
import functools
import math

import jax
import jax.numpy as jnp
from jax import lax
from jax.experimental import pallas as pl
from jax.experimental.pallas import tpu as pltpu

F32 = jnp.float32
BF16 = jnp.bfloat16
HIGHEST = lax.Precision.HIGHEST

D_MODEL = 1024
HEAD_DIM = 64
FOX_HEADS = 8
FOX_WIDTH = FOX_HEADS * HEAD_DIM
M_HEADS = 8
M_HEADDIM = 64
M_INNER = M_HEADS * M_HEADDIM
M_GROUPS = 2
HPG = M_HEADS // M_GROUPS
D_STATE = 128
CONV_W = 4
BC_W = M_GROUPS * D_STATE
CONV_CH = M_INNER + 2 * BC_W
SSD_CHUNK = 128
RMS_EPS = 1e-5
SW_HEADS = 16
SW_KV_HEADS = 4
SW_GROUP = SW_HEADS // SW_KV_HEADS
WINDOW = 128
ROPE_THETA = 10000.0
ODD_MIX = SW_HEADS * HEAD_DIM
KV_W = SW_KV_HEADS * HEAD_DIM
N_EXPERT_GROUPS = 4
EXPERTS_PER_GROUP = 8
N_EXPERTS = N_EXPERT_GROUPS * EXPERTS_PER_GROUP
EXPERT_FF = 512
PLE_DIM = 256
LN_EPS = 1e-5
PAGE_SIZE = 128

LANES = 128
SUBLANES = 8
VMEM_LIMIT = 48 * 1024 * 1024

NEG_INF = float("-inf")


def _params(sem, vmem=VMEM_LIMIT):
    return pltpu.CompilerParams(dimension_semantics=sem, vmem_limit_bytes=vmem)


def _nt_dot(a, b, precision=None):
    return lax.dot_general(a, b, (((1,), (1,)), ((), ())), precision=precision,
                           preferred_element_type=F32)


def _dot(a, b, precision=None):
    return jnp.dot(a, b, precision=precision, preferred_element_type=F32)


def _silu(x):
    return x * (1.0 / (1.0 + jnp.exp(-x)))


def _softplus(x):
    return jnp.maximum(x, 0.0) + jnp.log(1.0 + jnp.exp(-jnp.abs(x)))


def _iota(shape, dim):
    return lax.broadcasted_iota(jnp.int32, shape, dim)


def _mm_kernel(x_ref, w_ref, o_ref):
    o_ref[...] = _dot(x_ref[...].astype(BF16), w_ref[...])


def _mm_rope_kernel(x_ref, w_ref, cos_ref, sin_ref, o_ref, *, rope_tiles):
    acc = _dot(x_ref[...].astype(BF16), w_ref[...])
    j = pl.program_id(1)

    @pl.when(j < rope_tiles)
    def _():
        tn = acc.shape[1]
        half = HEAD_DIM // 2
        first = (_iota(acc.shape, 1) % HEAD_DIM) < half
        partner = jnp.where(first, pltpu.roll(acc, tn - half, 1), pltpu.roll(acc, half, 1))
        o_ref[...] = acc * cos_ref[...] + partner * sin_ref[...]

    @pl.when(j >= rope_tiles)
    def _():
        o_ref[...] = acc


def matmul(x, w, tm, tn):
    m, k = x.shape
    n = w.shape[1]
    return pl.pallas_call(
        _mm_kernel,
        grid=(m // tm, n // tn),
        in_specs=[pl.BlockSpec((tm, k), lambda i, j: (i, 0)),
                  pl.BlockSpec((k, tn), lambda i, j: (0, j))],
        out_specs=pl.BlockSpec((tm, tn), lambda i, j: (i, j)),
        out_shape=jax.ShapeDtypeStruct((m, n), F32),
        compiler_params=_params(("parallel", "arbitrary")),
        name="matmul",
    )(x, w)


def matmul_rope(x, w, cos, sin, tm, tn, rope_cols):
    m, k = x.shape
    n = w.shape[1]
    return pl.pallas_call(
        functools.partial(_mm_rope_kernel, rope_tiles=rope_cols // tn),
        grid=(m // tm, n // tn),
        in_specs=[pl.BlockSpec((tm, k), lambda i, j: (i, 0)),
                  pl.BlockSpec((k, tn), lambda i, j: (0, j)),
                  pl.BlockSpec((tm, tn), lambda i, j: (i, 0)),
                  pl.BlockSpec((tm, tn), lambda i, j: (i, 0))],
        out_specs=pl.BlockSpec((tm, tn), lambda i, j: (i, j)),
        out_shape=jax.ShapeDtypeStruct((m, n), F32),
        compiler_params=_params(("parallel", "arbitrary")),
        name="matmul_rope",
    )(x, w, cos, sin)


def _small_proj_kernel(x_ref, wt_ref, b_ref, o_ref):
    r = _nt_dot(wt_ref[...], x_ref[...], precision=HIGHEST) + b_ref[...]
    row = _iota(r.shape, 0)
    o_ref[...] = jnp.where(row < FOX_HEADS, -_softplus(-r), _softplus(r))


def small_proj(x, wt, b, tm):
    m, k = x.shape
    return pl.pallas_call(
        _small_proj_kernel,
        grid=(m // tm,),
        in_specs=[pl.BlockSpec((tm, k), lambda i: (i, 0)),
                  pl.BlockSpec((16, k), lambda i: (0, 0)),
                  pl.BlockSpec((16, 1), lambda i: (0, 0))],
        out_specs=pl.BlockSpec((16, tm), lambda i: (0, i)),
        out_shape=jax.ShapeDtypeStruct((16, m), F32),
        compiler_params=_params(("parallel",)),
        name="small_proj",
    )(x, wt, b)


def _cumsum_kernel(x_ref, o_ref, carry_ref):
    @pl.when(pl.program_id(0) == 0)
    def _():
        carry_ref[...] = jnp.zeros_like(carry_ref)

    x = x_ref[...]
    w = x.shape[1]
    tri = (_iota((w, w), 0) <= _iota((w, w), 1)).astype(F32)
    c = _dot(x, tri, precision=HIGHEST) + carry_ref[...]
    o_ref[...] = c
    carry_ref[...] = c[:, w - 1:w]


def cumsum_lanes(x, chunk):
    r, l = x.shape
    return pl.pallas_call(
        _cumsum_kernel,
        grid=(l // chunk,),
        in_specs=[pl.BlockSpec((r, chunk), lambda i: (0, i))],
        out_specs=pl.BlockSpec((r, chunk), lambda i: (0, i)),
        out_shape=jax.ShapeDtypeStruct((r, l), F32),
        scratch_shapes=[pltpu.VMEM((r, 1), F32)],
        compiler_params=_params(("arbitrary",)),
        name="cumsum_lanes",
    )(x)


def _fox_prompt_kernel(q_ref, k_ref, v_ref, ck_ref, o_ref, *, tq):
    qi = pl.program_id(2)
    scale = HEAD_DIM ** -0.5
    q = q_ref[...] * scale
    lane = _iota(q.shape, 1)
    q2 = jnp.concatenate([jnp.where(lane < HEAD_DIM, q, 0.0),
                          jnp.where(lane >= HEAD_DIM, q, 0.0)], axis=0).astype(BF16)

    def step(j, carry, masked):
        m, l, acc = carry
        start = pl.multiple_of(j * tq, tq)
        kb = k_ref[pl.ds(start, tq), :].astype(BF16)
        vb = v_ref[pl.ds(start, tq), :].astype(BF16)
        s = _nt_dot(q2, kb)
        ck = ck_ref[0, 0, j]
        bias = jnp.concatenate([jnp.broadcast_to(ck[0:1, :], (tq, tq)),
                                jnp.broadcast_to(ck[1:2, :], (tq, tq))], axis=0)
        s = s - bias
        if masked:
            row = _iota(s.shape, 0) % tq
            col = _iota(s.shape, 1)
            s = jnp.where(col <= row, s, NEG_INF)
        m_new = jnp.maximum(m, jnp.max(s, axis=1, keepdims=True))
        alpha = jnp.exp(m - m_new)
        p = jnp.exp(s - m_new)
        l = alpha * l + jnp.sum(p, axis=1, keepdims=True)
        acc = alpha * acc + _dot(p.astype(BF16), vb)
        return m_new, l, acc

    init = (jnp.full((2 * tq, 1), NEG_INF, F32), jnp.zeros((2 * tq, 1), F32),
            jnp.zeros((2 * tq, LANES), F32))
    carry = lax.fori_loop(0, qi, lambda j, c: step(j, c, False), init)
    m, l, acc = step(qi, carry, True)
    o = acc / l
    o_ref[...] = jnp.where(lane < HEAD_DIM, o[:tq], o[tq:])


def fox_prompt(u_main, ck, n_batch, seq, tq):
    nq = seq // tq
    pairs = FOX_HEADS // 2
    return pl.pallas_call(
        functools.partial(_fox_prompt_kernel, tq=tq),
        grid=(n_batch, pairs, nq),
        in_specs=[pl.BlockSpec((tq, LANES), lambda b, h, i: (b * nq + i, h)),
                  pl.BlockSpec((seq, LANES), lambda b, h, i: (b, pairs + h)),
                  pl.BlockSpec((seq, LANES), lambda b, h, i: (b, 2 * pairs + h)),
                  pl.BlockSpec((1, 1, nq, 2, tq), lambda b, h, i: (b, h, 0, 0, 0))],
        out_specs=pl.BlockSpec((tq, LANES), lambda b, h, i: (b * nq + i, h)),
        out_shape=jax.ShapeDtypeStruct((n_batch * seq, FOX_WIDTH), F32),
        compiler_params=_params(("parallel", "parallel", "arbitrary")),
        name="fox_prompt",
    )(u_main, u_main, u_main, ck)


def _fox_sample_kernel(pt_ref, q_ref, kn_ref, vn_ref, ln_ref, *refs, pages_per_step):
    del pt_ref
    pp = pages_per_step
    k_refs, v_refs, lf_refs = refs[:pp], refs[pp:2 * pp], refs[2 * pp:3 * pp]
    o_ref = refs[3 * pp]
    m_ref, l_ref, acc_ref, carry_ref = refs[3 * pp + 1:]
    t = pl.program_id(1)
    scale = HEAD_DIM ** -0.5
    q = q_ref[0] * scale

    @pl.when(t == 0)
    def _():
        m_ref[...] = jnp.sum(q * kn_ref[0], axis=1, keepdims=True)
        l_ref[...] = jnp.ones_like(l_ref)
        acc_ref[...] = vn_ref[0]
        carry_ref[...] = ln_ref[0]

    qb = q.astype(BF16)
    cols = PAGE_SIZE * FOX_HEADS
    match = (_iota((FOX_HEADS, cols), 1) % FOX_HEADS) == _iota((FOX_HEADS, cols), 0)
    lane = _iota((SUBLANES, LANES), 1)
    row = _iota((SUBLANES, LANES), 0)

    for r in range(pp):
        lf = lf_refs[r][0]
        x = lf
        for sh in (8, 16, 32, 64):
            x = x + jnp.where(lane + sh < LANES, pltpu.roll(x, LANES - sh, 1), 0.0)
        y = jnp.where(lane < FOX_HEADS, x, 0.0)
        for sh in (8, 16, 32, 64):
            y = y + pltpu.roll(y, sh, 1)
        z = y
        for sh in (1, 2, 4):
            z = z + jnp.where(row + sh < SUBLANES, pltpu.roll(z, SUBLANES - sh, 0), 0.0)
        carry = carry_ref[...]
        d8 = (x - lf) + (z - y) + carry
        carry_ref[...] = carry + z[0:1, :]

        k2 = k_refs[r][0, 0].reshape(cols, HEAD_DIM).astype(BF16)
        v2 = v_refs[r][0, 0].reshape(cols, HEAD_DIM).astype(BF16)
        s = _nt_dot(qb, k2)
        dfull = jnp.concatenate(
            [jnp.broadcast_to(d8[i:i + 1, :], (FOX_HEADS, LANES)) for i in range(SUBLANES)], axis=1)
        s = jnp.where(match, s + dfull, NEG_INF)
        m = m_ref[...]
        m_new = jnp.maximum(m, jnp.max(s, axis=1, keepdims=True))
        alpha = jnp.exp(m - m_new)
        p = jnp.exp(s - m_new)
        l_ref[...] = alpha * l_ref[...] + jnp.sum(p, axis=1, keepdims=True)
        acc_ref[...] = alpha * acc_ref[...] + _dot(p.astype(BF16), v2)
        m_ref[...] = m_new

    @pl.when(t == pl.num_programs(1) - 1)
    def _():
        o_ref[0] = acc_ref[...] / l_ref[...]


def fox_sample(page_table, q, kn, vn, ln_tiled, cache_k, cache_v, cache_lf_flat, layer, pages_per_step):
    bd, n_pages = page_table.shape
    n_phys = cache_k.shape[1]
    pp = pages_per_step
    steps = n_pages // pp

    def page(b, t, pt, r):
        return pt[b, n_pages - 1 - (t * pp + r)]

    kv_specs = [pl.BlockSpec((1, 1, PAGE_SIZE, FOX_HEADS, HEAD_DIM),
                             functools.partial(lambda b, t, pt, r: (layer, page(b, t, pt, r), 0, 0, 0), r=r))
                for r in range(pp)]
    lf_specs = [pl.BlockSpec((1, SUBLANES, LANES),
                             functools.partial(lambda b, t, pt, r: (layer * n_phys + page(b, t, pt, r), 0, 0), r=r))
                for r in range(pp)]
    tok = pl.BlockSpec((1, FOX_HEADS, HEAD_DIM), lambda b, t, pt: (b, 0, 0))
    grid_spec = pltpu.PrefetchScalarGridSpec(
        num_scalar_prefetch=1,
        grid=(bd, steps),
        in_specs=[tok, tok, tok, pl.BlockSpec((1, 1, LANES), lambda b, t, pt: (b, 0, 0))]
        + kv_specs + kv_specs + lf_specs,
        out_specs=tok,
        scratch_shapes=[pltpu.VMEM((FOX_HEADS, 1), F32), pltpu.VMEM((FOX_HEADS, 1), F32),
                        pltpu.VMEM((FOX_HEADS, HEAD_DIM), F32), pltpu.VMEM((1, LANES), F32)],
    )
    return pl.pallas_call(
        functools.partial(_fox_sample_kernel, pages_per_step=pp),
        grid_spec=grid_spec,
        out_shape=jax.ShapeDtypeStruct((bd, FOX_HEADS, HEAD_DIM), F32),
        compiler_params=_params(("parallel", "arbitrary")),
        name="fox_sample",
    )(page_table, q, kn, vn, ln_tiled, *([cache_k] * pp), *([cache_v] * pp), *([cache_lf_flat] * pp))


def _ssd_epilogue(y, xs, z, dskip_e, normw):
    y = (y + dskip_e * xs) * _silu(z)
    half = M_INNER // M_GROUPS
    outs = []
    for g in range(M_GROUPS):
        yg = y[:, g * half:(g + 1) * half]
        ms = jnp.sum(yg * yg, axis=1, keepdims=True) * (1.0 / half)
        outs.append(yg * lax.rsqrt(ms + RMS_EPS))
    return jnp.concatenate(outs, axis=1) * normw


def _ssd_prompt_kernel(xbc_ref, z_ref, dt_ref, cw_ref, cb_ref, nega_ref, dskip_ref, normw_ref,
                       o_ref, st_ref, ext_ref, h_ref):
    c = pl.program_id(1)
    L = SSD_CHUNK
    pad = SUBLANES

    @pl.when(c == 0)
    def _():
        ext_ref[0:pad, :] = jnp.zeros((pad, CONV_CH), F32)
        h_ref[...] = jnp.zeros_like(h_ref)

    ext_ref[pad:pad + L, :] = xbc_ref[...]
    acc = ext_ref[pad:pad + L, :] * cw_ref[CONV_W - 1:CONV_W, :]
    for j in range(CONV_W - 1):
        off = pad - (CONV_W - 1) + j
        acc = acc + ext_ref[off:off + L, :] * cw_ref[j:j + 1, :]
    u = _silu(acc + cb_ref[...])
    ext_ref[0:pad, :] = ext_ref[L:L + pad, :]

    xs = u[:, :M_INNER]
    dt_t = dt_ref[...]
    cum_t = _dot(dt_t * nega_ref[...], (_iota((L, L), 0) <= _iota((L, L), 1)).astype(F32),
                 precision=HIGHEST)
    eye = (_iota((L, L), 0) == _iota((L, L), 1)).astype(F32)
    cols = _nt_dot(eye, jnp.concatenate([cum_t, dt_t], axis=0), precision=HIGHEST)
    cum_last = cum_t[:, L - 1:L]
    tail_t = jnp.exp(cum_last - cum_t) * dt_t
    tril = _iota((L, L), 0) >= _iota((L, L), 1)
    lane = _iota((L, LANES), 1)
    rowi = _iota((L, LANES), 0)

    y_pairs = []
    for g in range(M_GROUPS):
        bm = u[:, M_INNER + g * D_STATE:M_INNER + (g + 1) * D_STATE]
        cm = u[:, M_INNER + BC_W + g * D_STATE:M_INNER + BC_W + (g + 1) * D_STATE]
        bmb = bm.astype(BF16)
        cmb = cm.astype(BF16)
        cb = _nt_dot(cmb, bmb)
        for pr in range(HPG // 2):
            pidx = g * (HPG // 2) + pr
            xs_pair = xs[:, pidx * LANES:(pidx + 1) * LANES]
            xs_pair_b = xs_pair.astype(BF16)
            h0 = h_ref[pidx]
            ych = _nt_dot(cmb, h0.astype(BF16))
            yw = []
            for k in range(2):
                hd = 2 * pidx + k
                diff = cols[:, hd:hd + 1] - cum_t[hd:hd + 1, :]
                decay = jnp.exp(jnp.where(tril, diff, NEG_INF))
                w = cb * decay * dt_t[hd:hd + 1, :]
                yw.append(_dot(w.astype(BF16), xs_pair_b))
            e0 = jnp.exp(cols[:, 2 * pidx:2 * pidx + 1])
            e1 = jnp.exp(cols[:, 2 * pidx + 1:2 * pidx + 2])
            first = lane < M_HEADDIM
            y_pairs.append(jnp.where(first, yw[0], yw[1]) + ych * jnp.where(first, e0, e1))
            top = rowi < M_HEADDIM
            tail_m = jnp.where(top, jnp.broadcast_to(tail_t[2 * pidx:2 * pidx + 1, :], (L, L)),
                               jnp.broadcast_to(tail_t[2 * pidx + 1:2 * pidx + 2, :], (L, L)))
            dec_m = jnp.where(top, jnp.exp(cum_last[2 * pidx:2 * pidx + 1, :]),
                              jnp.exp(cum_last[2 * pidx + 1:2 * pidx + 2, :]))
            xt = xs_pair.T * tail_m
            h_ref[pidx] = h0 * dec_m + _dot(xt.astype(BF16), bmb)

    y = jnp.concatenate(y_pairs, axis=1)
    o_ref[...] = _ssd_epilogue(y, xs, z_ref[...], dskip_ref[...], normw_ref[...])
    st_ref[0] = h_ref[...]


def ssd_prompt(u_main, dt_rows, conv_w, conv_b, nega, dskip_e, normw, n_batch, seq):
    L = SSD_CHUNK
    nc = seq // L
    pairs = M_HEADS // 2
    const = lambda b, c: (0, 0)
    return pl.pallas_call(
        _ssd_prompt_kernel,
        grid=(n_batch, nc),
        in_specs=[pl.BlockSpec((L, CONV_CH), lambda b, c: (b * nc + c, 2)),
                  pl.BlockSpec((L, M_INNER), lambda b, c: (b * nc + c, 3)),
                  pl.BlockSpec((M_HEADS, L), lambda b, c: (1, b * nc + c)),
                  pl.BlockSpec((CONV_W, CONV_CH), const),
                  pl.BlockSpec((1, CONV_CH), const),
                  pl.BlockSpec((M_HEADS, 1), const),
                  pl.BlockSpec((1, M_INNER), const),
                  pl.BlockSpec((1, M_INNER), const)],
        out_specs=[pl.BlockSpec((L, M_INNER), lambda b, c: (b * nc + c, 0)),
                   pl.BlockSpec((1, pairs, LANES, D_STATE), lambda b, c: (b, 0, 0, 0))],
        out_shape=[jax.ShapeDtypeStruct((n_batch * seq, M_INNER), F32),
                   jax.ShapeDtypeStruct((n_batch, pairs, LANES, D_STATE), F32)],
        scratch_shapes=[pltpu.VMEM((L + SUBLANES, CONV_CH), F32),
                        pltpu.VMEM((pairs, LANES, D_STATE), F32)],
        compiler_params=_params(("parallel", "arbitrary")),
        name="ssd_prompt",
    )(u_main, u_main, dt_rows, conv_w, conv_b, nega, dskip_e, normw)


def _ssd_sample_kernel(x_ref, wdt_ref, dtb_ref, nega_ref, xbc_ref, ctx_ref, z_ref, cw_ref, cb_ref,
                       dskip_ref, normw_ref, h0_ref, o_ref, hn_ref,
                       u_ref, coef_t_ref, dec_t_ref, dec_ref, dtx_ref, *, seqs_per_step):
    sb = seqs_per_step
    i = pl.program_id(0)
    nseq = x_ref.shape[0]

    @pl.when(i == 0)
    def _():
        acc = xbc_ref[...] * cw_ref[CONV_W - 1:CONV_W, :]
        for j in range(CONV_W - 1):
            acc = acc + ctx_ref[j] * cw_ref[j:j + 1, :]
        u = _silu(acc + cb_ref[...])
        u_ref[...] = u
        dt = _softplus(_dot(x_ref[...], wdt_ref[...], precision=HIGHEST) + dtb_ref[...])
        dec = jnp.exp(dt * nega_ref[...])
        coef = dt * u[:, :M_INNER]
        dec_ref[...] = dec
        dtx_ref[...] = coef
        for blk in range(M_INNER // LANES):
            sl = slice(blk * LANES, (blk + 1) * LANES)
            coef_t_ref[sl, :] = coef[:, sl].T
            dec_t_ref[sl, :] = dec[:, sl].T

    base = pl.multiple_of(i * sb, sb)
    ub = u_ref[pl.ds(base, sb), :]
    lane_seq = _iota((M_INNER, nseq), 1)
    rows = _iota((sb, M_INNER // M_GROUPS), 0)
    ch = [jnp.zeros((sb, M_INNER // M_GROUPS), F32) for _ in range(M_GROUPS)]
    for s in range(sb):
        onehot = lane_seq == base + s
        cx = jnp.sum(jnp.where(onehot, coef_t_ref[...], 0.0), axis=1, keepdims=True)
        dc = jnp.sum(jnp.where(onehot, dec_t_ref[...], 0.0), axis=1, keepdims=True)
        for g in range(M_GROUPS):
            brow = ub[s:s + 1, M_INNER + g * D_STATE:M_INNER + (g + 1) * D_STATE]
            cblk = ub[:, M_INNER + BC_W + g * D_STATE:M_INNER + BC_W + (g + 1) * D_STATE]
            hg = h0_ref[s, g * HPG:(g + 1) * HPG].reshape(HPG * M_HEADDIM, D_STATE)
            r = _nt_dot(cblk.astype(BF16), hg.astype(BF16))
            ch[g] = ch[g] + jnp.where(rows == s, r, 0.0)
            lo = g * HPG * M_HEADDIM
            hn = hg * dc[lo:lo + HPG * M_HEADDIM] + cx[lo:lo + HPG * M_HEADDIM] * brow
            hn_ref[s, g * HPG:(g + 1) * HPG] = hn.reshape(HPG, M_HEADDIM, D_STATE)

    xs = ub[:, :M_INNER]
    dec = dec_ref[pl.ds(base, sb), :]
    coef = dtx_ref[pl.ds(base, sb), :]
    ys = []
    half = M_INNER // M_GROUPS
    for g in range(M_GROUPS):
        bm = ub[:, M_INNER + g * D_STATE:M_INNER + (g + 1) * D_STATE]
        cm = ub[:, M_INNER + BC_W + g * D_STATE:M_INNER + BC_W + (g + 1) * D_STATE]
        cb = jnp.sum(cm * bm, axis=1, keepdims=True)
        ys.append(cb * coef[:, g * half:(g + 1) * half] + ch[g] * dec[:, g * half:(g + 1) * half])
    y = jnp.concatenate(ys, axis=1)
    o_ref[...] = _ssd_epilogue(y, xs, z_ref[...], dskip_ref[...], normw_ref[...])


def ssd_sample(x_s, wdt_e, dtb_e, nega_e, xbc_s, ctx, z_s, conv_w, conv_b, dskip_e, normw, h0, seqs_per_step):
    bd = x_s.shape[0]
    sb = seqs_per_step
    const = lambda i: (0, 0)
    return pl.pallas_call(
        functools.partial(_ssd_sample_kernel, seqs_per_step=sb),
        grid=(bd // sb,),
        in_specs=[pl.BlockSpec((bd, D_MODEL), const),
                  pl.BlockSpec((D_MODEL, M_INNER), const),
                  pl.BlockSpec((1, M_INNER), const),
                  pl.BlockSpec((1, M_INNER), const),
                  pl.BlockSpec((bd, CONV_CH), const),
                  pl.BlockSpec((CONV_W - 1, bd, CONV_CH), lambda i: (0, 0, 0)),
                  pl.BlockSpec((sb, M_INNER), lambda i: (i, 0)),
                  pl.BlockSpec((CONV_W, CONV_CH), const),
                  pl.BlockSpec((1, CONV_CH), const),
                  pl.BlockSpec((1, M_INNER), const),
                  pl.BlockSpec((1, M_INNER), const),
                  pl.BlockSpec((sb, M_HEADS, M_HEADDIM, D_STATE), lambda i: (i, 0, 0, 0))],
        out_specs=[pl.BlockSpec((sb, M_INNER), lambda i: (i, 0)),
                   pl.BlockSpec((sb, M_HEADS, M_HEADDIM, D_STATE), lambda i: (i, 0, 0, 0))],
        out_shape=[jax.ShapeDtypeStruct((bd, M_INNER), F32),
                   jax.ShapeDtypeStruct((bd, M_HEADS, M_HEADDIM, D_STATE), F32)],
        scratch_shapes=[pltpu.VMEM((bd, CONV_CH), F32),
                        pltpu.VMEM((M_INNER, bd), F32),
                        pltpu.VMEM((M_INNER, bd), F32),
                        pltpu.VMEM((bd, M_INNER), F32),
                        pltpu.VMEM((bd, M_INNER), F32)],
        compiler_params=_params(("arbitrary",)),
        name="ssd_sample",
    )(x_s, wdt_e, dtb_e, nega_e, xbc_s, ctx, z_s, conv_w, conv_b, dskip_e, normw, h0)


def _to_half(x, src_half, dst_half):
    return x if src_half == dst_half else pltpu.roll(x, HEAD_DIM, 1)


def _swa_prompt_kernel(sink_ref, q_ref, kp_ref, kc_ref, vp_ref, vc_ref, o_ref):
    n = pl.program_id(1)
    W = WINDOW
    scale = HEAD_DIM ** -0.5
    lane = _iota((W, LANES), 1)
    rows4 = _iota((SW_GROUP * W, 2 * W), 0)
    t = rows4 % W
    col = _iota((SW_GROUP * W, 2 * W), 1)
    valid = (col >= t) & (col <= t + W) & ((n > 0) | (col >= W))
    rcol = _iota((SW_GROUP * W, 1), 0) // W
    outs = [None] * SW_HEADS
    for j in range(SW_KV_HEADS):
        ch, hf = j // 2, j % 2
        kk = jnp.concatenate([kp_ref[:, ch * LANES:(ch + 1) * LANES],
                              kc_ref[:, ch * LANES:(ch + 1) * LANES]], axis=0).astype(BF16)
        vv = jnp.concatenate([vp_ref[:, ch * LANES:(ch + 1) * LANES],
                              vc_ref[:, ch * LANES:(ch + 1) * LANES]], axis=0).astype(BF16)
        qs = []
        sink = jnp.zeros((SW_GROUP * W, 1), F32)
        for g in range(SW_GROUP):
            hq = j * SW_GROUP + g
            qc = q_ref[:, (hq // 2) * LANES:(hq // 2 + 1) * LANES] * scale
            qc = _to_half(qc, hq % 2, hf)
            keep = (lane < HEAD_DIM) if hf == 0 else (lane >= HEAD_DIM)
            qs.append(jnp.where(keep, qc, 0.0))
            sink = jnp.where(rcol == g, sink_ref[hq], sink)
        q4 = jnp.concatenate(qs, axis=0).astype(BF16)
        s = jnp.where(valid, _nt_dot(q4, kk), NEG_INF)
        m = jnp.maximum(jnp.max(s, axis=1, keepdims=True), sink)
        p = jnp.exp(s - m)
        den = jnp.sum(p, axis=1, keepdims=True) + jnp.exp(sink - m)
        o = _dot(p.astype(BF16), vv) / den
        for g in range(SW_GROUP):
            hq = j * SW_GROUP + g
            outs[hq] = _to_half(o[g * W:(g + 1) * W], hf, hq % 2)
    for c in range(SW_HEADS // 2):
        o_ref[:, c * LANES:(c + 1) * LANES] = jnp.where(lane < HEAD_DIM, outs[2 * c], outs[2 * c + 1])


def swa_prompt(sinks, u_odd, n_batch, seq):
    nb = seq // WINDOW
    kcol = ODD_MIX // KV_W
    return pl.pallas_call(
        _swa_prompt_kernel,
        grid=(n_batch, nb),
        in_specs=[pl.BlockSpec(memory_space=pltpu.SMEM),
                  pl.BlockSpec((WINDOW, ODD_MIX), lambda b, n: (b * nb + n, 0)),
                  pl.BlockSpec((WINDOW, KV_W), lambda b, n: (b * nb + jnp.maximum(n - 1, 0), kcol)),
                  pl.BlockSpec((WINDOW, KV_W), lambda b, n: (b * nb + n, kcol)),
                  pl.BlockSpec((WINDOW, KV_W), lambda b, n: (b * nb + jnp.maximum(n - 1, 0), kcol + 1)),
                  pl.BlockSpec((WINDOW, KV_W), lambda b, n: (b * nb + n, kcol + 1))],
        out_specs=pl.BlockSpec((WINDOW, ODD_MIX), lambda b, n: (b * nb + n, 0)),
        out_shape=jax.ShapeDtypeStruct((n_batch * seq, ODD_MIX), F32),
        compiler_params=_params(("parallel", "arbitrary")),
        name="swa_prompt",
    )(sinks, u_odd, u_odd, u_odd, u_odd, u_odd)


def _swa_sample_kernel(sink_ref, q_ref, kn_ref, vn_ref, bk_ref, bv_ref, o_ref, *, seqs_per_step):
    scale = HEAD_DIM ** -0.5
    rowg = _iota((SW_HEADS, HEAD_DIM), 0) // SW_GROUP
    sink = sink_ref[...]
    for s in range(seqs_per_step):
        q16 = q_ref[s] * scale
        qbd = jnp.concatenate([jnp.where(rowg == j, q16, 0.0) for j in range(SW_KV_HEADS)], axis=1)
        kb = bk_ref[s].astype(BF16)
        vb = bv_ref[s].astype(BF16)
        sc = _nt_dot(qbd.astype(BF16), kb)
        s_new = jnp.sum(qbd * kn_ref[s:s + 1, :], axis=1, keepdims=True)
        m = jnp.maximum(jnp.maximum(jnp.max(sc, axis=1, keepdims=True), s_new), sink)
        p = jnp.exp(sc - m)
        pn = jnp.exp(s_new - m)
        den = jnp.sum(p, axis=1, keepdims=True) + pn + jnp.exp(sink - m)
        full = (_dot(p.astype(BF16), vb) + pn * vn_ref[s:s + 1, :]) / den
        o16 = jnp.zeros((SW_HEADS, HEAD_DIM), F32)
        for j in range(SW_KV_HEADS):
            o16 = o16 + jnp.where(rowg == j, full[:, j * HEAD_DIM:(j + 1) * HEAD_DIM], 0.0)
        o_ref[s] = o16


def swa_sample(sinks_col, q, kn, vn, buf_k, buf_v, seqs_per_step):
    bd, lw, _ = buf_k.shape
    sb = seqs_per_step
    return pl.pallas_call(
        functools.partial(_swa_sample_kernel, seqs_per_step=sb),
        grid=(bd // sb,),
        in_specs=[pl.BlockSpec((SW_HEADS, 1), lambda i: (0, 0)),
                  pl.BlockSpec((sb, SW_HEADS, HEAD_DIM), lambda i: (i, 0, 0)),
                  pl.BlockSpec((sb, KV_W), lambda i: (i, 0)),
                  pl.BlockSpec((sb, KV_W), lambda i: (i, 0)),
                  pl.BlockSpec((sb, lw, KV_W), lambda i: (i, 0, 0)),
                  pl.BlockSpec((sb, lw, KV_W), lambda i: (i, 0, 0))],
        out_specs=pl.BlockSpec((sb, SW_HEADS, HEAD_DIM), lambda i: (i, 0, 0)),
        out_shape=jax.ShapeDtypeStruct((bd, SW_HEADS, HEAD_DIM), F32),
        compiler_params=_params(("parallel",)),
        name="swa_sample",
    )(sinks_col, q, kn, vn, buf_k, buf_v)


def _layer_norm(h, g, b):
    mu = jnp.mean(h, axis=1, keepdims=True)
    d = h - mu
    var = jnp.mean(d * d, axis=1, keepdims=True)
    return d * lax.rsqrt(var + LN_EPS) * g + b


def _mix_route_kernel(am_ref, x_ref, wo_ref, g_ref, b_ref, wr_ref, br_ref,
                      x1_ref, x1b_ref, rw_ref, re_ref, *, alpha):
    mix = _dot(am_ref[...].astype(BF16), wo_ref[...])
    x1 = _layer_norm(alpha * x_ref[...] + mix, g_ref[...], b_ref[...])
    x1_ref[...] = x1
    x1b_ref[...] = x1.astype(BF16)
    logits = _dot(x1, wr_ref[...], precision=HIGHEST) + br_ref[...]
    lane_i = _iota(logits.shape, 1)
    lane = lane_i.astype(F32)
    big = float(LANES)
    gl = jnp.where(lane_i < N_EXPERT_GROUPS, logits, NEG_INF)
    gmax = jnp.max(gl, axis=1, keepdims=True)
    grp = jnp.min(jnp.where(gl == gmax, lane, big), axis=1, keepdims=True)
    pg = 1.0 / jnp.sum(jnp.exp(gl - gmax), axis=1, keepdims=True)
    rel = lane - (N_EXPERT_GROUPS + grp * EXPERTS_PER_GROUP)
    el = jnp.where(rel >= 0.0, jnp.where(rel < EXPERTS_PER_GROUP, logits, NEG_INF), NEG_INF)
    v1 = jnp.max(el, axis=1, keepdims=True)
    i1 = jnp.min(jnp.where(el == v1, lane, big), axis=1, keepdims=True)
    el2 = jnp.where(lane == i1, NEG_INF, el)
    v2 = jnp.max(el2, axis=1, keepdims=True)
    i2 = jnp.min(jnp.where(el2 == v2, lane, big), axis=1, keepdims=True)
    e = jnp.exp(v2 - v1)
    w1 = pg / (1.0 + e)
    w2 = w1 * e
    rw_ref[...] = jnp.where(lane_i == 0, w1, jnp.where(lane_i == 1, w2, 0.0))
    e1 = (i1 - N_EXPERT_GROUPS).astype(jnp.int32)
    e2 = (i2 - N_EXPERT_GROUPS).astype(jnp.int32)
    re_ref[...] = jnp.where(lane_i == 0, e1, jnp.where(lane_i == 1, e2, 0))


def mix_route(am, x, wo, g, b, wr, br, alpha, tm):
    m, d = x.shape
    k = am.shape[1]
    const = lambda i: (0, 0)
    row = lambda i: (i, 0)
    return pl.pallas_call(
        functools.partial(_mix_route_kernel, alpha=alpha),
        grid=(m // tm,),
        in_specs=[pl.BlockSpec((tm, k), row), pl.BlockSpec((tm, d), row),
                  pl.BlockSpec((k, d), const), pl.BlockSpec((1, d), const), pl.BlockSpec((1, d), const),
                  pl.BlockSpec((d, LANES), const), pl.BlockSpec((1, LANES), const)],
        out_specs=[pl.BlockSpec((tm, d), row), pl.BlockSpec((tm, d), row),
                   pl.BlockSpec((tm, LANES), row), pl.BlockSpec((tm, LANES), row)],
        out_shape=[jax.ShapeDtypeStruct((m, d), F32), jax.ShapeDtypeStruct((m, d), BF16),
                   jax.ShapeDtypeStruct((m, LANES), F32), jax.ShapeDtypeStruct((m, LANES), jnp.int32)],
        compiler_params=_params(("parallel",)),
        name="mix_route",
    )(am, x, wo, g, b, wr, br)


def _experts_kernel(te_ref, tv_ref, x_ref, wg_ref, wu_ref, wd_ref, y_ref):
    t = pl.program_id(0)

    @pl.when(tv_ref[t] > 0)
    def _():
        x = x_ref[...]
        h = _silu(_dot(x, wg_ref[0])) * _dot(x, wu_ref[0])
        y_ref[...] = _dot(h.astype(BF16), wd_ref[0])

    @pl.when(tv_ref[t] == 0)
    def _():
        y_ref[...] = jnp.zeros_like(y_ref)


def experts(tile_expert, tile_valid, xs, wg, wu, wd, te):
    r, d = xs.shape
    ff = wg.shape[2]
    grid_spec = pltpu.PrefetchScalarGridSpec(
        num_scalar_prefetch=2,
        grid=(r // te,),
        in_specs=[pl.BlockSpec((te, d), lambda t, e, v: (t, 0)),
                  pl.BlockSpec((1, d, ff), lambda t, e, v: (e[t], 0, 0)),
                  pl.BlockSpec((1, d, ff), lambda t, e, v: (e[t], 0, 0)),
                  pl.BlockSpec((1, ff, d), lambda t, e, v: (e[t], 0, 0))],
        out_specs=pl.BlockSpec((te, d), lambda t, e, v: (t, 0)),
    )
    return pl.pallas_call(
        _experts_kernel,
        grid_spec=grid_spec,
        out_shape=jax.ShapeDtypeStruct((r, d), F32),
        compiler_params=_params(("arbitrary",)),
        name="experts",
    )(tile_expert, tile_valid, xs, wg, wu, wd)


def _combine_ple_kernel(x1_ref, y0_ref, y1_ref, rw_ref, p_ref, g_ref, b_ref, wg_ref, bg_ref, wp_ref,
                        o_ref, *, alpha):
    rw = rw_ref[...]
    f = rw[:, 0:1] * y0_ref[...] + rw[:, 1:2] * y1_ref[...]
    x2 = _layer_norm(alpha * x1_ref[...] + f, g_ref[...], b_ref[...])
    gl = _dot(x2.astype(BF16), wg_ref[...]) + bg_ref[...]
    gate = 1.0 / (1.0 + jnp.exp(-gl))
    pp = _dot(p_ref[...].astype(BF16), wp_ref[...])
    o_ref[...] = x2 + gate * pp


def combine_ple(x1, y0, y1, rw, p, g, b, wg, bg, wp, alpha, tm):
    m, d = x1.shape
    pd = p.shape[1]
    const = lambda i: (0, 0)
    row = lambda i: (i, 0)
    return pl.pallas_call(
        functools.partial(_combine_ple_kernel, alpha=alpha),
        grid=(m // tm,),
        in_specs=[pl.BlockSpec((tm, d), row), pl.BlockSpec((tm, d), row), pl.BlockSpec((tm, d), row),
                  pl.BlockSpec((tm, LANES), row), pl.BlockSpec((tm, pd), row),
                  pl.BlockSpec((1, d), const), pl.BlockSpec((1, d), const),
                  pl.BlockSpec((d, d), const), pl.BlockSpec((1, d), const), pl.BlockSpec((pd, d), const)],
        out_specs=pl.BlockSpec((tm, d), row),
        out_shape=jax.ShapeDtypeStruct((m, d), F32),
        compiler_params=_params(("parallel",)),
        name="combine_ple",
    )(x1, y0, y1, rw, p, g, b, wg, bg, wp)


def _tiles(n_tokens):
    tm = 512 if n_tokens >= 4096 else 128
    return tm, ((n_tokens + tm - 1) // tm) * tm


def _expert_tile(n_tokens):
    return 256 if n_tokens >= 4096 else 32


def _channel_and_ple(x, am, p, li, w_out, w, tm):
    depth = w["ln_mix_g"].shape[0]
    alpha = (2 * depth) ** 0.25
    ntp, d = x.shape
    wr = jnp.concatenate([w["w_router_group"][li],
                          jnp.moveaxis(w["w_router_expert"][li], 0, 1).reshape(d, N_EXPERTS)], axis=1)
    wr = jnp.pad(wr, ((0, 0), (0, LANES - wr.shape[1])))
    br = jnp.concatenate([w["b_router_group"][li], w["b_router_expert"][li].reshape(-1)])
    br = jnp.pad(br, (0, LANES - br.shape[0]))[None, :]
    x1, x1b, rw, re = mix_route(am, x, w_out.astype(BF16), w["ln_mix_g"][li][None],
                                w["ln_mix_b"][li][None], wr, br, alpha, tm)

    te = _expert_tile(ntp)
    n_flat = 2 * ntp
    flat = re[:, :2].reshape(-1)
    order = jnp.argsort(flat, stable=True).astype(jnp.int32)
    counts = jnp.sum(flat[:, None] == jnp.arange(N_EXPERTS, dtype=jnp.int32)[None, :], axis=0).astype(jnp.int32)
    padded = ((counts + te - 1) // te) * te
    gend = jnp.cumsum(padded)
    gstart = gend - padded
    cstart = jnp.cumsum(counts) - counts
    sorted_e = flat[order]
    pos_sorted = gstart[sorted_e] + (jnp.arange(n_flat, dtype=jnp.int32) - cstart[sorted_e])
    n_rows = ((n_flat + N_EXPERTS * (te - 1) + te - 1) // te) * te
    row_token = jnp.zeros((n_rows,), jnp.int32).at[pos_sorted].set(order // 2)
    pos_flat = jnp.zeros((n_flat,), jnp.int32).at[order].set(pos_sorted).reshape(ntp, 2)
    tile_start = jnp.arange(n_rows // te, dtype=jnp.int32) * te
    tile_expert = jnp.minimum(jnp.searchsorted(gend, tile_start, side="right"), N_EXPERTS - 1).astype(jnp.int32)
    tile_valid = (tile_start < gend[-1]).astype(jnp.int32)

    xs = jnp.take(x1b, row_token, axis=0)
    y = experts(tile_expert, tile_valid, xs, w["w_exp_gate"][li].astype(BF16),
                w["w_exp_up"][li].astype(BF16), w["w_exp_down"][li].astype(BF16), te)
    y0 = jnp.take(y, pos_flat[:, 0], axis=0)
    y1 = jnp.take(y, pos_flat[:, 1], axis=0)
    return combine_ple(x1, y0, y1, rw, p, w["ln_ffn_g"][li][None], w["ln_ffn_b"][li][None],
                       w["w_ple_gate"][li].astype(BF16), w["b_ple_gate"][li][None],
                       w["w_ple_proj"][li].astype(BF16), alpha, tm)


def kernel(x_prompt, x_sample, p_prompt, p_sample, cache_fox_k, cache_fox_v, cache_fox_logf, state_ssm, state_conv, cache_win_k, cache_win_v, page_table, w_in_even, b_fgate, conv_w, conv_b, dt_bias, a_log, d_skip, ssm_norm_w, w_out_even, w_in_odd, attn_sinks, w_out_odd, ln_mix_g, ln_mix_b, ln_ffn_g, ln_ffn_b, w_router_group, b_router_group, w_router_expert, b_router_expert, w_exp_gate, w_exp_up, w_exp_down, w_ple_proj, w_ple_gate, b_ple_gate):
    bp, seq, d = x_prompt.shape
    bd, t_dec, _ = x_sample.shape
    assert t_dec == 1 and d == D_MODEL
    depth = p_prompt.shape[0]
    n_pages = page_table.shape[1]
    past_len = n_pages * PAGE_SIZE
    np_tok = bp * seq
    nt = np_tok + bd
    tm, ntp = _tiles(nt)
    pad = ntp - nt

    def tokens(a_p, a_s):
        parts = [a_p.reshape(np_tok, -1), a_s.reshape(bd, -1)]
        if pad:
            parts.append(jnp.zeros((pad, parts[0].shape[1]), parts[0].dtype))
        return jnp.concatenate(parts, axis=0)

    x = tokens(x_prompt, x_sample)
    shared = dict(ln_mix_g=ln_mix_g, ln_mix_b=ln_mix_b, ln_ffn_g=ln_ffn_g, ln_ffn_b=ln_ffn_b,
                  w_router_group=w_router_group, b_router_group=b_router_group,
                  w_router_expert=w_router_expert, b_router_expert=b_router_expert,
                  w_exp_gate=w_exp_gate, w_exp_up=w_exp_up, w_exp_down=w_exp_down,
                  w_ple_proj=w_ple_proj, w_ple_gate=w_ple_gate, b_ple_gate=b_ple_gate)

    half = HEAD_DIM // 2
    inv = jnp.exp(-math.log(ROPE_THETA) * jnp.arange(half, dtype=F32) / half)
    pos = jnp.concatenate([jnp.tile(jnp.arange(seq, dtype=jnp.int32), bp),
                           jnp.full((bd,), past_len, jnp.int32), jnp.zeros((pad,), jnp.int32)])
    ang = pos.astype(F32)[:, None] * inv[None, :]
    rope_tn = 256
    cos_t = jnp.tile(jnp.cos(ang), (1, rope_tn // half))
    sin_t = jnp.tile(jnp.concatenate([-jnp.sin(ang), jnp.sin(ang)], axis=1), (1, rope_tn // HEAD_DIM))

    even_p, even_s, odd_p, odd_s = [], [], [], []
    for li in range(depth):
        j = li // 2
        p = tokens(p_prompt[li], p_sample[li])
        if li % 2 == 0:
            wi = w_in_even[j]
            c0 = 3 * FOX_WIDTH
            c1 = c0 + FOX_HEADS
            c2 = c1 + M_INNER
            c3 = c2 + CONV_CH
            w_main = jnp.concatenate([wi[:, :c0], wi[:, c1:c3]], axis=1).astype(BF16)
            w_small_t = jnp.concatenate([wi[:, c0:c1], wi[:, c3:]], axis=1).T
            b_small = jnp.concatenate([b_fgate[j], dt_bias[j]])[:, None]
            u = matmul(x, w_main, tm, 1024)
            small = small_proj(x, w_small_t, b_small, tm)

            tq = 256 if seq >= 256 else seq
            logf_p = small[:FOX_HEADS, :np_tok].reshape(FOX_HEADS, bp, seq)
            cum = cumsum_lanes(jnp.moveaxis(logf_p, 1, 0).reshape(bp * FOX_HEADS, seq), min(512, seq))
            ck = cum.reshape(bp, FOX_HEADS // 2, 2, seq // tq, tq).transpose(0, 1, 3, 2, 4)
            a_p = fox_prompt(u, ck, bp, seq, tq)
            u_s = u[np_tok:nt]
            logf_s = small[:FOX_HEADS, np_tok:nt].T
            n_phys = cache_fox_k.shape[1]
            lf_flat = cache_fox_logf.reshape(cache_fox_logf.shape[0] * n_phys, SUBLANES, LANES)
            a_s = fox_sample(page_table, u_s[:, :FOX_WIDTH].reshape(bd, FOX_HEADS, HEAD_DIM),
                             u_s[:, FOX_WIDTH:2 * FOX_WIDTH].reshape(bd, FOX_HEADS, HEAD_DIM),
                             u_s[:, 2 * FOX_WIDTH:c0].reshape(bd, FOX_HEADS, HEAD_DIM),
                             jnp.tile(logf_s, (1, LANES // FOX_HEADS))[:, None, :],
                             cache_fox_k, cache_fox_v, lf_flat, j, min(8, n_pages))

            nega = -jnp.exp(a_log[j])
            dskip_e = jnp.repeat(d_skip[j], M_HEADDIM)[None, :]
            normw = ssm_norm_w[j][None, :]
            m_p, st_p = ssd_prompt(u, small, conv_w[j], conv_b[j][None, :], nega[:, None], dskip_e, normw, bp, seq)
            zc = c0 + M_INNER
            m_s, st_s = ssd_sample(x[np_tok:nt], jnp.repeat(wi[:, c3:], M_HEADDIM, axis=1),
                                   jnp.repeat(dt_bias[j], M_HEADDIM)[None, :], jnp.repeat(nega, M_HEADDIM)[None, :],
                                   u_s[:, zc:], jnp.moveaxis(state_conv[j], 1, 0), u_s[:, c0:zc], conv_w[j], conv_b[j][None, :],
                                   dskip_e, normw, state_ssm[j], min(8, bd))
            am = jnp.concatenate([tokens(a_p, a_s.reshape(bd, FOX_WIDTH)), tokens(m_p, m_s)], axis=1)
            w_out = w_out_even[j]

            kp = u[:np_tok, FOX_WIDTH:2 * FOX_WIDTH].reshape(bp, seq, FOX_HEADS, HEAD_DIM)
            vp = u[:np_tok, 2 * FOX_WIDTH:c0].reshape(bp, seq, FOX_HEADS, HEAD_DIM)
            xbc_p = u[:np_tok, zc:].reshape(bp, seq, CONV_CH)
            conv_p = jnp.concatenate([jnp.zeros((bp, CONV_W - 1, CONV_CH), F32), xbc_p], axis=1)[:, -(CONV_W - 1):]
            even_p.append((kp, vp, jnp.moveaxis(logf_p, 0, 2),
                           st_p.reshape(bp, M_HEADS, M_HEADDIM, D_STATE), conv_p))
            conv_s = jnp.concatenate([state_conv[j], u_s[:, None, zc:]], axis=1)[:, -(CONV_W - 1):]
            even_s.append((u_s[:, FOX_WIDTH:2 * FOX_WIDTH].reshape(bd, 1, FOX_HEADS, HEAD_DIM),
                           u_s[:, 2 * FOX_WIDTH:c0].reshape(bd, 1, FOX_HEADS, HEAD_DIM),
                           logf_s[:, None, :], st_s, conv_s))
        else:
            u = matmul_rope(x, w_in_odd[j].astype(BF16), cos_t, sin_t, tm, rope_tn, ODD_MIX + KV_W)
            o_p = swa_prompt(attn_sinks[j], u, bp, seq)
            u_s = u[np_tok:nt]
            lw = cache_win_k.shape[2]
            kn = u_s[:, ODD_MIX:ODD_MIX + KV_W]
            vn = u_s[:, ODD_MIX + KV_W:]
            o_s = swa_sample(attn_sinks[j][:, None], u_s[:, :ODD_MIX].reshape(bd, SW_HEADS, HEAD_DIM), kn, vn,
                             cache_win_k[j].reshape(bd, lw, KV_W), cache_win_v[j].reshape(bd, lw, KV_W), min(8, bd))
            am = tokens(o_p, o_s.reshape(bd, ODD_MIX))
            w_out = w_out_odd[j]

            rows = min(WINDOW, seq)
            kp = u[:np_tok, ODD_MIX:ODD_MIX + KV_W].reshape(bp, seq, SW_KV_HEADS, HEAD_DIM)[:, -rows:]
            vp = u[:np_tok, ODD_MIX + KV_W:].reshape(bp, seq, SW_KV_HEADS, HEAD_DIM)[:, -rows:]
            odd_p.append((kp, vp))
            ka = jnp.concatenate([cache_win_k[j], kn.reshape(bd, 1, SW_KV_HEADS, HEAD_DIM)], axis=1)[:, -lw:]
            va = jnp.concatenate([cache_win_v[j], vn.reshape(bd, 1, SW_KV_HEADS, HEAD_DIM)], axis=1)[:, -lw:]
            odd_s.append((ka, va))
        x = _channel_and_ple(x, am, p, li, w_out, shared, tm)

    yp = x[:np_tok].reshape(bp, seq, d)
    ys = x[np_tok:nt].reshape(bd, 1, d)
    return (yp, ys,
            jnp.stack([st[0] for st in even_p]), jnp.stack([st[1] for st in even_p]),
            jnp.stack([st[2] for st in even_p]), jnp.stack([st[3] for st in even_p]),
            jnp.stack([st[4] for st in even_p]),
            jnp.stack([st[0] for st in odd_p]), jnp.stack([st[1] for st in odd_p]),
            jnp.stack([st[0] for st in even_s]), jnp.stack([st[1] for st in even_s]),
            jnp.stack([st[2] for st in even_s]), jnp.stack([st[3] for st in even_s]),
            jnp.stack([st[4] for st in even_s]),
            jnp.stack([st[0] for st in odd_s]), jnp.stack([st[1] for st in odd_s]))
```

```python
import functools
import math

import jax
import jax.numpy as jnp
from jax import lax
from jax.experimental import pallas as pl
from jax.experimental.pallas import tpu as pltpu

F32 = jnp.float32
BF16 = jnp.bfloat16
HIGHEST = lax.Precision.HIGHEST

D_MODEL = 1024
HEAD_DIM = 64
FOX_HEADS = 8
FOX_WIDTH = FOX_HEADS * HEAD_DIM
M_HEADS = 8
M_HEADDIM = 64
M_INNER = M_HEADS * M_HEADDIM
M_GROUPS = 2
HPG = M_HEADS // M_GROUPS
D_STATE = 128
CONV_W = 4
BC_W = M_GROUPS * D_STATE
CONV_CH = M_INNER + 2 * BC_W
SSD_CHUNK = 128
RMS_EPS = 1e-5
SW_HEADS = 16
SW_KV_HEADS = 4
SW_GROUP = SW_HEADS // SW_KV_HEADS
WINDOW = 128
ROPE_THETA = 10000.0
ODD_MIX = SW_HEADS * HEAD_DIM
KV_W = SW_KV_HEADS * HEAD_DIM
N_EXPERT_GROUPS = 4
EXPERTS_PER_GROUP = 8
N_EXPERTS = N_EXPERT_GROUPS * EXPERTS_PER_GROUP
EXPERT_FF = 512
PLE_DIM = 256
LN_EPS = 1e-5
PAGE_SIZE = 128

LANES = 128
SUBLANES = 8
VMEM_LIMIT = 48 * 1024 * 1024

NEG_INF = float("-inf")


def _params(sem, vmem=VMEM_LIMIT):
    return pltpu.CompilerParams(dimension_semantics=sem, vmem_limit_bytes=vmem)


def _nt_dot(a, b, precision=None):
    return lax.dot_general(a, b, (((1,), (1,)), ((), ())), precision=precision,
                           preferred_element_type=F32)


def _dot(a, b, precision=None):
    return jnp.dot(a, b, precision=precision, preferred_element_type=F32)


def _silu(x):
    return x * (1.0 / (1.0 + jnp.exp(-x)))


def _softplus(x):
    return jnp.maximum(x, 0.0) + jnp.log(1.0 + jnp.exp(-jnp.abs(x)))


def _iota(shape, dim):
    return lax.broadcasted_iota(jnp.int32, shape, dim)


def _mm_rope_kernel(x_ref, w_ref, cos_ref, sin_ref, o_ref, *, rope_tiles):
    acc = _dot(x_ref[...].astype(BF16), w_ref[...])
    j = pl.program_id(1)

    @pl.when(j < rope_tiles)
    def _():
        tn = acc.shape[1]
        half = HEAD_DIM // 2
        first = (_iota(acc.shape, 1) % HEAD_DIM) < half
        partner = jnp.where(first, pltpu.roll(acc, tn - half, 1), pltpu.roll(acc, half, 1))
        o_ref[...] = acc * cos_ref[...] + partner * sin_ref[...]

    @pl.when(j >= rope_tiles)
    def _():
        o_ref[...] = acc


def _mm_dual_kernel(x_ref, w_ref, o_ref, ob_ref, *, bf16_tiles):
    acc = _dot(x_ref[...].astype(BF16), w_ref[...])
    o_ref[...] = acc

    @pl.when(pl.program_id(1) < bf16_tiles)
    def _():
        ob_ref[...] = acc.astype(BF16)


def matmul_dual(x, w, tm, tn, bf16_cols):
    m, k = x.shape
    n = w.shape[1]
    nb = bf16_cols // tn
    return pl.pallas_call(
        functools.partial(_mm_dual_kernel, bf16_tiles=nb),
        grid=(m // tm, n // tn),
        in_specs=[pl.BlockSpec((tm, k), lambda i, j: (i, 0)),
                  pl.BlockSpec((k, tn), lambda i, j: (0, j))],
        out_specs=[pl.BlockSpec((tm, tn), lambda i, j: (i, j)),
                   pl.BlockSpec((tm, tn), lambda i, j: (i, jnp.minimum(j, nb - 1)))],
        out_shape=[jax.ShapeDtypeStruct((m, n), F32), jax.ShapeDtypeStruct((m, bf16_cols), BF16)],
        compiler_params=_params(("parallel", "arbitrary")),
        name="matmul_dual",
    )(x, w)


def matmul_rope(x, w, cos, sin, tm, tn, rope_cols):
    m, k = x.shape
    n = w.shape[1]
    return pl.pallas_call(
        functools.partial(_mm_rope_kernel, rope_tiles=rope_cols // tn),
        grid=(m // tm, n // tn),
        in_specs=[pl.BlockSpec((tm, k), lambda i, j: (i, 0)),
                  pl.BlockSpec((k, tn), lambda i, j: (0, j)),
                  pl.BlockSpec((tm, tn), lambda i, j: (i, 0)),
                  pl.BlockSpec((tm, tn), lambda i, j: (i, 0))],
        out_specs=pl.BlockSpec((tm, tn), lambda i, j: (i, j)),
        out_shape=jax.ShapeDtypeStruct((m, n), F32),
        compiler_params=_params(("parallel", "arbitrary")),
        name="matmul_rope",
    )(x, w, cos, sin)


def _small_proj_kernel(x_ref, wt_ref, b_ref, o_ref):
    r = _nt_dot(wt_ref[...], x_ref[...], precision=HIGHEST) + b_ref[...]
    row = _iota(r.shape, 0)
    o_ref[...] = jnp.where(row < FOX_HEADS, -_softplus(-r), _softplus(r))


def small_proj(x, wt, b, tm):
    m, k = x.shape
    return pl.pallas_call(
        _small_proj_kernel,
        grid=(m // tm,),
        in_specs=[pl.BlockSpec((tm, k), lambda i: (i, 0)),
                  pl.BlockSpec((16, k), lambda i: (0, 0)),
                  pl.BlockSpec((16, 1), lambda i: (0, 0))],
        out_specs=pl.BlockSpec((16, tm), lambda i: (0, i)),
        out_shape=jax.ShapeDtypeStruct((16, m), F32),
        compiler_params=_params(("parallel",)),
        name="small_proj",
    )(x, wt, b)


def _cumsum_kernel(x_ref, o_ref, carry_ref):
    @pl.when(pl.program_id(0) == 0)
    def _():
        carry_ref[...] = jnp.zeros_like(carry_ref)

    x = x_ref[...]
    w = x.shape[1]
    tri = (_iota((w, w), 0) <= _iota((w, w), 1)).astype(F32)
    c = _dot(x, tri, precision=HIGHEST) + carry_ref[...]
    o_ref[...] = c
    carry_ref[...] = c[:, w - 1:w]


def cumsum_lanes(x, chunk):
    r, l = x.shape
    return pl.pallas_call(
        _cumsum_kernel,
        grid=(l // chunk,),
        in_specs=[pl.BlockSpec((r, chunk), lambda i: (0, i))],
        out_specs=pl.BlockSpec((r, chunk), lambda i: (0, i)),
        out_shape=jax.ShapeDtypeStruct((r, l), F32),
        scratch_shapes=[pltpu.VMEM((r, 1), F32)],
        compiler_params=_params(("arbitrary",)),
        name="cumsum_lanes",
    )(x)


def _fox_prompt_kernel(q_ref, k_ref, v_ref, ck_ref, o_ref, *, tq):
    qi = pl.program_id(2)
    q = q_ref[...] * (HEAD_DIM ** -0.5)
    lane = _iota(q.shape, 1)
    zero = jnp.zeros_like(q)
    q_heads = (jnp.where(lane < HEAD_DIM, q, zero), jnp.where(lane >= HEAD_DIM, q, zero))

    def step(j, carry, masked):
        start = pl.multiple_of(j * tq, tq)
        kb = k_ref[pl.ds(start, tq), :]
        vb = v_ref[pl.ds(start, tq), :]
        ck = ck_ref[0, 0, j]
        out = []
        for h in range(2):
            m, l, acc = carry[h]
            s = _nt_dot(q_heads[h], kb) - ck[h:h + 1, :]
            if masked:
                s = jnp.where(_iota(s.shape, 1) <= _iota(s.shape, 0), s, NEG_INF)
            m_new = jnp.maximum(m, jnp.max(s, axis=1, keepdims=True))
            alpha = jnp.exp(m - m_new)
            p = jnp.exp(s - m_new)
            l = alpha * l + jnp.sum(p, axis=1, keepdims=True)
            acc = alpha * acc + _dot(p.astype(BF16), vb)
            out.append((m_new, l, acc))
        return tuple(out)

    init1 = (jnp.full((tq, 1), NEG_INF, F32), jnp.zeros((tq, 1), F32), jnp.zeros((tq, LANES), F32))
    carry = lax.fori_loop(0, qi, lambda j, c: step(j, c, False), (init1, init1))
    (_, l0, a0), (_, l1, a1) = step(qi, carry, True)
    o_ref[...] = jnp.where(lane < HEAD_DIM, a0 / l0, a1 / l1)


def fox_prompt(qkv, ck, n_batch, seq, tq):
    nq = seq // tq
    pairs = FOX_HEADS // 2
    return pl.pallas_call(
        functools.partial(_fox_prompt_kernel, tq=tq),
        grid=(n_batch, pairs, nq),
        in_specs=[pl.BlockSpec((tq, LANES), lambda b, h, i: (b * nq + i, h)),
                  pl.BlockSpec((seq, LANES), lambda b, h, i: (b, pairs + h)),
                  pl.BlockSpec((seq, LANES), lambda b, h, i: (b, 2 * pairs + h)),
                  pl.BlockSpec((1, 1, nq, 2, tq), lambda b, h, i: (b, h, 0, 0, 0))],
        out_specs=pl.BlockSpec((tq, LANES), lambda b, h, i: (b * nq + i, h)),
        out_shape=jax.ShapeDtypeStruct((n_batch * seq, FOX_WIDTH), F32),
        compiler_params=_params(("parallel", "parallel", "arbitrary")),
        name="fox_prompt",
    )(qkv, qkv, qkv, ck)


def _block_diag_rows(full):
    rowh = _iota((FOX_HEADS, HEAD_DIM), 0)
    out = jnp.zeros((FOX_HEADS, HEAD_DIM), F32)
    for h in range(FOX_HEADS):
        out = out + jnp.where(rowh == h, full[:, h * HEAD_DIM:(h + 1) * HEAD_DIM], 0.0)
    return out


def _fox_sample_kernel(pt_ref, qbd_ref, q_ref, kn_ref, vbd_ref, ln_ref, *refs, pages_per_step):
    del pt_ref
    pp = pages_per_step
    k_refs, v_refs, lf_refs = refs[:pp], refs[pp:2 * pp], refs[2 * pp:3 * pp]
    o_ref = refs[3 * pp]
    m_ref, l_ref, acc_ref, carry_ref = refs[3 * pp + 1:]
    t = pl.program_id(1)
    scale = HEAD_DIM ** -0.5

    @pl.when(t == 0)
    def _():
        m_ref[...] = jnp.sum(q_ref[0] * kn_ref[0], axis=1, keepdims=True) * scale
        l_ref[...] = jnp.ones_like(l_ref)
        acc_ref[...] = vbd_ref[0]
        carry_ref[...] = ln_ref[0]

    qb = (qbd_ref[0] * scale).astype(BF16)
    lane = _iota((FOX_HEADS, PAGE_SIZE), 1)
    width = FOX_HEADS * HEAD_DIM
    carry = carry_ref[...]
    scores = []
    for r in range(pp):
        lf = lf_refs[r][0, 0]
        x = lf
        for sh in (1, 2, 4, 8, 16, 32, 64):
            x = x + jnp.where(lane + sh < PAGE_SIZE, pltpu.roll(x, PAGE_SIZE - sh, 1), 0.0)
        kp = k_refs[r][0, 0].reshape(width, PAGE_SIZE).astype(BF16)
        scores.append(_dot(qb, kp) + ((x - lf) + carry))
        carry = carry + x[:, 0:1]
    carry_ref[...] = carry
    s = jnp.concatenate(scores, axis=1)
    m = m_ref[...]
    m_new = jnp.maximum(m, jnp.max(s, axis=1, keepdims=True))
    alpha = jnp.exp(m - m_new)
    p = jnp.exp(s - m_new)
    l_ref[...] = alpha * l_ref[...] + jnp.sum(p, axis=1, keepdims=True)
    m_ref[...] = m_new
    pb = p.astype(BF16)
    acc = alpha * acc_ref[...]
    for r in range(pp):
        vp = v_refs[r][0, 0].reshape(width, PAGE_SIZE).astype(BF16)
        acc = acc + _nt_dot(pb[:, r * PAGE_SIZE:(r + 1) * PAGE_SIZE], vp)
    acc_ref[...] = acc

    @pl.when(t == pl.num_programs(1) - 1)
    def _():
        o_ref[0] = _block_diag_rows(acc / l_ref[...])


def fox_sample(page_table, qbd, q, kn, vbd, ln, cache_kt, cache_vt, cache_lft, layer, pages_per_step):
    bd, n_pages = page_table.shape
    pp = pages_per_step
    steps = n_pages // pp
    width = FOX_HEADS * HEAD_DIM

    def page(b, t, pt, r):
        return pt[b, n_pages - 1 - (t * pp + r)]

    kv_specs = [pl.BlockSpec((1, 1, FOX_HEADS, HEAD_DIM, PAGE_SIZE),
                             functools.partial(lambda b, t, pt, r: (layer, page(b, t, pt, r), 0, 0, 0), r=r))
                for r in range(pp)]
    lf_specs = [pl.BlockSpec((1, 1, FOX_HEADS, PAGE_SIZE),
                             functools.partial(lambda b, t, pt, r: (layer, page(b, t, pt, r), 0, 0), r=r))
                for r in range(pp)]
    tok = pl.BlockSpec((1, FOX_HEADS, HEAD_DIM), lambda b, t, pt: (b, 0, 0))
    wide = pl.BlockSpec((1, FOX_HEADS, width), lambda b, t, pt: (b, 0, 0))
    grid_spec = pltpu.PrefetchScalarGridSpec(
        num_scalar_prefetch=1,
        grid=(bd, steps),
        in_specs=[wide, tok, tok, wide, pl.BlockSpec((1, FOX_HEADS, 1), lambda b, t, pt: (b, 0, 0))]
        + kv_specs + kv_specs + lf_specs,
        out_specs=tok,
        scratch_shapes=[pltpu.VMEM((FOX_HEADS, 1), F32), pltpu.VMEM((FOX_HEADS, 1), F32),
                        pltpu.VMEM((FOX_HEADS, width), F32), pltpu.VMEM((FOX_HEADS, 1), F32)],
    )
    return pl.pallas_call(
        functools.partial(_fox_sample_kernel, pages_per_step=pp),
        grid_spec=grid_spec,
        out_shape=jax.ShapeDtypeStruct((bd, FOX_HEADS, HEAD_DIM), F32),
        compiler_params=_params(("parallel", "arbitrary")),
        name="fox_sample",
    )(page_table, qbd, q, kn, vbd, ln, *([cache_kt] * pp), *([cache_vt] * pp), *([cache_lft] * pp))


def _ssd_epilogue(y, xs, z, dskip_e, normw):
    y = (y + dskip_e * xs) * _silu(z)
    half = M_INNER // M_GROUPS
    outs = []
    for g in range(M_GROUPS):
        yg = y[:, g * half:(g + 1) * half]
        ms = jnp.sum(yg * yg, axis=1, keepdims=True) * (1.0 / half)
        outs.append(yg * lax.rsqrt(ms + RMS_EPS))
    return jnp.concatenate(outs, axis=1) * normw


def _ssd_prompt_kernel(xbc_ref, z_ref, dt_ref, cw_ref, cb_ref, nega_ref, dskip_ref, normw_ref,
                       o_ref, st_ref, ext_ref, h_ref):
    c = pl.program_id(1)
    L = SSD_CHUNK
    pad = SUBLANES

    @pl.when(c == 0)
    def _():
        ext_ref[0:pad, :] = jnp.zeros((pad, CONV_CH), F32)
        h_ref[...] = jnp.zeros_like(h_ref)

    ext_ref[pad:pad + L, :] = xbc_ref[...]
    acc = ext_ref[pad:pad + L, :] * cw_ref[CONV_W - 1:CONV_W, :]
    for j in range(CONV_W - 1):
        off = pad - (CONV_W - 1) + j
        acc = acc + ext_ref[off:off + L, :] * cw_ref[j:j + 1, :]
    u = _silu(acc + cb_ref[...])
    ext_ref[0:pad, :] = ext_ref[L:L + pad, :]

    xs = u[:, :M_INNER]
    dt_t = dt_ref[...]
    cum_t = _dot(dt_t * nega_ref[...], (_iota((L, L), 0) <= _iota((L, L), 1)).astype(F32),
                 precision=HIGHEST)
    eye = (_iota((L, L), 0) == _iota((L, L), 1)).astype(F32)
    cols = _nt_dot(eye, jnp.concatenate([cum_t, dt_t], axis=0), precision=HIGHEST)
    cum_last = cum_t[:, L - 1:L]
    tail_t = jnp.exp(cum_last - cum_t) * dt_t
    tril = _iota((L, L), 0) >= _iota((L, L), 1)
    lane = _iota((L, LANES), 1)
    rowi = _iota((L, LANES), 0)

    y_pairs = []
    for g in range(M_GROUPS):
        bm = u[:, M_INNER + g * D_STATE:M_INNER + (g + 1) * D_STATE]
        cm = u[:, M_INNER + BC_W + g * D_STATE:M_INNER + BC_W + (g + 1) * D_STATE]
        bmb = bm.astype(BF16)
        cmb = cm.astype(BF16)
        cb = _nt_dot(cmb, bmb)
        for pr in range(HPG // 2):
            pidx = g * (HPG // 2) + pr
            xs_pair = xs[:, pidx * LANES:(pidx + 1) * LANES]
            xs_pair_b = xs_pair.astype(BF16)
            h0 = h_ref[pidx]
            ych = _nt_dot(cmb, h0.astype(BF16))
            yw = []
            for k in range(2):
                hd = 2 * pidx + k
                diff = cols[:, hd:hd + 1] - cum_t[hd:hd + 1, :]
                decay = jnp.exp(jnp.where(tril, diff, NEG_INF))
                w = cb * decay * dt_t[hd:hd + 1, :]
                yw.append(_dot(w.astype(BF16), xs_pair_b))
            e0 = jnp.exp(cols[:, 2 * pidx:2 * pidx + 1])
            e1 = jnp.exp(cols[:, 2 * pidx + 1:2 * pidx + 2])
            first = lane < M_HEADDIM
            y_pairs.append(jnp.where(first, yw[0], yw[1]) + ych * jnp.where(first, e0, e1))
            top = rowi < M_HEADDIM
            tail_m = jnp.where(top, jnp.broadcast_to(tail_t[2 * pidx:2 * pidx + 1, :], (L, L)),
                               jnp.broadcast_to(tail_t[2 * pidx + 1:2 * pidx + 2, :], (L, L)))
            dec_m = jnp.where(top, jnp.exp(cum_last[2 * pidx:2 * pidx + 1, :]),
                              jnp.exp(cum_last[2 * pidx + 1:2 * pidx + 2, :]))
            xt = xs_pair.T * tail_m
            h_ref[pidx] = h0 * dec_m + _dot(xt.astype(BF16), bmb)

    y = jnp.concatenate(y_pairs, axis=1)
    o_ref[...] = _ssd_epilogue(y, xs, z_ref[...], dskip_ref[...], normw_ref[...])
    st_ref[0] = h_ref[...]


def ssd_prompt(u_main, dt_rows, conv_w, conv_b, nega, dskip_e, normw, n_batch, seq):
    L = SSD_CHUNK
    nc = seq // L
    pairs = M_HEADS // 2
    const = lambda b, c: (0, 0)
    return pl.pallas_call(
        _ssd_prompt_kernel,
        grid=(n_batch, nc),
        in_specs=[pl.BlockSpec((L, CONV_CH), lambda b, c: (b * nc + c, 2)),
                  pl.BlockSpec((L, M_INNER), lambda b, c: (b * nc + c, 3)),
                  pl.BlockSpec((M_HEADS, L), lambda b, c: (1, b * nc + c)),
                  pl.BlockSpec((CONV_W, CONV_CH), const),
                  pl.BlockSpec((1, CONV_CH), const),
                  pl.BlockSpec((M_HEADS, 1), const),
                  pl.BlockSpec((1, M_INNER), const),
                  pl.BlockSpec((1, M_INNER), const)],
        out_specs=[pl.BlockSpec((L, M_INNER), lambda b, c: (b * nc + c, 0)),
                   pl.BlockSpec((1, pairs, LANES, D_STATE), lambda b, c: (b, 0, 0, 0))],
        out_shape=[jax.ShapeDtypeStruct((n_batch * seq, M_INNER), F32),
                   jax.ShapeDtypeStruct((n_batch, pairs, LANES, D_STATE), F32)],
        scratch_shapes=[pltpu.VMEM((L + SUBLANES, CONV_CH), F32),
                        pltpu.VMEM((pairs, LANES, D_STATE), F32)],
        compiler_params=_params(("parallel", "arbitrary")),
        name="ssd_prompt",
    )(u_main, u_main, dt_rows, conv_w, conv_b, nega, dskip_e, normw)


def _ssd_sample_kernel(x_ref, wdt_ref, dtb_ref, nega_ref, xbc_ref, ctx_ref, z_ref, cw_ref, cb_ref,
                       dskip_ref, normw_ref, h0_ref, o_ref, hn_ref,
                       u_ref, coef_t_ref, dec_t_ref, dec_ref, dtx_ref, *, seqs_per_step):
    sb = seqs_per_step
    i = pl.program_id(0)
    nseq = x_ref.shape[0]

    @pl.when(i == 0)
    def _():
        acc = xbc_ref[...] * cw_ref[CONV_W - 1:CONV_W, :]
        for j in range(CONV_W - 1):
            acc = acc + ctx_ref[j] * cw_ref[j:j + 1, :]
        u = _silu(acc + cb_ref[...])
        u_ref[...] = u
        dt = _softplus(_dot(x_ref[...], wdt_ref[...], precision=HIGHEST) + dtb_ref[...])
        dec = jnp.exp(dt * nega_ref[...])
        coef = dt * u[:, :M_INNER]
        dec_ref[...] = dec
        dtx_ref[...] = coef
        for blk in range(M_INNER // LANES):
            sl = slice(blk * LANES, (blk + 1) * LANES)
            coef_t_ref[sl, :] = coef[:, sl].T
            dec_t_ref[sl, :] = dec[:, sl].T

    base = pl.multiple_of(i * sb, sb)
    ub = u_ref[pl.ds(base, sb), :]
    lane_seq = _iota((M_INNER, nseq), 1)
    rows = _iota((sb, M_INNER // M_GROUPS), 0)
    ch = [jnp.zeros((sb, M_INNER // M_GROUPS), F32) for _ in range(M_GROUPS)]
    for s in range(sb):
        onehot = lane_seq == base + s
        cx = jnp.sum(jnp.where(onehot, coef_t_ref[...], 0.0), axis=1, keepdims=True)
        dc = jnp.sum(jnp.where(onehot, dec_t_ref[...], 0.0), axis=1, keepdims=True)
        for g in range(M_GROUPS):
            brow = ub[s:s + 1, M_INNER + g * D_STATE:M_INNER + (g + 1) * D_STATE]
            cblk = ub[:, M_INNER + BC_W + g * D_STATE:M_INNER + BC_W + (g + 1) * D_STATE]
            hg = h0_ref[s, g * HPG:(g + 1) * HPG].reshape(HPG * M_HEADDIM, D_STATE)
            r = _nt_dot(cblk.astype(BF16), hg.astype(BF16))
            ch[g] = ch[g] + jnp.where(rows == s, r, 0.0)
            lo = g * HPG * M_HEADDIM
            hn = hg * dc[lo:lo + HPG * M_HEADDIM] + cx[lo:lo + HPG * M_HEADDIM] * brow
            hn_ref[s, g * HPG:(g + 1) * HPG] = hn.reshape(HPG, M_HEADDIM, D_STATE)

    xs = ub[:, :M_INNER]
    dec = dec_ref[pl.ds(base, sb), :]
    coef = dtx_ref[pl.ds(base, sb), :]
    ys = []
    half = M_INNER // M_GROUPS
    for g in range(M_GROUPS):
        bm = ub[:, M_INNER + g * D_STATE:M_INNER + (g + 1) * D_STATE]
        cm = ub[:, M_INNER + BC_W + g * D_STATE:M_INNER + BC_W + (g + 1) * D_STATE]
        cb = jnp.sum(cm * bm, axis=1, keepdims=True)
        ys.append(cb * coef[:, g * half:(g + 1) * half] + ch[g] * dec[:, g * half:(g + 1) * half])
    y = jnp.concatenate(ys, axis=1)
    o_ref[...] = _ssd_epilogue(y, xs, z_ref[...], dskip_ref[...], normw_ref[...])


def ssd_sample(x_s, wdt_e, dtb_e, nega_e, xbc_s, ctx, z_s, conv_w, conv_b, dskip_e, normw, h0, seqs_per_step):
    bd = x_s.shape[0]
    sb = seqs_per_step
    const = lambda i: (0, 0)
    return pl.pallas_call(
        functools.partial(_ssd_sample_kernel, seqs_per_step=sb),
        grid=(bd // sb,),
        in_specs=[pl.BlockSpec((bd, D_MODEL), const),
                  pl.BlockSpec((D_MODEL, M_INNER), const),
                  pl.BlockSpec((1, M_INNER), const),
                  pl.BlockSpec((1, M_INNER), const),
                  pl.BlockSpec((bd, CONV_CH), const),
                  pl.BlockSpec((CONV_W - 1, bd, CONV_CH), lambda i: (0, 0, 0)),
                  pl.BlockSpec((sb, M_INNER), lambda i: (i, 0)),
                  pl.BlockSpec((CONV_W, CONV_CH), const),
                  pl.BlockSpec((1, CONV_CH), const),
                  pl.BlockSpec((1, M_INNER), const),
                  pl.BlockSpec((1, M_INNER), const),
                  pl.BlockSpec((sb, M_HEADS, M_HEADDIM, D_STATE), lambda i: (i, 0, 0, 0))],
        out_specs=[pl.BlockSpec((sb, M_INNER), lambda i: (i, 0)),
                   pl.BlockSpec((sb, M_HEADS, M_HEADDIM, D_STATE), lambda i: (i, 0, 0, 0))],
        out_shape=[jax.ShapeDtypeStruct((bd, M_INNER), F32),
                   jax.ShapeDtypeStruct((bd, M_HEADS, M_HEADDIM, D_STATE), F32)],
        scratch_shapes=[pltpu.VMEM((bd, CONV_CH), F32),
                        pltpu.VMEM((M_INNER, bd), F32),
                        pltpu.VMEM((M_INNER, bd), F32),
                        pltpu.VMEM((bd, M_INNER), F32),
                        pltpu.VMEM((bd, M_INNER), F32)],
        compiler_params=_params(("arbitrary",)),
        name="ssd_sample",
    )(x_s, wdt_e, dtb_e, nega_e, xbc_s, ctx, z_s, conv_w, conv_b, dskip_e, normw, h0)


def _to_half(x, src_half, dst_half):
    return x if src_half == dst_half else pltpu.roll(x, HEAD_DIM, 1)


def _swa_prompt_kernel(sink_ref, q_ref, kp_ref, kc_ref, vp_ref, vc_ref, o_ref):
    n = pl.program_id(1)
    W = WINDOW
    scale = HEAD_DIM ** -0.5
    lane = _iota((W, LANES), 1)
    rows4 = _iota((SW_GROUP * W, 2 * W), 0)
    t = rows4 % W
    col = _iota((SW_GROUP * W, 2 * W), 1)
    valid = (col >= t) & (col <= t + W) & ((n > 0) | (col >= W))
    rcol = _iota((SW_GROUP * W, 1), 0) // W
    outs = [None] * SW_HEADS
    for j in range(SW_KV_HEADS):
        ch, hf = j // 2, j % 2
        kk = jnp.concatenate([kp_ref[:, ch * LANES:(ch + 1) * LANES],
                              kc_ref[:, ch * LANES:(ch + 1) * LANES]], axis=0).astype(BF16)
        vv = jnp.concatenate([vp_ref[:, ch * LANES:(ch + 1) * LANES],
                              vc_ref[:, ch * LANES:(ch + 1) * LANES]], axis=0).astype(BF16)
        qs = []
        sink = jnp.zeros((SW_GROUP * W, 1), F32)
        for g in range(SW_GROUP):
            hq = j * SW_GROUP + g
            qc = q_ref[:, (hq // 2) * LANES:(hq // 2 + 1) * LANES] * scale
            qc = _to_half(qc, hq % 2, hf)
            keep = (lane < HEAD_DIM) if hf == 0 else (lane >= HEAD_DIM)
            qs.append(jnp.where(keep, qc, 0.0))
            sink = jnp.where(rcol == g, sink_ref[hq], sink)
        q4 = jnp.concatenate(qs, axis=0).astype(BF16)
        s = jnp.where(valid, _nt_dot(q4, kk), NEG_INF)
        m = jnp.maximum(jnp.max(s, axis=1, keepdims=True), sink)
        p = jnp.exp(s - m)
        den = jnp.sum(p, axis=1, keepdims=True) + jnp.exp(sink - m)
        o = _dot(p.astype(BF16), vv) / den
        for g in range(SW_GROUP):
            hq = j * SW_GROUP + g
            outs[hq] = _to_half(o[g * W:(g + 1) * W], hf, hq % 2)
    for c in range(SW_HEADS // 2):
        o_ref[:, c * LANES:(c + 1) * LANES] = jnp.where(lane < HEAD_DIM, outs[2 * c], outs[2 * c + 1])


def swa_prompt(sinks, u_odd, n_batch, seq):
    nb = seq // WINDOW
    kcol = ODD_MIX // KV_W
    return pl.pallas_call(
        _swa_prompt_kernel,
        grid=(n_batch, nb),
        in_specs=[pl.BlockSpec(memory_space=pltpu.SMEM),
                  pl.BlockSpec((WINDOW, ODD_MIX), lambda b, n: (b * nb + n, 0)),
                  pl.BlockSpec((WINDOW, KV_W), lambda b, n: (b * nb + jnp.maximum(n - 1, 0), kcol)),
                  pl.BlockSpec((WINDOW, KV_W), lambda b, n: (b * nb + n, kcol)),
                  pl.BlockSpec((WINDOW, KV_W), lambda b, n: (b * nb + jnp.maximum(n - 1, 0), kcol + 1)),
                  pl.BlockSpec((WINDOW, KV_W), lambda b, n: (b * nb + n, kcol + 1))],
        out_specs=pl.BlockSpec((WINDOW, ODD_MIX), lambda b, n: (b * nb + n, 0)),
        out_shape=jax.ShapeDtypeStruct((n_batch * seq, ODD_MIX), F32),
        compiler_params=_params(("parallel", "arbitrary")),
        name="swa_prompt",
    )(sinks, u_odd, u_odd, u_odd, u_odd, u_odd)


def _swa_sample_kernel(sink_ref, q_ref, kn_ref, vn_ref, bk_ref, bv_ref, o_ref, *, seqs_per_step):
    scale = HEAD_DIM ** -0.5
    rowg = _iota((SW_HEADS, HEAD_DIM), 0) // SW_GROUP
    sink = sink_ref[...]
    for s in range(seqs_per_step):
        q16 = q_ref[s] * scale
        qbd = jnp.concatenate([jnp.where(rowg == j, q16, 0.0) for j in range(SW_KV_HEADS)], axis=1)
        kb = bk_ref[s].reshape(KV_W, -1).astype(BF16)
        vb = bv_ref[s].reshape(KV_W, -1).astype(BF16)
        sc = _dot(qbd.astype(BF16), kb)
        s_new = jnp.sum(qbd * kn_ref[s:s + 1, :], axis=1, keepdims=True)
        m = jnp.maximum(jnp.maximum(jnp.max(sc, axis=1, keepdims=True), s_new), sink)
        p = jnp.exp(sc - m)
        pn = jnp.exp(s_new - m)
        den = jnp.sum(p, axis=1, keepdims=True) + pn + jnp.exp(sink - m)
        full = (_nt_dot(p.astype(BF16), vb) + pn * vn_ref[s:s + 1, :]) / den
        o16 = jnp.zeros((SW_HEADS, HEAD_DIM), F32)
        for j in range(SW_KV_HEADS):
            o16 = o16 + jnp.where(rowg == j, full[:, j * HEAD_DIM:(j + 1) * HEAD_DIM], 0.0)
        o_ref[s] = o16


def swa_sample(sinks_col, q, kn, vn, buf_k, buf_v, seqs_per_step):
    bd, _, _, lw = buf_k.shape
    sb = seqs_per_step
    return pl.pallas_call(
        functools.partial(_swa_sample_kernel, seqs_per_step=sb),
        grid=(bd // sb,),
        in_specs=[pl.BlockSpec((SW_HEADS, 1), lambda i: (0, 0)),
                  pl.BlockSpec((sb, SW_HEADS, HEAD_DIM), lambda i: (i, 0, 0)),
                  pl.BlockSpec((sb, KV_W), lambda i: (i, 0)),
                  pl.BlockSpec((sb, KV_W), lambda i: (i, 0)),
                  pl.BlockSpec((sb, SW_KV_HEADS, HEAD_DIM, lw), lambda i: (i, 0, 0, 0)),
                  pl.BlockSpec((sb, SW_KV_HEADS, HEAD_DIM, lw), lambda i: (i, 0, 0, 0))],
        out_specs=pl.BlockSpec((sb, SW_HEADS, HEAD_DIM), lambda i: (i, 0, 0)),
        out_shape=jax.ShapeDtypeStruct((bd, SW_HEADS, HEAD_DIM), F32),
        compiler_params=_params(("parallel",)),
        name="swa_sample",
    )(sinks_col, q, kn, vn, buf_k, buf_v)


def _layer_norm(h, g, b):
    mu = jnp.mean(h, axis=1, keepdims=True)
    d = h - mu
    var = jnp.mean(d * d, axis=1, keepdims=True)
    return d * lax.rsqrt(var + LN_EPS) * g + b


def _mix_route_kernel(am_ref, x_ref, wo_ref, g_ref, b_ref, wr_ref, br_ref,
                      x1_ref, rw_ref, re_ref, *, alpha):
    mix = _dot(am_ref[...].astype(BF16), wo_ref[...])
    x1 = _layer_norm(alpha * x_ref[...] + mix, g_ref[...], b_ref[...])
    x1_ref[...] = x1
    logits = _dot(x1, wr_ref[...], precision=HIGHEST) + br_ref[...]
    lane_i = _iota(logits.shape, 1)
    lane = lane_i.astype(F32)
    big = float(LANES)
    gl = jnp.where(lane_i < N_EXPERT_GROUPS, logits, NEG_INF)
    gmax = jnp.max(gl, axis=1, keepdims=True)
    grp = jnp.min(jnp.where(gl == gmax, lane, big), axis=1, keepdims=True)
    pg = 1.0 / jnp.sum(jnp.exp(gl - gmax), axis=1, keepdims=True)
    rel = lane - (N_EXPERT_GROUPS + grp * EXPERTS_PER_GROUP)
    el = jnp.where(rel >= 0.0, jnp.where(rel < EXPERTS_PER_GROUP, logits, NEG_INF), NEG_INF)
    v1 = jnp.max(el, axis=1, keepdims=True)
    i1 = jnp.min(jnp.where(el == v1, lane, big), axis=1, keepdims=True)
    el2 = jnp.where(lane == i1, NEG_INF, el)
    v2 = jnp.max(el2, axis=1, keepdims=True)
    i2 = jnp.min(jnp.where(el2 == v2, lane, big), axis=1, keepdims=True)
    e = jnp.exp(v2 - v1)
    w1 = pg / (1.0 + e)
    w2 = w1 * e
    rw_ref[...] = jnp.where(lane_i == 0, w1, jnp.where(lane_i == 1, w2, 0.0))
    e1 = (i1 - N_EXPERT_GROUPS).astype(jnp.int32)
    e2 = (i2 - N_EXPERT_GROUPS).astype(jnp.int32)
    re_ref[...] = jnp.where(lane_i == 0, e1, jnp.where(lane_i == 1, e2, 0))


def mix_route(am, x, wo, g, b, wr, br, alpha, tm):
    m, d = x.shape
    k = am.shape[1]
    const = lambda i: (0, 0)
    row = lambda i: (i, 0)
    return pl.pallas_call(
        functools.partial(_mix_route_kernel, alpha=alpha),
        grid=(m // tm,),
        in_specs=[pl.BlockSpec((tm, k), row), pl.BlockSpec((tm, d), row),
                  pl.BlockSpec((k, d), const), pl.BlockSpec((1, d), const), pl.BlockSpec((1, d), const),
                  pl.BlockSpec((d, LANES), const), pl.BlockSpec((1, LANES), const)],
        out_specs=[pl.BlockSpec((tm, d), row),
                   pl.BlockSpec((tm, LANES), row), pl.BlockSpec((tm, LANES), row)],
        out_shape=[jax.ShapeDtypeStruct((m, d), F32),
                   jax.ShapeDtypeStruct((m, LANES), F32), jax.ShapeDtypeStruct((m, LANES), jnp.int32)],
        compiler_params=_params(("parallel",)),
        name="mix_route",
    )(am, x, wo, g, b, wr, br)


def _experts_kernel(te_ref, tv_ref, x_ref, wg_ref, wu_ref, wd_ref, y_ref):
    t = pl.program_id(0)

    @pl.when(tv_ref[t] > 0)
    def _():
        x = x_ref[...].astype(BF16)
        h = _silu(_dot(x, wg_ref[0].astype(BF16))) * _dot(x, wu_ref[0].astype(BF16))
        y_ref[...] = _dot(h.astype(BF16), wd_ref[0].astype(BF16))

    @pl.when(tv_ref[t] == 0)
    def _():
        y_ref[...] = jnp.zeros_like(y_ref)


def experts(tile_expert, tile_valid, xs, wg, wu, wd, te):
    r, d = xs.shape
    ff = wg.shape[2]
    grid_spec = pltpu.PrefetchScalarGridSpec(
        num_scalar_prefetch=2,
        grid=(r // te,),
        in_specs=[pl.BlockSpec((te, d), lambda t, e, v: (t, 0)),
                  pl.BlockSpec((1, d, ff), lambda t, e, v: (e[t], 0, 0)),
                  pl.BlockSpec((1, d, ff), lambda t, e, v: (e[t], 0, 0)),
                  pl.BlockSpec((1, ff, d), lambda t, e, v: (e[t], 0, 0))],
        out_specs=pl.BlockSpec((te, d), lambda t, e, v: (t, 0)),
    )
    return pl.pallas_call(
        _experts_kernel,
        grid_spec=grid_spec,
        out_shape=jax.ShapeDtypeStruct((r, d), F32),
        compiler_params=_params(("arbitrary",)),
        name="experts",
    )(tile_expert, tile_valid, xs, wg, wu, wd)


def _combine_ple_kernel(x1_ref, y0_ref, y1_ref, rw_ref, p_ref, g_ref, b_ref, wg_ref, bg_ref, wp_ref,
                        o_ref, *, alpha):
    rw = rw_ref[...]
    f = rw[:, 0:1] * y0_ref[...] + rw[:, 1:2] * y1_ref[...]
    x2 = _layer_norm(alpha * x1_ref[...] + f, g_ref[...], b_ref[...])
    gl = _dot(x2.astype(BF16), wg_ref[...]) + bg_ref[...]
    gate = 1.0 / (1.0 + jnp.exp(-gl))
    pp = _dot(p_ref[...].astype(BF16), wp_ref[...])
    o_ref[...] = x2 + gate * pp


def combine_ple(x1, y0, y1, rw, p, g, b, wg, bg, wp, alpha, tm):
    m, d = x1.shape
    pd = p.shape[1]
    const = lambda i: (0, 0)
    row = lambda i: (i, 0)
    return pl.pallas_call(
        functools.partial(_combine_ple_kernel, alpha=alpha),
        grid=(m // tm,),
        in_specs=[pl.BlockSpec((tm, d), row), pl.BlockSpec((tm, d), row), pl.BlockSpec((tm, d), row),
                  pl.BlockSpec((tm, LANES), row), pl.BlockSpec((tm, pd), row),
                  pl.BlockSpec((1, d), const), pl.BlockSpec((1, d), const),
                  pl.BlockSpec((d, d), const), pl.BlockSpec((1, d), const), pl.BlockSpec((pd, d), const)],
        out_specs=pl.BlockSpec((tm, d), row),
        out_shape=jax.ShapeDtypeStruct((m, d), F32),
        compiler_params=_params(("parallel",)),
        name="combine_ple",
    )(x1, y0, y1, rw, p, g, b, wg, bg, wp)


def _tiles(n_tokens):
    tm = 512 if n_tokens >= 4096 else 128
    return tm, ((n_tokens + tm - 1) // tm) * tm


def _expert_tile(n_tokens):
    return 256 if n_tokens >= 4096 else 32


def _channel_and_ple(x, am, p, li, w_out, w, tm):
    depth = w["ln_mix_g"].shape[0]
    alpha = (2 * depth) ** 0.25
    ntp, d = x.shape
    wr = jnp.concatenate([w["w_router_group"][li],
                          jnp.moveaxis(w["w_router_expert"][li], 0, 1).reshape(d, N_EXPERTS)], axis=1)
    wr = jnp.pad(wr, ((0, 0), (0, LANES - wr.shape[1])))
    br = jnp.concatenate([w["b_router_group"][li], w["b_router_expert"][li].reshape(-1)])
    br = jnp.pad(br, (0, LANES - br.shape[0]))[None, :]
    x1, rw, re = mix_route(am, x, w_out.astype(BF16), w["ln_mix_g"][li][None],
                           w["ln_mix_b"][li][None], wr, br, alpha, tm)

    te = _expert_tile(ntp)
    n_flat = 2 * ntp
    flat = re[:, :2].reshape(-1)
    onehot = (flat[:, None] == jnp.arange(N_EXPERTS, dtype=jnp.int32)[None, :]).astype(jnp.int32)
    running = jnp.cumsum(onehot, axis=0)
    counts = running[-1]
    padded = ((counts + te - 1) // te) * te
    gend = jnp.cumsum(padded)
    gstart = gend - padded
    pos_flat = jnp.sum(onehot * (running - 1 + gstart[None, :]), axis=1)
    n_rows = ((n_flat + N_EXPERTS * (te - 1) + te - 1) // te) * te
    row_token = jnp.zeros((n_rows,), jnp.int32).at[pos_flat].set(
        jnp.arange(n_flat, dtype=jnp.int32) // 2, mode="promise_in_bounds", unique_indices=True)
    pos_flat = pos_flat.reshape(ntp, 2)
    tile_start = jnp.arange(n_rows // te, dtype=jnp.int32) * te
    tile_expert = jnp.minimum(jnp.sum((gend[None, :] <= tile_start[:, None]).astype(jnp.int32), axis=1),
                              N_EXPERTS - 1)
    tile_valid = (tile_start < gend[-1]).astype(jnp.int32)

    def rows(a, idx):
        return a.at[idx].get(mode="promise_in_bounds")

    xs = rows(x1, row_token)
    y = experts(tile_expert, tile_valid, xs, w["w_exp_gate"][li], w["w_exp_up"][li], w["w_exp_down"][li], te)
    y0 = rows(y, pos_flat[:, 0])
    y1 = rows(y, pos_flat[:, 1])
    return combine_ple(x1, y0, y1, rw, p, w["ln_ffn_g"][li][None], w["ln_ffn_b"][li][None],
                       w["w_ple_gate"][li].astype(BF16), w["b_ple_gate"][li][None],
                       w["w_ple_proj"][li].astype(BF16), alpha, tm)


def kernel(x_prompt, x_sample, p_prompt, p_sample, cache_fox_k, cache_fox_v, cache_fox_logf, state_ssm, state_conv, cache_win_k, cache_win_v, page_table, w_in_even, b_fgate, conv_w, conv_b, dt_bias, a_log, d_skip, ssm_norm_w, w_out_even, w_in_odd, attn_sinks, w_out_odd, ln_mix_g, ln_mix_b, ln_ffn_g, ln_ffn_b, w_router_group, b_router_group, w_router_expert, b_router_expert, w_exp_gate, w_exp_up, w_exp_down, w_ple_proj, w_ple_gate, b_ple_gate):
    bp, seq, d = x_prompt.shape
    bd, t_dec, _ = x_sample.shape
    assert t_dec == 1 and d == D_MODEL
    depth = p_prompt.shape[0]
    n_pages = page_table.shape[1]
    past_len = n_pages * PAGE_SIZE
    np_tok = bp * seq
    nt = np_tok + bd
    tm, ntp = _tiles(nt)
    pad = ntp - nt

    def tokens(a_p, a_s):
        parts = [a_p.reshape(np_tok, -1), a_s.reshape(bd, -1)]
        if pad:
            parts.append(jnp.zeros((pad, parts[0].shape[1]), parts[0].dtype))
        return jnp.concatenate(parts, axis=0)

    x = tokens(x_prompt, x_sample)
    shared = dict(ln_mix_g=ln_mix_g, ln_mix_b=ln_mix_b, ln_ffn_g=ln_ffn_g, ln_ffn_b=ln_ffn_b,
                  w_router_group=w_router_group, b_router_group=b_router_group,
                  w_router_expert=w_router_expert, b_router_expert=b_router_expert,
                  w_exp_gate=w_exp_gate, w_exp_up=w_exp_up, w_exp_down=w_exp_down,
                  w_ple_proj=w_ple_proj, w_ple_gate=w_ple_gate, b_ple_gate=b_ple_gate)

    half = HEAD_DIM // 2
    inv = jnp.exp(-math.log(ROPE_THETA) * jnp.arange(half, dtype=F32) / half)
    pos = jnp.concatenate([jnp.tile(jnp.arange(seq, dtype=jnp.int32), bp),
                           jnp.full((bd,), past_len, jnp.int32), jnp.zeros((pad,), jnp.int32)])
    ang = pos.astype(F32)[:, None] * inv[None, :]
    rope_tn = 256
    cos_t = jnp.tile(jnp.cos(ang), (1, rope_tn // half))
    sin_t = jnp.tile(jnp.concatenate([-jnp.sin(ang), jnp.sin(ang)], axis=1), (1, rope_tn // HEAD_DIM))

    even_p, even_s, odd_p, odd_s = [], [], [], []
    for li in range(depth):
        j = li // 2
        p = tokens(p_prompt[li], p_sample[li])
        if li % 2 == 0:
            wi = w_in_even[j]
            c0 = 3 * FOX_WIDTH
            c1 = c0 + FOX_HEADS
            c2 = c1 + M_INNER
            c3 = c2 + CONV_CH
            w_main = jnp.concatenate([wi[:, :c0], wi[:, c1:c3]], axis=1).astype(BF16)
            w_small_t = jnp.concatenate([wi[:, c0:c1], wi[:, c3:]], axis=1).T
            b_small = jnp.concatenate([b_fgate[j], dt_bias[j]])[:, None]
            u, qkv = matmul_dual(x, w_main, tm, 512, c0)
            small = small_proj(x, w_small_t, b_small, tm)

            tq = min(512, seq)
            logf_p = small[:FOX_HEADS, :np_tok].reshape(FOX_HEADS, bp, seq)
            cum = cumsum_lanes(jnp.moveaxis(logf_p, 1, 0).reshape(bp * FOX_HEADS, seq), min(512, seq))
            ck = cum.reshape(bp, FOX_HEADS // 2, 2, seq // tq, tq).transpose(0, 1, 3, 2, 4)
            a_p = fox_prompt(qkv, ck, bp, seq, tq)
            u_s = u[np_tok:nt]
            logf_s = small[:FOX_HEADS, np_tok:nt].T
            q_s = u_s[:, :FOX_WIDTH].reshape(bd, FOX_HEADS, HEAD_DIM)
            v_s = u_s[:, 2 * FOX_WIDTH:c0].reshape(bd, FOX_HEADS, HEAD_DIM)
            eye = jnp.eye(FOX_HEADS, dtype=F32)[None, :, :, None]

            def block_diag(a):
                return (a[:, :, None, :] * eye).reshape(bd, FOX_HEADS, FOX_WIDTH)

            a_s = fox_sample(page_table, block_diag(q_s), q_s,
                             u_s[:, FOX_WIDTH:2 * FOX_WIDTH].reshape(bd, FOX_HEADS, HEAD_DIM),
                             block_diag(v_s), logf_s[:, :, None],
                             jnp.transpose(cache_fox_k, (0, 1, 3, 4, 2)), jnp.transpose(cache_fox_v, (0, 1, 3, 4, 2)),
                             jnp.transpose(cache_fox_logf, (0, 1, 3, 2)), j, min(16, n_pages))

            nega = -jnp.exp(a_log[j])
            dskip_e = jnp.repeat(d_skip[j], M_HEADDIM)[None, :]
            normw = ssm_norm_w[j][None, :]
            m_p, st_p = ssd_prompt(u, small, conv_w[j], conv_b[j][None, :], nega[:, None], dskip_e, normw, bp, seq)
            zc = c0 + M_INNER
            m_s, st_s = ssd_sample(x[np_tok:nt], jnp.repeat(wi[:, c3:], M_HEADDIM, axis=1),
                                   jnp.repeat(dt_bias[j], M_HEADDIM)[None, :], jnp.repeat(nega, M_HEADDIM)[None, :],
                                   u_s[:, zc:], jnp.moveaxis(state_conv[j], 1, 0), u_s[:, c0:zc], conv_w[j], conv_b[j][None, :],
                                   dskip_e, normw, state_ssm[j], min(8, bd))
            am = jnp.concatenate([tokens(a_p, a_s.reshape(bd, FOX_WIDTH)), tokens(m_p, m_s)], axis=1)
            w_out = w_out_even[j]

            kp = u[:np_tok, FOX_WIDTH:2 * FOX_WIDTH].reshape(bp, seq, FOX_HEADS, HEAD_DIM)
            vp = u[:np_tok, 2 * FOX_WIDTH:c0].reshape(bp, seq, FOX_HEADS, HEAD_DIM)
            xbc_p = u[:np_tok, zc:].reshape(bp, seq, CONV_CH)
            conv_p = jnp.concatenate([jnp.zeros((bp, CONV_W - 1, CONV_CH), F32), xbc_p], axis=1)[:, -(CONV_W - 1):]
            even_p.append((kp, vp, jnp.moveaxis(logf_p, 0, 2),
                           st_p.reshape(bp, M_HEADS, M_HEADDIM, D_STATE), conv_p))
            conv_s = jnp.concatenate([state_conv[j], u_s[:, None, zc:]], axis=1)[:, -(CONV_W - 1):]
            even_s.append((u_s[:, FOX_WIDTH:2 * FOX_WIDTH].reshape(bd, 1, FOX_HEADS, HEAD_DIM),
                           u_s[:, 2 * FOX_WIDTH:c0].reshape(bd, 1, FOX_HEADS, HEAD_DIM),
                           logf_s[:, None, :], st_s, conv_s))
        else:
            u = matmul_rope(x, w_in_odd[j].astype(BF16), cos_t, sin_t, tm, rope_tn, ODD_MIX + KV_W)
            o_p = swa_prompt(attn_sinks[j], u, bp, seq)
            u_s = u[np_tok:nt]
            lw = cache_win_k.shape[2]
            kn = u_s[:, ODD_MIX:ODD_MIX + KV_W]
            vn = u_s[:, ODD_MIX + KV_W:]
            o_s = swa_sample(attn_sinks[j][:, None], u_s[:, :ODD_MIX].reshape(bd, SW_HEADS, HEAD_DIM), kn, vn,
                             jnp.transpose(cache_win_k[j], (0, 2, 3, 1)), jnp.transpose(cache_win_v[j], (0, 2, 3, 1)),
                             min(8, bd))
            am = tokens(o_p, o_s.reshape(bd, ODD_MIX))
            w_out = w_out_odd[j]

            rows = min(WINDOW, seq)
            kp = u[:np_tok, ODD_MIX:ODD_MIX + KV_W].reshape(bp, seq, SW_KV_HEADS, HEAD_DIM)[:, -rows:]
            vp = u[:np_tok, ODD_MIX + KV_W:].reshape(bp, seq, SW_KV_HEADS, HEAD_DIM)[:, -rows:]
            odd_p.append((kp, vp))
            ka = jnp.concatenate([cache_win_k[j], kn.reshape(bd, 1, SW_KV_HEADS, HEAD_DIM)], axis=1)[:, -lw:]
            va = jnp.concatenate([cache_win_v[j], vn.reshape(bd, 1, SW_KV_HEADS, HEAD_DIM)], axis=1)[:, -lw:]
            odd_s.append((ka, va))
        x = _channel_and_ple(x, am, p, li, w_out, shared, tm)

    yp = x[:np_tok].reshape(bp, seq, d)
    ys = x[np_tok:nt].reshape(bd, 1, d)
    return (yp, ys,
            jnp.stack([st[0] for st in even_p]), jnp.stack([st[1] for st in even_p]),
            jnp.stack([st[2] for st in even_p]), jnp.stack([st[3] for st in even_p]),
            jnp.stack([st[4] for st in even_p]),
            jnp.stack([st[0] for st in odd_p]), jnp.stack([st[1] for st in odd_p]),
            jnp.stack([st[0] for st in even_s]), jnp.stack([st[1] for st in even_s]),
            jnp.stack([st[2] for st in even_s]), jnp.stack([st[3] for st in even_s]),
            jnp.stack([st[4] for st in even_s]),
            jnp.stack([st[0] for st in odd_s]), jnp.stack([st[1] for st in odd_s]))
```

```python
import functools
import math

import jax
import jax.numpy as jnp
from jax import lax
from jax.experimental import pallas as pl
from jax.experimental.pallas import tpu as pltpu

F32 = jnp.float32
BF16 = jnp.bfloat16
HIGHEST = lax.Precision.HIGHEST

D_MODEL = 1024
HEAD_DIM = 64
FOX_HEADS = 8
FOX_WIDTH = FOX_HEADS * HEAD_DIM
M_HEADS = 8
M_HEADDIM = 64
M_INNER = M_HEADS * M_HEADDIM
M_GROUPS = 2
HPG = M_HEADS // M_GROUPS
D_STATE = 128
CONV_W = 4
BC_W = M_GROUPS * D_STATE
CONV_CH = M_INNER + 2 * BC_W
SSD_CHUNK = 128
RMS_EPS = 1e-5
SW_HEADS = 16
SW_KV_HEADS = 4
SW_GROUP = SW_HEADS // SW_KV_HEADS
WINDOW = 128
ROPE_THETA = 10000.0
ODD_MIX = SW_HEADS * HEAD_DIM
KV_W = SW_KV_HEADS * HEAD_DIM
N_EXPERT_GROUPS = 4
EXPERTS_PER_GROUP = 8
N_EXPERTS = N_EXPERT_GROUPS * EXPERTS_PER_GROUP
EXPERT_FF = 512
PLE_DIM = 256
LN_EPS = 1e-5
PAGE_SIZE = 128

LANES = 128
SUBLANES = 8
VMEM_LIMIT = 48 * 1024 * 1024

NEG_INF = float("-inf")


def _params(sem, vmem=VMEM_LIMIT):
    return pltpu.CompilerParams(dimension_semantics=sem, vmem_limit_bytes=vmem)


def _nt_dot(a, b, precision=None):
    return lax.dot_general(a, b, (((1,), (1,)), ((), ())), precision=precision,
                           preferred_element_type=F32)


def _dot(a, b, precision=None):
    return jnp.dot(a, b, precision=precision, preferred_element_type=F32)


def _silu(x):
    return x * (1.0 / (1.0 + jnp.exp(-x)))


def _softplus(x):
    return jnp.maximum(x, 0.0) + jnp.log(1.0 + jnp.exp(-jnp.abs(x)))


def _iota(shape, dim):
    return lax.broadcasted_iota(jnp.int32, shape, dim)


def _mm_rope_kernel(x_ref, w_ref, cos_ref, sin_ref, o_ref, *, rope_tiles):
    acc = _dot(x_ref[...].astype(BF16), w_ref[...])
    j = pl.program_id(1)

    @pl.when(j < rope_tiles)
    def _():
        tn = acc.shape[1]
        half = HEAD_DIM // 2
        first = (_iota(acc.shape, 1) % HEAD_DIM) < half
        partner = jnp.where(first, pltpu.roll(acc, tn - half, 1), pltpu.roll(acc, half, 1))
        o_ref[...] = acc * cos_ref[...] + partner * sin_ref[...]

    @pl.when(j >= rope_tiles)
    def _():
        o_ref[...] = acc


def _mm_dual_kernel(x_ref, w_ref, o_ref, ob_ref, *, bf16_tiles):
    acc = _dot(x_ref[...].astype(BF16), w_ref[...])
    o_ref[...] = acc

    @pl.when(pl.program_id(1) < bf16_tiles)
    def _():
        ob_ref[...] = acc.astype(BF16)


def matmul_dual(x, w, tm, tn, bf16_cols):
    m, k = x.shape
    n = w.shape[1]
    nb = bf16_cols // tn
    return pl.pallas_call(
        functools.partial(_mm_dual_kernel, bf16_tiles=nb),
        grid=(m // tm, n // tn),
        in_specs=[pl.BlockSpec((tm, k), lambda i, j: (i, 0)),
                  pl.BlockSpec((k, tn), lambda i, j: (0, j))],
        out_specs=[pl.BlockSpec((tm, tn), lambda i, j: (i, j)),
                   pl.BlockSpec((tm, tn), lambda i, j: (i, jnp.minimum(j, nb - 1)))],
        out_shape=[jax.ShapeDtypeStruct((m, n), F32), jax.ShapeDtypeStruct((m, bf16_cols), BF16)],
        compiler_params=_params(("parallel", "arbitrary")),
        name="matmul_dual",
    )(x, w)


def matmul_rope(x, w, cos, sin, tm, tn, rope_cols):
    m, k = x.shape
    n = w.shape[1]
    return pl.pallas_call(
        functools.partial(_mm_rope_kernel, rope_tiles=rope_cols // tn),
        grid=(m // tm, n // tn),
        in_specs=[pl.BlockSpec((tm, k), lambda i, j: (i, 0)),
                  pl.BlockSpec((k, tn), lambda i, j: (0, j)),
                  pl.BlockSpec((tm, tn), lambda i, j: (i, 0)),
                  pl.BlockSpec((tm, tn), lambda i, j: (i, 0))],
        out_specs=pl.BlockSpec((tm, tn), lambda i, j: (i, j)),
        out_shape=jax.ShapeDtypeStruct((m, n), F32),
        compiler_params=_params(("parallel", "arbitrary")),
        name="matmul_rope",
    )(x, w, cos, sin)


def _small_proj_kernel(x_ref, wt_ref, b_ref, o_ref):
    r = _nt_dot(wt_ref[...], x_ref[...], precision=HIGHEST) + b_ref[...]
    row = _iota(r.shape, 0)
    o_ref[...] = jnp.where(row < FOX_HEADS, -_softplus(-r), _softplus(r))


def small_proj(x, wt, b, tm):
    m, k = x.shape
    return pl.pallas_call(
        _small_proj_kernel,
        grid=(m // tm,),
        in_specs=[pl.BlockSpec((tm, k), lambda i: (i, 0)),
                  pl.BlockSpec((16, k), lambda i: (0, 0)),
                  pl.BlockSpec((16, 1), lambda i: (0, 0))],
        out_specs=pl.BlockSpec((16, tm), lambda i: (0, i)),
        out_shape=jax.ShapeDtypeStruct((16, m), F32),
        compiler_params=_params(("parallel",)),
        name="small_proj",
    )(x, wt, b)


def _cumsum_kernel(x_ref, o_ref, carry_ref):
    @pl.when(pl.program_id(0) == 0)
    def _():
        carry_ref[...] = jnp.zeros_like(carry_ref)

    x = x_ref[...]
    w = x.shape[1]
    tri = (_iota((w, w), 0) <= _iota((w, w), 1)).astype(F32)
    c = _dot(x, tri, precision=HIGHEST) + carry_ref[...]
    o_ref[...] = c
    carry_ref[...] = c[:, w - 1:w]


def cumsum_lanes(x, chunk):
    r, l = x.shape
    return pl.pallas_call(
        _cumsum_kernel,
        grid=(l // chunk,),
        in_specs=[pl.BlockSpec((r, chunk), lambda i: (0, i))],
        out_specs=pl.BlockSpec((r, chunk), lambda i: (0, i)),
        out_shape=jax.ShapeDtypeStruct((r, l), F32),
        scratch_shapes=[pltpu.VMEM((r, 1), F32)],
        compiler_params=_params(("arbitrary",)),
        name="cumsum_lanes",
    )(x)


def _fox_prompt_kernel(q_ref, k_ref, v_ref, ck_ref, rest_ref, o_ref, *, tq):
    del rest_ref
    qi = pl.program_id(2)
    q = q_ref[...] * (HEAD_DIM ** -0.5)
    lane = _iota(q.shape, 1)
    zero = jnp.zeros_like(q)
    q_heads = (jnp.where(lane < HEAD_DIM, q, zero), jnp.where(lane >= HEAD_DIM, q, zero))

    def step(j, carry, masked):
        start = pl.multiple_of(j * tq, tq)
        kb = k_ref[pl.ds(start, tq), :]
        vb = v_ref[pl.ds(start, tq), :]
        ck = ck_ref[0, 0, j]
        out = []
        for h in range(2):
            m, l, acc = carry[h]
            s = _nt_dot(q_heads[h], kb) - ck[h:h + 1, :]
            if masked:
                s = jnp.where(_iota(s.shape, 1) <= _iota(s.shape, 0), s, NEG_INF)
            m_new = jnp.maximum(m, jnp.max(s, axis=1, keepdims=True))
            alpha = jnp.exp(m - m_new)
            p = jnp.exp(s - m_new)
            l = alpha * l + jnp.sum(p, axis=1, keepdims=True)
            acc = alpha * acc + _dot(p.astype(BF16), vb)
            out.append((m_new, l, acc))
        return tuple(out)

    init1 = (jnp.full((tq, 1), NEG_INF, F32), jnp.zeros((tq, 1), F32), jnp.zeros((tq, LANES), F32))
    carry = lax.fori_loop(0, qi, lambda j, c: step(j, c, False), (init1, init1))
    (_, l0, a0), (_, l1, a1) = step(qi, carry, True)
    o_ref[...] = jnp.where(lane < HEAD_DIM, a0 / l0, a1 / l1)


def fox_prompt(qkv, ck, rest, n_batch, seq, tq):
    nq = seq // tq
    pairs = FOX_HEADS // 2
    return pl.pallas_call(
        functools.partial(_fox_prompt_kernel, tq=tq),
        grid=(n_batch, pairs, nq),
        in_specs=[pl.BlockSpec((tq, LANES), lambda b, h, i: (b * nq + i, h)),
                  pl.BlockSpec((seq, LANES), lambda b, h, i: (b, pairs + h)),
                  pl.BlockSpec((seq, LANES), lambda b, h, i: (b, 2 * pairs + h)),
                  pl.BlockSpec((1, 1, nq, 2, tq), lambda b, h, i: (b, h, 0, 0, 0)),
                  pl.BlockSpec(memory_space=pl.ANY)],
        out_specs=pl.BlockSpec((tq, LANES), lambda b, h, i: (b * nq + i, h)),
        out_shape=jax.ShapeDtypeStruct(rest.shape, F32),
        input_output_aliases={4: 0},
        compiler_params=_params(("parallel", "parallel", "arbitrary")),
        name="fox_prompt",
    )(qkv, qkv, qkv, ck, rest)


def _block_diag_rows(full):
    rowh = _iota((FOX_HEADS, HEAD_DIM), 0)
    out = jnp.zeros((FOX_HEADS, HEAD_DIM), F32)
    for h in range(FOX_HEADS):
        out = out + jnp.where(rowh == h, full[:, h * HEAD_DIM:(h + 1) * HEAD_DIM], 0.0)
    return out


def _fox_sample_kernel(pt_ref, qbd_ref, q_ref, kn_ref, vbd_ref, ln_ref, *refs, pages_per_step):
    del pt_ref
    pp = pages_per_step
    k_refs, v_refs, lf_refs = refs[:pp], refs[pp:2 * pp], refs[2 * pp:3 * pp]
    o_ref = refs[3 * pp]
    m_ref, l_ref, acc_ref, carry_ref = refs[3 * pp + 1:]
    t = pl.program_id(1)
    scale = HEAD_DIM ** -0.5

    @pl.when(t == 0)
    def _():
        m_ref[...] = jnp.sum(q_ref[0] * kn_ref[0], axis=1, keepdims=True) * scale
        l_ref[...] = jnp.ones_like(l_ref)
        acc_ref[...] = vbd_ref[0]
        carry_ref[...] = ln_ref[0]

    qb = (qbd_ref[0] * scale).astype(BF16)
    lane = _iota((FOX_HEADS, PAGE_SIZE), 1)
    width = FOX_HEADS * HEAD_DIM
    carry = carry_ref[...]
    scores = []
    for r in range(pp):
        lf = lf_refs[r][0, 0]
        x = lf
        for sh in (1, 2, 4, 8, 16, 32, 64):
            x = x + jnp.where(lane + sh < PAGE_SIZE, pltpu.roll(x, PAGE_SIZE - sh, 1), 0.0)
        kp = k_refs[r][0, 0].reshape(width, PAGE_SIZE).astype(BF16)
        scores.append(_dot(qb, kp) + ((x - lf) + carry))
        carry = carry + x[:, 0:1]
    carry_ref[...] = carry
    s = jnp.concatenate(scores, axis=1)
    m = m_ref[...]
    m_new = jnp.maximum(m, jnp.max(s, axis=1, keepdims=True))
    alpha = jnp.exp(m - m_new)
    p = jnp.exp(s - m_new)
    l_ref[...] = alpha * l_ref[...] + jnp.sum(p, axis=1, keepdims=True)
    m_ref[...] = m_new
    pb = p.astype(BF16)
    acc = alpha * acc_ref[...]
    for r in range(pp):
        vp = v_refs[r][0, 0].reshape(width, PAGE_SIZE).astype(BF16)
        acc = acc + _nt_dot(pb[:, r * PAGE_SIZE:(r + 1) * PAGE_SIZE], vp)
    acc_ref[...] = acc

    @pl.when(t == pl.num_programs(1) - 1)
    def _():
        o_ref[0] = _block_diag_rows(acc / l_ref[...])


def fox_sample(page_table, qbd, q, kn, vbd, ln, cache_kt, cache_vt, cache_lft, layer, pages_per_step):
    bd, n_pages = page_table.shape
    pp = pages_per_step
    steps = n_pages // pp
    width = FOX_HEADS * HEAD_DIM

    def page(b, t, pt, r):
        return pt[b, n_pages - 1 - (t * pp + r)]

    kv_specs = [pl.BlockSpec((1, 1, FOX_HEADS, HEAD_DIM, PAGE_SIZE),
                             functools.partial(lambda b, t, pt, r: (layer, page(b, t, pt, r), 0, 0, 0), r=r))
                for r in range(pp)]
    lf_specs = [pl.BlockSpec((1, 1, FOX_HEADS, PAGE_SIZE),
                             functools.partial(lambda b, t, pt, r: (layer, page(b, t, pt, r), 0, 0), r=r))
                for r in range(pp)]
    tok = pl.BlockSpec((1, FOX_HEADS, HEAD_DIM), lambda b, t, pt: (b, 0, 0))
    wide = pl.BlockSpec((1, FOX_HEADS, width), lambda b, t, pt: (b, 0, 0))
    grid_spec = pltpu.PrefetchScalarGridSpec(
        num_scalar_prefetch=1,
        grid=(bd, steps),
        in_specs=[wide, tok, tok, wide, pl.BlockSpec((1, FOX_HEADS, 1), lambda b, t, pt: (b, 0, 0))]
        + kv_specs + kv_specs + lf_specs,
        out_specs=tok,
        scratch_shapes=[pltpu.VMEM((FOX_HEADS, 1), F32), pltpu.VMEM((FOX_HEADS, 1), F32),
                        pltpu.VMEM((FOX_HEADS, width), F32), pltpu.VMEM((FOX_HEADS, 1), F32)],
    )
    return pl.pallas_call(
        functools.partial(_fox_sample_kernel, pages_per_step=pp),
        grid_spec=grid_spec,
        out_shape=jax.ShapeDtypeStruct((bd, FOX_HEADS, HEAD_DIM), F32),
        compiler_params=_params(("parallel", "arbitrary")),
        name="fox_sample",
    )(page_table, qbd, q, kn, vbd, ln, *([cache_kt] * pp), *([cache_vt] * pp), *([cache_lft] * pp))


def _ssd_epilogue(y, xs, z, dskip_e, normw):
    y = (y + dskip_e * xs) * _silu(z)
    half = M_INNER // M_GROUPS
    outs = []
    for g in range(M_GROUPS):
        yg = y[:, g * half:(g + 1) * half]
        ms = jnp.sum(yg * yg, axis=1, keepdims=True) * (1.0 / half)
        outs.append(yg * lax.rsqrt(ms + RMS_EPS))
    return jnp.concatenate(outs, axis=1) * normw


def _ssd_prompt_kernel(xbc_ref, z_ref, dt_ref, cw_ref, cb_ref, nega_ref, dskip_ref, normw_ref, rest_ref,
                       o_ref, st_ref, ext_ref, h_ref):
    del rest_ref
    c = pl.program_id(1)
    L = SSD_CHUNK
    pad = SUBLANES

    @pl.when(c == 0)
    def _():
        ext_ref[0:pad, :] = jnp.zeros((pad, CONV_CH), F32)
        h_ref[...] = jnp.zeros_like(h_ref)

    ext_ref[pad:pad + L, :] = xbc_ref[...]
    acc = ext_ref[pad:pad + L, :] * cw_ref[CONV_W - 1:CONV_W, :]
    for j in range(CONV_W - 1):
        off = pad - (CONV_W - 1) + j
        acc = acc + ext_ref[off:off + L, :] * cw_ref[j:j + 1, :]
    u = _silu(acc + cb_ref[...])
    ext_ref[0:pad, :] = ext_ref[L:L + pad, :]

    xs = u[:, :M_INNER]
    dt_t = dt_ref[...]
    cum_t = _dot(dt_t * nega_ref[...], (_iota((L, L), 0) <= _iota((L, L), 1)).astype(F32),
                 precision=HIGHEST)
    eye = (_iota((L, L), 0) == _iota((L, L), 1)).astype(F32)
    cols = _nt_dot(eye, jnp.concatenate([cum_t, dt_t], axis=0), precision=HIGHEST)
    cum_last = cum_t[:, L - 1:L]
    tail_t = jnp.exp(cum_last - cum_t) * dt_t
    tril = _iota((L, L), 0) >= _iota((L, L), 1)
    lane = _iota((L, LANES), 1)
    rowi = _iota((L, LANES), 0)

    y_pairs = []
    for g in range(M_GROUPS):
        bm = u[:, M_INNER + g * D_STATE:M_INNER + (g + 1) * D_STATE]
        cm = u[:, M_INNER + BC_W + g * D_STATE:M_INNER + BC_W + (g + 1) * D_STATE]
        bmb = bm.astype(BF16)
        cmb = cm.astype(BF16)
        cb = _nt_dot(cmb, bmb)
        for pr in range(HPG // 2):
            pidx = g * (HPG // 2) + pr
            xs_pair = xs[:, pidx * LANES:(pidx + 1) * LANES]
            xs_pair_b = xs_pair.astype(BF16)
            h0 = h_ref[pidx]
            ych = _nt_dot(cmb, h0.astype(BF16))
            yw = []
            for k in range(2):
                hd = 2 * pidx + k
                diff = cols[:, hd:hd + 1] - cum_t[hd:hd + 1, :]
                decay = jnp.exp(jnp.where(tril, diff, NEG_INF))
                w = cb * decay * dt_t[hd:hd + 1, :]
                yw.append(_dot(w.astype(BF16), xs_pair_b))
            e0 = jnp.exp(cols[:, 2 * pidx:2 * pidx + 1])
            e1 = jnp.exp(cols[:, 2 * pidx + 1:2 * pidx + 2])
            first = lane < M_HEADDIM
            y_pairs.append(jnp.where(first, yw[0], yw[1]) + ych * jnp.where(first, e0, e1))
            top = rowi < M_HEADDIM
            tail_m = jnp.where(top, jnp.broadcast_to(tail_t[2 * pidx:2 * pidx + 1, :], (L, L)),
                               jnp.broadcast_to(tail_t[2 * pidx + 1:2 * pidx + 2, :], (L, L)))
            dec_m = jnp.where(top, jnp.exp(cum_last[2 * pidx:2 * pidx + 1, :]),
                              jnp.exp(cum_last[2 * pidx + 1:2 * pidx + 2, :]))
            xt = xs_pair.T * tail_m
            h_ref[pidx] = h0 * dec_m + _dot(xt.astype(BF16), bmb)

    y = jnp.concatenate(y_pairs, axis=1)
    o_ref[...] = _ssd_epilogue(y, xs, z_ref[...], dskip_ref[...], normw_ref[...])
    st_ref[0] = h_ref[...]


def ssd_prompt(u_main, dt_rows, conv_w, conv_b, nega, dskip_e, normw, rest, n_batch, seq):
    L = SSD_CHUNK
    nc = seq // L
    pairs = M_HEADS // 2
    const = lambda b, c: (0, 0)
    return pl.pallas_call(
        _ssd_prompt_kernel,
        grid=(n_batch, nc),
        in_specs=[pl.BlockSpec((L, CONV_CH), lambda b, c: (b * nc + c, 2)),
                  pl.BlockSpec((L, M_INNER), lambda b, c: (b * nc + c, 3)),
                  pl.BlockSpec((M_HEADS, L), lambda b, c: (1, b * nc + c)),
                  pl.BlockSpec((CONV_W, CONV_CH), const),
                  pl.BlockSpec((1, CONV_CH), const),
                  pl.BlockSpec((M_HEADS, 1), const),
                  pl.BlockSpec((1, M_INNER), const),
                  pl.BlockSpec((1, M_INNER), const),
                  pl.BlockSpec(memory_space=pl.ANY)],
        out_specs=[pl.BlockSpec((L, M_INNER), lambda b, c: (b * nc + c, 0)),
                   pl.BlockSpec((1, pairs, LANES, D_STATE), lambda b, c: (b, 0, 0, 0))],
        out_shape=[jax.ShapeDtypeStruct(rest.shape, F32),
                   jax.ShapeDtypeStruct((n_batch, pairs, LANES, D_STATE), F32)],
        input_output_aliases={8: 0},
        scratch_shapes=[pltpu.VMEM((L + SUBLANES, CONV_CH), F32),
                        pltpu.VMEM((pairs, LANES, D_STATE), F32)],
        compiler_params=_params(("parallel", "arbitrary")),
        name="ssd_prompt",
    )(u_main, u_main, dt_rows, conv_w, conv_b, nega, dskip_e, normw, rest)


def _ssd_sample_kernel(x_ref, wdt_ref, dtb_ref, nega_ref, xbc_ref, ctx_ref, z_ref, cw_ref, cb_ref,
                       dskip_ref, normw_ref, h0_ref, o_ref, hn_ref,
                       u_ref, coef_t_ref, dec_t_ref, dec_ref, dtx_ref, *, seqs_per_step):
    sb = seqs_per_step
    i = pl.program_id(0)
    nseq = x_ref.shape[0]

    @pl.when(i == 0)
    def _():
        acc = xbc_ref[...] * cw_ref[CONV_W - 1:CONV_W, :]
        for j in range(CONV_W - 1):
            acc = acc + ctx_ref[j] * cw_ref[j:j + 1, :]
        u = _silu(acc + cb_ref[...])
        u_ref[...] = u
        dt = _softplus(_dot(x_ref[...], wdt_ref[...], precision=HIGHEST) + dtb_ref[...])
        dec = jnp.exp(dt * nega_ref[...])
        coef = dt * u[:, :M_INNER]
        dec_ref[...] = dec
        dtx_ref[...] = coef
        for blk in range(M_INNER // LANES):
            sl = slice(blk * LANES, (blk + 1) * LANES)
            coef_t_ref[sl, :] = coef[:, sl].T
            dec_t_ref[sl, :] = dec[:, sl].T

    base = pl.multiple_of(i * sb, sb)
    ub = u_ref[pl.ds(base, sb), :]
    lane_seq = _iota((M_INNER, nseq), 1)
    rows = _iota((sb, M_INNER // M_GROUPS), 0)
    ch = [jnp.zeros((sb, M_INNER // M_GROUPS), F32) for _ in range(M_GROUPS)]
    for s in range(sb):
        onehot = lane_seq == base + s
        cx = jnp.sum(jnp.where(onehot, coef_t_ref[...], 0.0), axis=1, keepdims=True)
        dc = jnp.sum(jnp.where(onehot, dec_t_ref[...], 0.0), axis=1, keepdims=True)
        for g in range(M_GROUPS):
            brow = ub[s:s + 1, M_INNER + g * D_STATE:M_INNER + (g + 1) * D_STATE]
            cblk = ub[:, M_INNER + BC_W + g * D_STATE:M_INNER + BC_W + (g + 1) * D_STATE]
            hg = h0_ref[s, g * HPG:(g + 1) * HPG].reshape(HPG * M_HEADDIM, D_STATE)
            r = _nt_dot(cblk.astype(BF16), hg.astype(BF16))
            ch[g] = ch[g] + jnp.where(rows == s, r, 0.0)
            lo = g * HPG * M_HEADDIM
            hn = hg * dc[lo:lo + HPG * M_HEADDIM] + cx[lo:lo + HPG * M_HEADDIM] * brow
            hn_ref[s, g * HPG:(g + 1) * HPG] = hn.reshape(HPG, M_HEADDIM, D_STATE)

    xs = ub[:, :M_INNER]
    dec = dec_ref[pl.ds(base, sb), :]
    coef = dtx_ref[pl.ds(base, sb), :]
    ys = []
    half = M_INNER // M_GROUPS
    for g in range(M_GROUPS):
        bm = ub[:, M_INNER + g * D_STATE:M_INNER + (g + 1) * D_STATE]
        cm = ub[:, M_INNER + BC_W + g * D_STATE:M_INNER + BC_W + (g + 1) * D_STATE]
        cb = jnp.sum(cm * bm, axis=1, keepdims=True)
        ys.append(cb * coef[:, g * half:(g + 1) * half] + ch[g] * dec[:, g * half:(g + 1) * half])
    y = jnp.concatenate(ys, axis=1)
    o_ref[...] = _ssd_epilogue(y, xs, z_ref[...], dskip_ref[...], normw_ref[...])


def ssd_sample(x_s, wdt_e, dtb_e, nega_e, xbc_s, ctx, z_s, conv_w, conv_b, dskip_e, normw, h0, seqs_per_step):
    bd = x_s.shape[0]
    sb = seqs_per_step
    const = lambda i: (0, 0)
    return pl.pallas_call(
        functools.partial(_ssd_sample_kernel, seqs_per_step=sb),
        grid=(bd // sb,),
        in_specs=[pl.BlockSpec((bd, D_MODEL), const),
                  pl.BlockSpec((D_MODEL, M_INNER), const),
                  pl.BlockSpec((1, M_INNER), const),
                  pl.BlockSpec((1, M_INNER), const),
                  pl.BlockSpec((bd, CONV_CH), const),
                  pl.BlockSpec((CONV_W - 1, bd, CONV_CH), lambda i: (0, 0, 0)),
                  pl.BlockSpec((sb, M_INNER), lambda i: (i, 0)),
                  pl.BlockSpec((CONV_W, CONV_CH), const),
                  pl.BlockSpec((1, CONV_CH), const),
                  pl.BlockSpec((1, M_INNER), const),
                  pl.BlockSpec((1, M_INNER), const),
                  pl.BlockSpec((sb, M_HEADS, M_HEADDIM, D_STATE), lambda i: (i, 0, 0, 0))],
        out_specs=[pl.BlockSpec((sb, M_INNER), lambda i: (i, 0)),
                   pl.BlockSpec((sb, M_HEADS, M_HEADDIM, D_STATE), lambda i: (i, 0, 0, 0))],
        out_shape=[jax.ShapeDtypeStruct((bd, M_INNER), F32),
                   jax.ShapeDtypeStruct((bd, M_HEADS, M_HEADDIM, D_STATE), F32)],
        scratch_shapes=[pltpu.VMEM((bd, CONV_CH), F32),
                        pltpu.VMEM((M_INNER, bd), F32),
                        pltpu.VMEM((M_INNER, bd), F32),
                        pltpu.VMEM((bd, M_INNER), F32),
                        pltpu.VMEM((bd, M_INNER), F32)],
        compiler_params=_params(("arbitrary",)),
        name="ssd_sample",
    )(x_s, wdt_e, dtb_e, nega_e, xbc_s, ctx, z_s, conv_w, conv_b, dskip_e, normw, h0)


def _to_half(x, src_half, dst_half):
    return x if src_half == dst_half else pltpu.roll(x, HEAD_DIM, 1)


def _swa_prompt_kernel(sink_ref, q_ref, kp_ref, kc_ref, vp_ref, vc_ref, rest_ref, o_ref):
    del rest_ref
    n = pl.program_id(1)
    W = WINDOW
    scale = HEAD_DIM ** -0.5
    lane = _iota((W, LANES), 1)
    rows4 = _iota((SW_GROUP * W, 2 * W), 0)
    t = rows4 % W
    col = _iota((SW_GROUP * W, 2 * W), 1)
    valid = (col >= t) & (col <= t + W) & ((n > 0) | (col >= W))
    rcol = _iota((SW_GROUP * W, 1), 0) // W
    outs = [None] * SW_HEADS
    for j in range(SW_KV_HEADS):
        ch, hf = j // 2, j % 2
        kk = jnp.concatenate([kp_ref[:, ch * LANES:(ch + 1) * LANES],
                              kc_ref[:, ch * LANES:(ch + 1) * LANES]], axis=0).astype(BF16)
        vv = jnp.concatenate([vp_ref[:, ch * LANES:(ch + 1) * LANES],
                              vc_ref[:, ch * LANES:(ch + 1) * LANES]], axis=0).astype(BF16)
        qs = []
        sink = jnp.zeros((SW_GROUP * W, 1), F32)
        for g in range(SW_GROUP):
            hq = j * SW_GROUP + g
            qc = q_ref[:, (hq // 2) * LANES:(hq // 2 + 1) * LANES] * scale
            qc = _to_half(qc, hq % 2, hf)
            keep = (lane < HEAD_DIM) if hf == 0 else (lane >= HEAD_DIM)
            qs.append(jnp.where(keep, qc, 0.0))
            sink = jnp.where(rcol == g, sink_ref[hq], sink)
        q4 = jnp.concatenate(qs, axis=0).astype(BF16)
        s = jnp.where(valid, _nt_dot(q4, kk), NEG_INF)
        m = jnp.maximum(jnp.max(s, axis=1, keepdims=True), sink)
        p = jnp.exp(s - m)
        den = jnp.sum(p, axis=1, keepdims=True) + jnp.exp(sink - m)
        o = _dot(p.astype(BF16), vv) / den
        for g in range(SW_GROUP):
            hq = j * SW_GROUP + g
            outs[hq] = _to_half(o[g * W:(g + 1) * W], hf, hq % 2)
    for c in range(SW_HEADS // 2):
        o_ref[:, c * LANES:(c + 1) * LANES] = jnp.where(lane < HEAD_DIM, outs[2 * c], outs[2 * c + 1])


def swa_prompt(sinks, u_odd, rest, n_batch, seq):
    nb = seq // WINDOW
    kcol = ODD_MIX // KV_W
    return pl.pallas_call(
        _swa_prompt_kernel,
        grid=(n_batch, nb),
        in_specs=[pl.BlockSpec(memory_space=pltpu.SMEM),
                  pl.BlockSpec((WINDOW, ODD_MIX), lambda b, n: (b * nb + n, 0)),
                  pl.BlockSpec((WINDOW, KV_W), lambda b, n: (b * nb + jnp.maximum(n - 1, 0), kcol)),
                  pl.BlockSpec((WINDOW, KV_W), lambda b, n: (b * nb + n, kcol)),
                  pl.BlockSpec((WINDOW, KV_W), lambda b, n: (b * nb + jnp.maximum(n - 1, 0), kcol + 1)),
                  pl.BlockSpec((WINDOW, KV_W), lambda b, n: (b * nb + n, kcol + 1)),
                  pl.BlockSpec(memory_space=pl.ANY)],
        out_specs=pl.BlockSpec((WINDOW, ODD_MIX), lambda b, n: (b * nb + n, 0)),
        out_shape=jax.ShapeDtypeStruct(rest.shape, F32),
        input_output_aliases={6: 0},
        compiler_params=_params(("parallel", "arbitrary")),
        name="swa_prompt",
    )(sinks, u_odd, u_odd, u_odd, u_odd, u_odd, rest)


def _swa_sample_kernel(sink_ref, q_ref, kn_ref, vn_ref, bk_ref, bv_ref, o_ref, *, seqs_per_step):
    scale = HEAD_DIM ** -0.5
    rowg = _iota((SW_HEADS, HEAD_DIM), 0) // SW_GROUP
    sink = sink_ref[...]
    for s in range(seqs_per_step):
        q16 = q_ref[s] * scale
        qbd = jnp.concatenate([jnp.where(rowg == j, q16, 0.0) for j in range(SW_KV_HEADS)], axis=1)
        kb = bk_ref[s].reshape(KV_W, -1).astype(BF16)
        vb = bv_ref[s].reshape(KV_W, -1).astype(BF16)
        sc = _dot(qbd.astype(BF16), kb)
        s_new = jnp.sum(qbd * kn_ref[s:s + 1, :], axis=1, keepdims=True)
        m = jnp.maximum(jnp.maximum(jnp.max(sc, axis=1, keepdims=True), s_new), sink)
        p = jnp.exp(sc - m)
        pn = jnp.exp(s_new - m)
        den = jnp.sum(p, axis=1, keepdims=True) + pn + jnp.exp(sink - m)
        full = (_nt_dot(p.astype(BF16), vb) + pn * vn_ref[s:s + 1, :]) / den
        o16 = jnp.zeros((SW_HEADS, HEAD_DIM), F32)
        for j in range(SW_KV_HEADS):
            o16 = o16 + jnp.where(rowg == j, full[:, j * HEAD_DIM:(j + 1) * HEAD_DIM], 0.0)
        o_ref[s] = o16


def swa_sample(sinks_col, q, kn, vn, buf_k, buf_v, seqs_per_step):
    bd, _, _, lw = buf_k.shape
    sb = seqs_per_step
    return pl.pallas_call(
        functools.partial(_swa_sample_kernel, seqs_per_step=sb),
        grid=(bd // sb,),
        in_specs=[pl.BlockSpec((SW_HEADS, 1), lambda i: (0, 0)),
                  pl.BlockSpec((sb, SW_HEADS, HEAD_DIM), lambda i: (i, 0, 0)),
                  pl.BlockSpec((sb, KV_W), lambda i: (i, 0)),
                  pl.BlockSpec((sb, KV_W), lambda i: (i, 0)),
                  pl.BlockSpec((sb, SW_KV_HEADS, HEAD_DIM, lw), lambda i: (i, 0, 0, 0)),
                  pl.BlockSpec((sb, SW_KV_HEADS, HEAD_DIM, lw), lambda i: (i, 0, 0, 0))],
        out_specs=pl.BlockSpec((sb, SW_HEADS, HEAD_DIM), lambda i: (i, 0, 0)),
        out_shape=jax.ShapeDtypeStruct((bd, SW_HEADS, HEAD_DIM), F32),
        compiler_params=_params(("parallel",)),
        name="swa_sample",
    )(sinks_col, q, kn, vn, buf_k, buf_v)


def _layer_norm(h, g, b):
    mu = jnp.mean(h, axis=1, keepdims=True)
    d = h - mu
    var = jnp.mean(d * d, axis=1, keepdims=True)
    return d * lax.rsqrt(var + LN_EPS) * g + b


def _mix_route_kernel(*refs, alpha, n_feat):
    am_refs, wo_refs = refs[:n_feat], refs[n_feat:2 * n_feat]
    x_ref, g_ref, b_ref, wr_ref, br_ref, x1_ref, rw_ref, re_ref = refs[2 * n_feat:]
    mix = _dot(am_refs[0][...].astype(BF16), wo_refs[0][...])
    for a_ref, w_ref in zip(am_refs[1:], wo_refs[1:]):
        mix = mix + _dot(a_ref[...].astype(BF16), w_ref[...])
    x1 = _layer_norm(alpha * x_ref[...] + mix, g_ref[...], b_ref[...])
    x1_ref[...] = x1
    logits = _dot(x1, wr_ref[...], precision=HIGHEST) + br_ref[...]
    lane_i = _iota(logits.shape, 1)
    lane = lane_i.astype(F32)
    big = float(LANES)
    gl = jnp.where(lane_i < N_EXPERT_GROUPS, logits, NEG_INF)
    gmax = jnp.max(gl, axis=1, keepdims=True)
    grp = jnp.min(jnp.where(gl == gmax, lane, big), axis=1, keepdims=True)
    pg = 1.0 / jnp.sum(jnp.exp(gl - gmax), axis=1, keepdims=True)
    rel = lane - (N_EXPERT_GROUPS + grp * EXPERTS_PER_GROUP)
    el = jnp.where(rel >= 0.0, jnp.where(rel < EXPERTS_PER_GROUP, logits, NEG_INF), NEG_INF)
    v1 = jnp.max(el, axis=1, keepdims=True)
    i1 = jnp.min(jnp.where(el == v1, lane, big), axis=1, keepdims=True)
    el2 = jnp.where(lane == i1, NEG_INF, el)
    v2 = jnp.max(el2, axis=1, keepdims=True)
    i2 = jnp.min(jnp.where(el2 == v2, lane, big), axis=1, keepdims=True)
    e = jnp.exp(v2 - v1)
    w1 = pg / (1.0 + e)
    w2 = w1 * e
    rw_ref[...] = jnp.where(lane_i == 0, w1, jnp.where(lane_i == 1, w2, 0.0))
    e1 = (i1 - N_EXPERT_GROUPS).astype(jnp.int32)
    e2 = (i2 - N_EXPERT_GROUPS).astype(jnp.int32)
    re_ref[...] = jnp.where(lane_i == 0, e1, jnp.where(lane_i == 1, e2, 0))


def mix_route(feats, wos, x, g, b, wr, br, alpha, tm):
    m, d = x.shape
    const = lambda i: (0, 0)
    row = lambda i: (i, 0)
    return pl.pallas_call(
        functools.partial(_mix_route_kernel, alpha=alpha, n_feat=len(feats)),
        grid=(m // tm,),
        in_specs=[pl.BlockSpec((tm, a.shape[1]), row) for a in feats]
        + [pl.BlockSpec((w.shape[0], d), const) for w in wos]
        + [pl.BlockSpec((tm, d), row), pl.BlockSpec((1, d), const), pl.BlockSpec((1, d), const),
           pl.BlockSpec((d, LANES), const), pl.BlockSpec((1, LANES), const)],
        out_specs=[pl.BlockSpec((tm, d), row),
                   pl.BlockSpec((tm, LANES), row), pl.BlockSpec((tm, LANES), row)],
        out_shape=[jax.ShapeDtypeStruct((m, d), F32),
                   jax.ShapeDtypeStruct((m, LANES), F32), jax.ShapeDtypeStruct((m, LANES), jnp.int32)],
        compiler_params=_params(("parallel",)),
        name="mix_route",
    )(*feats, *wos, x, g, b, wr, br)


def _experts_kernel(te_ref, tv_ref, x_ref, wg_ref, wu_ref, wd_ref, y_ref):
    t = pl.program_id(0)

    @pl.when(tv_ref[t] > 0)
    def _():
        x = x_ref[...].astype(BF16)
        h = _silu(_dot(x, wg_ref[0, 0].astype(BF16))) * _dot(x, wu_ref[0, 0].astype(BF16))
        y_ref[...] = _dot(h.astype(BF16), wd_ref[0, 0].astype(BF16))

    @pl.when(tv_ref[t] == 0)
    def _():
        y_ref[...] = jnp.zeros_like(y_ref)


def experts(tile_expert, tile_valid, xs, wg, wu, wd, layer, te):
    r, d = xs.shape
    ff = wg.shape[3]
    grid_spec = pltpu.PrefetchScalarGridSpec(
        num_scalar_prefetch=2,
        grid=(r // te,),
        in_specs=[pl.BlockSpec((te, d), lambda t, e, v: (t, 0)),
                  pl.BlockSpec((1, 1, d, ff), lambda t, e, v: (layer, e[t], 0, 0)),
                  pl.BlockSpec((1, 1, d, ff), lambda t, e, v: (layer, e[t], 0, 0)),
                  pl.BlockSpec((1, 1, ff, d), lambda t, e, v: (layer, e[t], 0, 0))],
        out_specs=pl.BlockSpec((te, d), lambda t, e, v: (t, 0)),
    )
    return pl.pallas_call(
        _experts_kernel,
        grid_spec=grid_spec,
        out_shape=jax.ShapeDtypeStruct((r, d), F32),
        compiler_params=_params(("arbitrary",)),
        name="experts",
    )(tile_expert, tile_valid, xs, wg, wu, wd)


def _combine_ple_kernel(x1_ref, y0_ref, y1_ref, rw_ref, pp_ref, pt_ref, g_ref, b_ref, wg_ref, bg_ref, wp_ref,
                        o_ref, *, alpha, prompt_tiles):
    rw = rw_ref[...]
    f = rw[:, 0:1] * y0_ref[...] + rw[:, 1:2] * y1_ref[...]
    x2 = _layer_norm(alpha * x1_ref[...] + f, g_ref[...], b_ref[...])
    gl = _dot(x2.astype(BF16), wg_ref[...]) + bg_ref[...]
    gate = 1.0 / (1.0 + jnp.exp(-gl))
    p = jnp.where(pl.program_id(0) < prompt_tiles, pp_ref[0], pt_ref[...])
    o_ref[...] = x2 + gate * _dot(p.astype(BF16), wp_ref[...])


def combine_ple(x1, y0, y1, rw, p_prompt, p_tail, layer, g, b, wg, bg, wp, alpha, tm):
    m, d = x1.shape
    pd = p_prompt.shape[2]
    prompt_tiles = p_prompt.shape[1] // tm
    const = lambda i: (0, 0)
    row = lambda i: (i, 0)
    return pl.pallas_call(
        functools.partial(_combine_ple_kernel, alpha=alpha, prompt_tiles=prompt_tiles),
        grid=(m // tm,),
        in_specs=[pl.BlockSpec((tm, d), row), pl.BlockSpec((tm, d), row), pl.BlockSpec((tm, d), row),
                  pl.BlockSpec((tm, LANES), row),
                  pl.BlockSpec((1, tm, pd), lambda i: (layer, jnp.minimum(i, prompt_tiles - 1), 0)),
                  pl.BlockSpec((tm, pd), lambda i: (jnp.maximum(i - prompt_tiles, 0), 0)),
                  pl.BlockSpec((1, d), const), pl.BlockSpec((1, d), const),
                  pl.BlockSpec((d, d), const), pl.BlockSpec((1, d), const), pl.BlockSpec((pd, d), const)],
        out_specs=pl.BlockSpec((tm, d), row),
        out_shape=jax.ShapeDtypeStruct((m, d), F32),
        compiler_params=_params(("parallel",)),
        name="combine_ple",
    )(x1, y0, y1, rw, p_prompt, p_tail, g, b, wg, bg, wp)


def _tiles(n_tokens):
    tm = 512 if n_tokens >= 4096 else 128
    return tm, ((n_tokens + tm - 1) // tm) * tm


def _expert_tile(n_tokens):
    return 256 if n_tokens >= 4096 else 32


def _channel_and_ple(x, feats, p_prompt, p_tail, li, w_out, w, tm):
    depth = w["ln_mix_g"].shape[0]
    alpha = (2 * depth) ** 0.25
    ntp, d = x.shape
    wr = jnp.concatenate([w["w_router_group"][li],
                          jnp.moveaxis(w["w_router_expert"][li], 0, 1).reshape(d, N_EXPERTS)], axis=1)
    wr = jnp.pad(wr, ((0, 0), (0, LANES - wr.shape[1])))
    br = jnp.concatenate([w["b_router_group"][li], w["b_router_expert"][li].reshape(-1)])
    br = jnp.pad(br, (0, LANES - br.shape[0]))[None, :]
    wo = w_out.astype(BF16)
    splits = [0]
    for a in feats:
        splits.append(splits[-1] + a.shape[1])
    x1, rw, re = mix_route(feats, [wo[lo:hi] for lo, hi in zip(splits[:-1], splits[1:])], x,
                           w["ln_mix_g"][li][None], w["ln_mix_b"][li][None], wr, br, alpha, tm)

    te = _expert_tile(ntp)
    n_flat = 2 * ntp
    flat = re[:, :2].reshape(-1)
    onehot = (flat[:, None] == jnp.arange(N_EXPERTS, dtype=jnp.int32)[None, :]).astype(jnp.int32)
    running = jnp.cumsum(onehot, axis=0)
    counts = running[-1]
    padded = ((counts + te - 1) // te) * te
    gend = jnp.cumsum(padded)
    gstart = gend - padded
    pos_flat = jnp.sum(onehot * (running - 1 + gstart[None, :]), axis=1)
    n_rows = ((n_flat + N_EXPERTS * (te - 1) + te - 1) // te) * te
    row_token = (jnp.arange(n_rows, dtype=jnp.int32) % ntp).at[pos_flat].set(
        jnp.arange(n_flat, dtype=jnp.int32) // 2, mode="promise_in_bounds", unique_indices=True)
    pos_flat = pos_flat.reshape(ntp, 2)
    tile_start = jnp.arange(n_rows // te, dtype=jnp.int32) * te
    tile_expert = jnp.minimum(jnp.sum((gend[None, :] <= tile_start[:, None]).astype(jnp.int32), axis=1),
                              N_EXPERTS - 1)
    tile_valid = (tile_start < gend[-1]).astype(jnp.int32)

    def rows(a, idx):
        return a.at[idx].get(mode="promise_in_bounds")

    xs = rows(x1, row_token)
    y = experts(tile_expert, tile_valid, xs, w["w_exp_gate"], w["w_exp_up"], w["w_exp_down"], li, te)
    y0 = rows(y, pos_flat[:, 0])
    y1 = rows(y, pos_flat[:, 1])
    return combine_ple(x1, y0, y1, rw, p_prompt, p_tail, li, w["ln_ffn_g"][li][None], w["ln_ffn_b"][li][None],
                       w["w_ple_gate"][li].astype(BF16), w["b_ple_gate"][li][None],
                       w["w_ple_proj"][li].astype(BF16), alpha, tm)


def kernel(x_prompt, x_sample, p_prompt, p_sample, cache_fox_k, cache_fox_v, cache_fox_logf, state_ssm, state_conv, cache_win_k, cache_win_v, page_table, w_in_even, b_fgate, conv_w, conv_b, dt_bias, a_log, d_skip, ssm_norm_w, w_out_even, w_in_odd, attn_sinks, w_out_odd, ln_mix_g, ln_mix_b, ln_ffn_g, ln_ffn_b, w_router_group, b_router_group, w_router_expert, b_router_expert, w_exp_gate, w_exp_up, w_exp_down, w_ple_proj, w_ple_gate, b_ple_gate):
    bp, seq, d = x_prompt.shape
    bd, t_dec, _ = x_sample.shape
    assert t_dec == 1 and d == D_MODEL
    depth = p_prompt.shape[0]
    n_pages = page_table.shape[1]
    past_len = n_pages * PAGE_SIZE
    np_tok = bp * seq
    nt = np_tok + bd
    tm, ntp = _tiles(nt)
    pad = ntp - nt

    def tokens(a_p, a_s):
        parts = [a_p.reshape(np_tok, -1), a_s.reshape(bd, -1)]
        if pad:
            parts.append(jnp.zeros((pad, parts[0].shape[1]), parts[0].dtype))
        return jnp.concatenate(parts, axis=0)

    def past_prompt(a_s):
        a_s = a_s.reshape(bd, -1)
        return jnp.zeros((ntp, a_s.shape[1]), a_s.dtype).at[np_tok:nt].set(a_s)

    assert np_tok % tm == 0 and seq >= CONV_W - 1
    x = tokens(x_prompt, x_sample)
    p_all = p_prompt.reshape(depth, np_tok, -1)
    shared = dict(ln_mix_g=ln_mix_g, ln_mix_b=ln_mix_b, ln_ffn_g=ln_ffn_g, ln_ffn_b=ln_ffn_b,
                  w_router_group=w_router_group, b_router_group=b_router_group,
                  w_router_expert=w_router_expert, b_router_expert=b_router_expert,
                  w_exp_gate=w_exp_gate, w_exp_up=w_exp_up, w_exp_down=w_exp_down,
                  w_ple_proj=w_ple_proj, w_ple_gate=w_ple_gate, b_ple_gate=b_ple_gate)

    half = HEAD_DIM // 2
    inv = jnp.exp(-math.log(ROPE_THETA) * jnp.arange(half, dtype=F32) / half)
    pos = jnp.concatenate([jnp.tile(jnp.arange(seq, dtype=jnp.int32), bp),
                           jnp.full((bd,), past_len, jnp.int32), jnp.zeros((pad,), jnp.int32)])
    ang = pos.astype(F32)[:, None] * inv[None, :]
    rope_tn = 256
    cos_t = jnp.tile(jnp.cos(ang), (1, rope_tn // half))
    sin_t = jnp.tile(jnp.concatenate([-jnp.sin(ang), jnp.sin(ang)], axis=1), (1, rope_tn // HEAD_DIM))

    even_p, even_s, odd_p, odd_s = [], [], [], []
    for li in range(depth):
        j = li // 2
        if li % 2 == 0:
            wi = w_in_even[j]
            c0 = 3 * FOX_WIDTH
            c1 = c0 + FOX_HEADS
            c2 = c1 + M_INNER
            c3 = c2 + CONV_CH
            w_main = jnp.concatenate([wi[:, :c0], wi[:, c1:c3]], axis=1).astype(BF16)
            w_small_t = jnp.concatenate([wi[:, c0:c1], wi[:, c3:]], axis=1).T
            b_small = jnp.concatenate([b_fgate[j], dt_bias[j]])[:, None]
            u, qkv = matmul_dual(x, w_main, tm, 512, c0)
            small = small_proj(x, w_small_t, b_small, tm)

            tq = min(512, seq)
            logf_p = small[:FOX_HEADS, :np_tok].reshape(FOX_HEADS, bp, seq)
            cum = cumsum_lanes(jnp.moveaxis(logf_p, 1, 0).reshape(bp * FOX_HEADS, seq), min(512, seq))
            ck = cum.reshape(bp, FOX_HEADS // 2, 2, seq // tq, tq).transpose(0, 1, 3, 2, 4)
            u_s = u[np_tok:nt]
            logf_s = small[:FOX_HEADS, np_tok:nt].T
            q_s = u_s[:, :FOX_WIDTH].reshape(bd, FOX_HEADS, HEAD_DIM)
            v_s = u_s[:, 2 * FOX_WIDTH:c0].reshape(bd, FOX_HEADS, HEAD_DIM)
            eye = jnp.eye(FOX_HEADS, dtype=F32)[None, :, :, None]

            def block_diag(a):
                return (a[:, :, None, :] * eye).reshape(bd, FOX_HEADS, FOX_WIDTH)

            a_s = fox_sample(page_table, block_diag(q_s), q_s,
                             u_s[:, FOX_WIDTH:2 * FOX_WIDTH].reshape(bd, FOX_HEADS, HEAD_DIM),
                             block_diag(v_s), logf_s[:, :, None],
                             jnp.transpose(cache_fox_k, (0, 1, 3, 4, 2)), jnp.transpose(cache_fox_v, (0, 1, 3, 4, 2)),
                             jnp.transpose(cache_fox_logf, (0, 1, 3, 2)), j, min(16, n_pages))
            a_all = fox_prompt(qkv, ck, past_prompt(a_s), bp, seq, tq)

            nega = -jnp.exp(a_log[j])
            dskip_e = jnp.repeat(d_skip[j], M_HEADDIM)[None, :]
            normw = ssm_norm_w[j][None, :]
            zc = c0 + M_INNER
            m_s, st_s = ssd_sample(x[np_tok:nt], jnp.repeat(wi[:, c3:], M_HEADDIM, axis=1),
                                   jnp.repeat(dt_bias[j], M_HEADDIM)[None, :], jnp.repeat(nega, M_HEADDIM)[None, :],
                                   u_s[:, zc:], jnp.moveaxis(state_conv[j], 1, 0), u_s[:, c0:zc], conv_w[j], conv_b[j][None, :],
                                   dskip_e, normw, state_ssm[j], min(8, bd))
            m_all, st_p = ssd_prompt(u, small, conv_w[j], conv_b[j][None, :], nega[:, None], dskip_e, normw,
                                     past_prompt(m_s), bp, seq)
            feats = [a_all, m_all]
            w_out = w_out_even[j]

            kp = u[:np_tok, FOX_WIDTH:2 * FOX_WIDTH].reshape(bp, seq, FOX_HEADS, HEAD_DIM)
            vp = u[:np_tok, 2 * FOX_WIDTH:c0].reshape(bp, seq, FOX_HEADS, HEAD_DIM)
            last = u[:np_tok].reshape(bp, seq, -1)[:, -(CONV_W - 1):, zc:]
            conv_p = jnp.concatenate([jnp.zeros((bp, CONV_W - 1, CONV_CH), F32), last], axis=1)[:, -(CONV_W - 1):]
            even_p.append((kp, vp, jnp.moveaxis(logf_p, 0, 2),
                           st_p.reshape(bp, M_HEADS, M_HEADDIM, D_STATE), conv_p))
            conv_s = jnp.concatenate([state_conv[j], u_s[:, None, zc:]], axis=1)[:, -(CONV_W - 1):]
            even_s.append((u_s[:, FOX_WIDTH:2 * FOX_WIDTH].reshape(bd, 1, FOX_HEADS, HEAD_DIM),
                           u_s[:, 2 * FOX_WIDTH:c0].reshape(bd, 1, FOX_HEADS, HEAD_DIM),
                           logf_s[:, None, :], st_s, conv_s))
        else:
            u = matmul_rope(x, w_in_odd[j].astype(BF16), cos_t, sin_t, tm, rope_tn, ODD_MIX + KV_W)
            u_s = u[np_tok:nt]
            lw = cache_win_k.shape[2]
            kn = u_s[:, ODD_MIX:ODD_MIX + KV_W]
            vn = u_s[:, ODD_MIX + KV_W:]
            o_s = swa_sample(attn_sinks[j][:, None], u_s[:, :ODD_MIX].reshape(bd, SW_HEADS, HEAD_DIM), kn, vn,
                             jnp.transpose(cache_win_k[j], (0, 2, 3, 1)), jnp.transpose(cache_win_v[j], (0, 2, 3, 1)),
                             min(8, bd))
            feats = [swa_prompt(attn_sinks[j], u, past_prompt(o_s), bp, seq)]
            w_out = w_out_odd[j]

            rows = min(WINDOW, seq)
            kp = u[:np_tok, ODD_MIX:ODD_MIX + KV_W].reshape(bp, seq, SW_KV_HEADS, HEAD_DIM)[:, -rows:]
            vp = u[:np_tok, ODD_MIX + KV_W:].reshape(bp, seq, SW_KV_HEADS, HEAD_DIM)[:, -rows:]
            odd_p.append((kp, vp))
            ka = jnp.concatenate([cache_win_k[j], kn.reshape(bd, 1, SW_KV_HEADS, HEAD_DIM)], axis=1)[:, -lw:]
            va = jnp.concatenate([cache_win_v[j], vn.reshape(bd, 1, SW_KV_HEADS, HEAD_DIM)], axis=1)[:, -lw:]
            odd_s.append((ka, va))
        x = _channel_and_ple(x, feats, p_all, past_prompt(p_sample[li])[np_tok:], li, w_out, shared, tm)

    yp = x[:np_tok].reshape(bp, seq, d)
    ys = x[np_tok:nt].reshape(bd, 1, d)
    return (yp, ys,
            jnp.stack([st[0] for st in even_p]), jnp.stack([st[1] for st in even_p]),
            jnp.stack([st[2] for st in even_p]), jnp.stack([st[3] for st in even_p]),
            jnp.stack([st[4] for st in even_p]),
            jnp.stack([st[0] for st in odd_p]), jnp.stack([st[1] for st in odd_p]),
            jnp.stack([st[0] for st in even_s]), jnp.stack([st[1] for st in even_s]),
            jnp.stack([st[2] for st in even_s]), jnp.stack([st[3] for st in even_s]),
            jnp.stack([st[4] for st in even_s]),
            jnp.stack([st[0] for st in odd_s]), jnp.stack([st[1] for st in odd_s]))
```

```python
import functools
import math

import jax
import jax.numpy as jnp
from jax import lax
from jax.experimental import pallas as pl
from jax.experimental.pallas import tpu as pltpu

F32 = jnp.float32
BF16 = jnp.bfloat16
HIGHEST = lax.Precision.HIGHEST

D_MODEL = 1024
HEAD_DIM = 64
FOX_HEADS = 8
FOX_WIDTH = FOX_HEADS * HEAD_DIM
M_HEADS = 8
M_HEADDIM = 64
M_INNER = M_HEADS * M_HEADDIM
M_GROUPS = 2
HPG = M_HEADS // M_GROUPS
D_STATE = 128
CONV_W = 4
BC_W = M_GROUPS * D_STATE
CONV_CH = M_INNER + 2 * BC_W
SSD_CHUNK = 128
RMS_EPS = 1e-5
SW_HEADS = 16
SW_KV_HEADS = 4
SW_GROUP = SW_HEADS // SW_KV_HEADS
WINDOW = 128
ROPE_THETA = 10000.0
ODD_MIX = SW_HEADS * HEAD_DIM
KV_W = SW_KV_HEADS * HEAD_DIM
N_EXPERT_GROUPS = 4
EXPERTS_PER_GROUP = 8
N_EXPERTS = N_EXPERT_GROUPS * EXPERTS_PER_GROUP
EXPERT_FF = 512
PLE_DIM = 256
LN_EPS = 1e-5
PAGE_SIZE = 128

LANES = 128
SUBLANES = 8
VMEM_LIMIT = 48 * 1024 * 1024

NEG_INF = float("-inf")
FOX_Q_STRIP = 512


def _params(sem, vmem=VMEM_LIMIT):
    return pltpu.CompilerParams(dimension_semantics=sem, vmem_limit_bytes=vmem)


def _nt_dot(a, b, precision=None):
    return lax.dot_general(a, b, (((1,), (1,)), ((), ())), precision=precision,
                           preferred_element_type=F32)


def _dot(a, b, precision=None):
    return jnp.dot(a, b, precision=precision, preferred_element_type=F32)


def _silu(x):
    return x * (1.0 / (1.0 + jnp.exp(-x)))


def _softplus(x):
    return jnp.maximum(x, 0.0) + jnp.log(1.0 + jnp.exp(-jnp.abs(x)))


def _iota(shape, dim):
    return lax.broadcasted_iota(jnp.int32, shape, dim)


def _mm_rope_kernel(x_ref, w_ref, cos_ref, sin_ref, o_ref, *, tn, rope_cols):
    xb = x_ref[...].astype(BF16)
    reps = tn // LANES
    cos = jnp.concatenate([cos_ref[...]] * reps, axis=1)
    sin = jnp.concatenate([sin_ref[...]] * reps, axis=1)
    half = HEAD_DIM // 2
    first = (_iota((xb.shape[0], tn), 1) % HEAD_DIM) < half
    for c in range(w_ref.shape[1] // tn):
        acc = _dot(xb, w_ref[:, c * tn:(c + 1) * tn])
        if c * tn < rope_cols:
            partner = jnp.where(first, pltpu.roll(acc, tn - half, 1), pltpu.roll(acc, half, 1))
            acc = acc * cos + partner * sin
        o_ref[:, c * tn:(c + 1) * tn] = acc


def _mm_dual_kernel(x_ref, w_ref, o_ref, ob_ref, *, tn):
    xb = x_ref[...].astype(BF16)
    for c in range(w_ref.shape[1] // tn):
        acc = _dot(xb, w_ref[:, c * tn:(c + 1) * tn])
        o_ref[:, c * tn:(c + 1) * tn] = acc
        if (c + 1) * tn <= ob_ref.shape[1]:
            ob_ref[:, c * tn:(c + 1) * tn] = acc.astype(BF16)


def matmul_dual(x, w, tm, tn, bf16_cols):
    m, k = x.shape
    n = w.shape[1]
    assert n % tn == 0 and bf16_cols % tn == 0
    return pl.pallas_call(
        functools.partial(_mm_dual_kernel, tn=tn),
        grid=(m // tm,),
        in_specs=[pl.BlockSpec((tm, k), lambda i: (i, 0)),
                  pl.BlockSpec((k, n), lambda i: (0, 0))],
        out_specs=[pl.BlockSpec((tm, n), lambda i: (i, 0)),
                   pl.BlockSpec((tm, bf16_cols), lambda i: (i, 0))],
        out_shape=[jax.ShapeDtypeStruct((m, n), F32), jax.ShapeDtypeStruct((m, bf16_cols), BF16)],
        compiler_params=_params(("parallel",)),
        name="matmul_dual",
    )(x, w)


def matmul_rope(x, w, cos, sin, tm, tn, rope_cols):
    m, k = x.shape
    n = w.shape[1]
    assert n % tn == 0 and rope_cols % tn == 0
    return pl.pallas_call(
        functools.partial(_mm_rope_kernel, tn=tn, rope_cols=rope_cols),
        grid=(m // tm,),
        in_specs=[pl.BlockSpec((tm, k), lambda i: (i, 0)),
                  pl.BlockSpec((k, n), lambda i: (0, 0)),
                  pl.BlockSpec((tm, LANES), lambda i: (i, 0)),
                  pl.BlockSpec((tm, LANES), lambda i: (i, 0))],
        out_specs=pl.BlockSpec((tm, n), lambda i: (i, 0)),
        out_shape=jax.ShapeDtypeStruct((m, n), F32),
        compiler_params=_params(("parallel",)),
        name="matmul_rope",
    )(x, w, cos, sin)


def _small_proj_kernel(x_ref, wt_ref, b_ref, o_ref):
    r = _nt_dot(wt_ref[...], x_ref[...], precision=HIGHEST) + b_ref[...]
    row = _iota(r.shape, 0)
    o_ref[...] = jnp.where(row < FOX_HEADS, -_softplus(-r), _softplus(r))


def small_proj(x, wt, b, tm):
    m, k = x.shape
    return pl.pallas_call(
        _small_proj_kernel,
        grid=(m // tm,),
        in_specs=[pl.BlockSpec((tm, k), lambda i: (i, 0)),
                  pl.BlockSpec((16, k), lambda i: (0, 0)),
                  pl.BlockSpec((16, 1), lambda i: (0, 0))],
        out_specs=pl.BlockSpec((16, tm), lambda i: (0, i)),
        out_shape=jax.ShapeDtypeStruct((16, m), F32),
        compiler_params=_params(("parallel",)),
        name="small_proj",
    )(x, wt, b)


def _cumsum_kernel(x_ref, o_ref, carry_ref):
    @pl.when(pl.program_id(0) == 0)
    def _():
        carry_ref[...] = jnp.zeros_like(carry_ref)

    x = x_ref[...]
    w = x.shape[1]
    tri = (_iota((w, w), 0) <= _iota((w, w), 1)).astype(F32)
    c = _dot(x, tri, precision=HIGHEST) + carry_ref[...]
    o_ref[...] = c
    carry_ref[...] = c[:, w - 1:w]


def cumsum_lanes(x, chunk):
    r, l = x.shape
    return pl.pallas_call(
        _cumsum_kernel,
        grid=(l // chunk,),
        in_specs=[pl.BlockSpec((r, chunk), lambda i: (0, i))],
        out_specs=pl.BlockSpec((r, chunk), lambda i: (0, i)),
        out_shape=jax.ShapeDtypeStruct((r, l), F32),
        scratch_shapes=[pltpu.VMEM((r, 1), F32)],
        compiler_params=_params(("arbitrary",)),
        name="cumsum_lanes",
    )(x)


def _fox_prompt_kernel(q_ref, k_ref, v_ref, ck_ref, rest_ref, o_ref, *, tq):
    del rest_ref
    qi = pl.program_id(2)
    strip = min(FOX_Q_STRIP, tq)
    n_strips = tq // strip
    chains = [(h, r) for h in range(2) for r in range(n_strips)]
    lane = _iota((strip, LANES), 1)
    q_chain = []
    for h, r in chains:
        q = q_ref[r * strip:(r + 1) * strip, :] * (HEAD_DIM ** -0.5)
        keep = (lane < HEAD_DIM) if h == 0 else (lane >= HEAD_DIM)
        q_chain.append(jnp.where(keep, q, jnp.zeros_like(q)))

    def step(j, carry, masked):
        start = pl.multiple_of(j * tq, tq)
        kb = k_ref[pl.ds(start, tq), :]
        vb = v_ref[pl.ds(start, tq), :]
        ck = ck_ref[0, 0, j]
        out = []
        for (h, r), qc, (m, l, acc) in zip(chains, q_chain, carry):
            s = _nt_dot(qc, kb) - ck[h:h + 1, :]
            if masked:
                s = jnp.where(_iota(s.shape, 1) <= _iota(s.shape, 0) + r * strip, s, NEG_INF)
            m_new = jnp.maximum(m, jnp.max(s, axis=1, keepdims=True))
            alpha = jnp.exp(m - m_new)
            p = jnp.exp(s - m_new)
            l = alpha * l + jnp.sum(p, axis=1, keepdims=True)
            acc = alpha * acc + _dot(p.astype(BF16), vb)
            out.append((m_new, l, acc))
        return tuple(out)

    init1 = (jnp.full((strip, 1), NEG_INF, F32), jnp.zeros((strip, 1), F32), jnp.zeros((strip, LANES), F32))
    carry = lax.fori_loop(0, qi, lambda j, c: step(j, c, False), (init1,) * len(chains))
    final = step(qi, carry, True)
    for r in range(n_strips):
        (_, l0, a0), (_, l1, a1) = final[r], final[n_strips + r]
        o_ref[r * strip:(r + 1) * strip, :] = jnp.where(lane < HEAD_DIM, a0 / l0, a1 / l1)


def fox_prompt(qkv, ck, rest, n_batch, seq, tq):
    nq = seq // tq
    pairs = FOX_HEADS // 2
    return pl.pallas_call(
        functools.partial(_fox_prompt_kernel, tq=tq),
        grid=(n_batch, pairs, nq),
        in_specs=[pl.BlockSpec((tq, LANES), lambda b, h, i: (b * nq + i, h)),
                  pl.BlockSpec((seq, LANES), lambda b, h, i: (b, pairs + h)),
                  pl.BlockSpec((seq, LANES), lambda b, h, i: (b, 2 * pairs + h)),
                  pl.BlockSpec((1, 1, nq, 2, tq), lambda b, h, i: (b, h, 0, 0, 0)),
                  pl.BlockSpec(memory_space=pl.ANY)],
        out_specs=pl.BlockSpec((tq, LANES), lambda b, h, i: (b * nq + i, h)),
        out_shape=jax.ShapeDtypeStruct(rest.shape, F32),
        input_output_aliases={4: 0},
        compiler_params=_params(("parallel", "parallel", "arbitrary")),
        name="fox_prompt",
    )(qkv, qkv, qkv, ck, rest)


def _block_diag_rows(full):
    rowh = _iota((FOX_HEADS, HEAD_DIM), 0)
    out = jnp.zeros((FOX_HEADS, HEAD_DIM), F32)
    for h in range(FOX_HEADS):
        out = out + jnp.where(rowh == h, full[:, h * HEAD_DIM:(h + 1) * HEAD_DIM], 0.0)
    return out


def _fox_sample_kernel(pt_ref, qbd_ref, q_ref, kn_ref, vbd_ref, ln_ref, *refs, pages_per_step):
    del pt_ref
    pp = pages_per_step
    k_refs, v_refs, lf_refs = refs[:pp], refs[pp:2 * pp], refs[2 * pp:3 * pp]
    o_ref = refs[3 * pp]
    m_ref, l_ref, acc_ref, carry_ref = refs[3 * pp + 1:]
    t = pl.program_id(1)
    scale = HEAD_DIM ** -0.5

    @pl.when(t == 0)
    def _():
        m_ref[...] = jnp.sum(q_ref[0] * kn_ref[0], axis=1, keepdims=True) * scale
        l_ref[...] = jnp.ones_like(l_ref)
        acc_ref[...] = vbd_ref[0]
        carry_ref[...] = ln_ref[0]

    qb = (qbd_ref[0] * scale).astype(BF16)
    lane = _iota((FOX_HEADS, PAGE_SIZE), 1)
    width = FOX_HEADS * HEAD_DIM
    carry = carry_ref[...]
    scores = []
    for r in range(pp):
        lf = lf_refs[r][0, 0]
        x = lf
        for sh in (1, 2, 4, 8, 16, 32, 64):
            x = x + jnp.where(lane + sh < PAGE_SIZE, pltpu.roll(x, PAGE_SIZE - sh, 1), 0.0)
        kp = k_refs[r][0, 0].reshape(width, PAGE_SIZE).astype(BF16)
        scores.append(_dot(qb, kp) + ((x - lf) + carry))
        carry = carry + x[:, 0:1]
    carry_ref[...] = carry
    s = jnp.concatenate(scores, axis=1)
    m = m_ref[...]
    m_new = jnp.maximum(m, jnp.max(s, axis=1, keepdims=True))
    alpha = jnp.exp(m - m_new)
    p = jnp.exp(s - m_new)
    l_ref[...] = alpha * l_ref[...] + jnp.sum(p, axis=1, keepdims=True)
    m_ref[...] = m_new
    pb = p.astype(BF16)
    acc = alpha * acc_ref[...]
    for r in range(pp):
        vp = v_refs[r][0, 0].reshape(width, PAGE_SIZE).astype(BF16)
        acc = acc + _nt_dot(pb[:, r * PAGE_SIZE:(r + 1) * PAGE_SIZE], vp)
    acc_ref[...] = acc

    @pl.when(t == pl.num_programs(1) - 1)
    def _():
        o_ref[0] = _block_diag_rows(acc / l_ref[...])


def fox_sample(page_table, qbd, q, kn, vbd, ln, cache_kt, cache_vt, cache_lft, layer, pages_per_step):
    bd, n_pages = page_table.shape
    pp = pages_per_step
    steps = n_pages // pp
    width = FOX_HEADS * HEAD_DIM

    def page(b, t, pt, r):
        return pt[b, n_pages - 1 - (t * pp + r)]

    kv_specs = [pl.BlockSpec((1, 1, FOX_HEADS, HEAD_DIM, PAGE_SIZE),
                             functools.partial(lambda b, t, pt, r: (layer, page(b, t, pt, r), 0, 0, 0), r=r))
                for r in range(pp)]
    lf_specs = [pl.BlockSpec((1, 1, FOX_HEADS, PAGE_SIZE),
                             functools.partial(lambda b, t, pt, r: (layer, page(b, t, pt, r), 0, 0), r=r))
                for r in range(pp)]
    tok = pl.BlockSpec((1, FOX_HEADS, HEAD_DIM), lambda b, t, pt: (b, 0, 0))
    wide = pl.BlockSpec((1, FOX_HEADS, width), lambda b, t, pt: (b, 0, 0))
    grid_spec = pltpu.PrefetchScalarGridSpec(
        num_scalar_prefetch=1,
        grid=(bd, steps),
        in_specs=[wide, tok, tok, wide, pl.BlockSpec((1, FOX_HEADS, 1), lambda b, t, pt: (b, 0, 0))]
        + kv_specs + kv_specs + lf_specs,
        out_specs=tok,
        scratch_shapes=[pltpu.VMEM((FOX_HEADS, 1), F32), pltpu.VMEM((FOX_HEADS, 1), F32),
                        pltpu.VMEM((FOX_HEADS, width), F32), pltpu.VMEM((FOX_HEADS, 1), F32)],
    )
    return pl.pallas_call(
        functools.partial(_fox_sample_kernel, pages_per_step=pp),
        grid_spec=grid_spec,
        out_shape=jax.ShapeDtypeStruct((bd, FOX_HEADS, HEAD_DIM), F32),
        compiler_params=_params(("parallel", "arbitrary")),
        name="fox_sample",
    )(page_table, qbd, q, kn, vbd, ln, *([cache_kt] * pp), *([cache_vt] * pp), *([cache_lft] * pp))


def _ssd_epilogue(y, xs, z, dskip_e, normw):
    y = (y + dskip_e * xs) * _silu(z)
    half = M_INNER // M_GROUPS
    outs = []
    for g in range(M_GROUPS):
        yg = y[:, g * half:(g + 1) * half]
        ms = jnp.sum(yg * yg, axis=1, keepdims=True) * (1.0 / half)
        outs.append(yg * lax.rsqrt(ms + RMS_EPS))
    return jnp.concatenate(outs, axis=1) * normw


def _ssd_prompt_kernel(xbc_ref, z_ref, dt_ref, cw_ref, cb_ref, nega_ref, dskip_ref, normw_ref, rest_ref,
                       o_ref, st_ref, ext_ref, h_ref):
    del rest_ref
    c = pl.program_id(1)
    L = SSD_CHUNK
    pad = SUBLANES

    @pl.when(c == 0)
    def _():
        ext_ref[0:pad, :] = jnp.zeros((pad, CONV_CH), F32)
        h_ref[...] = jnp.zeros_like(h_ref)

    ext_ref[pad:pad + L, :] = xbc_ref[...]
    acc = ext_ref[pad:pad + L, :] * cw_ref[CONV_W - 1:CONV_W, :]
    for j in range(CONV_W - 1):
        off = pad - (CONV_W - 1) + j
        acc = acc + ext_ref[off:off + L, :] * cw_ref[j:j + 1, :]
    u = _silu(acc + cb_ref[...])
    ext_ref[0:pad, :] = ext_ref[L:L + pad, :]

    xs = u[:, :M_INNER]
    dt_t = dt_ref[...]
    cum_t = _dot(dt_t * nega_ref[...], (_iota((L, L), 0) <= _iota((L, L), 1)).astype(F32),
                 precision=HIGHEST)
    eye = (_iota((L, L), 0) == _iota((L, L), 1)).astype(F32)
    cols = _nt_dot(eye, jnp.concatenate([cum_t, dt_t], axis=0), precision=HIGHEST)
    cum_last = cum_t[:, L - 1:L]
    tail_t = jnp.exp(cum_last - cum_t) * dt_t
    tril = _iota((L, L), 0) >= _iota((L, L), 1)
    lane = _iota((L, LANES), 1)
    rowi = _iota((L, LANES), 0)

    y_pairs = []
    for g in range(M_GROUPS):
        bm = u[:, M_INNER + g * D_STATE:M_INNER + (g + 1) * D_STATE]
        cm = u[:, M_INNER + BC_W + g * D_STATE:M_INNER + BC_W + (g + 1) * D_STATE]
        bmb = bm.astype(BF16)
        cmb = cm.astype(BF16)
        cb = _nt_dot(cmb, bmb)
        for pr in range(HPG // 2):
            pidx = g * (HPG // 2) + pr
            xs_pair = xs[:, pidx * LANES:(pidx + 1) * LANES]
            xs_pair_b = xs_pair.astype(BF16)
            h0 = h_ref[pidx]
            ych = _nt_dot(cmb, h0.astype(BF16))
            yw = []
            for k in range(2):
                hd = 2 * pidx + k
                diff = cols[:, hd:hd + 1] - cum_t[hd:hd + 1, :]
                decay = jnp.exp(jnp.where(tril, diff, NEG_INF))
                w = cb * decay * dt_t[hd:hd + 1, :]
                yw.append(_dot(w.astype(BF16), xs_pair_b))
            e0 = jnp.exp(cols[:, 2 * pidx:2 * pidx + 1])
            e1 = jnp.exp(cols[:, 2 * pidx + 1:2 * pidx + 2])
            first = lane < M_HEADDIM
            y_pairs.append(jnp.where(first, yw[0], yw[1]) + ych * jnp.where(first, e0, e1))
            top = rowi < M_HEADDIM
            tail_m = jnp.where(top, jnp.broadcast_to(tail_t[2 * pidx:2 * pidx + 1, :], (L, L)),
                               jnp.broadcast_to(tail_t[2 * pidx + 1:2 * pidx + 2, :], (L, L)))
            dec_m = jnp.where(top, jnp.exp(cum_last[2 * pidx:2 * pidx + 1, :]),
                              jnp.exp(cum_last[2 * pidx + 1:2 * pidx + 2, :]))
            xt = xs_pair.T * tail_m
            h_ref[pidx] = h0 * dec_m + _dot(xt.astype(BF16), bmb)

    y = jnp.concatenate(y_pairs, axis=1)
    o_ref[...] = _ssd_epilogue(y, xs, z_ref[...], dskip_ref[...], normw_ref[...])
    st_ref[0] = h_ref[...]


def ssd_prompt(u_main, dt_rows, conv_w, conv_b, nega, dskip_e, normw, rest, n_batch, seq):
    L = SSD_CHUNK
    nc = seq // L
    pairs = M_HEADS // 2
    const = lambda b, c: (0, 0)
    return pl.pallas_call(
        _ssd_prompt_kernel,
        grid=(n_batch, nc),
        in_specs=[pl.BlockSpec((L, CONV_CH), lambda b, c: (b * nc + c, 2)),
                  pl.BlockSpec((L, M_INNER), lambda b, c: (b * nc + c, 3)),
                  pl.BlockSpec((M_HEADS, L), lambda b, c: (1, b * nc + c)),
                  pl.BlockSpec((CONV_W, CONV_CH), const),
                  pl.BlockSpec((1, CONV_CH), const),
                  pl.BlockSpec((M_HEADS, 1), const),
                  pl.BlockSpec((1, M_INNER), const),
                  pl.BlockSpec((1, M_INNER), const),
                  pl.BlockSpec(memory_space=pl.ANY)],
        out_specs=[pl.BlockSpec((L, M_INNER), lambda b, c: (b * nc + c, 0)),
                   pl.BlockSpec((1, pairs, LANES, D_STATE), lambda b, c: (b, 0, 0, 0))],
        out_shape=[jax.ShapeDtypeStruct(rest.shape, F32),
                   jax.ShapeDtypeStruct((n_batch, pairs, LANES, D_STATE), F32)],
        input_output_aliases={8: 0},
        scratch_shapes=[pltpu.VMEM((L + SUBLANES, CONV_CH), F32),
                        pltpu.VMEM((pairs, LANES, D_STATE), F32)],
        compiler_params=_params(("parallel", "arbitrary")),
        name="ssd_prompt",
    )(u_main, u_main, dt_rows, conv_w, conv_b, nega, dskip_e, normw, rest)


def _ssd_sample_kernel(x_ref, wdt_ref, dtb_ref, nega_ref, xbc_ref, ctx_ref, z_ref, cw_ref, cb_ref,
                       dskip_ref, normw_ref, h0_ref, o_ref, hn_ref,
                       u_ref, coef_t_ref, dec_t_ref, dec_ref, dtx_ref, *, seqs_per_step):
    sb = seqs_per_step
    i = pl.program_id(0)
    nseq = x_ref.shape[0]

    @pl.when(i == 0)
    def _():
        acc = xbc_ref[...] * cw_ref[CONV_W - 1:CONV_W, :]
        for j in range(CONV_W - 1):
            acc = acc + ctx_ref[j] * cw_ref[j:j + 1, :]
        u = _silu(acc + cb_ref[...])
        u_ref[...] = u
        dt = _softplus(_dot(x_ref[...], wdt_ref[...], precision=HIGHEST) + dtb_ref[...])
        dec = jnp.exp(dt * nega_ref[...])
        coef = dt * u[:, :M_INNER]
        dec_ref[...] = dec
        dtx_ref[...] = coef
        for blk in range(M_INNER // LANES):
            sl = slice(blk * LANES, (blk + 1) * LANES)
            coef_t_ref[sl, :] = coef[:, sl].T
            dec_t_ref[sl, :] = dec[:, sl].T

    base = pl.multiple_of(i * sb, sb)
    ub = u_ref[pl.ds(base, sb), :]
    lane_seq = _iota((M_INNER, nseq), 1)
    rows = _iota((sb, M_INNER // M_GROUPS), 0)
    ch = [jnp.zeros((sb, M_INNER // M_GROUPS), F32) for _ in range(M_GROUPS)]
    for s in range(sb):
        onehot = lane_seq == base + s
        cx = jnp.sum(jnp.where(onehot, coef_t_ref[...], 0.0), axis=1, keepdims=True)
        dc = jnp.sum(jnp.where(onehot, dec_t_ref[...], 0.0), axis=1, keepdims=True)
        for g in range(M_GROUPS):
            brow = ub[s:s + 1, M_INNER + g * D_STATE:M_INNER + (g + 1) * D_STATE]
            cblk = ub[:, M_INNER + BC_W + g * D_STATE:M_INNER + BC_W + (g + 1) * D_STATE]
            hg = h0_ref[s, g * HPG:(g + 1) * HPG].reshape(HPG * M_HEADDIM, D_STATE)
            r = _nt_dot(cblk.astype(BF16), hg.astype(BF16))
            ch[g] = ch[g] + jnp.where(rows == s, r, 0.0)
            lo = g * HPG * M_HEADDIM
            hn = hg * dc[lo:lo + HPG * M_HEADDIM] + cx[lo:lo + HPG * M_HEADDIM] * brow
            hn_ref[s, g * HPG:(g + 1) * HPG] = hn.reshape(HPG, M_HEADDIM, D_STATE)

    xs = ub[:, :M_INNER]
    dec = dec_ref[pl.ds(base, sb), :]
    coef = dtx_ref[pl.ds(base, sb), :]
    ys = []
    half = M_INNER // M_GROUPS
    for g in range(M_GROUPS):
        bm = ub[:, M_INNER + g * D_STATE:M_INNER + (g + 1) * D_STATE]
        cm = ub[:, M_INNER + BC_W + g * D_STATE:M_INNER + BC_W + (g + 1) * D_STATE]
        cb = jnp.sum(cm * bm, axis=1, keepdims=True)
        ys.append(cb * coef[:, g * half:(g + 1) * half] + ch[g] * dec[:, g * half:(g + 1) * half])
    y = jnp.concatenate(ys, axis=1)
    o_ref[...] = _ssd_epilogue(y, xs, z_ref[...], dskip_ref[...], normw_ref[...])


def ssd_sample(x_s, wdt_e, dtb_e, nega_e, xbc_s, ctx, z_s, conv_w, conv_b, dskip_e, normw, h0, seqs_per_step):
    bd = x_s.shape[0]
    sb = seqs_per_step
    const = lambda i: (0, 0)
    return pl.pallas_call(
        functools.partial(_ssd_sample_kernel, seqs_per_step=sb),
        grid=(bd // sb,),
        in_specs=[pl.BlockSpec((bd, D_MODEL), const),
                  pl.BlockSpec((D_MODEL, M_INNER), const),
                  pl.BlockSpec((1, M_INNER), const),
                  pl.BlockSpec((1, M_INNER), const),
                  pl.BlockSpec((bd, CONV_CH), const),
                  pl.BlockSpec((CONV_W - 1, bd, CONV_CH), lambda i: (0, 0, 0)),
                  pl.BlockSpec((sb, M_INNER), lambda i: (i, 0)),
                  pl.BlockSpec((CONV_W, CONV_CH), const),
                  pl.BlockSpec((1, CONV_CH), const),
                  pl.BlockSpec((1, M_INNER), const),
                  pl.BlockSpec((1, M_INNER), const),
                  pl.BlockSpec((sb, M_HEADS, M_HEADDIM, D_STATE), lambda i: (i, 0, 0, 0))],
        out_specs=[pl.BlockSpec((sb, M_INNER), lambda i: (i, 0)),
                   pl.BlockSpec((sb, M_HEADS, M_HEADDIM, D_STATE), lambda i: (i, 0, 0, 0))],
        out_shape=[jax.ShapeDtypeStruct((bd, M_INNER), F32),
                   jax.ShapeDtypeStruct((bd, M_HEADS, M_HEADDIM, D_STATE), F32)],
        scratch_shapes=[pltpu.VMEM((bd, CONV_CH), F32),
                        pltpu.VMEM((M_INNER, bd), F32),
                        pltpu.VMEM((M_INNER, bd), F32),
                        pltpu.VMEM((bd, M_INNER), F32),
                        pltpu.VMEM((bd, M_INNER), F32)],
        compiler_params=_params(("arbitrary",)),
        name="ssd_sample",
    )(x_s, wdt_e, dtb_e, nega_e, xbc_s, ctx, z_s, conv_w, conv_b, dskip_e, normw, h0)


def _to_half(x, src_half, dst_half):
    return x if src_half == dst_half else pltpu.roll(x, HEAD_DIM, 1)


def _swa_prompt_kernel(sink_ref, q_ref, kp_ref, kc_ref, vp_ref, vc_ref, rest_ref, o_ref):
    del rest_ref
    n = pl.program_id(1)
    W = WINDOW
    scale = HEAD_DIM ** -0.5
    lane = _iota((W, LANES), 1)
    rows4 = _iota((SW_GROUP * W, 2 * W), 0)
    t = rows4 % W
    col = _iota((SW_GROUP * W, 2 * W), 1)
    valid = (col >= t) & (col <= t + W) & ((n > 0) | (col >= W))
    rcol = _iota((SW_GROUP * W, 1), 0) // W
    outs = [None] * SW_HEADS
    for j in range(SW_KV_HEADS):
        ch, hf = j // 2, j % 2
        kk = jnp.concatenate([kp_ref[:, ch * LANES:(ch + 1) * LANES],
                              kc_ref[:, ch * LANES:(ch + 1) * LANES]], axis=0).astype(BF16)
        vv = jnp.concatenate([vp_ref[:, ch * LANES:(ch + 1) * LANES],
                              vc_ref[:, ch * LANES:(ch + 1) * LANES]], axis=0).astype(BF16)
        qs = []
        sink = jnp.zeros((SW_GROUP * W, 1), F32)
        for g in range(SW_GROUP):
            hq = j * SW_GROUP + g
            qc = q_ref[:, (hq // 2) * LANES:(hq // 2 + 1) * LANES] * scale
            qc = _to_half(qc, hq % 2, hf)
            keep = (lane < HEAD_DIM) if hf == 0 else (lane >= HEAD_DIM)
            qs.append(jnp.where(keep, qc, 0.0))
            sink = jnp.where(rcol == g, sink_ref[hq], sink)
        q4 = jnp.concatenate(qs, axis=0).astype(BF16)
        s = jnp.where(valid, _nt_dot(q4, kk), NEG_INF)
        m = jnp.maximum(jnp.max(s, axis=1, keepdims=True), sink)
        p = jnp.exp(s - m)
        den = jnp.sum(p, axis=1, keepdims=True) + jnp.exp(sink - m)
        o = _dot(p.astype(BF16), vv) / den
        for g in range(SW_GROUP):
            hq = j * SW_GROUP + g
            outs[hq] = _to_half(o[g * W:(g + 1) * W], hf, hq % 2)
    for c in range(SW_HEADS // 2):
        o_ref[:, c * LANES:(c + 1) * LANES] = jnp.where(lane < HEAD_DIM, outs[2 * c], outs[2 * c + 1])


def swa_prompt(sinks, u_odd, rest, n_batch, seq):
    nb = seq // WINDOW
    kcol = ODD_MIX // KV_W
    return pl.pallas_call(
        _swa_prompt_kernel,
        grid=(n_batch, nb),
        in_specs=[pl.BlockSpec(memory_space=pltpu.SMEM),
                  pl.BlockSpec((WINDOW, ODD_MIX), lambda b, n: (b * nb + n, 0)),
                  pl.BlockSpec((WINDOW, KV_W), lambda b, n: (b * nb + jnp.maximum(n - 1, 0), kcol)),
                  pl.BlockSpec((WINDOW, KV_W), lambda b, n: (b * nb + n, kcol)),
                  pl.BlockSpec((WINDOW, KV_W), lambda b, n: (b * nb + jnp.maximum(n - 1, 0), kcol + 1)),
                  pl.BlockSpec((WINDOW, KV_W), lambda b, n: (b * nb + n, kcol + 1)),
                  pl.BlockSpec(memory_space=pl.ANY)],
        out_specs=pl.BlockSpec((WINDOW, ODD_MIX), lambda b, n: (b * nb + n, 0)),
        out_shape=jax.ShapeDtypeStruct(rest.shape, F32),
        input_output_aliases={6: 0},
        compiler_params=_params(("parallel", "arbitrary")),
        name="swa_prompt",
    )(sinks, u_odd, u_odd, u_odd, u_odd, u_odd, rest)


def _swa_sample_kernel(sink_ref, q_ref, kn_ref, vn_ref, bk_ref, bv_ref, o_ref, *, seqs_per_step):
    scale = HEAD_DIM ** -0.5
    rowg = _iota((SW_HEADS, HEAD_DIM), 0) // SW_GROUP
    sink = sink_ref[...]
    for s in range(seqs_per_step):
        q16 = q_ref[s] * scale
        qbd = jnp.concatenate([jnp.where(rowg == j, q16, 0.0) for j in range(SW_KV_HEADS)], axis=1)
        kb = bk_ref[s].reshape(KV_W, -1).astype(BF16)
        vb = bv_ref[s].reshape(KV_W, -1).astype(BF16)
        sc = _dot(qbd.astype(BF16), kb)
        s_new = jnp.sum(qbd * kn_ref[s:s + 1, :], axis=1, keepdims=True)
        m = jnp.maximum(jnp.maximum(jnp.max(sc, axis=1, keepdims=True), s_new), sink)
        p = jnp.exp(sc - m)
        pn = jnp.exp(s_new - m)
        den = jnp.sum(p, axis=1, keepdims=True) + pn + jnp.exp(sink - m)
        full = (_nt_dot(p.astype(BF16), vb) + pn * vn_ref[s:s + 1, :]) / den
        o16 = jnp.zeros((SW_HEADS, HEAD_DIM), F32)
        for j in range(SW_KV_HEADS):
            o16 = o16 + jnp.where(rowg == j, full[:, j * HEAD_DIM:(j + 1) * HEAD_DIM], 0.0)
        o_ref[s] = o16


def swa_sample(sinks_col, q, kn, vn, buf_k, buf_v, seqs_per_step):
    bd, _, _, lw = buf_k.shape
    sb = seqs_per_step
    return pl.pallas_call(
        functools.partial(_swa_sample_kernel, seqs_per_step=sb),
        grid=(bd // sb,),
        in_specs=[pl.BlockSpec((SW_HEADS, 1), lambda i: (0, 0)),
                  pl.BlockSpec((sb, SW_HEADS, HEAD_DIM), lambda i: (i, 0, 0)),
                  pl.BlockSpec((sb, KV_W), lambda i: (i, 0)),
                  pl.BlockSpec((sb, KV_W), lambda i: (i, 0)),
                  pl.BlockSpec((sb, SW_KV_HEADS, HEAD_DIM, lw), lambda i: (i, 0, 0, 0)),
                  pl.BlockSpec((sb, SW_KV_HEADS, HEAD_DIM, lw), lambda i: (i, 0, 0, 0))],
        out_specs=pl.BlockSpec((sb, SW_HEADS, HEAD_DIM), lambda i: (i, 0, 0)),
        out_shape=jax.ShapeDtypeStruct((bd, SW_HEADS, HEAD_DIM), F32),
        compiler_params=_params(("parallel",)),
        name="swa_sample",
    )(sinks_col, q, kn, vn, buf_k, buf_v)


def _layer_norm(h, g, b):
    mu = jnp.mean(h, axis=1, keepdims=True)
    d = h - mu
    var = jnp.mean(d * d, axis=1, keepdims=True)
    return d * lax.rsqrt(var + LN_EPS) * g + b


def _mix_route_kernel(*refs, alpha, n_feat):
    am_refs, wo_refs = refs[:n_feat], refs[n_feat:2 * n_feat]
    x_ref, g_ref, b_ref, wr_ref, br_ref, x1_ref, rw_ref, re_ref = refs[2 * n_feat:]
    mix = _dot(am_refs[0][...].astype(BF16), wo_refs[0][...])
    for a_ref, w_ref in zip(am_refs[1:], wo_refs[1:]):
        mix = mix + _dot(a_ref[...].astype(BF16), w_ref[...])
    x1 = _layer_norm(alpha * x_ref[...] + mix, g_ref[...], b_ref[...])
    x1_ref[...] = x1
    wr = wr_ref[...]
    x_hi = x1.astype(BF16)
    w_hi = wr.astype(BF16)
    x_lo = (x1 - x_hi.astype(F32)).astype(BF16)
    w_lo = (wr - w_hi.astype(F32)).astype(BF16)
    logits = _dot(x_hi, w_hi) + _dot(x_lo, w_hi) + _dot(x_hi, w_lo) + br_ref[...]
    lane_i = _iota(logits.shape, 1)
    lane = lane_i.astype(F32)
    big = float(LANES)
    gl = jnp.where(lane_i < N_EXPERT_GROUPS, logits, NEG_INF)
    gmax = jnp.max(gl, axis=1, keepdims=True)
    grp = jnp.min(jnp.where(gl == gmax, lane, big), axis=1, keepdims=True)
    pg = 1.0 / jnp.sum(jnp.exp(gl - gmax), axis=1, keepdims=True)
    rel = lane - (N_EXPERT_GROUPS + grp * EXPERTS_PER_GROUP)
    el = jnp.where(rel >= 0.0, jnp.where(rel < EXPERTS_PER_GROUP, logits, NEG_INF), NEG_INF)
    v1 = jnp.max(el, axis=1, keepdims=True)
    i1 = jnp.min(jnp.where(el == v1, lane, big), axis=1, keepdims=True)
    el2 = jnp.where(lane == i1, NEG_INF, el)
    v2 = jnp.max(el2, axis=1, keepdims=True)
    i2 = jnp.min(jnp.where(el2 == v2, lane, big), axis=1, keepdims=True)
    e = jnp.exp(v2 - v1)
    w1 = pg / (1.0 + e)
    w2 = w1 * e
    rw_ref[...] = jnp.where(lane_i == 0, w1, jnp.where(lane_i == 1, w2, 0.0))
    e1 = (i1 - N_EXPERT_GROUPS).astype(jnp.int32)
    e2 = (i2 - N_EXPERT_GROUPS).astype(jnp.int32)
    re_ref[...] = jnp.where(lane_i == 0, e1, jnp.where(lane_i == 1, e2, 0))


def mix_route(feats, wos, x, g, b, wr, br, alpha, tm):
    m, d = x.shape
    const = lambda i: (0, 0)
    row = lambda i: (i, 0)
    return pl.pallas_call(
        functools.partial(_mix_route_kernel, alpha=alpha, n_feat=len(feats)),
        grid=(m // tm,),
        in_specs=[pl.BlockSpec((tm, a.shape[1]), row) for a in feats]
        + [pl.BlockSpec((w.shape[0], d), const) for w in wos]
        + [pl.BlockSpec((tm, d), row), pl.BlockSpec((1, d), const), pl.BlockSpec((1, d), const),
           pl.BlockSpec((d, LANES), const), pl.BlockSpec((1, LANES), const)],
        out_specs=[pl.BlockSpec((tm, d), row),
                   pl.BlockSpec((tm, LANES), row), pl.BlockSpec((tm, LANES), row)],
        out_shape=[jax.ShapeDtypeStruct((m, d), F32),
                   jax.ShapeDtypeStruct((m, LANES), F32), jax.ShapeDtypeStruct((m, LANES), jnp.int32)],
        compiler_params=_params(("parallel",)),
        name="mix_route",
    )(*feats, *wos, x, g, b, wr, br)


def _experts_kernel(te_ref, tv_ref, x_ref, wg_ref, wu_ref, wd_ref, y_ref):
    t = pl.program_id(0)

    @pl.when(tv_ref[t] > 0)
    def _():
        x = x_ref[...].astype(BF16)
        h = _silu(_dot(x, wg_ref[0, 0].astype(BF16))) * _dot(x, wu_ref[0, 0].astype(BF16))
        y_ref[...] = _dot(h.astype(BF16), wd_ref[0, 0].astype(BF16))

    @pl.when(tv_ref[t] == 0)
    def _():
        y_ref[...] = jnp.zeros_like(y_ref)


def experts(tile_expert, tile_valid, xs, wg, wu, wd, layer, te):
    r, d = xs.shape
    ff = wg.shape[3]
    grid_spec = pltpu.PrefetchScalarGridSpec(
        num_scalar_prefetch=2,
        grid=(r // te,),
        in_specs=[pl.BlockSpec((te, d), lambda t, e, v: (t, 0)),
                  pl.BlockSpec((1, 1, d, ff), lambda t, e, v: (layer, e[t], 0, 0)),
                  pl.BlockSpec((1, 1, d, ff), lambda t, e, v: (layer, e[t], 0, 0)),
                  pl.BlockSpec((1, 1, ff, d), lambda t, e, v: (layer, e[t], 0, 0))],
        out_specs=pl.BlockSpec((te, d), lambda t, e, v: (t, 0)),
    )
    return pl.pallas_call(
        _experts_kernel,
        grid_spec=grid_spec,
        out_shape=jax.ShapeDtypeStruct((r, d), F32),
        compiler_params=_params(("arbitrary",)),
        name="experts",
    )(tile_expert, tile_valid, xs, wg, wu, wd)


def _combine_ple_kernel(x1_ref, y0_ref, y1_ref, rw_ref, pp_ref, pt_ref, g_ref, b_ref, wg_ref, bg_ref, wp_ref,
                        o_ref, *, alpha, prompt_tiles):
    rw = rw_ref[...]
    f = rw[:, 0:1] * y0_ref[...] + rw[:, 1:2] * y1_ref[...]
    x2 = _layer_norm(alpha * x1_ref[...] + f, g_ref[...], b_ref[...])
    gl = _dot(x2.astype(BF16), wg_ref[...]) + bg_ref[...]
    gate = 1.0 / (1.0 + jnp.exp(-gl))
    p = jnp.where(pl.program_id(0) < prompt_tiles, pp_ref[0], pt_ref[...])
    o_ref[...] = x2 + gate * _dot(p.astype(BF16), wp_ref[...])


def combine_ple(x1, y0, y1, rw, p_prompt, p_tail, layer, g, b, wg, bg, wp, alpha, tm):
    m, d = x1.shape
    pd = p_prompt.shape[2]
    prompt_tiles = p_prompt.shape[1] // tm
    const = lambda i: (0, 0)
    row = lambda i: (i, 0)
    return pl.pallas_call(
        functools.partial(_combine_ple_kernel, alpha=alpha, prompt_tiles=prompt_tiles),
        grid=(m // tm,),
        in_specs=[pl.BlockSpec((tm, d), row), pl.BlockSpec((tm, d), row), pl.BlockSpec((tm, d), row),
                  pl.BlockSpec((tm, LANES), row),
                  pl.BlockSpec((1, tm, pd), lambda i: (layer, jnp.minimum(i, prompt_tiles - 1), 0)),
                  pl.BlockSpec((tm, pd), lambda i: (jnp.maximum(i - prompt_tiles, 0), 0)),
                  pl.BlockSpec((1, d), const), pl.BlockSpec((1, d), const),
                  pl.BlockSpec((d, d), const), pl.BlockSpec((1, d), const), pl.BlockSpec((pd, d), const)],
        out_specs=pl.BlockSpec((tm, d), row),
        out_shape=jax.ShapeDtypeStruct((m, d), F32),
        compiler_params=_params(("parallel",)),
        name="combine_ple",
    )(x1, y0, y1, rw, p_prompt, p_tail, g, b, wg, bg, wp)


def _tiles(n_tokens):
    tm = 512 if n_tokens >= 4096 else 128
    return tm, ((n_tokens + tm - 1) // tm) * tm


def _expert_tile(n_tokens):
    return 256 if n_tokens >= 4096 else 32


def _channel_and_ple(x, feats, p_prompt, p_tail, li, w_out, w, tm):
    depth = w["ln_mix_g"].shape[0]
    alpha = (2 * depth) ** 0.25
    ntp, d = x.shape
    wr = jnp.concatenate([w["w_router_group"][li],
                          jnp.moveaxis(w["w_router_expert"][li], 0, 1).reshape(d, N_EXPERTS)], axis=1)
    wr = jnp.pad(wr, ((0, 0), (0, LANES - wr.shape[1])))
    br = jnp.concatenate([w["b_router_group"][li], w["b_router_expert"][li].reshape(-1)])
    br = jnp.pad(br, (0, LANES - br.shape[0]))[None, :]
    wo = w_out.astype(BF16)
    splits = [0]
    for a in feats:
        splits.append(splits[-1] + a.shape[1])
    x1, rw, re = mix_route(feats, [wo[lo:hi] for lo, hi in zip(splits[:-1], splits[1:])], x,
                           w["ln_mix_g"][li][None], w["ln_mix_b"][li][None], wr, br, alpha, tm)

    te = _expert_tile(ntp)
    n_flat = 2 * ntp
    flat = re[:, :2].reshape(-1)
    onehot = (flat[:, None] == jnp.arange(N_EXPERTS, dtype=jnp.int32)[None, :]).astype(jnp.int32)
    running = jnp.cumsum(onehot, axis=0)
    counts = running[-1]
    padded = ((counts + te - 1) // te) * te
    gend = jnp.cumsum(padded)
    gstart = gend - padded
    pos_flat = jnp.sum(onehot * (running - 1 + gstart[None, :]), axis=1)
    n_rows = ((n_flat + N_EXPERTS * (te - 1) + te - 1) // te) * te
    row_token = (jnp.arange(n_rows, dtype=jnp.int32) % ntp).at[pos_flat].set(
        jnp.arange(n_flat, dtype=jnp.int32) // 2, mode="promise_in_bounds", unique_indices=True)
    pos_flat = pos_flat.reshape(ntp, 2)
    tile_start = jnp.arange(n_rows // te, dtype=jnp.int32) * te
    tile_expert = jnp.minimum(jnp.sum((gend[None, :] <= tile_start[:, None]).astype(jnp.int32), axis=1),
                              N_EXPERTS - 1)
    tile_valid = (tile_start < gend[-1]).astype(jnp.int32)

    def rows(a, idx):
        return a.at[idx].get(mode="promise_in_bounds")

    xs = rows(x1, row_token)
    y = experts(tile_expert, tile_valid, xs, w["w_exp_gate"], w["w_exp_up"], w["w_exp_down"], li, te)
    y0 = rows(y, pos_flat[:, 0])
    y1 = rows(y, pos_flat[:, 1])
    return combine_ple(x1, y0, y1, rw, p_prompt, p_tail, li, w["ln_ffn_g"][li][None], w["ln_ffn_b"][li][None],
                       w["w_ple_gate"][li].astype(BF16), w["b_ple_gate"][li][None],
                       w["w_ple_proj"][li].astype(BF16), alpha, tm)


def kernel(x_prompt, x_sample, p_prompt, p_sample, cache_fox_k, cache_fox_v, cache_fox_logf, state_ssm, state_conv, cache_win_k, cache_win_v, page_table, w_in_even, b_fgate, conv_w, conv_b, dt_bias, a_log, d_skip, ssm_norm_w, w_out_even, w_in_odd, attn_sinks, w_out_odd, ln_mix_g, ln_mix_b, ln_ffn_g, ln_ffn_b, w_router_group, b_router_group, w_router_expert, b_router_expert, w_exp_gate, w_exp_up, w_exp_down, w_ple_proj, w_ple_gate, b_ple_gate):
    bp, seq, d = x_prompt.shape
    bd, t_dec, _ = x_sample.shape
    assert t_dec == 1 and d == D_MODEL
    depth = p_prompt.shape[0]
    n_pages = page_table.shape[1]
    past_len = n_pages * PAGE_SIZE
    np_tok = bp * seq
    nt = np_tok + bd
    tm, ntp = _tiles(nt)
    pad = ntp - nt

    def tokens(a_p, a_s):
        parts = [a_p.reshape(np_tok, -1), a_s.reshape(bd, -1)]
        if pad:
            parts.append(jnp.zeros((pad, parts[0].shape[1]), parts[0].dtype))
        return jnp.concatenate(parts, axis=0)

    def past_prompt(a_s):
        a_s = a_s.reshape(bd, -1)
        return jnp.zeros((ntp, a_s.shape[1]), a_s.dtype).at[np_tok:nt].set(a_s)

    assert np_tok % tm == 0 and seq >= CONV_W - 1
    x = tokens(x_prompt, x_sample)
    p_all = p_prompt.reshape(depth, np_tok, -1)
    shared = dict(ln_mix_g=ln_mix_g, ln_mix_b=ln_mix_b, ln_ffn_g=ln_ffn_g, ln_ffn_b=ln_ffn_b,
                  w_router_group=w_router_group, b_router_group=b_router_group,
                  w_router_expert=w_router_expert, b_router_expert=b_router_expert,
                  w_exp_gate=w_exp_gate, w_exp_up=w_exp_up, w_exp_down=w_exp_down,
                  w_ple_proj=w_ple_proj, w_ple_gate=w_ple_gate, b_ple_gate=b_ple_gate)

    half = HEAD_DIM // 2
    inv = jnp.exp(-math.log(ROPE_THETA) * jnp.arange(half, dtype=F32) / half)
    pos = jnp.concatenate([jnp.tile(jnp.arange(seq, dtype=jnp.int32), bp),
                           jnp.full((bd,), past_len, jnp.int32), jnp.zeros((pad,), jnp.int32)])
    ang = pos.astype(F32)[:, None] * inv[None, :]
    cos_t = jnp.tile(jnp.cos(ang), (1, LANES // half))
    sin_t = jnp.tile(jnp.concatenate([-jnp.sin(ang), jnp.sin(ang)], axis=1), (1, LANES // HEAD_DIM))

    even_p, even_s, odd_p, odd_s = [], [], [], []
    for li in range(depth):
        j = li // 2
        if li % 2 == 0:
            wi = w_in_even[j]
            c0 = 3 * FOX_WIDTH
            c1 = c0 + FOX_HEADS
            c2 = c1 + M_INNER
            c3 = c2 + CONV_CH
            w_main = jnp.concatenate([wi[:, :c0], wi[:, c1:c3]], axis=1).astype(BF16)
            w_small_t = jnp.concatenate([wi[:, c0:c1], wi[:, c3:]], axis=1).T
            b_small = jnp.concatenate([b_fgate[j], dt_bias[j]])[:, None]
            u, qkv = matmul_dual(x, w_main, tm, 512, c0)
            small = small_proj(x, w_small_t, b_small, tm)

            tq = min(512, seq)
            logf_p = small[:FOX_HEADS, :np_tok].reshape(FOX_HEADS, bp, seq)
            cum = cumsum_lanes(jnp.moveaxis(logf_p, 1, 0).reshape(bp * FOX_HEADS, seq), min(512, seq))
            ck = cum.reshape(bp, FOX_HEADS // 2, 2, seq // tq, tq).transpose(0, 1, 3, 2, 4)
            u_s = u[np_tok:nt]
            logf_s = small[:FOX_HEADS, np_tok:nt].T
            q_s = u_s[:, :FOX_WIDTH].reshape(bd, FOX_HEADS, HEAD_DIM)
            v_s = u_s[:, 2 * FOX_WIDTH:c0].reshape(bd, FOX_HEADS, HEAD_DIM)
            eye = jnp.eye(FOX_HEADS, dtype=F32)[None, :, :, None]

            def block_diag(a):
                return (a[:, :, None, :] * eye).reshape(bd, FOX_HEADS, FOX_WIDTH)

            a_s = fox_sample(page_table, block_diag(q_s), q_s,
                             u_s[:, FOX_WIDTH:2 * FOX_WIDTH].reshape(bd, FOX_HEADS, HEAD_DIM),
                             block_diag(v_s), logf_s[:, :, None],
                             jnp.transpose(cache_fox_k, (0, 1, 3, 4, 2)), jnp.transpose(cache_fox_v, (0, 1, 3, 4, 2)),
                             jnp.transpose(cache_fox_logf, (0, 1, 3, 2)), j, min(16, n_pages))
            a_all = fox_prompt(qkv, ck, past_prompt(a_s), bp, seq, tq)

            nega = -jnp.exp(a_log[j])
            dskip_e = jnp.repeat(d_skip[j], M_HEADDIM)[None, :]
            normw = ssm_norm_w[j][None, :]
            zc = c0 + M_INNER
            m_s, st_s = ssd_sample(x[np_tok:nt], jnp.repeat(wi[:, c3:], M_HEADDIM, axis=1),
                                   jnp.repeat(dt_bias[j], M_HEADDIM)[None, :], jnp.repeat(nega, M_HEADDIM)[None, :],
                                   u_s[:, zc:], jnp.moveaxis(state_conv[j], 1, 0), u_s[:, c0:zc], conv_w[j], conv_b[j][None, :],
                                   dskip_e, normw, state_ssm[j], min(8, bd))
            m_all, st_p = ssd_prompt(u, small, conv_w[j], conv_b[j][None, :], nega[:, None], dskip_e, normw,
                                     past_prompt(m_s), bp, seq)
            feats = [a_all, m_all]
            w_out = w_out_even[j]

            kp = u[:np_tok, FOX_WIDTH:2 * FOX_WIDTH].reshape(bp, seq, FOX_HEADS, HEAD_DIM)
            vp = u[:np_tok, 2 * FOX_WIDTH:c0].reshape(bp, seq, FOX_HEADS, HEAD_DIM)
            last = jnp.stack([u[(b + 1) * seq - (CONV_W - 1):(b + 1) * seq, zc:] for b in range(bp)])
            conv_p = jnp.concatenate([jnp.zeros((bp, CONV_W - 1, CONV_CH), F32), last], axis=1)[:, -(CONV_W - 1):]
            even_p.append((kp, vp, jnp.moveaxis(logf_p, 0, 2),
                           st_p.reshape(bp, M_HEADS, M_HEADDIM, D_STATE), conv_p))
            conv_s = jnp.concatenate([state_conv[j], u_s[:, None, zc:]], axis=1)[:, -(CONV_W - 1):]
            even_s.append((u_s[:, FOX_WIDTH:2 * FOX_WIDTH].reshape(bd, 1, FOX_HEADS, HEAD_DIM),
                           u_s[:, 2 * FOX_WIDTH:c0].reshape(bd, 1, FOX_HEADS, HEAD_DIM),
                           logf_s[:, None, :], st_s, conv_s))
        else:
            u = matmul_rope(x, w_in_odd[j].astype(BF16), cos_t, sin_t, tm, 256, ODD_MIX + KV_W)
            u_s = u[np_tok:nt]
            lw = cache_win_k.shape[2]
            kn = u_s[:, ODD_MIX:ODD_MIX + KV_W]
            vn = u_s[:, ODD_MIX + KV_W:]
            o_s = swa_sample(attn_sinks[j][:, None], u_s[:, :ODD_MIX].reshape(bd, SW_HEADS, HEAD_DIM), kn, vn,
                             jnp.transpose(cache_win_k[j], (0, 2, 3, 1)), jnp.transpose(cache_win_v[j], (0, 2, 3, 1)),
                             min(8, bd))
            feats = [swa_prompt(attn_sinks[j], u, past_prompt(o_s), bp, seq)]
            w_out = w_out_odd[j]

            rows = min(WINDOW, seq)
            kp = u[:np_tok, ODD_MIX:ODD_MIX + KV_W].reshape(bp, seq, SW_KV_HEADS, HEAD_DIM)[:, -rows:]
            vp = u[:np_tok, ODD_MIX + KV_W:].reshape(bp, seq, SW_KV_HEADS, HEAD_DIM)[:, -rows:]
            odd_p.append((kp, vp))
            ka = jnp.concatenate([cache_win_k[j], kn.reshape(bd, 1, SW_KV_HEADS, HEAD_DIM)], axis=1)[:, -lw:]
            va = jnp.concatenate([cache_win_v[j], vn.reshape(bd, 1, SW_KV_HEADS, HEAD_DIM)], axis=1)[:, -lw:]
            odd_s.append((ka, va))
        x = _channel_and_ple(x, feats, p_all, past_prompt(p_sample[li])[np_tok:], li, w_out, shared, tm)

    yp = x[:np_tok].reshape(bp, seq, d)
    ys = x[np_tok:nt].reshape(bd, 1, d)
    return (yp, ys,
            jnp.stack([st[0] for st in even_p]), jnp.stack([st[1] for st in even_p]),
            jnp.stack([st[2] for st in even_p]), jnp.stack([st[3] for st in even_p]),
            jnp.stack([st[4] for st in even_p]),
            jnp.stack([st[0] for st in odd_p]), jnp.stack([st[1] for st in odd_p]),
            jnp.stack([st[0] for st in even_s]), jnp.stack([st[1] for st in even_s]),
            jnp.stack([st[2] for st in even_s]), jnp.stack([st[3] for st in even_s]),
            jnp.stack([st[4] for st in even_s]),
            jnp.stack([st[0] for st in odd_s]), jnp.stack([st[1] for st in odd_s]))
```

```python
import functools
import math

import jax
import jax.numpy as jnp
from jax import lax
from jax.experimental import pallas as pl
from jax.experimental.pallas import tpu as pltpu

F32 = jnp.float32
BF16 = jnp.bfloat16
HIGHEST = lax.Precision.HIGHEST

D_MODEL = 1024
HEAD_DIM = 64
FOX_HEADS = 8
FOX_WIDTH = FOX_HEADS * HEAD_DIM
M_HEADS = 8
M_HEADDIM = 64
M_INNER = M_HEADS * M_HEADDIM
M_GROUPS = 2
HPG = M_HEADS // M_GROUPS
D_STATE = 128
CONV_W = 4
BC_W = M_GROUPS * D_STATE
CONV_CH = M_INNER + 2 * BC_W
SSD_CHUNK = 128
RMS_EPS = 1e-5
SW_HEADS = 16
SW_KV_HEADS = 4
SW_GROUP = SW_HEADS // SW_KV_HEADS
WINDOW = 128
ROPE_THETA = 10000.0
ODD_MIX = SW_HEADS * HEAD_DIM
KV_W = SW_KV_HEADS * HEAD_DIM
N_EXPERT_GROUPS = 4
EXPERTS_PER_GROUP = 8
N_EXPERTS = N_EXPERT_GROUPS * EXPERTS_PER_GROUP
EXPERT_FF = 512
PLE_DIM = 256
LN_EPS = 1e-5
PAGE_SIZE = 128

LANES = 128
SUBLANES = 8
VMEM_LIMIT = 48 * 1024 * 1024

NEG_INF = float("-inf")
FOX_TILE = 1024
FOX_Q_STRIP = 1024


def _params(sem, vmem=VMEM_LIMIT):
    return pltpu.CompilerParams(dimension_semantics=sem, vmem_limit_bytes=vmem)


def _nt_dot(a, b, precision=None):
    return lax.dot_general(a, b, (((1,), (1,)), ((), ())), precision=precision,
                           preferred_element_type=F32)


def _dot(a, b, precision=None):
    return jnp.dot(a, b, precision=precision, preferred_element_type=F32)


def _silu(x):
    return x * (1.0 / (1.0 + jnp.exp(-x)))


def _softplus(x):
    return jnp.maximum(x, 0.0) + jnp.log(1.0 + jnp.exp(-jnp.abs(x)))


def _iota(shape, dim):
    return lax.broadcasted_iota(jnp.int32, shape, dim)


def _mm_rope_kernel(x_ref, w_ref, cos_ref, sin_ref, o_ref, *, tn, rope_cols):
    xb = x_ref[...].astype(BF16)
    reps = tn // LANES
    cos = jnp.concatenate([cos_ref[...]] * reps, axis=1)
    sin = jnp.concatenate([sin_ref[...]] * reps, axis=1)
    half = HEAD_DIM // 2
    first = (_iota((xb.shape[0], tn), 1) % HEAD_DIM) < half
    for c in range(w_ref.shape[1] // tn):
        acc = _dot(xb, w_ref[:, c * tn:(c + 1) * tn])
        if c * tn < rope_cols:
            partner = jnp.where(first, pltpu.roll(acc, tn - half, 1), pltpu.roll(acc, half, 1))
            acc = acc * cos + partner * sin
        o_ref[:, c * tn:(c + 1) * tn] = acc


def _even_proj_kernel(x_ref, w_ref, k_ref, v_ref, xz_ref, qkv_ref, qt_ref):
    xb = x_ref[...].astype(BF16)
    w = FOX_WIDTH
    f32_dst = (None, k_ref, v_ref)
    for c in range(3):
        acc = _dot(xb, w_ref[:, c * w:(c + 1) * w])
        qkv_ref[:, c * w:(c + 1) * w] = acc.astype(BF16)
        if f32_dst[c] is not None:
            f32_dst[c][...] = acc
        else:
            @pl.when(pl.program_id(0) == pl.num_programs(0) - 1)
            def _():
                qt_ref[...] = acc
    for c in range(xz_ref.shape[1] // w):
        xz_ref[:, c * w:(c + 1) * w] = _dot(xb, w_ref[:, (3 + c) * w:(4 + c) * w])


def even_proj(x, w, tm):
    m, k = x.shape
    n = w.shape[1]
    wq = 3 * FOX_WIDTH
    row = lambda i: (i, 0)
    return pl.pallas_call(
        _even_proj_kernel,
        grid=(m // tm,),
        in_specs=[pl.BlockSpec((tm, k), row), pl.BlockSpec((k, n), lambda i: (0, 0))],
        out_specs=[pl.BlockSpec((tm, FOX_WIDTH), row), pl.BlockSpec((tm, FOX_WIDTH), row),
                   pl.BlockSpec((tm, n - wq), row), pl.BlockSpec((tm, wq), row),
                   pl.BlockSpec((tm, FOX_WIDTH), lambda i: (0, 0))],
        out_shape=[jax.ShapeDtypeStruct((m, FOX_WIDTH), F32), jax.ShapeDtypeStruct((m, FOX_WIDTH), F32),
                   jax.ShapeDtypeStruct((m, n - wq), F32), jax.ShapeDtypeStruct((m, wq), BF16),
                   jax.ShapeDtypeStruct((tm, FOX_WIDTH), F32)],
        compiler_params=_params(("arbitrary",)),
        name="even_proj",
    )(x, w)


def matmul_rope(x, w, cos, sin, tm, tn, rope_cols):
    m, k = x.shape
    n = w.shape[1]
    assert n % tn == 0 and rope_cols % tn == 0
    return pl.pallas_call(
        functools.partial(_mm_rope_kernel, tn=tn, rope_cols=rope_cols),
        grid=(m // tm,),
        in_specs=[pl.BlockSpec((tm, k), lambda i: (i, 0)),
                  pl.BlockSpec((k, n), lambda i: (0, 0)),
                  pl.BlockSpec((tm, LANES), lambda i: (i, 0)),
                  pl.BlockSpec((tm, LANES), lambda i: (i, 0))],
        out_specs=pl.BlockSpec((tm, n), lambda i: (i, 0)),
        out_shape=jax.ShapeDtypeStruct((m, n), F32),
        compiler_params=_params(("parallel",)),
        name="matmul_rope",
    )(x, w, cos, sin)


def _small_proj_kernel(x_ref, wt_ref, b_ref, o_ref):
    r = _nt_dot(wt_ref[...], x_ref[...], precision=HIGHEST) + b_ref[...]
    row = _iota(r.shape, 0)
    o_ref[...] = jnp.where(row < FOX_HEADS, -_softplus(-r), _softplus(r))


def small_proj(x, wt, b, tm):
    m, k = x.shape
    return pl.pallas_call(
        _small_proj_kernel,
        grid=(m // tm,),
        in_specs=[pl.BlockSpec((tm, k), lambda i: (i, 0)),
                  pl.BlockSpec((16, k), lambda i: (0, 0)),
                  pl.BlockSpec((16, 1), lambda i: (0, 0))],
        out_specs=pl.BlockSpec((16, tm), lambda i: (0, i)),
        out_shape=jax.ShapeDtypeStruct((16, m), F32),
        compiler_params=_params(("parallel",)),
        name="small_proj",
    )(x, wt, b)


def _cumsum_kernel(x_ref, o_ref, carry_ref):
    @pl.when(pl.program_id(0) == 0)
    def _():
        carry_ref[...] = jnp.zeros_like(carry_ref)

    x = x_ref[...]
    w = x.shape[1]
    tri = (_iota((w, w), 0) <= _iota((w, w), 1)).astype(F32)
    c = _dot(x, tri, precision=HIGHEST) + carry_ref[...]
    o_ref[...] = c
    carry_ref[...] = c[:, w - 1:w]


def cumsum_lanes(x, chunk):
    r, l = x.shape
    return pl.pallas_call(
        _cumsum_kernel,
        grid=(l // chunk,),
        in_specs=[pl.BlockSpec((r, chunk), lambda i: (0, i))],
        out_specs=pl.BlockSpec((r, chunk), lambda i: (0, i)),
        out_shape=jax.ShapeDtypeStruct((r, l), F32),
        scratch_shapes=[pltpu.VMEM((r, 1), F32)],
        compiler_params=_params(("arbitrary",)),
        name="cumsum_lanes",
    )(x)


def _fox_prompt_kernel(q_ref, k_ref, v_ref, ck_ref, rest_ref, o_ref, *, tq):
    del rest_ref
    qi = pl.program_id(2)
    strip = min(FOX_Q_STRIP, tq)
    n_strips = tq // strip
    chains = [(h, r) for h in range(2) for r in range(n_strips)]
    lane = _iota((strip, LANES), 1)
    q_chain = []
    for h, r in chains:
        q = q_ref[r * strip:(r + 1) * strip, :] * (HEAD_DIM ** -0.5)
        keep = (lane < HEAD_DIM) if h == 0 else (lane >= HEAD_DIM)
        q_chain.append(jnp.where(keep, q, jnp.zeros_like(q)))

    def step(j, carry, masked):
        start = pl.multiple_of(j * tq, tq)
        kb = k_ref[pl.ds(start, tq), :]
        vb = v_ref[pl.ds(start, tq), :]
        ck = ck_ref[0, 0, j]
        out = []
        for (h, r), qc, (m, l, acc) in zip(chains, q_chain, carry):
            s = _nt_dot(qc, kb) - ck[h:h + 1, :]
            if masked:
                s = jnp.where(_iota(s.shape, 1) <= _iota(s.shape, 0) + r * strip, s, NEG_INF)
            m_new = jnp.maximum(m, jnp.max(s, axis=1, keepdims=True))
            alpha = jnp.exp(m - m_new)
            p = jnp.exp(s - m_new)
            l = alpha * l + jnp.sum(p, axis=1, keepdims=True)
            acc = alpha * acc + _dot(p.astype(BF16), vb)
            out.append((m_new, l, acc))
        return tuple(out)

    init1 = (jnp.full((strip, 1), NEG_INF, F32), jnp.zeros((strip, 1), F32), jnp.zeros((strip, LANES), F32))
    carry = lax.fori_loop(0, qi, lambda j, c: step(j, c, False), (init1,) * len(chains))
    final = step(qi, carry, True)
    for r in range(n_strips):
        (_, l0, a0), (_, l1, a1) = final[r], final[n_strips + r]
        o_ref[r * strip:(r + 1) * strip, :] = jnp.where(lane < HEAD_DIM, a0 / l0, a1 / l1)


def fox_prompt(qkv, ck, rest, n_batch, seq, tq):
    nq = seq // tq
    pairs = FOX_HEADS // 2
    return pl.pallas_call(
        functools.partial(_fox_prompt_kernel, tq=tq),
        grid=(n_batch, pairs, nq),
        in_specs=[pl.BlockSpec((tq, LANES), lambda b, h, i: (b * nq + i, h)),
                  pl.BlockSpec((seq, LANES), lambda b, h, i: (b, pairs + h)),
                  pl.BlockSpec((seq, LANES), lambda b, h, i: (b, 2 * pairs + h)),
                  pl.BlockSpec((1, 1, nq, 2, tq), lambda b, h, i: (b, h, 0, 0, 0)),
                  pl.BlockSpec(memory_space=pl.ANY)],
        out_specs=pl.BlockSpec((tq, LANES), lambda b, h, i: (b * nq + i, h)),
        out_shape=jax.ShapeDtypeStruct(rest.shape, F32),
        input_output_aliases={4: 0},
        compiler_params=_params(("parallel", "parallel", "arbitrary")),
        name="fox_prompt",
    )(qkv, qkv, qkv, ck, rest)


def _block_diag_rows(full):
    rowh = _iota((FOX_HEADS, HEAD_DIM), 0)
    out = jnp.zeros((FOX_HEADS, HEAD_DIM), F32)
    for h in range(FOX_HEADS):
        out = out + jnp.where(rowh == h, full[:, h * HEAD_DIM:(h + 1) * HEAD_DIM], 0.0)
    return out


def _fox_sample_kernel(pt_ref, qbd_ref, q_ref, kn_ref, vbd_ref, ln_ref, *refs, pages_per_step):
    del pt_ref
    pp = pages_per_step
    k_refs, v_refs, lf_refs = refs[:pp], refs[pp:2 * pp], refs[2 * pp:3 * pp]
    o_ref = refs[3 * pp]
    m_ref, l_ref, acc_ref, carry_ref = refs[3 * pp + 1:]
    t = pl.program_id(1)
    scale = HEAD_DIM ** -0.5

    @pl.when(t == 0)
    def _():
        m_ref[...] = jnp.sum(q_ref[0] * kn_ref[0], axis=1, keepdims=True) * scale
        l_ref[...] = jnp.ones_like(l_ref)
        acc_ref[...] = vbd_ref[0]
        carry_ref[...] = ln_ref[0]

    qb = (qbd_ref[0] * scale).astype(BF16)
    lane = _iota((FOX_HEADS, PAGE_SIZE), 1)
    width = FOX_HEADS * HEAD_DIM
    carry = carry_ref[...]
    scores = []
    for r in range(pp):
        lf = lf_refs[r][0, 0]
        x = lf
        for sh in (1, 2, 4, 8, 16, 32, 64):
            x = x + jnp.where(lane + sh < PAGE_SIZE, pltpu.roll(x, PAGE_SIZE - sh, 1), 0.0)
        kp = k_refs[r][0, 0].reshape(width, PAGE_SIZE).astype(BF16)
        scores.append(_dot(qb, kp) + ((x - lf) + carry))
        carry = carry + x[:, 0:1]
    carry_ref[...] = carry
    s = jnp.concatenate(scores, axis=1)
    m = m_ref[...]
    m_new = jnp.maximum(m, jnp.max(s, axis=1, keepdims=True))
    alpha = jnp.exp(m - m_new)
    p = jnp.exp(s - m_new)
    l_ref[...] = alpha * l_ref[...] + jnp.sum(p, axis=1, keepdims=True)
    m_ref[...] = m_new
    pb = p.astype(BF16)
    acc = alpha * acc_ref[...]
    for r in range(pp):
        vp = v_refs[r][0, 0].reshape(width, PAGE_SIZE).astype(BF16)
        acc = acc + _nt_dot(pb[:, r * PAGE_SIZE:(r + 1) * PAGE_SIZE], vp)
    acc_ref[...] = acc

    @pl.when(t == pl.num_programs(1) - 1)
    def _():
        o_ref[0] = _block_diag_rows(acc / l_ref[...])


def fox_sample(page_table, qbd, q, kn, vbd, ln, cache_kt, cache_vt, cache_lft, layer, pages_per_step):
    bd, n_pages = page_table.shape
    pp = pages_per_step
    steps = n_pages // pp
    width = FOX_HEADS * HEAD_DIM

    def page(b, t, pt, r):
        return pt[b, n_pages - 1 - (t * pp + r)]

    kv_specs = [pl.BlockSpec((1, 1, FOX_HEADS, HEAD_DIM, PAGE_SIZE),
                             functools.partial(lambda b, t, pt, r: (layer, page(b, t, pt, r), 0, 0, 0), r=r))
                for r in range(pp)]
    lf_specs = [pl.BlockSpec((1, 1, FOX_HEADS, PAGE_SIZE),
                             functools.partial(lambda b, t, pt, r: (layer, page(b, t, pt, r), 0, 0), r=r))
                for r in range(pp)]
    tok = pl.BlockSpec((1, FOX_HEADS, HEAD_DIM), lambda b, t, pt: (b, 0, 0))
    wide = pl.BlockSpec((1, FOX_HEADS, width), lambda b, t, pt: (b, 0, 0))
    grid_spec = pltpu.PrefetchScalarGridSpec(
        num_scalar_prefetch=1,
        grid=(bd, steps),
        in_specs=[wide, tok, tok, wide, pl.BlockSpec((1, FOX_HEADS, 1), lambda b, t, pt: (b, 0, 0))]
        + kv_specs + kv_specs + lf_specs,
        out_specs=tok,
        scratch_shapes=[pltpu.VMEM((FOX_HEADS, 1), F32), pltpu.VMEM((FOX_HEADS, 1), F32),
                        pltpu.VMEM((FOX_HEADS, width), F32), pltpu.VMEM((FOX_HEADS, 1), F32)],
    )
    return pl.pallas_call(
        functools.partial(_fox_sample_kernel, pages_per_step=pp),
        grid_spec=grid_spec,
        out_shape=jax.ShapeDtypeStruct((bd, FOX_HEADS, HEAD_DIM), F32),
        compiler_params=_params(("parallel", "arbitrary")),
        name="fox_sample",
    )(page_table, qbd, q, kn, vbd, ln, *([cache_kt] * pp), *([cache_vt] * pp), *([cache_lft] * pp))


def _ssd_epilogue(y, xs, z, dskip_e, normw):
    y = (y + dskip_e * xs) * _silu(z)
    half = M_INNER // M_GROUPS
    outs = []
    for g in range(M_GROUPS):
        yg = y[:, g * half:(g + 1) * half]
        ms = jnp.sum(yg * yg, axis=1, keepdims=True) * (1.0 / half)
        outs.append(yg * lax.rsqrt(ms + RMS_EPS))
    return jnp.concatenate(outs, axis=1) * normw


def _ssd_prompt_kernel(xbc_ref, z_ref, dt_ref, cw_ref, cb_ref, nega_ref, dskip_ref, normw_ref, rest_ref,
                       o_ref, st_ref, ext_ref, h_ref):
    del rest_ref
    c = pl.program_id(1)
    L = SSD_CHUNK
    pad = SUBLANES

    @pl.when(c == 0)
    def _():
        ext_ref[0:pad, :] = jnp.zeros((pad, CONV_CH), F32)
        h_ref[...] = jnp.zeros_like(h_ref)

    ext_ref[pad:pad + L, :] = xbc_ref[...]
    acc = ext_ref[pad:pad + L, :] * cw_ref[CONV_W - 1:CONV_W, :]
    for j in range(CONV_W - 1):
        off = pad - (CONV_W - 1) + j
        acc = acc + ext_ref[off:off + L, :] * cw_ref[j:j + 1, :]
    u = _silu(acc + cb_ref[...])
    ext_ref[0:pad, :] = ext_ref[L:L + pad, :]

    xs = u[:, :M_INNER]
    dt_t = dt_ref[...]
    cum_t = _dot(dt_t * nega_ref[...], (_iota((L, L), 0) <= _iota((L, L), 1)).astype(F32),
                 precision=HIGHEST)
    eye = (_iota((L, L), 0) == _iota((L, L), 1)).astype(F32)
    cols = _nt_dot(eye, jnp.concatenate([cum_t, dt_t], axis=0), precision=HIGHEST)
    cum_last = cum_t[:, L - 1:L]
    tail_t = jnp.exp(cum_last - cum_t) * dt_t
    tril = _iota((L, L), 0) >= _iota((L, L), 1)
    lane = _iota((L, LANES), 1)
    rowi = _iota((L, LANES), 0)

    y_pairs = []
    for g in range(M_GROUPS):
        bm = u[:, M_INNER + g * D_STATE:M_INNER + (g + 1) * D_STATE]
        cm = u[:, M_INNER + BC_W + g * D_STATE:M_INNER + BC_W + (g + 1) * D_STATE]
        bmb = bm.astype(BF16)
        cmb = cm.astype(BF16)
        cb = _nt_dot(cmb, bmb)
        for pr in range(HPG // 2):
            pidx = g * (HPG // 2) + pr
            xs_pair = xs[:, pidx * LANES:(pidx + 1) * LANES]
            xs_pair_b = xs_pair.astype(BF16)
            h0 = h_ref[pidx]
            ych = _nt_dot(cmb, h0.astype(BF16))
            yw = []
            for k in range(2):
                hd = 2 * pidx + k
                diff = cols[:, hd:hd + 1] - cum_t[hd:hd + 1, :]
                decay = jnp.exp(jnp.where(tril, diff, NEG_INF))
                w = cb * decay * dt_t[hd:hd + 1, :]
                yw.append(_dot(w.astype(BF16), xs_pair_b))
            e0 = jnp.exp(cols[:, 2 * pidx:2 * pidx + 1])
            e1 = jnp.exp(cols[:, 2 * pidx + 1:2 * pidx + 2])
            first = lane < M_HEADDIM
            y_pairs.append(jnp.where(first, yw[0], yw[1]) + ych * jnp.where(first, e0, e1))
            top = rowi < M_HEADDIM
            tail_m = jnp.where(top, jnp.broadcast_to(tail_t[2 * pidx:2 * pidx + 1, :], (L, L)),
                               jnp.broadcast_to(tail_t[2 * pidx + 1:2 * pidx + 2, :], (L, L)))
            dec_m = jnp.where(top, jnp.exp(cum_last[2 * pidx:2 * pidx + 1, :]),
                              jnp.exp(cum_last[2 * pidx + 1:2 * pidx + 2, :]))
            xt = xs_pair.T * tail_m
            h_ref[pidx] = h0 * dec_m + _dot(xt.astype(BF16), bmb)

    y = jnp.concatenate(y_pairs, axis=1)
    o_ref[...] = _ssd_epilogue(y, xs, z_ref[...], dskip_ref[...], normw_ref[...])
    st_ref[0] = h_ref[...]


def ssd_prompt(xz, dt_rows, conv_w, conv_b, nega, dskip_e, normw, rest, n_batch, seq):
    L = SSD_CHUNK
    nc = seq // L
    pairs = M_HEADS // 2
    const = lambda b, c: (0, 0)
    return pl.pallas_call(
        _ssd_prompt_kernel,
        grid=(n_batch, nc),
        in_specs=[pl.BlockSpec((L, CONV_CH), lambda b, c: (b * nc + c, 0)),
                  pl.BlockSpec((L, M_INNER), lambda b, c: (b * nc + c, CONV_CH // M_INNER)),
                  pl.BlockSpec((M_HEADS, L), lambda b, c: (1, b * nc + c)),
                  pl.BlockSpec((CONV_W, CONV_CH), const),
                  pl.BlockSpec((1, CONV_CH), const),
                  pl.BlockSpec((M_HEADS, 1), const),
                  pl.BlockSpec((1, M_INNER), const),
                  pl.BlockSpec((1, M_INNER), const),
                  pl.BlockSpec(memory_space=pl.ANY)],
        out_specs=[pl.BlockSpec((L, M_INNER), lambda b, c: (b * nc + c, 0)),
                   pl.BlockSpec((1, pairs, LANES, D_STATE), lambda b, c: (b, 0, 0, 0))],
        out_shape=[jax.ShapeDtypeStruct(rest.shape, F32),
                   jax.ShapeDtypeStruct((n_batch, pairs, LANES, D_STATE), F32)],
        input_output_aliases={8: 0},
        scratch_shapes=[pltpu.VMEM((L + SUBLANES, CONV_CH), F32),
                        pltpu.VMEM((pairs, LANES, D_STATE), F32)],
        compiler_params=_params(("parallel", "arbitrary")),
        name="ssd_prompt",
    )(xz, xz, dt_rows, conv_w, conv_b, nega, dskip_e, normw, rest)


def _ssd_sample_kernel(x_ref, wdt_ref, dtb_ref, nega_ref, xbc_ref, ctx_ref, z_ref, cw_ref, cb_ref,
                       dskip_ref, normw_ref, h0_ref, o_ref, hn_ref,
                       u_ref, coef_t_ref, dec_t_ref, dec_ref, dtx_ref, *, seqs_per_step):
    sb = seqs_per_step
    i = pl.program_id(0)
    nseq = x_ref.shape[0]

    @pl.when(i == 0)
    def _():
        acc = xbc_ref[...] * cw_ref[CONV_W - 1:CONV_W, :]
        for j in range(CONV_W - 1):
            acc = acc + ctx_ref[j] * cw_ref[j:j + 1, :]
        u = _silu(acc + cb_ref[...])
        u_ref[...] = u
        dt = _softplus(_dot(x_ref[...], wdt_ref[...], precision=HIGHEST) + dtb_ref[...])
        dec = jnp.exp(dt * nega_ref[...])
        coef = dt * u[:, :M_INNER]
        dec_ref[...] = dec
        dtx_ref[...] = coef
        for blk in range(M_INNER // LANES):
            sl = slice(blk * LANES, (blk + 1) * LANES)
            coef_t_ref[sl, :] = coef[:, sl].T
            dec_t_ref[sl, :] = dec[:, sl].T

    base = pl.multiple_of(i * sb, sb)
    ub = u_ref[pl.ds(base, sb), :]
    lane_seq = _iota((M_INNER, nseq), 1)
    rows = _iota((sb, M_INNER // M_GROUPS), 0)
    ch = [jnp.zeros((sb, M_INNER // M_GROUPS), F32) for _ in range(M_GROUPS)]
    for s in range(sb):
        onehot = lane_seq == base + s
        cx = jnp.sum(jnp.where(onehot, coef_t_ref[...], 0.0), axis=1, keepdims=True)
        dc = jnp.sum(jnp.where(onehot, dec_t_ref[...], 0.0), axis=1, keepdims=True)
        for g in range(M_GROUPS):
            brow = ub[s:s + 1, M_INNER + g * D_STATE:M_INNER + (g + 1) * D_STATE]
            cblk = ub[:, M_INNER + BC_W + g * D_STATE:M_INNER + BC_W + (g + 1) * D_STATE]
            hg = h0_ref[s, g * HPG:(g + 1) * HPG].reshape(HPG * M_HEADDIM, D_STATE)
            r = _nt_dot(cblk.astype(BF16), hg.astype(BF16))
            ch[g] = ch[g] + jnp.where(rows == s, r, 0.0)
            lo = g * HPG * M_HEADDIM
            hn = hg * dc[lo:lo + HPG * M_HEADDIM] + cx[lo:lo + HPG * M_HEADDIM] * brow
            hn_ref[s, g * HPG:(g + 1) * HPG] = hn.reshape(HPG, M_HEADDIM, D_STATE)

    xs = ub[:, :M_INNER]
    dec = dec_ref[pl.ds(base, sb), :]
    coef = dtx_ref[pl.ds(base, sb), :]
    ys = []
    half = M_INNER // M_GROUPS
    for g in range(M_GROUPS):
        bm = ub[:, M_INNER + g * D_STATE:M_INNER + (g + 1) * D_STATE]
        cm = ub[:, M_INNER + BC_W + g * D_STATE:M_INNER + BC_W + (g + 1) * D_STATE]
        cb = jnp.sum(cm * bm, axis=1, keepdims=True)
        ys.append(cb * coef[:, g * half:(g + 1) * half] + ch[g] * dec[:, g * half:(g + 1) * half])
    y = jnp.concatenate(ys, axis=1)
    o_ref[...] = _ssd_epilogue(y, xs, z_ref[...], dskip_ref[...], normw_ref[...])


def ssd_sample(x_s, wdt_e, dtb_e, nega_e, xbc_s, ctx, z_s, conv_w, conv_b, dskip_e, normw, h0, seqs_per_step):
    bd = x_s.shape[0]
    sb = seqs_per_step
    const = lambda i: (0, 0)
    return pl.pallas_call(
        functools.partial(_ssd_sample_kernel, seqs_per_step=sb),
        grid=(bd // sb,),
        in_specs=[pl.BlockSpec((bd, D_MODEL), const),
                  pl.BlockSpec((D_MODEL, M_INNER), const),
                  pl.BlockSpec((1, M_INNER), const),
                  pl.BlockSpec((1, M_INNER), const),
                  pl.BlockSpec((bd, CONV_CH), const),
                  pl.BlockSpec((CONV_W - 1, bd, CONV_CH), lambda i: (0, 0, 0)),
                  pl.BlockSpec((sb, M_INNER), lambda i: (i, 0)),
                  pl.BlockSpec((CONV_W, CONV_CH), const),
                  pl.BlockSpec((1, CONV_CH), const),
                  pl.BlockSpec((1, M_INNER), const),
                  pl.BlockSpec((1, M_INNER), const),
                  pl.BlockSpec((sb, M_HEADS, M_HEADDIM, D_STATE), lambda i: (i, 0, 0, 0))],
        out_specs=[pl.BlockSpec((sb, M_INNER), lambda i: (i, 0)),
                   pl.BlockSpec((sb, M_HEADS, M_HEADDIM, D_STATE), lambda i: (i, 0, 0, 0))],
        out_shape=[jax.ShapeDtypeStruct((bd, M_INNER), F32),
                   jax.ShapeDtypeStruct((bd, M_HEADS, M_HEADDIM, D_STATE), F32)],
        scratch_shapes=[pltpu.VMEM((bd, CONV_CH), F32),
                        pltpu.VMEM((M_INNER, bd), F32),
                        pltpu.VMEM((M_INNER, bd), F32),
                        pltpu.VMEM((bd, M_INNER), F32),
                        pltpu.VMEM((bd, M_INNER), F32)],
        compiler_params=_params(("arbitrary",)),
        name="ssd_sample",
    )(x_s, wdt_e, dtb_e, nega_e, xbc_s, ctx, z_s, conv_w, conv_b, dskip_e, normw, h0)


def _to_half(x, src_half, dst_half):
    return x if src_half == dst_half else pltpu.roll(x, HEAD_DIM, 1)


def _swa_prompt_kernel(sink_ref, q_ref, kp_ref, kc_ref, vp_ref, vc_ref, rest_ref, o_ref):
    del rest_ref
    n = pl.program_id(1)
    W = WINDOW
    scale = HEAD_DIM ** -0.5
    lane = _iota((W, LANES), 1)
    rows4 = _iota((SW_GROUP * W, 2 * W), 0)
    t = rows4 % W
    col = _iota((SW_GROUP * W, 2 * W), 1)
    valid = (col >= t) & (col <= t + W) & ((n > 0) | (col >= W))
    rcol = _iota((SW_GROUP * W, 1), 0) // W
    outs = [None] * SW_HEADS
    for j in range(SW_KV_HEADS):
        ch, hf = j // 2, j % 2
        kk = jnp.concatenate([kp_ref[:, ch * LANES:(ch + 1) * LANES],
                              kc_ref[:, ch * LANES:(ch + 1) * LANES]], axis=0).astype(BF16)
        vv = jnp.concatenate([vp_ref[:, ch * LANES:(ch + 1) * LANES],
                              vc_ref[:, ch * LANES:(ch + 1) * LANES]], axis=0).astype(BF16)
        qs = []
        sink = jnp.zeros((SW_GROUP * W, 1), F32)
        for g in range(SW_GROUP):
            hq = j * SW_GROUP + g
            qc = q_ref[:, (hq // 2) * LANES:(hq // 2 + 1) * LANES] * scale
            qc = _to_half(qc, hq % 2, hf)
            keep = (lane < HEAD_DIM) if hf == 0 else (lane >= HEAD_DIM)
            qs.append(jnp.where(keep, qc, 0.0))
            sink = jnp.where(rcol == g, sink_ref[hq], sink)
        q4 = jnp.concatenate(qs, axis=0).astype(BF16)
        s = jnp.where(valid, _nt_dot(q4, kk), NEG_INF)
        m = jnp.maximum(jnp.max(s, axis=1, keepdims=True), sink)
        p = jnp.exp(s - m)
        den = jnp.sum(p, axis=1, keepdims=True) + jnp.exp(sink - m)
        o = _dot(p.astype(BF16), vv) / den
        for g in range(SW_GROUP):
            hq = j * SW_GROUP + g
            outs[hq] = _to_half(o[g * W:(g + 1) * W], hf, hq % 2)
    for c in range(SW_HEADS // 2):
        o_ref[:, c * LANES:(c + 1) * LANES] = jnp.where(lane < HEAD_DIM, outs[2 * c], outs[2 * c + 1])


def swa_prompt(sinks, u_odd, rest, n_batch, seq):
    nb = seq // WINDOW
    kcol = ODD_MIX // KV_W
    return pl.pallas_call(
        _swa_prompt_kernel,
        grid=(n_batch, nb),
        in_specs=[pl.BlockSpec(memory_space=pltpu.SMEM),
                  pl.BlockSpec((WINDOW, ODD_MIX), lambda b, n: (b * nb + n, 0)),
                  pl.BlockSpec((WINDOW, KV_W), lambda b, n: (b * nb + jnp.maximum(n - 1, 0), kcol)),
                  pl.BlockSpec((WINDOW, KV_W), lambda b, n: (b * nb + n, kcol)),
                  pl.BlockSpec((WINDOW, KV_W), lambda b, n: (b * nb + jnp.maximum(n - 1, 0), kcol + 1)),
                  pl.BlockSpec((WINDOW, KV_W), lambda b, n: (b * nb + n, kcol + 1)),
                  pl.BlockSpec(memory_space=pl.ANY)],
        out_specs=pl.BlockSpec((WINDOW, ODD_MIX), lambda b, n: (b * nb + n, 0)),
        out_shape=jax.ShapeDtypeStruct(rest.shape, F32),
        input_output_aliases={6: 0},
        compiler_params=_params(("parallel", "arbitrary")),
        name="swa_prompt",
    )(sinks, u_odd, u_odd, u_odd, u_odd, u_odd, rest)


def _swa_sample_kernel(sink_ref, q_ref, kn_ref, vn_ref, bk_ref, bv_ref, o_ref, *, seqs_per_step):
    scale = HEAD_DIM ** -0.5
    rowg = _iota((SW_HEADS, HEAD_DIM), 0) // SW_GROUP
    sink = sink_ref[...]
    for s in range(seqs_per_step):
        q16 = q_ref[s] * scale
        qbd = jnp.concatenate([jnp.where(rowg == j, q16, 0.0) for j in range(SW_KV_HEADS)], axis=1)
        kb = bk_ref[s].reshape(KV_W, -1).astype(BF16)
        vb = bv_ref[s].reshape(KV_W, -1).astype(BF16)
        sc = _dot(qbd.astype(BF16), kb)
        s_new = jnp.sum(qbd * kn_ref[s:s + 1, :], axis=1, keepdims=True)
        m = jnp.maximum(jnp.maximum(jnp.max(sc, axis=1, keepdims=True), s_new), sink)
        p = jnp.exp(sc - m)
        pn = jnp.exp(s_new - m)
        den = jnp.sum(p, axis=1, keepdims=True) + pn + jnp.exp(sink - m)
        full = (_nt_dot(p.astype(BF16), vb) + pn * vn_ref[s:s + 1, :]) / den
        o16 = jnp.zeros((SW_HEADS, HEAD_DIM), F32)
        for j in range(SW_KV_HEADS):
            o16 = o16 + jnp.where(rowg == j, full[:, j * HEAD_DIM:(j + 1) * HEAD_DIM], 0.0)
        o_ref[s] = o16


def swa_sample(sinks_col, q, kn, vn, buf_k, buf_v, seqs_per_step):
    bd, _, _, lw = buf_k.shape
    sb = seqs_per_step
    return pl.pallas_call(
        functools.partial(_swa_sample_kernel, seqs_per_step=sb),
        grid=(bd // sb,),
        in_specs=[pl.BlockSpec((SW_HEADS, 1), lambda i: (0, 0)),
                  pl.BlockSpec((sb, SW_HEADS, HEAD_DIM), lambda i: (i, 0, 0)),
                  pl.BlockSpec((sb, KV_W), lambda i: (i, 0)),
                  pl.BlockSpec((sb, KV_W), lambda i: (i, 0)),
                  pl.BlockSpec((sb, SW_KV_HEADS, HEAD_DIM, lw), lambda i: (i, 0, 0, 0)),
                  pl.BlockSpec((sb, SW_KV_HEADS, HEAD_DIM, lw), lambda i: (i, 0, 0, 0))],
        out_specs=pl.BlockSpec((sb, SW_HEADS, HEAD_DIM), lambda i: (i, 0, 0)),
        out_shape=jax.ShapeDtypeStruct((bd, SW_HEADS, HEAD_DIM), F32),
        compiler_params=_params(("parallel",)),
        name="swa_sample",
    )(sinks_col, q, kn, vn, buf_k, buf_v)


def _layer_norm(h, g, b):
    mu = jnp.mean(h, axis=1, keepdims=True)
    d = h - mu
    var = jnp.mean(d * d, axis=1, keepdims=True)
    return d * lax.rsqrt(var + LN_EPS) * g + b


def _mix_route_kernel(*refs, alpha, n_feat):
    am_refs, wo_refs = refs[:n_feat], refs[n_feat:2 * n_feat]
    x_ref, g_ref, b_ref, wr_ref, br_ref, x1_ref, rw_ref, re_ref = refs[2 * n_feat:]
    mix = _dot(am_refs[0][...].astype(BF16), wo_refs[0][...])
    for a_ref, w_ref in zip(am_refs[1:], wo_refs[1:]):
        mix = mix + _dot(a_ref[...].astype(BF16), w_ref[...])
    x1 = _layer_norm(alpha * x_ref[...] + mix, g_ref[...], b_ref[...])
    x1_ref[...] = x1
    wr = wr_ref[...]
    x_hi = x1.astype(BF16)
    w_hi = wr.astype(BF16)
    x_lo = (x1 - x_hi.astype(F32)).astype(BF16)
    w_lo = (wr - w_hi.astype(F32)).astype(BF16)
    logits = _dot(x_hi, w_hi) + _dot(x_lo, w_hi) + _dot(x_hi, w_lo) + br_ref[...]
    lane_i = _iota(logits.shape, 1)
    lane = lane_i.astype(F32)
    big = float(LANES)
    gl = jnp.where(lane_i < N_EXPERT_GROUPS, logits, NEG_INF)
    gmax = jnp.max(gl, axis=1, keepdims=True)
    grp = jnp.min(jnp.where(gl == gmax, lane, big), axis=1, keepdims=True)
    pg = 1.0 / jnp.sum(jnp.exp(gl - gmax), axis=1, keepdims=True)
    rel = lane - (N_EXPERT_GROUPS + grp * EXPERTS_PER_GROUP)
    el = jnp.where(rel >= 0.0, jnp.where(rel < EXPERTS_PER_GROUP, logits, NEG_INF), NEG_INF)
    v1 = jnp.max(el, axis=1, keepdims=True)
    i1 = jnp.min(jnp.where(el == v1, lane, big), axis=1, keepdims=True)
    el2 = jnp.where(lane == i1, NEG_INF, el)
    v2 = jnp.max(el2, axis=1, keepdims=True)
    i2 = jnp.min(jnp.where(el2 == v2, lane, big), axis=1, keepdims=True)
    e = jnp.exp(v2 - v1)
    w1 = pg / (1.0 + e)
    w2 = w1 * e
    rw_ref[...] = jnp.where(lane_i == 0, w1, jnp.where(lane_i == 1, w2, 0.0))
    e1 = (i1 - N_EXPERT_GROUPS).astype(jnp.int32)
    e2 = (i2 - N_EXPERT_GROUPS).astype(jnp.int32)
    re_ref[...] = jnp.where(lane_i == 0, e1, jnp.where(lane_i == 1, e2, 0))


def mix_route(feats, wos, x, g, b, wr, br, alpha, tm):
    m, d = x.shape
    const = lambda i: (0, 0)
    row = lambda i: (i, 0)
    return pl.pallas_call(
        functools.partial(_mix_route_kernel, alpha=alpha, n_feat=len(feats)),
        grid=(m // tm,),
        in_specs=[pl.BlockSpec((tm, a.shape[1]), row) for a in feats]
        + [pl.BlockSpec((w.shape[0], d), const) for w in wos]
        + [pl.BlockSpec((tm, d), row), pl.BlockSpec((1, d), const), pl.BlockSpec((1, d), const),
           pl.BlockSpec((d, LANES), const), pl.BlockSpec((1, LANES), const)],
        out_specs=[pl.BlockSpec((tm, d), row),
                   pl.BlockSpec((tm, LANES), row), pl.BlockSpec((tm, LANES), row)],
        out_shape=[jax.ShapeDtypeStruct((m, d), F32),
                   jax.ShapeDtypeStruct((m, LANES), F32), jax.ShapeDtypeStruct((m, LANES), jnp.int32)],
        compiler_params=_params(("parallel",)),
        name="mix_route",
    )(*feats, *wos, x, g, b, wr, br)


def _experts_kernel(te_ref, tv_ref, x_ref, wg_ref, wu_ref, wd_ref, y_ref):
    t = pl.program_id(0)

    @pl.when(tv_ref[t] > 0)
    def _():
        x = x_ref[...].astype(BF16)
        h = _silu(_dot(x, wg_ref[0, 0].astype(BF16))) * _dot(x, wu_ref[0, 0].astype(BF16))
        y_ref[...] = _dot(h.astype(BF16), wd_ref[0, 0].astype(BF16))

    @pl.when(tv_ref[t] == 0)
    def _():
        y_ref[...] = jnp.zeros_like(y_ref)


def experts(tile_expert, tile_valid, xs, wg, wu, wd, layer, te):
    r, d = xs.shape
    ff = wg.shape[3]
    grid_spec = pltpu.PrefetchScalarGridSpec(
        num_scalar_prefetch=2,
        grid=(r // te,),
        in_specs=[pl.BlockSpec((te, d), lambda t, e, v: (t, 0)),
                  pl.BlockSpec((1, 1, d, ff), lambda t, e, v: (layer, e[t], 0, 0)),
                  pl.BlockSpec((1, 1, d, ff), lambda t, e, v: (layer, e[t], 0, 0)),
                  pl.BlockSpec((1, 1, ff, d), lambda t, e, v: (layer, e[t], 0, 0))],
        out_specs=pl.BlockSpec((te, d), lambda t, e, v: (t, 0)),
    )
    return pl.pallas_call(
        _experts_kernel,
        grid_spec=grid_spec,
        out_shape=jax.ShapeDtypeStruct((r, d), F32),
        compiler_params=_params(("arbitrary",)),
        name="experts",
    )(tile_expert, tile_valid, xs, wg, wu, wd)


def _combine_ple_kernel(x1_ref, y0_ref, y1_ref, rw_ref, pp_ref, pt_ref, g_ref, b_ref, wg_ref, bg_ref, wp_ref,
                        o_ref, *, alpha, prompt_tiles):
    rw = rw_ref[...]
    f = rw[:, 0:1] * y0_ref[...] + rw[:, 1:2] * y1_ref[...]
    x2 = _layer_norm(alpha * x1_ref[...] + f, g_ref[...], b_ref[...])
    gl = _dot(x2.astype(BF16), wg_ref[...]) + bg_ref[...]
    gate = 1.0 / (1.0 + jnp.exp(-gl))
    p = jnp.where(pl.program_id(0) < prompt_tiles, pp_ref[0], pt_ref[...])
    o_ref[...] = x2 + gate * _dot(p.astype(BF16), wp_ref[...])


def combine_ple(x1, y0, y1, rw, p_prompt, p_tail, layer, g, b, wg, bg, wp, alpha, tm):
    m, d = x1.shape
    pd = p_prompt.shape[2]
    prompt_tiles = p_prompt.shape[1] // tm
    const = lambda i: (0, 0)
    row = lambda i: (i, 0)
    return pl.pallas_call(
        functools.partial(_combine_ple_kernel, alpha=alpha, prompt_tiles=prompt_tiles),
        grid=(m // tm,),
        in_specs=[pl.BlockSpec((tm, d), row), pl.BlockSpec((tm, d), row), pl.BlockSpec((tm, d), row),
                  pl.BlockSpec((tm, LANES), row),
                  pl.BlockSpec((1, tm, pd), lambda i: (layer, jnp.minimum(i, prompt_tiles - 1), 0)),
                  pl.BlockSpec((tm, pd), lambda i: (jnp.maximum(i - prompt_tiles, 0), 0)),
                  pl.BlockSpec((1, d), const), pl.BlockSpec((1, d), const),
                  pl.BlockSpec((d, d), const), pl.BlockSpec((1, d), const), pl.BlockSpec((pd, d), const)],
        out_specs=pl.BlockSpec((tm, d), row),
        out_shape=jax.ShapeDtypeStruct((m, d), F32),
        compiler_params=_params(("parallel",)),
        name="combine_ple",
    )(x1, y0, y1, rw, p_prompt, p_tail, g, b, wg, bg, wp)


def _tiles(n_tokens):
    tm = 512 if n_tokens >= 4096 else 128
    return tm, ((n_tokens + tm - 1) // tm) * tm


def _expert_tile(n_tokens):
    return 256 if n_tokens >= 4096 else 32


def _channel_and_ple(x, feats, p_prompt, p_tail, li, w_out, w, tm):
    depth = w["ln_mix_g"].shape[0]
    alpha = (2 * depth) ** 0.25
    ntp, d = x.shape
    wr = jnp.concatenate([w["w_router_group"][li],
                          jnp.moveaxis(w["w_router_expert"][li], 0, 1).reshape(d, N_EXPERTS)], axis=1)
    wr = jnp.pad(wr, ((0, 0), (0, LANES - wr.shape[1])))
    br = jnp.concatenate([w["b_router_group"][li], w["b_router_expert"][li].reshape(-1)])
    br = jnp.pad(br, (0, LANES - br.shape[0]))[None, :]
    wo = w_out.astype(BF16)
    splits = [0]
    for a in feats:
        splits.append(splits[-1] + a.shape[1])
    x1, rw, re = mix_route(feats, [wo[lo:hi] for lo, hi in zip(splits[:-1], splits[1:])], x,
                           w["ln_mix_g"][li][None], w["ln_mix_b"][li][None], wr, br, alpha, tm)

    te = _expert_tile(ntp)
    n_flat = 2 * ntp
    flat = re[:, :2].reshape(-1)
    onehot = (flat[:, None] == jnp.arange(N_EXPERTS, dtype=jnp.int32)[None, :]).astype(jnp.int32)
    running = jnp.cumsum(onehot, axis=0)
    counts = running[-1]
    padded = ((counts + te - 1) // te) * te
    gend = jnp.cumsum(padded)
    gstart = gend - padded
    pos_flat = jnp.sum(onehot * (running - 1 + gstart[None, :]), axis=1)
    n_rows = ((n_flat + N_EXPERTS * (te - 1) + te - 1) // te) * te
    row_token = (jnp.arange(n_rows, dtype=jnp.int32) % ntp).at[pos_flat].set(
        jnp.arange(n_flat, dtype=jnp.int32) // 2, mode="promise_in_bounds", unique_indices=True)
    pos_flat = pos_flat.reshape(ntp, 2)
    tile_start = jnp.arange(n_rows // te, dtype=jnp.int32) * te
    tile_expert = jnp.minimum(jnp.sum((gend[None, :] <= tile_start[:, None]).astype(jnp.int32), axis=1),
                              N_EXPERTS - 1)
    tile_valid = (tile_start < gend[-1]).astype(jnp.int32)

    def rows(a, idx):
        return a.at[idx].get(mode="promise_in_bounds")

    xs = rows(x1, row_token)
    y = experts(tile_expert, tile_valid, xs, w["w_exp_gate"], w["w_exp_up"], w["w_exp_down"], li, te)
    y0 = rows(y, pos_flat[:, 0])
    y1 = rows(y, pos_flat[:, 1])
    return combine_ple(x1, y0, y1, rw, p_prompt, p_tail, li, w["ln_ffn_g"][li][None], w["ln_ffn_b"][li][None],
                       w["w_ple_gate"][li].astype(BF16), w["b_ple_gate"][li][None],
                       w["w_ple_proj"][li].astype(BF16), alpha, tm)


def kernel(x_prompt, x_sample, p_prompt, p_sample, cache_fox_k, cache_fox_v, cache_fox_logf, state_ssm, state_conv, cache_win_k, cache_win_v, page_table, w_in_even, b_fgate, conv_w, conv_b, dt_bias, a_log, d_skip, ssm_norm_w, w_out_even, w_in_odd, attn_sinks, w_out_odd, ln_mix_g, ln_mix_b, ln_ffn_g, ln_ffn_b, w_router_group, b_router_group, w_router_expert, b_router_expert, w_exp_gate, w_exp_up, w_exp_down, w_ple_proj, w_ple_gate, b_ple_gate):
    bp, seq, d = x_prompt.shape
    bd, t_dec, _ = x_sample.shape
    assert t_dec == 1 and d == D_MODEL
    depth = p_prompt.shape[0]
    n_pages = page_table.shape[1]
    past_len = n_pages * PAGE_SIZE
    np_tok = bp * seq
    nt = np_tok + bd
    tm, ntp = _tiles(nt)
    pad = ntp - nt

    def tokens(a_p, a_s):
        parts = [a_p.reshape(np_tok, -1), a_s.reshape(bd, -1)]
        if pad:
            parts.append(jnp.zeros((pad, parts[0].shape[1]), parts[0].dtype))
        return jnp.concatenate(parts, axis=0)

    def past_prompt(a_s):
        a_s = a_s.reshape(bd, -1)
        return jnp.zeros((ntp, a_s.shape[1]), a_s.dtype).at[np_tok:nt].set(a_s)

    assert np_tok % tm == 0 and seq >= CONV_W - 1
    x = tokens(x_prompt, x_sample)
    p_all = p_prompt.reshape(depth, np_tok, -1)
    shared = dict(ln_mix_g=ln_mix_g, ln_mix_b=ln_mix_b, ln_ffn_g=ln_ffn_g, ln_ffn_b=ln_ffn_b,
                  w_router_group=w_router_group, b_router_group=b_router_group,
                  w_router_expert=w_router_expert, b_router_expert=b_router_expert,
                  w_exp_gate=w_exp_gate, w_exp_up=w_exp_up, w_exp_down=w_exp_down,
                  w_ple_proj=w_ple_proj, w_ple_gate=w_ple_gate, b_ple_gate=b_ple_gate)

    half = HEAD_DIM // 2
    inv = jnp.exp(-math.log(ROPE_THETA) * jnp.arange(half, dtype=F32) / half)
    pos = jnp.concatenate([jnp.tile(jnp.arange(seq, dtype=jnp.int32), bp),
                           jnp.full((bd,), past_len, jnp.int32), jnp.zeros((pad,), jnp.int32)])
    ang = pos.astype(F32)[:, None] * inv[None, :]
    cos_t = jnp.tile(jnp.cos(ang), (1, LANES // half))
    sin_t = jnp.tile(jnp.concatenate([-jnp.sin(ang), jnp.sin(ang)], axis=1), (1, LANES // HEAD_DIM))

    even_p, even_s, odd_p, odd_s = [], [], [], []
    for li in range(depth):
        j = li // 2
        if li % 2 == 0:
            wi = w_in_even[j]
            c0 = 3 * FOX_WIDTH
            c1 = c0 + FOX_HEADS
            c2 = c1 + M_INNER
            c3 = c2 + CONV_CH
            w_main = jnp.concatenate([wi[:, :c0], wi[:, c2:c3], wi[:, c1:c2]], axis=1).astype(BF16)
            w_small_t = jnp.concatenate([wi[:, c0:c1], wi[:, c3:]], axis=1).T
            b_small = jnp.concatenate([b_fgate[j], dt_bias[j]])[:, None]
            k_all, v_all, xz, qkv, q_tail = even_proj(x, w_main, tm)
            small = small_proj(x, w_small_t, b_small, tm)

            tq = min(FOX_TILE, seq)
            logf_p = small[:FOX_HEADS, :np_tok].reshape(FOX_HEADS, bp, seq)
            cum = cumsum_lanes(jnp.moveaxis(logf_p, 1, 0).reshape(bp * FOX_HEADS, seq), min(512, seq))
            ck = cum.reshape(bp, FOX_HEADS // 2, 2, seq // tq, tq).transpose(0, 1, 3, 2, 4)
            logf_s = small[:FOX_HEADS, np_tok:nt].T
            q_s = q_tail[:bd].reshape(bd, FOX_HEADS, HEAD_DIM)
            k_s = k_all[np_tok:nt].reshape(bd, FOX_HEADS, HEAD_DIM)
            v_s = v_all[np_tok:nt].reshape(bd, FOX_HEADS, HEAD_DIM)
            eye = jnp.eye(FOX_HEADS, dtype=F32)[None, :, :, None]

            def block_diag(a):
                return (a[:, :, None, :] * eye).reshape(bd, FOX_HEADS, FOX_WIDTH)

            a_s = fox_sample(page_table, block_diag(q_s), q_s, k_s, block_diag(v_s), logf_s[:, :, None],
                             jnp.transpose(cache_fox_k, (0, 1, 3, 4, 2)), jnp.transpose(cache_fox_v, (0, 1, 3, 4, 2)),
                             jnp.transpose(cache_fox_logf, (0, 1, 3, 2)), j, min(16, n_pages))
            a_all = fox_prompt(qkv, ck, past_prompt(a_s), bp, seq, tq)

            nega = -jnp.exp(a_log[j])
            dskip_e = jnp.repeat(d_skip[j], M_HEADDIM)[None, :]
            normw = ssm_norm_w[j][None, :]
            xz_s = xz[np_tok:nt]
            m_s, st_s = ssd_sample(x[np_tok:nt], jnp.repeat(wi[:, c3:], M_HEADDIM, axis=1),
                                   jnp.repeat(dt_bias[j], M_HEADDIM)[None, :], jnp.repeat(nega, M_HEADDIM)[None, :],
                                   xz_s[:, :CONV_CH], jnp.moveaxis(state_conv[j], 1, 0), xz_s[:, CONV_CH:],
                                   conv_w[j], conv_b[j][None, :], dskip_e, normw, state_ssm[j], min(8, bd))
            m_all, st_p = ssd_prompt(xz, small, conv_w[j], conv_b[j][None, :], nega[:, None], dskip_e, normw,
                                     past_prompt(m_s), bp, seq)
            feats = [a_all, m_all]
            w_out = w_out_even[j]

            kp = k_all[:np_tok].reshape(bp, seq, FOX_HEADS, HEAD_DIM)
            vp = v_all[:np_tok].reshape(bp, seq, FOX_HEADS, HEAD_DIM)
            last = jnp.stack([xz[(b + 1) * seq - (CONV_W - 1):(b + 1) * seq, :CONV_CH] for b in range(bp)])
            conv_p = jnp.concatenate([jnp.zeros((bp, CONV_W - 1, CONV_CH), F32), last], axis=1)[:, -(CONV_W - 1):]
            even_p.append((kp, vp, jnp.moveaxis(logf_p, 0, 2),
                           st_p.reshape(bp, M_HEADS, M_HEADDIM, D_STATE), conv_p))
            conv_s = jnp.concatenate([state_conv[j], xz_s[:, None, :CONV_CH]], axis=1)[:, -(CONV_W - 1):]
            even_s.append((k_s[:, None], v_s[:, None], logf_s[:, None, :], st_s, conv_s))
        else:
            u = matmul_rope(x, w_in_odd[j].astype(BF16), cos_t, sin_t, tm, 256, ODD_MIX + KV_W)
            u_s = u[np_tok:nt]
            lw = cache_win_k.shape[2]
            kn = u_s[:, ODD_MIX:ODD_MIX + KV_W]
            vn = u_s[:, ODD_MIX + KV_W:]
            o_s = swa_sample(attn_sinks[j][:, None], u_s[:, :ODD_MIX].reshape(bd, SW_HEADS, HEAD_DIM), kn, vn,
                             jnp.transpose(cache_win_k[j], (0, 2, 3, 1)), jnp.transpose(cache_win_v[j], (0, 2, 3, 1)),
                             min(8, bd))
            feats = [swa_prompt(attn_sinks[j], u, past_prompt(o_s), bp, seq)]
            w_out = w_out_odd[j]

            rows = min(WINDOW, seq)
            tail = jnp.stack([u[(b + 1) * seq - rows:(b + 1) * seq, ODD_MIX:] for b in range(bp)])
            odd_p.append((tail[:, :, :KV_W].reshape(bp, rows, SW_KV_HEADS, HEAD_DIM),
                          tail[:, :, KV_W:].reshape(bp, rows, SW_KV_HEADS, HEAD_DIM)))
            ka = jnp.concatenate([cache_win_k[j], kn.reshape(bd, 1, SW_KV_HEADS, HEAD_DIM)], axis=1)[:, -lw:]
            va = jnp.concatenate([cache_win_v[j], vn.reshape(bd, 1, SW_KV_HEADS, HEAD_DIM)], axis=1)[:, -lw:]
            odd_s.append((ka, va))
        x = _channel_and_ple(x, feats, p_all, past_prompt(p_sample[li])[np_tok:], li, w_out, shared, tm)

    yp = x[:np_tok].reshape(bp, seq, d)
    ys = x[np_tok:nt].reshape(bd, 1, d)
    return (yp, ys,
            jnp.stack([st[0] for st in even_p]), jnp.stack([st[1] for st in even_p]),
            jnp.stack([st[2] for st in even_p]), jnp.stack([st[3] for st in even_p]),
            jnp.stack([st[4] for st in even_p]),
            jnp.stack([st[0] for st in odd_p]), jnp.stack([st[1] for st in odd_p]),
            jnp.stack([st[0] for st in even_s]), jnp.stack([st[1] for st in even_s]),
            jnp.stack([st[2] for st in even_s]), jnp.stack([st[3] for st in even_s]),
            jnp.stack([st[4] for st in even_s]),
            jnp.stack([st[0] for st in odd_s]), jnp.stack([st[1] for st in odd_s]))
```

```python
import functools
import math

import jax
import jax.numpy as jnp
from jax import lax
from jax.experimental import pallas as pl
from jax.experimental.pallas import tpu as pltpu

F32 = jnp.float32
BF16 = jnp.bfloat16
HIGHEST = lax.Precision.HIGHEST

D_MODEL = 1024
HEAD_DIM = 64
FOX_HEADS = 8
FOX_WIDTH = FOX_HEADS * HEAD_DIM
M_HEADS = 8
M_HEADDIM = 64
M_INNER = M_HEADS * M_HEADDIM
M_GROUPS = 2
HPG = M_HEADS // M_GROUPS
D_STATE = 128
CONV_W = 4
BC_W = M_GROUPS * D_STATE
CONV_CH = M_INNER + 2 * BC_W
SSD_CHUNK = 128
RMS_EPS = 1e-5
SW_HEADS = 16
SW_KV_HEADS = 4
SW_GROUP = SW_HEADS // SW_KV_HEADS
WINDOW = 128
ROPE_THETA = 10000.0
ODD_MIX = SW_HEADS * HEAD_DIM
KV_W = SW_KV_HEADS * HEAD_DIM
N_EXPERT_GROUPS = 4
EXPERTS_PER_GROUP = 8
N_EXPERTS = N_EXPERT_GROUPS * EXPERTS_PER_GROUP
EXPERT_FF = 512
PLE_DIM = 256
LN_EPS = 1e-5
PAGE_SIZE = 128

LANES = 128
SUBLANES = 8
VMEM_LIMIT = 48 * 1024 * 1024

NEG_INF = float("-inf")
FOX_TILE = 1024
FOX_Q_STRIP = 1024


def _params(sem, vmem=VMEM_LIMIT):
    return pltpu.CompilerParams(dimension_semantics=sem, vmem_limit_bytes=vmem)


def _nt_dot(a, b, precision=None):
    return lax.dot_general(a, b, (((1,), (1,)), ((), ())), precision=precision,
                           preferred_element_type=F32)


def _dot(a, b, precision=None):
    return jnp.dot(a, b, precision=precision, preferred_element_type=F32)


def _silu(x):
    return x * (1.0 / (1.0 + jnp.exp(-x)))


def _softplus(x):
    return jnp.maximum(x, 0.0) + jnp.log(1.0 + jnp.exp(-jnp.abs(x)))


def _iota(shape, dim):
    return lax.broadcasted_iota(jnp.int32, shape, dim)


def _mm_rope_kernel(x_ref, w_ref, cos_ref, sin_ref, o_ref, *, tn, rope_cols):
    xb = x_ref[...].astype(BF16)
    reps = tn // LANES
    cos = jnp.concatenate([cos_ref[...]] * reps, axis=1)
    sin = jnp.concatenate([sin_ref[...]] * reps, axis=1)
    half = HEAD_DIM // 2
    first = (_iota((xb.shape[0], tn), 1) % HEAD_DIM) < half
    for c in range(w_ref.shape[1] // tn):
        acc = _dot(xb, w_ref[:, c * tn:(c + 1) * tn])
        if c * tn < rope_cols:
            partner = jnp.where(first, pltpu.roll(acc, tn - half, 1), pltpu.roll(acc, half, 1))
            acc = acc * cos + partner * sin
        o_ref[:, c * tn:(c + 1) * tn] = acc


def _even_proj_kernel(x_ref, w_ref, k_ref, v_ref, xz_ref, qkv_ref, qt_ref):
    xb = x_ref[...].astype(BF16)
    w = FOX_WIDTH
    f32_dst = (None, k_ref, v_ref)
    for c in range(3):
        acc = _dot(xb, w_ref[:, c * w:(c + 1) * w])
        qkv_ref[:, c * w:(c + 1) * w] = acc.astype(BF16)
        if f32_dst[c] is not None:
            f32_dst[c][...] = acc
        else:
            @pl.when(pl.program_id(0) == pl.num_programs(0) - 1)
            def _():
                qt_ref[...] = acc
    for c in range(xz_ref.shape[1] // w):
        xz_ref[:, c * w:(c + 1) * w] = _dot(xb, w_ref[:, (3 + c) * w:(4 + c) * w])


def even_proj(x, w, tm, first_tile, n_tiles):
    k = x.shape[1]
    m = n_tiles * tm
    n = w.shape[1]
    wq = 3 * FOX_WIDTH
    row = lambda i: (i, 0)
    return pl.pallas_call(
        _even_proj_kernel,
        grid=(n_tiles,),
        in_specs=[pl.BlockSpec((tm, k), lambda i: (first_tile + i, 0)), pl.BlockSpec((k, n), lambda i: (0, 0))],
        out_specs=[pl.BlockSpec((tm, FOX_WIDTH), row), pl.BlockSpec((tm, FOX_WIDTH), row),
                   pl.BlockSpec((tm, n - wq), row), pl.BlockSpec((tm, wq), row),
                   pl.BlockSpec((tm, FOX_WIDTH), lambda i: (0, 0))],
        out_shape=[jax.ShapeDtypeStruct((m, FOX_WIDTH), F32), jax.ShapeDtypeStruct((m, FOX_WIDTH), F32),
                   jax.ShapeDtypeStruct((m, n - wq), F32), jax.ShapeDtypeStruct((m, wq), BF16),
                   jax.ShapeDtypeStruct((tm, FOX_WIDTH), F32)],
        compiler_params=_params(("arbitrary",)),
        name="even_proj",
    )(x, w)


def matmul_rope(x, w, cos, sin, tm, tn, rope_cols):
    m, k = x.shape
    n = w.shape[1]
    assert n % tn == 0 and rope_cols % tn == 0
    return pl.pallas_call(
        functools.partial(_mm_rope_kernel, tn=tn, rope_cols=rope_cols),
        grid=(m // tm,),
        in_specs=[pl.BlockSpec((tm, k), lambda i: (i, 0)),
                  pl.BlockSpec((k, n), lambda i: (0, 0)),
                  pl.BlockSpec((tm, LANES), lambda i: (i, 0)),
                  pl.BlockSpec((tm, LANES), lambda i: (i, 0))],
        out_specs=pl.BlockSpec((tm, n), lambda i: (i, 0)),
        out_shape=jax.ShapeDtypeStruct((m, n), F32),
        compiler_params=_params(("parallel",)),
        name="matmul_rope",
    )(x, w, cos, sin)


def _small_proj_kernel(x_ref, wt_ref, b_ref, o_ref):
    r = _nt_dot(wt_ref[...], x_ref[...], precision=HIGHEST) + b_ref[...]
    row = _iota(r.shape, 0)
    o_ref[...] = jnp.where(row < FOX_HEADS, -_softplus(-r), _softplus(r))


def small_proj(x, wt, b, tm):
    m, k = x.shape
    return pl.pallas_call(
        _small_proj_kernel,
        grid=(m // tm,),
        in_specs=[pl.BlockSpec((tm, k), lambda i: (i, 0)),
                  pl.BlockSpec((16, k), lambda i: (0, 0)),
                  pl.BlockSpec((16, 1), lambda i: (0, 0))],
        out_specs=pl.BlockSpec((16, tm), lambda i: (0, i)),
        out_shape=jax.ShapeDtypeStruct((16, m), F32),
        compiler_params=_params(("parallel",)),
        name="small_proj",
    )(x, wt, b)


def _cumsum_kernel(x_ref, o_ref, carry_ref):
    @pl.when(pl.program_id(0) == 0)
    def _():
        carry_ref[...] = jnp.zeros_like(carry_ref)

    x = x_ref[...]
    w = x.shape[1]
    tri = (_iota((w, w), 0) <= _iota((w, w), 1)).astype(F32)
    c = _dot(x, tri, precision=HIGHEST) + carry_ref[...]
    o_ref[...] = c
    carry_ref[...] = c[:, w - 1:w]


def cumsum_lanes(x, chunk):
    r, l = x.shape
    return pl.pallas_call(
        _cumsum_kernel,
        grid=(l // chunk,),
        in_specs=[pl.BlockSpec((r, chunk), lambda i: (0, i))],
        out_specs=pl.BlockSpec((r, chunk), lambda i: (0, i)),
        out_shape=jax.ShapeDtypeStruct((r, l), F32),
        scratch_shapes=[pltpu.VMEM((r, 1), F32)],
        compiler_params=_params(("arbitrary",)),
        name="cumsum_lanes",
    )(x)


def _fox_prompt_kernel(q_ref, k_ref, v_ref, ck_ref, rest_ref, o_ref, *, tq):
    del rest_ref
    qi = pl.program_id(2)
    strip = min(FOX_Q_STRIP, tq)
    n_strips = tq // strip
    chains = [(h, r) for h in range(2) for r in range(n_strips)]
    lane = _iota((strip, LANES), 1)
    q_chain = []
    for h, r in chains:
        q = q_ref[r * strip:(r + 1) * strip, :] * (HEAD_DIM ** -0.5)
        keep = (lane < HEAD_DIM) if h == 0 else (lane >= HEAD_DIM)
        q_chain.append(jnp.where(keep, q, jnp.zeros_like(q)))

    lane_kv = _iota((tq, LANES), 1)
    den_lane = (HEAD_DIM, 0)
    own = (jnp.where(lane_kv < HEAD_DIM, 1.0, 0.0).astype(BF16), jnp.where(lane_kv >= HEAD_DIM, 1.0, 0.0).astype(BF16))
    den = tuple(jnp.where(lane_kv == d, 1.0, 0.0).astype(BF16) for d in den_lane)

    def step(j, carry, masked):
        start = pl.multiple_of(j * tq, tq)
        kb = k_ref[pl.ds(start, tq), :]
        vb = v_ref[pl.ds(start, tq), :]
        v_heads = (vb * own[0] + den[0], vb * own[1] + den[1])
        ck = ck_ref[0, 0, j]
        out = []
        for (h, r), qc, (m, acc) in zip(chains, q_chain, carry):
            s = _nt_dot(qc, kb) - ck[h:h + 1, :]
            if masked:
                s = jnp.where(_iota(s.shape, 1) <= _iota(s.shape, 0) + r * strip, s, NEG_INF)
            m_new = jnp.maximum(m, jnp.max(s, axis=1, keepdims=True))
            p = jnp.exp(s - m_new).astype(BF16)
            acc = jnp.exp(m - m_new) * acc + _dot(p, v_heads[h])
            out.append((m_new, acc))
        return tuple(out)

    init1 = (jnp.full((strip, 1), NEG_INF, F32), jnp.zeros((strip, LANES), F32))
    carry = lax.fori_loop(0, qi, lambda j, c: step(j, c, False), (init1,) * len(chains))
    final = step(qi, carry, True)
    for r in range(n_strips):
        a0, a1 = final[r][1], final[n_strips + r][1]
        o0 = a0 / a0[:, den_lane[0]:den_lane[0] + 1]
        o1 = a1 / a1[:, den_lane[1]:den_lane[1] + 1]
        o_ref[r * strip:(r + 1) * strip, :] = jnp.where(lane < HEAD_DIM, o0, o1)


def fox_prompt(qkv, ck, rest, n_batch, seq, tq):
    nq = seq // tq
    pairs = FOX_HEADS // 2
    return pl.pallas_call(
        functools.partial(_fox_prompt_kernel, tq=tq),
        grid=(n_batch, pairs, nq),
        in_specs=[pl.BlockSpec((tq, LANES), lambda b, h, i: (b * nq + i, h)),
                  pl.BlockSpec((seq, LANES), lambda b, h, i: (b, pairs + h)),
                  pl.BlockSpec((seq, LANES), lambda b, h, i: (b, 2 * pairs + h)),
                  pl.BlockSpec((1, 1, nq, 2, tq), lambda b, h, i: (b, h, 0, 0, 0)),
                  pl.BlockSpec(memory_space=pl.ANY)],
        out_specs=pl.BlockSpec((tq, LANES), lambda b, h, i: (b * nq + i, h)),
        out_shape=jax.ShapeDtypeStruct(rest.shape, F32),
        input_output_aliases={4: 0},
        compiler_params=_params(("parallel", "parallel", "arbitrary")),
        name="fox_prompt",
    )(qkv, qkv, qkv, ck, rest)


def _block_diag_rows(full):
    rowh = _iota((FOX_HEADS, HEAD_DIM), 0)
    out = jnp.zeros((FOX_HEADS, HEAD_DIM), F32)
    for h in range(FOX_HEADS):
        out = out + jnp.where(rowh == h, full[:, h * HEAD_DIM:(h + 1) * HEAD_DIM], 0.0)
    return out


def _fox_sample_kernel(pt_ref, qbd_ref, q_ref, kn_ref, vbd_ref, ln_ref, *refs, pages_per_step):
    del pt_ref
    pp = pages_per_step
    k_refs, v_refs, lf_refs = refs[:pp], refs[pp:2 * pp], refs[2 * pp:3 * pp]
    o_ref = refs[3 * pp]
    m_ref, l_ref, acc_ref, carry_ref = refs[3 * pp + 1:]
    t = pl.program_id(1)
    scale = HEAD_DIM ** -0.5

    @pl.when(t == 0)
    def _():
        m_ref[...] = jnp.sum(q_ref[0] * kn_ref[0], axis=1, keepdims=True) * scale
        l_ref[...] = jnp.ones_like(l_ref)
        acc_ref[...] = vbd_ref[0]
        carry_ref[...] = ln_ref[0]

    qb = (qbd_ref[0] * scale).astype(BF16)
    lane = _iota((FOX_HEADS, PAGE_SIZE), 1)
    width = FOX_HEADS * HEAD_DIM
    carry = carry_ref[...]
    scores = []
    for r in range(pp):
        lf = lf_refs[r][0, 0]
        x = lf
        for sh in (1, 2, 4, 8, 16, 32, 64):
            x = x + jnp.where(lane + sh < PAGE_SIZE, pltpu.roll(x, PAGE_SIZE - sh, 1), 0.0)
        kp = k_refs[r][0, 0].reshape(width, PAGE_SIZE).astype(BF16)
        scores.append(_dot(qb, kp) + ((x - lf) + carry))
        carry = carry + x[:, 0:1]
    carry_ref[...] = carry
    s = jnp.concatenate(scores, axis=1)
    m = m_ref[...]
    m_new = jnp.maximum(m, jnp.max(s, axis=1, keepdims=True))
    alpha = jnp.exp(m - m_new)
    p = jnp.exp(s - m_new)
    l_ref[...] = alpha * l_ref[...] + jnp.sum(p, axis=1, keepdims=True)
    m_ref[...] = m_new
    pb = p.astype(BF16)
    acc = alpha * acc_ref[...]
    for r in range(pp):
        vp = v_refs[r][0, 0].reshape(width, PAGE_SIZE).astype(BF16)
        acc = acc + _nt_dot(pb[:, r * PAGE_SIZE:(r + 1) * PAGE_SIZE], vp)
    acc_ref[...] = acc

    @pl.when(t == pl.num_programs(1) - 1)
    def _():
        o_ref[0] = _block_diag_rows(acc / l_ref[...])


def fox_sample(page_table, qbd, q, kn, vbd, ln, cache_kt, cache_vt, cache_lft, layer, pages_per_step):
    bd, n_pages = page_table.shape
    pp = pages_per_step
    steps = n_pages // pp
    width = FOX_HEADS * HEAD_DIM

    def page(b, t, pt, r):
        return pt[b, n_pages - 1 - (t * pp + r)]

    kv_specs = [pl.BlockSpec((1, 1, FOX_HEADS, HEAD_DIM, PAGE_SIZE),
                             functools.partial(lambda b, t, pt, r: (layer, page(b, t, pt, r), 0, 0, 0), r=r))
                for r in range(pp)]
    lf_specs = [pl.BlockSpec((1, 1, FOX_HEADS, PAGE_SIZE),
                             functools.partial(lambda b, t, pt, r: (layer, page(b, t, pt, r), 0, 0), r=r))
                for r in range(pp)]
    tok = pl.BlockSpec((1, FOX_HEADS, HEAD_DIM), lambda b, t, pt: (b, 0, 0))
    wide = pl.BlockSpec((1, FOX_HEADS, width), lambda b, t, pt: (b, 0, 0))
    grid_spec = pltpu.PrefetchScalarGridSpec(
        num_scalar_prefetch=1,
        grid=(bd, steps),
        in_specs=[wide, tok, tok, wide, pl.BlockSpec((1, FOX_HEADS, 1), lambda b, t, pt: (b, 0, 0))]
        + kv_specs + kv_specs + lf_specs,
        out_specs=tok,
        scratch_shapes=[pltpu.VMEM((FOX_HEADS, 1), F32), pltpu.VMEM((FOX_HEADS, 1), F32),
                        pltpu.VMEM((FOX_HEADS, width), F32), pltpu.VMEM((FOX_HEADS, 1), F32)],
    )
    return pl.pallas_call(
        functools.partial(_fox_sample_kernel, pages_per_step=pp),
        grid_spec=grid_spec,
        out_shape=jax.ShapeDtypeStruct((bd, FOX_HEADS, HEAD_DIM), F32),
        compiler_params=_params(("parallel", "arbitrary")),
        name="fox_sample",
    )(page_table, qbd, q, kn, vbd, ln, *([cache_kt] * pp), *([cache_vt] * pp), *([cache_lft] * pp))


def _ssd_epilogue(y, xs, z, dskip_e, normw):
    y = (y + dskip_e * xs) * _silu(z)
    half = M_INNER // M_GROUPS
    outs = []
    for g in range(M_GROUPS):
        yg = y[:, g * half:(g + 1) * half]
        ms = jnp.sum(yg * yg, axis=1, keepdims=True) * (1.0 / half)
        outs.append(yg * lax.rsqrt(ms + RMS_EPS))
    return jnp.concatenate(outs, axis=1) * normw


def _ssd_prompt_kernel(xbc_ref, z_ref, dt_ref, cw_ref, cb_ref, nega_ref, dskip_ref, normw_ref, rest_ref,
                       o_ref, st_ref, ext_ref, h_ref):
    del rest_ref
    c = pl.program_id(1)
    L = SSD_CHUNK
    pad = SUBLANES

    @pl.when(c == 0)
    def _():
        ext_ref[0:pad, :] = jnp.zeros((pad, CONV_CH), F32)
        h_ref[...] = jnp.zeros_like(h_ref)

    ext_ref[pad:pad + L, :] = xbc_ref[...]
    acc = ext_ref[pad:pad + L, :] * cw_ref[CONV_W - 1:CONV_W, :]
    for j in range(CONV_W - 1):
        off = pad - (CONV_W - 1) + j
        acc = acc + ext_ref[off:off + L, :] * cw_ref[j:j + 1, :]
    u = _silu(acc + cb_ref[...])
    ext_ref[0:pad, :] = ext_ref[L:L + pad, :]

    xs = u[:, :M_INNER]
    dt_t = dt_ref[...]
    cum_t = _dot(dt_t * nega_ref[...], (_iota((L, L), 0) <= _iota((L, L), 1)).astype(F32),
                 precision=HIGHEST)
    eye = (_iota((L, L), 0) == _iota((L, L), 1)).astype(F32)
    cols = _nt_dot(eye, jnp.concatenate([cum_t, dt_t], axis=0), precision=HIGHEST)
    cum_last = cum_t[:, L - 1:L]
    tail_t = jnp.exp(cum_last - cum_t) * dt_t
    tril = _iota((L, L), 0) >= _iota((L, L), 1)
    lane = _iota((L, LANES), 1)
    rowi = _iota((L, LANES), 0)

    y_pairs = []
    for g in range(M_GROUPS):
        bm = u[:, M_INNER + g * D_STATE:M_INNER + (g + 1) * D_STATE]
        cm = u[:, M_INNER + BC_W + g * D_STATE:M_INNER + BC_W + (g + 1) * D_STATE]
        bmb = bm.astype(BF16)
        cmb = cm.astype(BF16)
        cb = _nt_dot(cmb, bmb)
        for pr in range(HPG // 2):
            pidx = g * (HPG // 2) + pr
            xs_pair = xs[:, pidx * LANES:(pidx + 1) * LANES]
            xs_pair_b = xs_pair.astype(BF16)
            h0 = h_ref[pidx]
            ych = _nt_dot(cmb, h0.astype(BF16))
            yw = []
            for k in range(2):
                hd = 2 * pidx + k
                diff = cols[:, hd:hd + 1] - cum_t[hd:hd + 1, :]
                decay = jnp.exp(jnp.where(tril, diff, NEG_INF))
                w = cb * decay * dt_t[hd:hd + 1, :]
                yw.append(_dot(w.astype(BF16), xs_pair_b))
            e0 = jnp.exp(cols[:, 2 * pidx:2 * pidx + 1])
            e1 = jnp.exp(cols[:, 2 * pidx + 1:2 * pidx + 2])
            first = lane < M_HEADDIM
            y_pairs.append(jnp.where(first, yw[0], yw[1]) + ych * jnp.where(first, e0, e1))
            top = rowi < M_HEADDIM
            tail_m = jnp.where(top, jnp.broadcast_to(tail_t[2 * pidx:2 * pidx + 1, :], (L, L)),
                               jnp.broadcast_to(tail_t[2 * pidx + 1:2 * pidx + 2, :], (L, L)))
            dec_m = jnp.where(top, jnp.exp(cum_last[2 * pidx:2 * pidx + 1, :]),
                              jnp.exp(cum_last[2 * pidx + 1:2 * pidx + 2, :]))
            xt = xs_pair.T * tail_m
            h_ref[pidx] = h0 * dec_m + _dot(xt.astype(BF16), bmb)

    y = jnp.concatenate(y_pairs, axis=1)
    o_ref[...] = _ssd_epilogue(y, xs, z_ref[...], dskip_ref[...], normw_ref[...])
    st_ref[0] = h_ref[...]


def ssd_prompt(xz, dt_rows, conv_w, conv_b, nega, dskip_e, normw, rest, n_batch, seq):
    L = SSD_CHUNK
    nc = seq // L
    pairs = M_HEADS // 2
    const = lambda b, c: (0, 0)
    return pl.pallas_call(
        _ssd_prompt_kernel,
        grid=(n_batch, nc),
        in_specs=[pl.BlockSpec((L, CONV_CH), lambda b, c: (b * nc + c, 0)),
                  pl.BlockSpec((L, M_INNER), lambda b, c: (b * nc + c, CONV_CH // M_INNER)),
                  pl.BlockSpec((M_HEADS, L), lambda b, c: (1, b * nc + c)),
                  pl.BlockSpec((CONV_W, CONV_CH), const),
                  pl.BlockSpec((1, CONV_CH), const),
                  pl.BlockSpec((M_HEADS, 1), const),
                  pl.BlockSpec((1, M_INNER), const),
                  pl.BlockSpec((1, M_INNER), const),
                  pl.BlockSpec(memory_space=pl.ANY)],
        out_specs=[pl.BlockSpec((L, M_INNER), lambda b, c: (b * nc + c, 0)),
                   pl.BlockSpec((1, pairs, LANES, D_STATE), lambda b, c: (b, 0, 0, 0))],
        out_shape=[jax.ShapeDtypeStruct(rest.shape, F32),
                   jax.ShapeDtypeStruct((n_batch, pairs, LANES, D_STATE), F32)],
        input_output_aliases={8: 0},
        scratch_shapes=[pltpu.VMEM((L + SUBLANES, CONV_CH), F32),
                        pltpu.VMEM((pairs, LANES, D_STATE), F32)],
        compiler_params=_params(("parallel", "arbitrary")),
        name="ssd_prompt",
    )(xz, xz, dt_rows, conv_w, conv_b, nega, dskip_e, normw, rest)


def _ssd_sample_kernel(x_ref, wdt_ref, dtb_ref, nega_ref, xbc_ref, ctx_ref, z_ref, cw_ref, cb_ref,
                       dskip_ref, normw_ref, h0_ref, o_ref, hn_ref,
                       u_ref, coef_t_ref, dec_t_ref, dec_ref, dtx_ref, *, seqs_per_step):
    sb = seqs_per_step
    i = pl.program_id(0)
    nseq = x_ref.shape[0]

    @pl.when(i == 0)
    def _():
        acc = xbc_ref[...] * cw_ref[CONV_W - 1:CONV_W, :]
        for j in range(CONV_W - 1):
            acc = acc + ctx_ref[j] * cw_ref[j:j + 1, :]
        u = _silu(acc + cb_ref[...])
        u_ref[...] = u
        dt = _softplus(_dot(x_ref[...], wdt_ref[...], precision=HIGHEST) + dtb_ref[...])
        dec = jnp.exp(dt * nega_ref[...])
        coef = dt * u[:, :M_INNER]
        dec_ref[...] = dec
        dtx_ref[...] = coef
        for blk in range(M_INNER // LANES):
            sl = slice(blk * LANES, (blk + 1) * LANES)
            coef_t_ref[sl, :] = coef[:, sl].T
            dec_t_ref[sl, :] = dec[:, sl].T

    base = pl.multiple_of(i * sb, sb)
    ub = u_ref[pl.ds(base, sb), :]
    lane_seq = _iota((M_INNER, nseq), 1)
    rows = _iota((sb, M_INNER // M_GROUPS), 0)
    ch = [jnp.zeros((sb, M_INNER // M_GROUPS), F32) for _ in range(M_GROUPS)]
    for s in range(sb):
        onehot = lane_seq == base + s
        cx = jnp.sum(jnp.where(onehot, coef_t_ref[...], 0.0), axis=1, keepdims=True)
        dc = jnp.sum(jnp.where(onehot, dec_t_ref[...], 0.0), axis=1, keepdims=True)
        for g in range(M_GROUPS):
            brow = ub[s:s + 1, M_INNER + g * D_STATE:M_INNER + (g + 1) * D_STATE]
            cblk = ub[:, M_INNER + BC_W + g * D_STATE:M_INNER + BC_W + (g + 1) * D_STATE]
            hg = h0_ref[s, g * HPG:(g + 1) * HPG].reshape(HPG * M_HEADDIM, D_STATE)
            r = _nt_dot(cblk.astype(BF16), hg.astype(BF16))
            ch[g] = ch[g] + jnp.where(rows == s, r, 0.0)
            lo = g * HPG * M_HEADDIM
            hn = hg * dc[lo:lo + HPG * M_HEADDIM] + cx[lo:lo + HPG * M_HEADDIM] * brow
            hn_ref[s, g * HPG:(g + 1) * HPG] = hn.reshape(HPG, M_HEADDIM, D_STATE)

    xs = ub[:, :M_INNER]
    dec = dec_ref[pl.ds(base, sb), :]
    coef = dtx_ref[pl.ds(base, sb), :]
    ys = []
    half = M_INNER // M_GROUPS
    for g in range(M_GROUPS):
        bm = ub[:, M_INNER + g * D_STATE:M_INNER + (g + 1) * D_STATE]
        cm = ub[:, M_INNER + BC_W + g * D_STATE:M_INNER + BC_W + (g + 1) * D_STATE]
        cb = jnp.sum(cm * bm, axis=1, keepdims=True)
        ys.append(cb * coef[:, g * half:(g + 1) * half] + ch[g] * dec[:, g * half:(g + 1) * half])
    y = jnp.concatenate(ys, axis=1)
    o_ref[...] = _ssd_epilogue(y, xs, z_ref[...], dskip_ref[...], normw_ref[...])


def ssd_sample(x_s, wdt_e, dtb_e, nega_e, xbc_s, ctx, z_s, conv_w, conv_b, dskip_e, normw, h0, seqs_per_step):
    bd = x_s.shape[0]
    sb = seqs_per_step
    const = lambda i: (0, 0)
    return pl.pallas_call(
        functools.partial(_ssd_sample_kernel, seqs_per_step=sb),
        grid=(bd // sb,),
        in_specs=[pl.BlockSpec((bd, D_MODEL), const),
                  pl.BlockSpec((D_MODEL, M_INNER), const),
                  pl.BlockSpec((1, M_INNER), const),
                  pl.BlockSpec((1, M_INNER), const),
                  pl.BlockSpec((bd, CONV_CH), const),
                  pl.BlockSpec((CONV_W - 1, bd, CONV_CH), lambda i: (0, 0, 0)),
                  pl.BlockSpec((sb, M_INNER), lambda i: (i, 0)),
                  pl.BlockSpec((CONV_W, CONV_CH), const),
                  pl.BlockSpec((1, CONV_CH), const),
                  pl.BlockSpec((1, M_INNER), const),
                  pl.BlockSpec((1, M_INNER), const),
                  pl.BlockSpec((sb, M_HEADS, M_HEADDIM, D_STATE), lambda i: (i, 0, 0, 0))],
        out_specs=[pl.BlockSpec((sb, M_INNER), lambda i: (i, 0)),
                   pl.BlockSpec((sb, M_HEADS, M_HEADDIM, D_STATE), lambda i: (i, 0, 0, 0))],
        out_shape=[jax.ShapeDtypeStruct((bd, M_INNER), F32),
                   jax.ShapeDtypeStruct((bd, M_HEADS, M_HEADDIM, D_STATE), F32)],
        scratch_shapes=[pltpu.VMEM((bd, CONV_CH), F32),
                        pltpu.VMEM((M_INNER, bd), F32),
                        pltpu.VMEM((M_INNER, bd), F32),
                        pltpu.VMEM((bd, M_INNER), F32),
                        pltpu.VMEM((bd, M_INNER), F32)],
        compiler_params=_params(("arbitrary",)),
        name="ssd_sample",
    )(x_s, wdt_e, dtb_e, nega_e, xbc_s, ctx, z_s, conv_w, conv_b, dskip_e, normw, h0)


def _to_half(x, src_half, dst_half):
    return x if src_half == dst_half else pltpu.roll(x, HEAD_DIM, 1)


def _swa_prompt_kernel(sink_ref, q_ref, kp_ref, kc_ref, vp_ref, vc_ref, rest_ref, o_ref):
    del rest_ref
    n = pl.program_id(1)
    W = WINDOW
    scale = HEAD_DIM ** -0.5
    lane = _iota((W, LANES), 1)
    rows4 = _iota((SW_GROUP * W, 2 * W), 0)
    t = rows4 % W
    col = _iota((SW_GROUP * W, 2 * W), 1)
    valid = (col >= t) & (col <= t + W) & ((n > 0) | (col >= W))
    rcol = _iota((SW_GROUP * W, 1), 0) // W
    outs = [None] * SW_HEADS
    for j in range(SW_KV_HEADS):
        ch, hf = j // 2, j % 2
        kk = jnp.concatenate([kp_ref[:, ch * LANES:(ch + 1) * LANES],
                              kc_ref[:, ch * LANES:(ch + 1) * LANES]], axis=0).astype(BF16)
        vv = jnp.concatenate([vp_ref[:, ch * LANES:(ch + 1) * LANES],
                              vc_ref[:, ch * LANES:(ch + 1) * LANES]], axis=0).astype(BF16)
        qs = []
        sink = jnp.zeros((SW_GROUP * W, 1), F32)
        for g in range(SW_GROUP):
            hq = j * SW_GROUP + g
            qc = q_ref[:, (hq // 2) * LANES:(hq // 2 + 1) * LANES] * scale
            qc = _to_half(qc, hq % 2, hf)
            keep = (lane < HEAD_DIM) if hf == 0 else (lane >= HEAD_DIM)
            qs.append(jnp.where(keep, qc, 0.0))
            sink = jnp.where(rcol == g, sink_ref[hq], sink)
        q4 = jnp.concatenate(qs, axis=0).astype(BF16)
        s = jnp.where(valid, _nt_dot(q4, kk), NEG_INF)
        m = jnp.maximum(jnp.max(s, axis=1, keepdims=True), sink)
        p = jnp.exp(s - m)
        den = jnp.sum(p, axis=1, keepdims=True) + jnp.exp(sink - m)
        o = _dot(p.astype(BF16), vv) / den
        for g in range(SW_GROUP):
            hq = j * SW_GROUP + g
            outs[hq] = _to_half(o[g * W:(g + 1) * W], hf, hq % 2)
    for c in range(SW_HEADS // 2):
        o_ref[:, c * LANES:(c + 1) * LANES] = jnp.where(lane < HEAD_DIM, outs[2 * c], outs[2 * c + 1])


def swa_prompt(sinks, u_odd, rest, n_batch, seq):
    nb = seq // WINDOW
    kcol = ODD_MIX // KV_W
    return pl.pallas_call(
        _swa_prompt_kernel,
        grid=(n_batch, nb),
        in_specs=[pl.BlockSpec(memory_space=pltpu.SMEM),
                  pl.BlockSpec((WINDOW, ODD_MIX), lambda b, n: (b * nb + n, 0)),
                  pl.BlockSpec((WINDOW, KV_W), lambda b, n: (b * nb + jnp.maximum(n - 1, 0), kcol)),
                  pl.BlockSpec((WINDOW, KV_W), lambda b, n: (b * nb + n, kcol)),
                  pl.BlockSpec((WINDOW, KV_W), lambda b, n: (b * nb + jnp.maximum(n - 1, 0), kcol + 1)),
                  pl.BlockSpec((WINDOW, KV_W), lambda b, n: (b * nb + n, kcol + 1)),
                  pl.BlockSpec(memory_space=pl.ANY)],
        out_specs=pl.BlockSpec((WINDOW, ODD_MIX), lambda b, n: (b * nb + n, 0)),
        out_shape=jax.ShapeDtypeStruct(rest.shape, F32),
        input_output_aliases={6: 0},
        compiler_params=_params(("parallel", "arbitrary")),
        name="swa_prompt",
    )(sinks, u_odd, u_odd, u_odd, u_odd, u_odd, rest)


def _swa_sample_kernel(sink_ref, q_ref, kn_ref, vn_ref, bk_ref, bv_ref, o_ref, *, seqs_per_step):
    scale = HEAD_DIM ** -0.5
    rowg = _iota((SW_HEADS, HEAD_DIM), 0) // SW_GROUP
    sink = sink_ref[...]
    for s in range(seqs_per_step):
        q16 = q_ref[s] * scale
        qbd = jnp.concatenate([jnp.where(rowg == j, q16, 0.0) for j in range(SW_KV_HEADS)], axis=1)
        kb = bk_ref[s].reshape(KV_W, -1).astype(BF16)
        vb = bv_ref[s].reshape(KV_W, -1).astype(BF16)
        sc = _dot(qbd.astype(BF16), kb)
        s_new = jnp.sum(qbd * kn_ref[s:s + 1, :], axis=1, keepdims=True)
        m = jnp.maximum(jnp.maximum(jnp.max(sc, axis=1, keepdims=True), s_new), sink)
        p = jnp.exp(sc - m)
        pn = jnp.exp(s_new - m)
        den = jnp.sum(p, axis=1, keepdims=True) + pn + jnp.exp(sink - m)
        full = (_nt_dot(p.astype(BF16), vb) + pn * vn_ref[s:s + 1, :]) / den
        o16 = jnp.zeros((SW_HEADS, HEAD_DIM), F32)
        for j in range(SW_KV_HEADS):
            o16 = o16 + jnp.where(rowg == j, full[:, j * HEAD_DIM:(j + 1) * HEAD_DIM], 0.0)
        o_ref[s] = o16


def swa_sample(sinks_col, q, kn, vn, buf_k, buf_v, seqs_per_step):
    bd, _, _, lw = buf_k.shape
    sb = seqs_per_step
    return pl.pallas_call(
        functools.partial(_swa_sample_kernel, seqs_per_step=sb),
        grid=(bd // sb,),
        in_specs=[pl.BlockSpec((SW_HEADS, 1), lambda i: (0, 0)),
                  pl.BlockSpec((sb, SW_HEADS, HEAD_DIM), lambda i: (i, 0, 0)),
                  pl.BlockSpec((sb, KV_W), lambda i: (i, 0)),
                  pl.BlockSpec((sb, KV_W), lambda i: (i, 0)),
                  pl.BlockSpec((sb, SW_KV_HEADS, HEAD_DIM, lw), lambda i: (i, 0, 0, 0)),
                  pl.BlockSpec((sb, SW_KV_HEADS, HEAD_DIM, lw), lambda i: (i, 0, 0, 0))],
        out_specs=pl.BlockSpec((sb, SW_HEADS, HEAD_DIM), lambda i: (i, 0, 0)),
        out_shape=jax.ShapeDtypeStruct((bd, SW_HEADS, HEAD_DIM), F32),
        compiler_params=_params(("parallel",)),
        name="swa_sample",
    )(sinks_col, q, kn, vn, buf_k, buf_v)


def _layer_norm(h, g, b):
    mu = jnp.mean(h, axis=1, keepdims=True)
    d = h - mu
    var = jnp.mean(d * d, axis=1, keepdims=True)
    return d * lax.rsqrt(var + LN_EPS) * g + b


def _mix_route_kernel(*refs, alpha, n_feat):
    am_refs, wo_refs = refs[:n_feat], refs[n_feat:2 * n_feat]
    x_ref, g_ref, b_ref, wr_ref, br_ref, x1_ref, rw_ref, re_ref = refs[2 * n_feat:]
    mix = _dot(am_refs[0][...].astype(BF16), wo_refs[0][...])
    for a_ref, w_ref in zip(am_refs[1:], wo_refs[1:]):
        mix = mix + _dot(a_ref[...].astype(BF16), w_ref[...])
    x1 = _layer_norm(alpha * x_ref[...] + mix, g_ref[...], b_ref[...])
    x1_ref[...] = x1
    wr = wr_ref[...]
    x_hi = x1.astype(BF16)
    w_hi = wr.astype(BF16)
    x_lo = (x1 - x_hi.astype(F32)).astype(BF16)
    w_lo = (wr - w_hi.astype(F32)).astype(BF16)
    logits = _dot(x_hi, w_hi) + _dot(x_lo, w_hi) + _dot(x_hi, w_lo) + br_ref[...]
    lane_i = _iota(logits.shape, 1)
    lane = lane_i.astype(F32)
    big = float(LANES)
    gl = jnp.where(lane_i < N_EXPERT_GROUPS, logits, NEG_INF)
    gmax = jnp.max(gl, axis=1, keepdims=True)
    grp = jnp.min(jnp.where(gl == gmax, lane, big), axis=1, keepdims=True)
    pg = 1.0 / jnp.sum(jnp.exp(gl - gmax), axis=1, keepdims=True)
    rel = lane - (N_EXPERT_GROUPS + grp * EXPERTS_PER_GROUP)
    el = jnp.where(rel >= 0.0, jnp.where(rel < EXPERTS_PER_GROUP, logits, NEG_INF), NEG_INF)
    v1 = jnp.max(el, axis=1, keepdims=True)
    i1 = jnp.min(jnp.where(el == v1, lane, big), axis=1, keepdims=True)
    el2 = jnp.where(lane == i1, NEG_INF, el)
    v2 = jnp.max(el2, axis=1, keepdims=True)
    i2 = jnp.min(jnp.where(el2 == v2, lane, big), axis=1, keepdims=True)
    e = jnp.exp(v2 - v1)
    w1 = pg / (1.0 + e)
    w2 = w1 * e
    rw_ref[...] = jnp.where(lane_i == 0, w1, jnp.where(lane_i == 1, w2, 0.0))
    e1 = (i1 - N_EXPERT_GROUPS).astype(jnp.int32)
    e2 = (i2 - N_EXPERT_GROUPS).astype(jnp.int32)
    re_ref[...] = jnp.where(lane_i == 0, e1, jnp.where(lane_i == 1, e2, 0))


def mix_route(feats, wos, x, g, b, wr, br, alpha, tm):
    m, d = x.shape
    const = lambda i: (0, 0)
    row = lambda i: (i, 0)
    return pl.pallas_call(
        functools.partial(_mix_route_kernel, alpha=alpha, n_feat=len(feats)),
        grid=(m // tm,),
        in_specs=[pl.BlockSpec((tm, a.shape[1]), row) for a in feats]
        + [pl.BlockSpec((w.shape[0], d), const) for w in wos]
        + [pl.BlockSpec((tm, d), row), pl.BlockSpec((1, d), const), pl.BlockSpec((1, d), const),
           pl.BlockSpec((d, LANES), const), pl.BlockSpec((1, LANES), const)],
        out_specs=[pl.BlockSpec((tm, d), row),
                   pl.BlockSpec((tm, LANES), row), pl.BlockSpec((tm, LANES), row)],
        out_shape=[jax.ShapeDtypeStruct((m, d), F32),
                   jax.ShapeDtypeStruct((m, LANES), F32), jax.ShapeDtypeStruct((m, LANES), jnp.int32)],
        compiler_params=_params(("parallel",)),
        name="mix_route",
    )(*feats, *wos, x, g, b, wr, br)


def _experts_kernel(te_ref, tv_ref, x_ref, wg_ref, wu_ref, wd_ref, y_ref):
    t = pl.program_id(0)

    @pl.when(tv_ref[t] > 0)
    def _():
        x = x_ref[...].astype(BF16)
        h = _silu(_dot(x, wg_ref[0, 0].astype(BF16))) * _dot(x, wu_ref[0, 0].astype(BF16))
        y_ref[...] = _dot(h.astype(BF16), wd_ref[0, 0].astype(BF16))

    @pl.when(tv_ref[t] == 0)
    def _():
        y_ref[...] = jnp.zeros_like(y_ref)


def experts(tile_expert, tile_valid, xs, wg, wu, wd, layer, te):
    r, d = xs.shape
    ff = wg.shape[3]
    grid_spec = pltpu.PrefetchScalarGridSpec(
        num_scalar_prefetch=2,
        grid=(r // te,),
        in_specs=[pl.BlockSpec((te, d), lambda t, e, v: (t, 0)),
                  pl.BlockSpec((1, 1, d, ff), lambda t, e, v: (layer, e[t], 0, 0)),
                  pl.BlockSpec((1, 1, d, ff), lambda t, e, v: (layer, e[t], 0, 0)),
                  pl.BlockSpec((1, 1, ff, d), lambda t, e, v: (layer, e[t], 0, 0))],
        out_specs=pl.BlockSpec((te, d), lambda t, e, v: (t, 0)),
    )
    return pl.pallas_call(
        _experts_kernel,
        grid_spec=grid_spec,
        out_shape=jax.ShapeDtypeStruct((r, d), F32),
        compiler_params=_params(("arbitrary",)),
        name="experts",
    )(tile_expert, tile_valid, xs, wg, wu, wd)


def _combine_ple_kernel(x1_ref, y0_ref, y1_ref, rw_ref, pp_ref, pt_ref, g_ref, b_ref, wg_ref, bg_ref, wp_ref,
                        o_ref, *, alpha, prompt_tiles):
    rw = rw_ref[...]
    f = rw[:, 0:1] * y0_ref[...] + rw[:, 1:2] * y1_ref[...]
    x2 = _layer_norm(alpha * x1_ref[...] + f, g_ref[...], b_ref[...])
    gl = _dot(x2.astype(BF16), wg_ref[...]) + bg_ref[...]
    gate = 1.0 / (1.0 + jnp.exp(-gl))
    p = jnp.where(pl.program_id(0) < prompt_tiles, pp_ref[0], pt_ref[...])
    o_ref[...] = x2 + gate * _dot(p.astype(BF16), wp_ref[...])


def combine_ple(x1, y0, y1, rw, p_prompt, p_tail, layer, g, b, wg, bg, wp, alpha, tm):
    m, d = x1.shape
    pd = p_prompt.shape[2]
    prompt_tiles = p_prompt.shape[1] // tm
    const = lambda i: (0, 0)
    row = lambda i: (i, 0)
    return pl.pallas_call(
        functools.partial(_combine_ple_kernel, alpha=alpha, prompt_tiles=prompt_tiles),
        grid=(m // tm,),
        in_specs=[pl.BlockSpec((tm, d), row), pl.BlockSpec((tm, d), row), pl.BlockSpec((tm, d), row),
                  pl.BlockSpec((tm, LANES), row),
                  pl.BlockSpec((1, tm, pd), lambda i: (layer, jnp.minimum(i, prompt_tiles - 1), 0)),
                  pl.BlockSpec((tm, pd), lambda i: (jnp.maximum(i - prompt_tiles, 0), 0)),
                  pl.BlockSpec((1, d), const), pl.BlockSpec((1, d), const),
                  pl.BlockSpec((d, d), const), pl.BlockSpec((1, d), const), pl.BlockSpec((pd, d), const)],
        out_specs=pl.BlockSpec((tm, d), row),
        out_shape=jax.ShapeDtypeStruct((m, d), F32),
        compiler_params=_params(("parallel",)),
        name="combine_ple",
    )(x1, y0, y1, rw, p_prompt, p_tail, g, b, wg, bg, wp)


def _tiles(n_tokens):
    tm = 512 if n_tokens >= 4096 else 128
    return tm, ((n_tokens + tm - 1) // tm) * tm


def _expert_tile(n_tokens):
    return 256 if n_tokens >= 4096 else 32


def _channel_and_ple(x, feats, p_prompt, p_tail, li, w_out, w, tm):
    depth = w["ln_mix_g"].shape[0]
    alpha = (2 * depth) ** 0.25
    ntp, d = x.shape
    wr = jnp.concatenate([w["w_router_group"][li],
                          jnp.moveaxis(w["w_router_expert"][li], 0, 1).reshape(d, N_EXPERTS)], axis=1)
    wr = jnp.pad(wr, ((0, 0), (0, LANES - wr.shape[1])))
    br = jnp.concatenate([w["b_router_group"][li], w["b_router_expert"][li].reshape(-1)])
    br = jnp.pad(br, (0, LANES - br.shape[0]))[None, :]
    wo = w_out.astype(BF16)
    splits = [0]
    for a in feats:
        splits.append(splits[-1] + a.shape[1])
    x1, rw, re = mix_route(feats, [wo[lo:hi] for lo, hi in zip(splits[:-1], splits[1:])], x,
                           w["ln_mix_g"][li][None], w["ln_mix_b"][li][None], wr, br, alpha, tm)

    te = _expert_tile(ntp)
    n_flat = 2 * ntp
    flat = re[:, :2].reshape(-1)
    onehot = (flat[:, None] == jnp.arange(N_EXPERTS, dtype=jnp.int32)[None, :]).astype(jnp.int32)
    running = jnp.cumsum(onehot, axis=0)
    counts = running[-1]
    padded = ((counts + te - 1) // te) * te
    gend = jnp.cumsum(padded)
    gstart = gend - padded
    pos_flat = jnp.sum(onehot * (running - 1 + gstart[None, :]), axis=1)
    n_rows = ((n_flat + N_EXPERTS * (te - 1) + te - 1) // te) * te
    row_token = (jnp.arange(n_rows, dtype=jnp.int32) % ntp).at[pos_flat].set(
        jnp.arange(n_flat, dtype=jnp.int32) // 2, mode="promise_in_bounds", unique_indices=True)
    pos_flat = pos_flat.reshape(ntp, 2)
    tile_start = jnp.arange(n_rows // te, dtype=jnp.int32) * te
    tile_expert = jnp.minimum(jnp.sum((gend[None, :] <= tile_start[:, None]).astype(jnp.int32), axis=1),
                              N_EXPERTS - 1)
    tile_valid = (tile_start < gend[-1]).astype(jnp.int32)

    def rows(a, idx):
        return a.at[idx].get(mode="promise_in_bounds")

    xs = rows(x1, row_token)
    y = experts(tile_expert, tile_valid, xs, w["w_exp_gate"], w["w_exp_up"], w["w_exp_down"], li, te)
    y0 = rows(y, pos_flat[:, 0])
    y1 = rows(y, pos_flat[:, 1])
    return combine_ple(x1, y0, y1, rw, p_prompt, p_tail, li, w["ln_ffn_g"][li][None], w["ln_ffn_b"][li][None],
                       w["w_ple_gate"][li].astype(BF16), w["b_ple_gate"][li][None],
                       w["w_ple_proj"][li].astype(BF16), alpha, tm)


def kernel(x_prompt, x_sample, p_prompt, p_sample, cache_fox_k, cache_fox_v, cache_fox_logf, state_ssm, state_conv, cache_win_k, cache_win_v, page_table, w_in_even, b_fgate, conv_w, conv_b, dt_bias, a_log, d_skip, ssm_norm_w, w_out_even, w_in_odd, attn_sinks, w_out_odd, ln_mix_g, ln_mix_b, ln_ffn_g, ln_ffn_b, w_router_group, b_router_group, w_router_expert, b_router_expert, w_exp_gate, w_exp_up, w_exp_down, w_ple_proj, w_ple_gate, b_ple_gate):
    bp, seq, d = x_prompt.shape
    bd, t_dec, _ = x_sample.shape
    assert t_dec == 1 and d == D_MODEL
    depth = p_prompt.shape[0]
    n_pages = page_table.shape[1]
    past_len = n_pages * PAGE_SIZE
    np_tok = bp * seq
    nt = np_tok + bd
    tm, ntp = _tiles(nt)
    pad = ntp - nt

    def tokens(a_p, a_s):
        parts = [a_p.reshape(np_tok, -1), a_s.reshape(bd, -1)]
        if pad:
            parts.append(jnp.zeros((pad, parts[0].shape[1]), parts[0].dtype))
        return jnp.concatenate(parts, axis=0)

    def past_prompt(a_s):
        a_s = a_s.reshape(bd, -1)
        return jnp.zeros((ntp, a_s.shape[1]), a_s.dtype).at[np_tok:nt].set(a_s)

    assert np_tok % tm == 0 and seq >= CONV_W - 1
    x = tokens(x_prompt, x_sample)
    p_all = p_prompt.reshape(depth, np_tok, -1)
    shared = dict(ln_mix_g=ln_mix_g, ln_mix_b=ln_mix_b, ln_ffn_g=ln_ffn_g, ln_ffn_b=ln_ffn_b,
                  w_router_group=w_router_group, b_router_group=b_router_group,
                  w_router_expert=w_router_expert, b_router_expert=b_router_expert,
                  w_exp_gate=w_exp_gate, w_exp_up=w_exp_up, w_exp_down=w_exp_down,
                  w_ple_proj=w_ple_proj, w_ple_gate=w_ple_gate, b_ple_gate=b_ple_gate)

    half = HEAD_DIM // 2
    inv = jnp.exp(-math.log(ROPE_THETA) * jnp.arange(half, dtype=F32) / half)
    pos = jnp.concatenate([jnp.tile(jnp.arange(seq, dtype=jnp.int32), bp),
                           jnp.full((bd,), past_len, jnp.int32), jnp.zeros((pad,), jnp.int32)])
    ang = pos.astype(F32)[:, None] * inv[None, :]
    cos_t = jnp.tile(jnp.cos(ang), (1, LANES // half))
    sin_t = jnp.tile(jnp.concatenate([-jnp.sin(ang), jnp.sin(ang)], axis=1), (1, LANES // HEAD_DIM))

    even_p, even_s, odd_p, odd_s = [], [], [], []
    for li in range(depth):
        j = li // 2
        if li % 2 == 0:
            wi = w_in_even[j]
            c0 = 3 * FOX_WIDTH
            c1 = c0 + FOX_HEADS
            c2 = c1 + M_INNER
            c3 = c2 + CONV_CH
            w_main = jnp.concatenate([wi[:, :c0], wi[:, c2:c3], wi[:, c1:c2]], axis=1).astype(BF16)
            w_small_t = jnp.concatenate([wi[:, c0:c1], wi[:, c3:]], axis=1).T
            b_small = jnp.concatenate([b_fgate[j], dt_bias[j]])[:, None]
            prompt_tiles = np_tok // tm
            k_p, v_p, xz, qkv, _ = even_proj(x, w_main, tm, 0, prompt_tiles)
            k_t, v_t, xz_t, _, q_t = even_proj(x, w_main, tm, prompt_tiles, ntp // tm - prompt_tiles)
            small = small_proj(x, w_small_t, b_small, tm)

            tq = min(FOX_TILE, seq)
            logf_p = small[:FOX_HEADS, :np_tok].reshape(FOX_HEADS, bp, seq)
            cum = cumsum_lanes(jnp.moveaxis(logf_p, 1, 0).reshape(bp * FOX_HEADS, seq), min(512, seq))
            ck = cum.reshape(bp, FOX_HEADS // 2, 2, seq // tq, tq).transpose(0, 1, 3, 2, 4)
            logf_s = small[:FOX_HEADS, np_tok:nt].T
            q_s = q_t[:bd].reshape(bd, FOX_HEADS, HEAD_DIM)
            k_s = k_t[:bd].reshape(bd, FOX_HEADS, HEAD_DIM)
            v_s = v_t[:bd].reshape(bd, FOX_HEADS, HEAD_DIM)
            eye = jnp.eye(FOX_HEADS, dtype=F32)[None, :, :, None]

            def block_diag(a):
                return (a[:, :, None, :] * eye).reshape(bd, FOX_HEADS, FOX_WIDTH)

            a_s = fox_sample(page_table, block_diag(q_s), q_s, k_s, block_diag(v_s), logf_s[:, :, None],
                             jnp.transpose(cache_fox_k, (0, 1, 3, 4, 2)), jnp.transpose(cache_fox_v, (0, 1, 3, 4, 2)),
                             jnp.transpose(cache_fox_logf, (0, 1, 3, 2)), j, min(16, n_pages))
            a_all = fox_prompt(qkv, ck, past_prompt(a_s), bp, seq, tq)

            nega = -jnp.exp(a_log[j])
            dskip_e = jnp.repeat(d_skip[j], M_HEADDIM)[None, :]
            normw = ssm_norm_w[j][None, :]
            xz_s = xz_t[:bd]
            m_s, st_s = ssd_sample(x[np_tok:nt], jnp.repeat(wi[:, c3:], M_HEADDIM, axis=1),
                                   jnp.repeat(dt_bias[j], M_HEADDIM)[None, :], jnp.repeat(nega, M_HEADDIM)[None, :],
                                   xz_s[:, :CONV_CH], jnp.moveaxis(state_conv[j], 1, 0), xz_s[:, CONV_CH:],
                                   conv_w[j], conv_b[j][None, :], dskip_e, normw, state_ssm[j], min(8, bd))
            m_all, st_p = ssd_prompt(xz, small, conv_w[j], conv_b[j][None, :], nega[:, None], dskip_e, normw,
                                     past_prompt(m_s), bp, seq)
            feats = [a_all, m_all]
            w_out = w_out_even[j]

            kp = k_p.reshape(bp, seq, FOX_HEADS, HEAD_DIM)
            vp = v_p.reshape(bp, seq, FOX_HEADS, HEAD_DIM)
            last = jnp.stack([xz[(b + 1) * seq - (CONV_W - 1):(b + 1) * seq, :CONV_CH] for b in range(bp)])
            conv_p = jnp.concatenate([jnp.zeros((bp, CONV_W - 1, CONV_CH), F32), last], axis=1)[:, -(CONV_W - 1):]
            even_p.append((kp, vp, jnp.moveaxis(logf_p, 0, 2),
                           st_p.reshape(bp, M_HEADS, M_HEADDIM, D_STATE), conv_p))
            conv_s = jnp.concatenate([state_conv[j], xz_s[:, None, :CONV_CH]], axis=1)[:, -(CONV_W - 1):]
            even_s.append((k_s[:, None], v_s[:, None], logf_s[:, None, :], st_s, conv_s))
        else:
            u = matmul_rope(x, w_in_odd[j].astype(BF16), cos_t, sin_t, tm, 256, ODD_MIX + KV_W)
            u_s = u[np_tok:nt]
            lw = cache_win_k.shape[2]
            kn = u_s[:, ODD_MIX:ODD_MIX + KV_W]
            vn = u_s[:, ODD_MIX + KV_W:]
            o_s = swa_sample(attn_sinks[j][:, None], u_s[:, :ODD_MIX].reshape(bd, SW_HEADS, HEAD_DIM), kn, vn,
                             jnp.transpose(cache_win_k[j], (0, 2, 3, 1)), jnp.transpose(cache_win_v[j], (0, 2, 3, 1)),
                             min(8, bd))
            feats = [swa_prompt(attn_sinks[j], u, past_prompt(o_s), bp, seq)]
            w_out = w_out_odd[j]

            rows = min(WINDOW, seq)
            tail = jnp.stack([u[(b + 1) * seq - rows:(b + 1) * seq, ODD_MIX:] for b in range(bp)])
            odd_p.append((tail[:, :, :KV_W].reshape(bp, rows, SW_KV_HEADS, HEAD_DIM),
                          tail[:, :, KV_W:].reshape(bp, rows, SW_KV_HEADS, HEAD_DIM)))
            ka = jnp.concatenate([cache_win_k[j], kn.reshape(bd, 1, SW_KV_HEADS, HEAD_DIM)], axis=1)[:, -lw:]
            va = jnp.concatenate([cache_win_v[j], vn.reshape(bd, 1, SW_KV_HEADS, HEAD_DIM)], axis=1)[:, -lw:]
            odd_s.append((ka, va))
        x = _channel_and_ple(x, feats, p_all, past_prompt(p_sample[li])[np_tok:], li, w_out, shared, tm)

    yp = x[:np_tok].reshape(bp, seq, d)
    ys = x[np_tok:nt].reshape(bd, 1, d)
    return (yp, ys,
            jnp.stack([st[0] for st in even_p]), jnp.stack([st[1] for st in even_p]),
            jnp.stack([st[2] for st in even_p]), jnp.stack([st[3] for st in even_p]),
            jnp.stack([st[4] for st in even_p]),
            jnp.stack([st[0] for st in odd_p]), jnp.stack([st[1] for st in odd_p]),
            jnp.stack([st[0] for st in even_s]), jnp.stack([st[1] for st in even_s]),
            jnp.stack([st[2] for st in even_s]), jnp.stack([st[3] for st in even_s]),
            jnp.stack([st[4] for st in even_s]),
            jnp.stack([st[0] for st in odd_s]), jnp.stack([st[1] for st in odd_s]))
```

```python
import functools
import math

import jax
import jax.numpy as jnp
from jax import lax
from jax.experimental import pallas as pl
from jax.experimental.pallas import tpu as pltpu

F32 = jnp.float32
BF16 = jnp.bfloat16
HIGHEST = lax.Precision.HIGHEST

D_MODEL = 1024
HEAD_DIM = 64
FOX_HEADS = 8
FOX_WIDTH = FOX_HEADS * HEAD_DIM
M_HEADS = 8
M_HEADDIM = 64
M_INNER = M_HEADS * M_HEADDIM
M_GROUPS = 2
HPG = M_HEADS // M_GROUPS
D_STATE = 128
CONV_W = 4
BC_W = M_GROUPS * D_STATE
CONV_CH = M_INNER + 2 * BC_W
SSD_CHUNK = 128
RMS_EPS = 1e-5
SW_HEADS = 16
SW_KV_HEADS = 4
SW_GROUP = SW_HEADS // SW_KV_HEADS
WINDOW = 128
ROPE_THETA = 10000.0
ODD_MIX = SW_HEADS * HEAD_DIM
KV_W = SW_KV_HEADS * HEAD_DIM
N_EXPERT_GROUPS = 4
EXPERTS_PER_GROUP = 8
N_EXPERTS = N_EXPERT_GROUPS * EXPERTS_PER_GROUP
EXPERT_FF = 512
PLE_DIM = 256
LN_EPS = 1e-5
PAGE_SIZE = 128

LANES = 128
SUBLANES = 8
VMEM_LIMIT = 48 * 1024 * 1024

NEG_INF = float("-inf")
EXPERT_CHUNKS = 2
FOX_TILE = 1024
FOX_Q_STRIP = 1024


def _params(sem, vmem=VMEM_LIMIT):
    return pltpu.CompilerParams(dimension_semantics=sem, vmem_limit_bytes=vmem)


def _nt_dot(a, b, precision=None):
    return lax.dot_general(a, b, (((1,), (1,)), ((), ())), precision=precision,
                           preferred_element_type=F32)


def _dot(a, b, precision=None):
    return jnp.dot(a, b, precision=precision, preferred_element_type=F32)


def _silu(x):
    return x * (1.0 / (1.0 + jnp.exp(-x)))


def _softplus(x):
    return jnp.maximum(x, 0.0) + jnp.log(1.0 + jnp.exp(-jnp.abs(x)))


def _iota(shape, dim):
    return lax.broadcasted_iota(jnp.int32, shape, dim)


def _mm_rope_kernel(x_ref, w_ref, cos_ref, sin_ref, o_ref, *, tn, rope_cols):
    xb = x_ref[...].astype(BF16)
    reps = tn // LANES
    cos = jnp.concatenate([cos_ref[...]] * reps, axis=1)
    sin = jnp.concatenate([sin_ref[...]] * reps, axis=1)
    half = HEAD_DIM // 2
    first = (_iota((xb.shape[0], tn), 1) % HEAD_DIM) < half
    for c in range(w_ref.shape[1] // tn):
        acc = _dot(xb, w_ref[:, c * tn:(c + 1) * tn])
        if c * tn < rope_cols:
            partner = jnp.where(first, pltpu.roll(acc, tn - half, 1), pltpu.roll(acc, half, 1))
            acc = acc * cos + partner * sin
        o_ref[:, c * tn:(c + 1) * tn] = acc


def _even_proj_kernel(x_ref, w_ref, k_ref, v_ref, xz_ref, qkv_ref, qt_ref):
    xb = x_ref[...].astype(BF16)
    w = FOX_WIDTH
    f32_dst = (None, k_ref, v_ref)
    for c in range(3):
        acc = _dot(xb, w_ref[:, c * w:(c + 1) * w])
        qkv_ref[:, c * w:(c + 1) * w] = acc.astype(BF16)
        if f32_dst[c] is not None:
            f32_dst[c][...] = acc
        else:
            @pl.when(pl.program_id(0) == pl.num_programs(0) - 1)
            def _():
                qt_ref[...] = acc
    for c in range(xz_ref.shape[1] // w):
        xz_ref[:, c * w:(c + 1) * w] = _dot(xb, w_ref[:, (3 + c) * w:(4 + c) * w])


def even_proj(x, w, tm, first_tile, n_tiles):
    k = x.shape[1]
    m = n_tiles * tm
    n = w.shape[1]
    wq = 3 * FOX_WIDTH
    row = lambda i: (i, 0)
    return pl.pallas_call(
        _even_proj_kernel,
        grid=(n_tiles,),
        in_specs=[pl.BlockSpec((tm, k), lambda i: (first_tile + i, 0)), pl.BlockSpec((k, n), lambda i: (0, 0))],
        out_specs=[pl.BlockSpec((tm, FOX_WIDTH), row), pl.BlockSpec((tm, FOX_WIDTH), row),
                   pl.BlockSpec((tm, n - wq), row), pl.BlockSpec((tm, wq), row),
                   pl.BlockSpec((tm, FOX_WIDTH), lambda i: (0, 0))],
        out_shape=[jax.ShapeDtypeStruct((m, FOX_WIDTH), F32), jax.ShapeDtypeStruct((m, FOX_WIDTH), F32),
                   jax.ShapeDtypeStruct((m, n - wq), F32), jax.ShapeDtypeStruct((m, wq), BF16),
                   jax.ShapeDtypeStruct((tm, FOX_WIDTH), F32)],
        compiler_params=_params(("arbitrary",)),
        name="even_proj",
    )(x, w)


def matmul_rope(x, w, cos, sin, tm, tn, rope_cols):
    m, k = x.shape
    n = w.shape[1]
    assert n % tn == 0 and rope_cols % tn == 0
    return pl.pallas_call(
        functools.partial(_mm_rope_kernel, tn=tn, rope_cols=rope_cols),
        grid=(m // tm,),
        in_specs=[pl.BlockSpec((tm, k), lambda i: (i, 0)),
                  pl.BlockSpec((k, n), lambda i: (0, 0)),
                  pl.BlockSpec((tm, LANES), lambda i: (i, 0)),
                  pl.BlockSpec((tm, LANES), lambda i: (i, 0))],
        out_specs=pl.BlockSpec((tm, n), lambda i: (i, 0)),
        out_shape=jax.ShapeDtypeStruct((m, n), F32),
        compiler_params=_params(("parallel",)),
        name="matmul_rope",
    )(x, w, cos, sin)


def _small_proj_kernel(x_ref, wt_ref, b_ref, o_ref):
    r = _nt_dot(wt_ref[...], x_ref[...], precision=HIGHEST) + b_ref[...]
    row = _iota(r.shape, 0)
    o_ref[...] = jnp.where(row < FOX_HEADS, -_softplus(-r), _softplus(r))


def small_proj(x, wt, b, tm):
    m, k = x.shape
    return pl.pallas_call(
        _small_proj_kernel,
        grid=(m // tm,),
        in_specs=[pl.BlockSpec((tm, k), lambda i: (i, 0)),
                  pl.BlockSpec((16, k), lambda i: (0, 0)),
                  pl.BlockSpec((16, 1), lambda i: (0, 0))],
        out_specs=pl.BlockSpec((16, tm), lambda i: (0, i)),
        out_shape=jax.ShapeDtypeStruct((16, m), F32),
        compiler_params=_params(("parallel",)),
        name="small_proj",
    )(x, wt, b)


def _cumsum_kernel(x_ref, o_ref, carry_ref):
    @pl.when(pl.program_id(0) == 0)
    def _():
        carry_ref[...] = jnp.zeros_like(carry_ref)

    x = x_ref[...]
    w = x.shape[1]
    tri = (_iota((w, w), 0) <= _iota((w, w), 1)).astype(F32)
    c = _dot(x, tri, precision=HIGHEST) + carry_ref[...]
    o_ref[...] = c
    carry_ref[...] = c[:, w - 1:w]


def cumsum_lanes(x, chunk):
    r, l = x.shape
    return pl.pallas_call(
        _cumsum_kernel,
        grid=(l // chunk,),
        in_specs=[pl.BlockSpec((r, chunk), lambda i: (0, i))],
        out_specs=pl.BlockSpec((r, chunk), lambda i: (0, i)),
        out_shape=jax.ShapeDtypeStruct((r, l), F32),
        scratch_shapes=[pltpu.VMEM((r, 1), F32)],
        compiler_params=_params(("arbitrary",)),
        name="cumsum_lanes",
    )(x)


def _fox_prompt_kernel(q_ref, k_ref, v_ref, ck_ref, rest_ref, o_ref, *, tq):
    del rest_ref
    qi = pl.program_id(2)
    strip = min(FOX_Q_STRIP, tq)
    n_strips = tq // strip
    chains = [(h, r) for h in range(2) for r in range(n_strips)]
    lane = _iota((strip, LANES), 1)
    q_chain = []
    for h, r in chains:
        q = q_ref[r * strip:(r + 1) * strip, :] * (HEAD_DIM ** -0.5)
        keep = (lane < HEAD_DIM) if h == 0 else (lane >= HEAD_DIM)
        q_chain.append(jnp.where(keep, q, jnp.zeros_like(q)))

    lane_kv = _iota((tq, LANES), 1)
    den_lane = (HEAD_DIM, 0)
    own = (jnp.where(lane_kv < HEAD_DIM, 1.0, 0.0).astype(BF16), jnp.where(lane_kv >= HEAD_DIM, 1.0, 0.0).astype(BF16))
    den = tuple(jnp.where(lane_kv == d, 1.0, 0.0).astype(BF16) for d in den_lane)

    def step(j, carry, masked):
        start = pl.multiple_of(j * tq, tq)
        kb = k_ref[pl.ds(start, tq), :]
        vb = v_ref[pl.ds(start, tq), :]
        v_heads = (vb * own[0] + den[0], vb * own[1] + den[1])
        ck = ck_ref[0, 0, j]
        out = []
        for (h, r), qc, (m, acc) in zip(chains, q_chain, carry):
            s = _nt_dot(qc, kb) - ck[h:h + 1, :]
            if masked:
                s = jnp.where(_iota(s.shape, 1) <= _iota(s.shape, 0) + r * strip, s, NEG_INF)
            m_new = jnp.maximum(m, jnp.max(s, axis=1, keepdims=True))
            p = jnp.exp(s - m_new).astype(BF16)
            acc = jnp.exp(m - m_new) * acc + _dot(p, v_heads[h])
            out.append((m_new, acc))
        return tuple(out)

    init1 = (jnp.full((strip, 1), NEG_INF, F32), jnp.zeros((strip, LANES), F32))
    carry = lax.fori_loop(0, qi, lambda j, c: step(j, c, False), (init1,) * len(chains))
    final = step(qi, carry, True)
    for r in range(n_strips):
        a0, a1 = final[r][1], final[n_strips + r][1]
        o0 = a0 / a0[:, den_lane[0]:den_lane[0] + 1]
        o1 = a1 / a1[:, den_lane[1]:den_lane[1] + 1]
        o_ref[r * strip:(r + 1) * strip, :] = jnp.where(lane < HEAD_DIM, o0, o1)


def fox_prompt(qkv, ck, rest, n_batch, seq, tq):
    nq = seq // tq
    pairs = FOX_HEADS // 2
    return pl.pallas_call(
        functools.partial(_fox_prompt_kernel, tq=tq),
        grid=(n_batch, pairs, nq),
        in_specs=[pl.BlockSpec((tq, LANES), lambda b, h, i: (b * nq + i, h)),
                  pl.BlockSpec((seq, LANES), lambda b, h, i: (b, pairs + h)),
                  pl.BlockSpec((seq, LANES), lambda b, h, i: (b, 2 * pairs + h)),
                  pl.BlockSpec((1, 1, nq, 2, tq), lambda b, h, i: (b, h, 0, 0, 0)),
                  pl.BlockSpec(memory_space=pl.ANY)],
        out_specs=pl.BlockSpec((tq, LANES), lambda b, h, i: (b * nq + i, h)),
        out_shape=jax.ShapeDtypeStruct(rest.shape, F32),
        input_output_aliases={4: 0},
        compiler_params=_params(("parallel", "parallel", "arbitrary")),
        name="fox_prompt",
    )(qkv, qkv, qkv, ck, rest)


def _block_diag_rows(full):
    rowh = _iota((FOX_HEADS, HEAD_DIM), 0)
    out = jnp.zeros((FOX_HEADS, HEAD_DIM), F32)
    for h in range(FOX_HEADS):
        out = out + jnp.where(rowh == h, full[:, h * HEAD_DIM:(h + 1) * HEAD_DIM], 0.0)
    return out


def _fox_sample_kernel(pt_ref, qbd_ref, q_ref, kn_ref, vbd_ref, ln_ref, *refs, pages_per_step):
    del pt_ref
    pp = pages_per_step
    k_refs, v_refs, lf_refs = refs[:pp], refs[pp:2 * pp], refs[2 * pp:3 * pp]
    o_ref = refs[3 * pp]
    m_ref, l_ref, acc_ref, carry_ref = refs[3 * pp + 1:]
    t = pl.program_id(1)
    scale = HEAD_DIM ** -0.5

    @pl.when(t == 0)
    def _():
        m_ref[...] = jnp.sum(q_ref[0] * kn_ref[0], axis=1, keepdims=True) * scale
        l_ref[...] = jnp.ones_like(l_ref)
        acc_ref[...] = vbd_ref[0]
        carry_ref[...] = ln_ref[0]

    qb = (qbd_ref[0] * scale).astype(BF16)
    lane = _iota((FOX_HEADS, PAGE_SIZE), 1)
    width = FOX_HEADS * HEAD_DIM
    carry = carry_ref[...]
    scores = []
    for r in range(pp):
        lf = lf_refs[r][0, 0]
        x = lf
        for sh in (1, 2, 4, 8, 16, 32, 64):
            x = x + jnp.where(lane + sh < PAGE_SIZE, pltpu.roll(x, PAGE_SIZE - sh, 1), 0.0)
        kp = k_refs[r][0, 0].reshape(width, PAGE_SIZE).astype(BF16)
        scores.append(_dot(qb, kp) + ((x - lf) + carry))
        carry = carry + x[:, 0:1]
    carry_ref[...] = carry
    s = jnp.concatenate(scores, axis=1)
    m = m_ref[...]
    m_new = jnp.maximum(m, jnp.max(s, axis=1, keepdims=True))
    alpha = jnp.exp(m - m_new)
    p = jnp.exp(s - m_new)
    l_ref[...] = alpha * l_ref[...] + jnp.sum(p, axis=1, keepdims=True)
    m_ref[...] = m_new
    pb = p.astype(BF16)
    acc = alpha * acc_ref[...]
    for r in range(pp):
        vp = v_refs[r][0, 0].reshape(width, PAGE_SIZE).astype(BF16)
        acc = acc + _nt_dot(pb[:, r * PAGE_SIZE:(r + 1) * PAGE_SIZE], vp)
    acc_ref[...] = acc

    @pl.when(t == pl.num_programs(1) - 1)
    def _():
        o_ref[0] = _block_diag_rows(acc / l_ref[...])


def fox_sample(page_table, qbd, q, kn, vbd, ln, cache_kt, cache_vt, cache_lft, layer, pages_per_step):
    bd, n_pages = page_table.shape
    pp = pages_per_step
    steps = n_pages // pp
    width = FOX_HEADS * HEAD_DIM

    def page(b, t, pt, r):
        return pt[b, n_pages - 1 - (t * pp + r)]

    kv_specs = [pl.BlockSpec((1, 1, FOX_HEADS, HEAD_DIM, PAGE_SIZE),
                             functools.partial(lambda b, t, pt, r: (layer, page(b, t, pt, r), 0, 0, 0), r=r))
                for r in range(pp)]
    lf_specs = [pl.BlockSpec((1, 1, FOX_HEADS, PAGE_SIZE),
                             functools.partial(lambda b, t, pt, r: (layer, page(b, t, pt, r), 0, 0), r=r))
                for r in range(pp)]
    tok = pl.BlockSpec((1, FOX_HEADS, HEAD_DIM), lambda b, t, pt: (b, 0, 0))
    wide = pl.BlockSpec((1, FOX_HEADS, width), lambda b, t, pt: (b, 0, 0))
    grid_spec = pltpu.PrefetchScalarGridSpec(
        num_scalar_prefetch=1,
        grid=(bd, steps),
        in_specs=[wide, tok, tok, wide, pl.BlockSpec((1, FOX_HEADS, 1), lambda b, t, pt: (b, 0, 0))]
        + kv_specs + kv_specs + lf_specs,
        out_specs=tok,
        scratch_shapes=[pltpu.VMEM((FOX_HEADS, 1), F32), pltpu.VMEM((FOX_HEADS, 1), F32),
                        pltpu.VMEM((FOX_HEADS, width), F32), pltpu.VMEM((FOX_HEADS, 1), F32)],
    )
    return pl.pallas_call(
        functools.partial(_fox_sample_kernel, pages_per_step=pp),
        grid_spec=grid_spec,
        out_shape=jax.ShapeDtypeStruct((bd, FOX_HEADS, HEAD_DIM), F32),
        compiler_params=_params(("parallel", "arbitrary")),
        name="fox_sample",
    )(page_table, qbd, q, kn, vbd, ln, *([cache_kt] * pp), *([cache_vt] * pp), *([cache_lft] * pp))


def _ssd_epilogue(y, xs, z, dskip_e, normw):
    y = (y + dskip_e * xs) * _silu(z)
    half = M_INNER // M_GROUPS
    outs = []
    for g in range(M_GROUPS):
        yg = y[:, g * half:(g + 1) * half]
        ms = jnp.sum(yg * yg, axis=1, keepdims=True) * (1.0 / half)
        outs.append(yg * lax.rsqrt(ms + RMS_EPS))
    return jnp.concatenate(outs, axis=1) * normw


def _ssd_prompt_kernel(xbc_ref, z_ref, dt_ref, cw_ref, cb_ref, nega_ref, dskip_ref, normw_ref, rest_ref,
                       o_ref, st_ref, ext_ref, h_ref):
    del rest_ref
    c = pl.program_id(1)
    L = SSD_CHUNK
    pad = SUBLANES

    @pl.when(c == 0)
    def _():
        ext_ref[0:pad, :] = jnp.zeros((pad, CONV_CH), F32)
        h_ref[...] = jnp.zeros_like(h_ref)

    ext_ref[pad:pad + L, :] = xbc_ref[...]
    acc = ext_ref[pad:pad + L, :] * cw_ref[CONV_W - 1:CONV_W, :]
    for j in range(CONV_W - 1):
        off = pad - (CONV_W - 1) + j
        acc = acc + ext_ref[off:off + L, :] * cw_ref[j:j + 1, :]
    u = _silu(acc + cb_ref[...])
    ext_ref[0:pad, :] = ext_ref[L:L + pad, :]

    xs = u[:, :M_INNER]
    dt_t = dt_ref[...]
    cum_t = _dot(dt_t * nega_ref[...], (_iota((L, L), 0) <= _iota((L, L), 1)).astype(F32),
                 precision=HIGHEST)
    eye = (_iota((L, L), 0) == _iota((L, L), 1)).astype(F32)
    cols = _nt_dot(eye, jnp.concatenate([cum_t, dt_t], axis=0), precision=HIGHEST)
    cum_last = cum_t[:, L - 1:L]
    tail_t = jnp.exp(cum_last - cum_t) * dt_t
    tril = _iota((L, L), 0) >= _iota((L, L), 1)
    lane = _iota((L, LANES), 1)
    rowi = _iota((L, LANES), 0)

    y_pairs = []
    for g in range(M_GROUPS):
        bm = u[:, M_INNER + g * D_STATE:M_INNER + (g + 1) * D_STATE]
        cm = u[:, M_INNER + BC_W + g * D_STATE:M_INNER + BC_W + (g + 1) * D_STATE]
        bmb = bm.astype(BF16)
        cmb = cm.astype(BF16)
        cb = _nt_dot(cmb, bmb)
        for pr in range(HPG // 2):
            pidx = g * (HPG // 2) + pr
            xs_pair = xs[:, pidx * LANES:(pidx + 1) * LANES]
            xs_pair_b = xs_pair.astype(BF16)
            h0 = h_ref[pidx]
            ych = _nt_dot(cmb, h0.astype(BF16))
            yw = []
            for k in range(2):
                hd = 2 * pidx + k
                diff = cols[:, hd:hd + 1] - cum_t[hd:hd + 1, :]
                decay = jnp.exp(jnp.where(tril, diff, NEG_INF))
                w = cb * decay * dt_t[hd:hd + 1, :]
                yw.append(_dot(w.astype(BF16), xs_pair_b))
            e0 = jnp.exp(cols[:, 2 * pidx:2 * pidx + 1])
            e1 = jnp.exp(cols[:, 2 * pidx + 1:2 * pidx + 2])
            first = lane < M_HEADDIM
            y_pairs.append(jnp.where(first, yw[0], yw[1]) + ych * jnp.where(first, e0, e1))
            top = rowi < M_HEADDIM
            tail_m = jnp.where(top, jnp.broadcast_to(tail_t[2 * pidx:2 * pidx + 1, :], (L, L)),
                               jnp.broadcast_to(tail_t[2 * pidx + 1:2 * pidx + 2, :], (L, L)))
            dec_m = jnp.where(top, jnp.exp(cum_last[2 * pidx:2 * pidx + 1, :]),
                              jnp.exp(cum_last[2 * pidx + 1:2 * pidx + 2, :]))
            xt = xs_pair.T * tail_m
            h_ref[pidx] = h0 * dec_m + _dot(xt.astype(BF16), bmb)

    y = jnp.concatenate(y_pairs, axis=1)
    o_ref[...] = _ssd_epilogue(y, xs, z_ref[...], dskip_ref[...], normw_ref[...])
    st_ref[0] = h_ref[...]


def ssd_prompt(xz, dt_rows, conv_w, conv_b, nega, dskip_e, normw, rest, n_batch, seq):
    L = SSD_CHUNK
    nc = seq // L
    pairs = M_HEADS // 2
    const = lambda b, c: (0, 0)
    return pl.pallas_call(
        _ssd_prompt_kernel,
        grid=(n_batch, nc),
        in_specs=[pl.BlockSpec((L, CONV_CH), lambda b, c: (b * nc + c, 0)),
                  pl.BlockSpec((L, M_INNER), lambda b, c: (b * nc + c, CONV_CH // M_INNER)),
                  pl.BlockSpec((M_HEADS, L), lambda b, c: (1, b * nc + c)),
                  pl.BlockSpec((CONV_W, CONV_CH), const),
                  pl.BlockSpec((1, CONV_CH), const),
                  pl.BlockSpec((M_HEADS, 1), const),
                  pl.BlockSpec((1, M_INNER), const),
                  pl.BlockSpec((1, M_INNER), const),
                  pl.BlockSpec(memory_space=pl.ANY)],
        out_specs=[pl.BlockSpec((L, M_INNER), lambda b, c: (b * nc + c, 0)),
                   pl.BlockSpec((1, pairs, LANES, D_STATE), lambda b, c: (b, 0, 0, 0))],
        out_shape=[jax.ShapeDtypeStruct(rest.shape, F32),
                   jax.ShapeDtypeStruct((n_batch, pairs, LANES, D_STATE), F32)],
        input_output_aliases={8: 0},
        scratch_shapes=[pltpu.VMEM((L + SUBLANES, CONV_CH), F32),
                        pltpu.VMEM((pairs, LANES, D_STATE), F32)],
        compiler_params=_params(("parallel", "arbitrary")),
        name="ssd_prompt",
    )(xz, xz, dt_rows, conv_w, conv_b, nega, dskip_e, normw, rest)


def _ssd_sample_kernel(x_ref, wdt_ref, dtb_ref, nega_ref, xbc_ref, ctx_ref, z_ref, cw_ref, cb_ref,
                       dskip_ref, normw_ref, h0_ref, o_ref, hn_ref,
                       u_ref, coef_t_ref, dec_t_ref, dec_ref, dtx_ref, *, seqs_per_step):
    sb = seqs_per_step
    i = pl.program_id(0)
    nseq = x_ref.shape[0]

    @pl.when(i == 0)
    def _():
        acc = xbc_ref[...] * cw_ref[CONV_W - 1:CONV_W, :]
        for j in range(CONV_W - 1):
            acc = acc + ctx_ref[j] * cw_ref[j:j + 1, :]
        u = _silu(acc + cb_ref[...])
        u_ref[...] = u
        dt = _softplus(_dot(x_ref[...], wdt_ref[...], precision=HIGHEST) + dtb_ref[...])
        dec = jnp.exp(dt * nega_ref[...])
        coef = dt * u[:, :M_INNER]
        dec_ref[...] = dec
        dtx_ref[...] = coef
        for blk in range(M_INNER // LANES):
            sl = slice(blk * LANES, (blk + 1) * LANES)
            coef_t_ref[sl, :] = coef[:, sl].T
            dec_t_ref[sl, :] = dec[:, sl].T

    base = pl.multiple_of(i * sb, sb)
    ub = u_ref[pl.ds(base, sb), :]
    lane_seq = _iota((M_INNER, nseq), 1)
    rows = _iota((sb, M_INNER // M_GROUPS), 0)
    ch = [jnp.zeros((sb, M_INNER // M_GROUPS), F32) for _ in range(M_GROUPS)]
    for s in range(sb):
        onehot = lane_seq == base + s
        cx = jnp.sum(jnp.where(onehot, coef_t_ref[...], 0.0), axis=1, keepdims=True)
        dc = jnp.sum(jnp.where(onehot, dec_t_ref[...], 0.0), axis=1, keepdims=True)
        for g in range(M_GROUPS):
            brow = ub[s:s + 1, M_INNER + g * D_STATE:M_INNER + (g + 1) * D_STATE]
            cblk = ub[:, M_INNER + BC_W + g * D_STATE:M_INNER + BC_W + (g + 1) * D_STATE]
            hg = h0_ref[s, g * HPG:(g + 1) * HPG].reshape(HPG * M_HEADDIM, D_STATE)
            r = _nt_dot(cblk.astype(BF16), hg.astype(BF16))
            ch[g] = ch[g] + jnp.where(rows == s, r, 0.0)
            lo = g * HPG * M_HEADDIM
            hn = hg * dc[lo:lo + HPG * M_HEADDIM] + cx[lo:lo + HPG * M_HEADDIM] * brow
            hn_ref[s, g * HPG:(g + 1) * HPG] = hn.reshape(HPG, M_HEADDIM, D_STATE)

    xs = ub[:, :M_INNER]
    dec = dec_ref[pl.ds(base, sb), :]
    coef = dtx_ref[pl.ds(base, sb), :]
    ys = []
    half = M_INNER // M_GROUPS
    for g in range(M_GROUPS):
        bm = ub[:, M_INNER + g * D_STATE:M_INNER + (g + 1) * D_STATE]
        cm = ub[:, M_INNER + BC_W + g * D_STATE:M_INNER + BC_W + (g + 1) * D_STATE]
        cb = jnp.sum(cm * bm, axis=1, keepdims=True)
        ys.append(cb * coef[:, g * half:(g + 1) * half] + ch[g] * dec[:, g * half:(g + 1) * half])
    y = jnp.concatenate(ys, axis=1)
    o_ref[...] = _ssd_epilogue(y, xs, z_ref[...], dskip_ref[...], normw_ref[...])


def ssd_sample(x_s, wdt_e, dtb_e, nega_e, xbc_s, ctx, z_s, conv_w, conv_b, dskip_e, normw, h0, seqs_per_step):
    bd = x_s.shape[0]
    sb = seqs_per_step
    const = lambda i: (0, 0)
    return pl.pallas_call(
        functools.partial(_ssd_sample_kernel, seqs_per_step=sb),
        grid=(bd // sb,),
        in_specs=[pl.BlockSpec((bd, D_MODEL), const),
                  pl.BlockSpec((D_MODEL, M_INNER), const),
                  pl.BlockSpec((1, M_INNER), const),
                  pl.BlockSpec((1, M_INNER), const),
                  pl.BlockSpec((bd, CONV_CH), const),
                  pl.BlockSpec((CONV_W - 1, bd, CONV_CH), lambda i: (0, 0, 0)),
                  pl.BlockSpec((sb, M_INNER), lambda i: (i, 0)),
                  pl.BlockSpec((CONV_W, CONV_CH), const),
                  pl.BlockSpec((1, CONV_CH), const),
                  pl.BlockSpec((1, M_INNER), const),
                  pl.BlockSpec((1, M_INNER), const),
                  pl.BlockSpec((sb, M_HEADS, M_HEADDIM, D_STATE), lambda i: (i, 0, 0, 0))],
        out_specs=[pl.BlockSpec((sb, M_INNER), lambda i: (i, 0)),
                   pl.BlockSpec((sb, M_HEADS, M_HEADDIM, D_STATE), lambda i: (i, 0, 0, 0))],
        out_shape=[jax.ShapeDtypeStruct((bd, M_INNER), F32),
                   jax.ShapeDtypeStruct((bd, M_HEADS, M_HEADDIM, D_STATE), F32)],
        scratch_shapes=[pltpu.VMEM((bd, CONV_CH), F32),
                        pltpu.VMEM((M_INNER, bd), F32),
                        pltpu.VMEM((M_INNER, bd), F32),
                        pltpu.VMEM((bd, M_INNER), F32),
                        pltpu.VMEM((bd, M_INNER), F32)],
        compiler_params=_params(("arbitrary",)),
        name="ssd_sample",
    )(x_s, wdt_e, dtb_e, nega_e, xbc_s, ctx, z_s, conv_w, conv_b, dskip_e, normw, h0)


def _to_half(x, src_half, dst_half):
    return x if src_half == dst_half else pltpu.roll(x, HEAD_DIM, 1)


def _swa_prompt_kernel(sink_ref, q_ref, kp_ref, kc_ref, vp_ref, vc_ref, rest_ref, o_ref):
    del rest_ref
    n = pl.program_id(1)
    W = WINDOW
    scale = HEAD_DIM ** -0.5
    lane = _iota((W, LANES), 1)
    rows4 = _iota((SW_GROUP * W, 2 * W), 0)
    t = rows4 % W
    col = _iota((SW_GROUP * W, 2 * W), 1)
    valid = (col >= t) & (col <= t + W) & ((n > 0) | (col >= W))
    rcol = _iota((SW_GROUP * W, 1), 0) // W
    outs = [None] * SW_HEADS
    for j in range(SW_KV_HEADS):
        ch, hf = j // 2, j % 2
        kk = jnp.concatenate([kp_ref[:, ch * LANES:(ch + 1) * LANES],
                              kc_ref[:, ch * LANES:(ch + 1) * LANES]], axis=0).astype(BF16)
        vv = jnp.concatenate([vp_ref[:, ch * LANES:(ch + 1) * LANES],
                              vc_ref[:, ch * LANES:(ch + 1) * LANES]], axis=0).astype(BF16)
        qs = []
        sink = jnp.zeros((SW_GROUP * W, 1), F32)
        for g in range(SW_GROUP):
            hq = j * SW_GROUP + g
            qc = q_ref[:, (hq // 2) * LANES:(hq // 2 + 1) * LANES] * scale
            qc = _to_half(qc, hq % 2, hf)
            keep = (lane < HEAD_DIM) if hf == 0 else (lane >= HEAD_DIM)
            qs.append(jnp.where(keep, qc, 0.0))
            sink = jnp.where(rcol == g, sink_ref[hq], sink)
        q4 = jnp.concatenate(qs, axis=0).astype(BF16)
        s = jnp.where(valid, _nt_dot(q4, kk), NEG_INF)
        m = jnp.maximum(jnp.max(s, axis=1, keepdims=True), sink)
        p = jnp.exp(s - m)
        den = jnp.sum(p, axis=1, keepdims=True) + jnp.exp(sink - m)
        o = _dot(p.astype(BF16), vv) / den
        for g in range(SW_GROUP):
            hq = j * SW_GROUP + g
            outs[hq] = _to_half(o[g * W:(g + 1) * W], hf, hq % 2)
    for c in range(SW_HEADS // 2):
        o_ref[:, c * LANES:(c + 1) * LANES] = jnp.where(lane < HEAD_DIM, outs[2 * c], outs[2 * c + 1])


def swa_prompt(sinks, u_odd, rest, n_batch, seq):
    nb = seq // WINDOW
    kcol = ODD_MIX // KV_W
    return pl.pallas_call(
        _swa_prompt_kernel,
        grid=(n_batch, nb),
        in_specs=[pl.BlockSpec(memory_space=pltpu.SMEM),
                  pl.BlockSpec((WINDOW, ODD_MIX), lambda b, n: (b * nb + n, 0)),
                  pl.BlockSpec((WINDOW, KV_W), lambda b, n: (b * nb + jnp.maximum(n - 1, 0), kcol)),
                  pl.BlockSpec((WINDOW, KV_W), lambda b, n: (b * nb + n, kcol)),
                  pl.BlockSpec((WINDOW, KV_W), lambda b, n: (b * nb + jnp.maximum(n - 1, 0), kcol + 1)),
                  pl.BlockSpec((WINDOW, KV_W), lambda b, n: (b * nb + n, kcol + 1)),
                  pl.BlockSpec(memory_space=pl.ANY)],
        out_specs=pl.BlockSpec((WINDOW, ODD_MIX), lambda b, n: (b * nb + n, 0)),
        out_shape=jax.ShapeDtypeStruct(rest.shape, F32),
        input_output_aliases={6: 0},
        compiler_params=_params(("parallel", "arbitrary")),
        name="swa_prompt",
    )(sinks, u_odd, u_odd, u_odd, u_odd, u_odd, rest)


def _swa_sample_kernel(sink_ref, q_ref, kn_ref, vn_ref, bk_ref, bv_ref, o_ref, *, seqs_per_step):
    scale = HEAD_DIM ** -0.5
    rowg = _iota((SW_HEADS, HEAD_DIM), 0) // SW_GROUP
    sink = sink_ref[...]
    for s in range(seqs_per_step):
        q16 = q_ref[s] * scale
        qbd = jnp.concatenate([jnp.where(rowg == j, q16, 0.0) for j in range(SW_KV_HEADS)], axis=1)
        kb = bk_ref[s].reshape(KV_W, -1).astype(BF16)
        vb = bv_ref[s].reshape(KV_W, -1).astype(BF16)
        sc = _dot(qbd.astype(BF16), kb)
        s_new = jnp.sum(qbd * kn_ref[s:s + 1, :], axis=1, keepdims=True)
        m = jnp.maximum(jnp.maximum(jnp.max(sc, axis=1, keepdims=True), s_new), sink)
        p = jnp.exp(sc - m)
        pn = jnp.exp(s_new - m)
        den = jnp.sum(p, axis=1, keepdims=True) + pn + jnp.exp(sink - m)
        full = (_nt_dot(p.astype(BF16), vb) + pn * vn_ref[s:s + 1, :]) / den
        o16 = jnp.zeros((SW_HEADS, HEAD_DIM), F32)
        for j in range(SW_KV_HEADS):
            o16 = o16 + jnp.where(rowg == j, full[:, j * HEAD_DIM:(j + 1) * HEAD_DIM], 0.0)
        o_ref[s] = o16


def swa_sample(sinks_col, q, kn, vn, buf_k, buf_v, seqs_per_step):
    bd, _, _, lw = buf_k.shape
    sb = seqs_per_step
    return pl.pallas_call(
        functools.partial(_swa_sample_kernel, seqs_per_step=sb),
        grid=(bd // sb,),
        in_specs=[pl.BlockSpec((SW_HEADS, 1), lambda i: (0, 0)),
                  pl.BlockSpec((sb, SW_HEADS, HEAD_DIM), lambda i: (i, 0, 0)),
                  pl.BlockSpec((sb, KV_W), lambda i: (i, 0)),
                  pl.BlockSpec((sb, KV_W), lambda i: (i, 0)),
                  pl.BlockSpec((sb, SW_KV_HEADS, HEAD_DIM, lw), lambda i: (i, 0, 0, 0)),
                  pl.BlockSpec((sb, SW_KV_HEADS, HEAD_DIM, lw), lambda i: (i, 0, 0, 0))],
        out_specs=pl.BlockSpec((sb, SW_HEADS, HEAD_DIM), lambda i: (i, 0, 0)),
        out_shape=jax.ShapeDtypeStruct((bd, SW_HEADS, HEAD_DIM), F32),
        compiler_params=_params(("parallel",)),
        name="swa_sample",
    )(sinks_col, q, kn, vn, buf_k, buf_v)


def _layer_norm(h, g, b):
    mu = jnp.mean(h, axis=1, keepdims=True)
    d = h - mu
    var = jnp.mean(d * d, axis=1, keepdims=True)
    return d * lax.rsqrt(var + LN_EPS) * g + b


def _mix_route_kernel(*refs, alpha, n_feat):
    am_refs, wo_refs = refs[:n_feat], refs[n_feat:2 * n_feat]
    x_ref, g_ref, b_ref, wr_ref, br_ref, x1_ref, rw_ref, re_ref = refs[2 * n_feat:]
    mix = _dot(am_refs[0][...].astype(BF16), wo_refs[0][...])
    for a_ref, w_ref in zip(am_refs[1:], wo_refs[1:]):
        mix = mix + _dot(a_ref[...].astype(BF16), w_ref[...])
    x1 = _layer_norm(alpha * x_ref[...] + mix, g_ref[...], b_ref[...])
    x1_ref[...] = x1
    wr = wr_ref[...]
    x_hi = x1.astype(BF16)
    w_hi = wr.astype(BF16)
    x_lo = (x1 - x_hi.astype(F32)).astype(BF16)
    w_lo = (wr - w_hi.astype(F32)).astype(BF16)
    logits = _dot(x_hi, w_hi) + _dot(x_lo, w_hi) + _dot(x_hi, w_lo) + br_ref[...]
    lane_i = _iota(logits.shape, 1)
    lane = lane_i.astype(F32)
    big = float(LANES)
    gl = jnp.where(lane_i < N_EXPERT_GROUPS, logits, NEG_INF)
    gmax = jnp.max(gl, axis=1, keepdims=True)
    grp = jnp.min(jnp.where(gl == gmax, lane, big), axis=1, keepdims=True)
    pg = 1.0 / jnp.sum(jnp.exp(gl - gmax), axis=1, keepdims=True)
    rel = lane - (N_EXPERT_GROUPS + grp * EXPERTS_PER_GROUP)
    el = jnp.where(rel >= 0.0, jnp.where(rel < EXPERTS_PER_GROUP, logits, NEG_INF), NEG_INF)
    v1 = jnp.max(el, axis=1, keepdims=True)
    i1 = jnp.min(jnp.where(el == v1, lane, big), axis=1, keepdims=True)
    el2 = jnp.where(lane == i1, NEG_INF, el)
    v2 = jnp.max(el2, axis=1, keepdims=True)
    i2 = jnp.min(jnp.where(el2 == v2, lane, big), axis=1, keepdims=True)
    e = jnp.exp(v2 - v1)
    w1 = pg / (1.0 + e)
    w2 = w1 * e
    rw_ref[...] = jnp.where(lane_i == 0, w1, jnp.where(lane_i == 1, w2, 0.0))
    e1 = (i1 - N_EXPERT_GROUPS).astype(jnp.int32)
    e2 = (i2 - N_EXPERT_GROUPS).astype(jnp.int32)
    re_ref[...] = jnp.where(lane_i == 0, e1, jnp.where(lane_i == 1, e2, 0))


def mix_route(feats, wos, x, g, b, wr, br, alpha, tm):
    m, d = x.shape
    const = lambda i: (0, 0)
    row = lambda i: (i, 0)
    return pl.pallas_call(
        functools.partial(_mix_route_kernel, alpha=alpha, n_feat=len(feats)),
        grid=(m // tm,),
        in_specs=[pl.BlockSpec((tm, a.shape[1]), row) for a in feats]
        + [pl.BlockSpec((w.shape[0], d), const) for w in wos]
        + [pl.BlockSpec((tm, d), row), pl.BlockSpec((1, d), const), pl.BlockSpec((1, d), const),
           pl.BlockSpec((d, LANES), const), pl.BlockSpec((1, LANES), const)],
        out_specs=[pl.BlockSpec((tm, d), row),
                   pl.BlockSpec((tm, LANES), row), pl.BlockSpec((tm, LANES), row)],
        out_shape=[jax.ShapeDtypeStruct((m, d), F32),
                   jax.ShapeDtypeStruct((m, LANES), F32), jax.ShapeDtypeStruct((m, LANES), jnp.int32)],
        compiler_params=_params(("parallel",)),
        name="mix_route",
    )(*feats, *wos, x, g, b, wr, br)


def _experts_kernel(te_ref, tv_ref, x_ref, wg_ref, wu_ref, wd_ref, *rest, first_tile):
    y_ref = rest[-1]
    t = first_tile + pl.program_id(0)

    @pl.when(tv_ref[t] > 0)
    def _():
        x = x_ref[...].astype(BF16)
        h = _silu(_dot(x, wg_ref[0, 0].astype(BF16))) * _dot(x, wu_ref[0, 0].astype(BF16))
        y_ref[...] = _dot(h.astype(BF16), wd_ref[0, 0].astype(BF16))

    @pl.when(tv_ref[t] == 0)
    def _():
        y_ref[...] = jnp.zeros_like(y_ref)


def experts(tile_expert, tile_valid, xs, wg, wu, wd, layer, te, first_tile, n_rows, y_prev):
    r, d = xs.shape
    ff = wg.shape[3]
    ex = lambda t, e, v: (layer, e[first_tile + t], 0, 0)
    in_specs = [pl.BlockSpec((te, d), lambda t, e, v: (t, 0)),
                pl.BlockSpec((1, 1, d, ff), ex), pl.BlockSpec((1, 1, d, ff), ex), pl.BlockSpec((1, 1, ff, d), ex)]
    args = [tile_expert, tile_valid, xs, wg, wu, wd]
    aliases = {}
    if y_prev is not None:
        in_specs.append(pl.BlockSpec(memory_space=pl.ANY))
        aliases = {len(args): 0}
        args.append(y_prev)
    grid_spec = pltpu.PrefetchScalarGridSpec(
        num_scalar_prefetch=2,
        grid=(r // te,),
        in_specs=in_specs,
        out_specs=pl.BlockSpec((te, d), lambda t, e, v: (first_tile + t, 0)),
    )
    return pl.pallas_call(
        functools.partial(_experts_kernel, first_tile=first_tile),
        grid_spec=grid_spec,
        out_shape=jax.ShapeDtypeStruct((n_rows, d), F32),
        input_output_aliases=aliases,
        compiler_params=_params(("arbitrary",)),
        name="experts",
    )(*args)


def _combine_ple_kernel(x1_ref, y0_ref, y1_ref, rw_ref, pp_ref, pt_ref, g_ref, b_ref, wg_ref, bg_ref, wp_ref,
                        o_ref, *, alpha, prompt_tiles, first_tile):
    rw = rw_ref[...]
    f = rw[:, 0:1] * y0_ref[...] + rw[:, 1:2] * y1_ref[...]
    x2 = _layer_norm(alpha * x1_ref[...] + f, g_ref[...], b_ref[...])
    gl = _dot(x2.astype(BF16), wg_ref[...]) + bg_ref[...]
    gate = 1.0 / (1.0 + jnp.exp(-gl))
    p = jnp.where(first_tile + pl.program_id(0) < prompt_tiles, pp_ref[0], pt_ref[...])
    o_ref[...] = x2 + gate * _dot(p.astype(BF16), wp_ref[...])


def combine_ple(x1, y0, y1, rw, p_prompt, p_tail, layer, g, b, wg, bg, wp, alpha, tm, first_tile, n_tiles):
    d = x1.shape[1]
    pd = p_prompt.shape[2]
    prompt_tiles = p_prompt.shape[1] // tm
    const = lambda i: (0, 0)
    row = lambda i: (first_tile + i, 0)
    return pl.pallas_call(
        functools.partial(_combine_ple_kernel, alpha=alpha, prompt_tiles=prompt_tiles, first_tile=first_tile),
        grid=(n_tiles,),
        in_specs=[pl.BlockSpec((tm, d), row), pl.BlockSpec((tm, d), row), pl.BlockSpec((tm, d), row),
                  pl.BlockSpec((tm, LANES), row),
                  pl.BlockSpec((1, tm, pd), lambda i: (layer, jnp.minimum(first_tile + i, prompt_tiles - 1), 0)),
                  pl.BlockSpec((tm, pd), lambda i: (jnp.maximum(first_tile + i - prompt_tiles, 0), 0)),
                  pl.BlockSpec((1, d), const), pl.BlockSpec((1, d), const),
                  pl.BlockSpec((d, d), const), pl.BlockSpec((1, d), const), pl.BlockSpec((pd, d), const)],
        out_specs=pl.BlockSpec((tm, d), lambda i: (i, 0)),
        out_shape=jax.ShapeDtypeStruct((n_tiles * tm, d), F32),
        compiler_params=_params(("parallel",)),
        name="combine_ple",
    )(x1, y0, y1, rw, p_prompt, p_tail, g, b, wg, bg, wp)


def _tiles(n_tokens):
    tm = 512 if n_tokens >= 4096 else 128
    return tm, ((n_tokens + tm - 1) // tm) * tm


def _expert_tile(n_tokens):
    return 256 if n_tokens >= 4096 else 32


def _channel_and_ple(x, feats, p_prompt, p_tail, li, w_out, w, tm):
    depth = w["ln_mix_g"].shape[0]
    alpha = (2 * depth) ** 0.25
    ntp, d = x.shape
    wr = jnp.concatenate([w["w_router_group"][li],
                          jnp.moveaxis(w["w_router_expert"][li], 0, 1).reshape(d, N_EXPERTS)], axis=1)
    wr = jnp.pad(wr, ((0, 0), (0, LANES - wr.shape[1])))
    br = jnp.concatenate([w["b_router_group"][li], w["b_router_expert"][li].reshape(-1)])
    br = jnp.pad(br, (0, LANES - br.shape[0]))[None, :]
    wo = w_out.astype(BF16)
    splits = [0]
    for a in feats:
        splits.append(splits[-1] + a.shape[1])
    x1, rw, re = mix_route(feats, [wo[lo:hi] for lo, hi in zip(splits[:-1], splits[1:])], x,
                           w["ln_mix_g"][li][None], w["ln_mix_b"][li][None], wr, br, alpha, tm)

    te = _expert_tile(ntp)
    n_flat = 2 * ntp
    flat = re[:, :2].reshape(-1)
    onehot = (flat[:, None] == jnp.arange(N_EXPERTS, dtype=jnp.int32)[None, :]).astype(jnp.int32)
    running = jnp.cumsum(onehot, axis=0)
    counts = running[-1]
    padded = ((counts + te - 1) // te) * te
    gend = jnp.cumsum(padded)
    gstart = gend - padded
    pos_flat = jnp.sum(onehot * (running - 1 + gstart[None, :]), axis=1)
    n_rows = ((n_flat + N_EXPERTS * (te - 1) + te - 1) // te) * te
    row_token = (jnp.arange(n_rows, dtype=jnp.int32) % ntp).at[pos_flat].set(
        jnp.arange(n_flat, dtype=jnp.int32) // 2, mode="promise_in_bounds", unique_indices=True)
    pos_flat = pos_flat.reshape(ntp, 2)
    tile_start = jnp.arange(n_rows // te, dtype=jnp.int32) * te
    tile_expert = jnp.minimum(jnp.sum((gend[None, :] <= tile_start[:, None]).astype(jnp.int32), axis=1),
                              N_EXPERTS - 1)
    tile_valid = (tile_start < gend[-1]).astype(jnp.int32)

    def rows(a, idx):
        return a.at[idx].get(mode="promise_in_bounds")

    n_tiles = n_rows // te
    bounds = [(c * n_tiles) // EXPERT_CHUNKS for c in range(EXPERT_CHUNKS + 1)]
    y = None
    for t0, t1 in zip(bounds[:-1], bounds[1:]):
        xs = rows(x1, row_token[t0 * te:t1 * te])
        y = experts(tile_expert, tile_valid, xs, w["w_exp_gate"], w["w_exp_up"], w["w_exp_down"], li, te,
                    t0, n_rows, y)
    y0 = rows(y, pos_flat[:, 0])
    y1 = rows(y, pos_flat[:, 1])
    prompt_tiles = p_prompt.shape[1] // tm
    parts = [(0, ntp // tm)] if li + 1 < depth else [(0, prompt_tiles), (prompt_tiles, ntp // tm - prompt_tiles)]
    outs = [combine_ple(x1, y0, y1, rw, p_prompt, p_tail, li, w["ln_ffn_g"][li][None], w["ln_ffn_b"][li][None],
                        w["w_ple_gate"][li].astype(BF16), w["b_ple_gate"][li][None],
                        w["w_ple_proj"][li].astype(BF16), alpha, tm, first, n) for first, n in parts]
    return outs[0] if len(outs) == 1 else tuple(outs)


def kernel(x_prompt, x_sample, p_prompt, p_sample, cache_fox_k, cache_fox_v, cache_fox_logf, state_ssm, state_conv, cache_win_k, cache_win_v, page_table, w_in_even, b_fgate, conv_w, conv_b, dt_bias, a_log, d_skip, ssm_norm_w, w_out_even, w_in_odd, attn_sinks, w_out_odd, ln_mix_g, ln_mix_b, ln_ffn_g, ln_ffn_b, w_router_group, b_router_group, w_router_expert, b_router_expert, w_exp_gate, w_exp_up, w_exp_down, w_ple_proj, w_ple_gate, b_ple_gate):
    bp, seq, d = x_prompt.shape
    bd, t_dec, _ = x_sample.shape
    assert t_dec == 1 and d == D_MODEL
    depth = p_prompt.shape[0]
    n_pages = page_table.shape[1]
    past_len = n_pages * PAGE_SIZE
    np_tok = bp * seq
    nt = np_tok + bd
    tm, ntp = _tiles(nt)
    pad = ntp - nt

    def tokens(a_p, a_s):
        parts = [a_p.reshape(np_tok, -1), a_s.reshape(bd, -1)]
        if pad:
            parts.append(jnp.zeros((pad, parts[0].shape[1]), parts[0].dtype))
        return jnp.concatenate(parts, axis=0)

    def past_prompt(a_s):
        a_s = a_s.reshape(bd, -1)
        return jnp.zeros((ntp, a_s.shape[1]), a_s.dtype).at[np_tok:nt].set(a_s)

    assert np_tok % tm == 0 and seq >= CONV_W - 1
    x = tokens(x_prompt, x_sample)
    p_all = p_prompt.reshape(depth, np_tok, -1)
    shared = dict(ln_mix_g=ln_mix_g, ln_mix_b=ln_mix_b, ln_ffn_g=ln_ffn_g, ln_ffn_b=ln_ffn_b,
                  w_router_group=w_router_group, b_router_group=b_router_group,
                  w_router_expert=w_router_expert, b_router_expert=b_router_expert,
                  w_exp_gate=w_exp_gate, w_exp_up=w_exp_up, w_exp_down=w_exp_down,
                  w_ple_proj=w_ple_proj, w_ple_gate=w_ple_gate, b_ple_gate=b_ple_gate)

    half = HEAD_DIM // 2
    inv = jnp.exp(-math.log(ROPE_THETA) * jnp.arange(half, dtype=F32) / half)
    pos = jnp.concatenate([jnp.tile(jnp.arange(seq, dtype=jnp.int32), bp),
                           jnp.full((bd,), past_len, jnp.int32), jnp.zeros((pad,), jnp.int32)])
    ang = pos.astype(F32)[:, None] * inv[None, :]
    cos_t = jnp.tile(jnp.cos(ang), (1, LANES // half))
    sin_t = jnp.tile(jnp.concatenate([-jnp.sin(ang), jnp.sin(ang)], axis=1), (1, LANES // HEAD_DIM))

    even_p, even_s, odd_p, odd_s = [], [], [], []
    for li in range(depth):
        j = li // 2
        if li % 2 == 0:
            wi = w_in_even[j]
            c0 = 3 * FOX_WIDTH
            c1 = c0 + FOX_HEADS
            c2 = c1 + M_INNER
            c3 = c2 + CONV_CH
            w_main = jnp.concatenate([wi[:, :c0], wi[:, c2:c3], wi[:, c1:c2]], axis=1).astype(BF16)
            w_small_t = jnp.concatenate([wi[:, c0:c1], wi[:, c3:]], axis=1).T
            b_small = jnp.concatenate([b_fgate[j], dt_bias[j]])[:, None]
            prompt_tiles = np_tok // tm
            k_p, v_p, xz, qkv, _ = even_proj(x, w_main, tm, 0, prompt_tiles)
            k_t, v_t, xz_t, _, q_t = even_proj(x, w_main, tm, prompt_tiles, ntp // tm - prompt_tiles)
            small = small_proj(x, w_small_t, b_small, tm)

            tq = min(FOX_TILE, seq)
            logf_p = small[:FOX_HEADS, :np_tok].reshape(FOX_HEADS, bp, seq)
            cum = cumsum_lanes(jnp.moveaxis(logf_p, 1, 0).reshape(bp * FOX_HEADS, seq), min(512, seq))
            ck = cum.reshape(bp, FOX_HEADS // 2, 2, seq // tq, tq).transpose(0, 1, 3, 2, 4)
            logf_s = small[:FOX_HEADS, np_tok:nt].T
            q_s = q_t[:bd].reshape(bd, FOX_HEADS, HEAD_DIM)
            k_s = k_t[:bd].reshape(bd, FOX_HEADS, HEAD_DIM)
            v_s = v_t[:bd].reshape(bd, FOX_HEADS, HEAD_DIM)
            eye = jnp.eye(FOX_HEADS, dtype=F32)[None, :, :, None]

            def block_diag(a):
                return (a[:, :, None, :] * eye).reshape(bd, FOX_HEADS, FOX_WIDTH)

            a_s = fox_sample(page_table, block_diag(q_s), q_s, k_s, block_diag(v_s), logf_s[:, :, None],
                             jnp.transpose(cache_fox_k, (0, 1, 3, 4, 2)), jnp.transpose(cache_fox_v, (0, 1, 3, 4, 2)),
                             jnp.transpose(cache_fox_logf, (0, 1, 3, 2)), j, min(16, n_pages))
            a_all = fox_prompt(qkv, ck, past_prompt(a_s), bp, seq, tq)

            nega = -jnp.exp(a_log[j])
            dskip_e = jnp.repeat(d_skip[j], M_HEADDIM)[None, :]
            normw = ssm_norm_w[j][None, :]
            xz_s = xz_t[:bd]
            m_s, st_s = ssd_sample(x[np_tok:nt], jnp.repeat(wi[:, c3:], M_HEADDIM, axis=1),
                                   jnp.repeat(dt_bias[j], M_HEADDIM)[None, :], jnp.repeat(nega, M_HEADDIM)[None, :],
                                   xz_s[:, :CONV_CH], jnp.moveaxis(state_conv[j], 1, 0), xz_s[:, CONV_CH:],
                                   conv_w[j], conv_b[j][None, :], dskip_e, normw, state_ssm[j], min(8, bd))
            m_all, st_p = ssd_prompt(xz, small, conv_w[j], conv_b[j][None, :], nega[:, None], dskip_e, normw,
                                     past_prompt(m_s), bp, seq)
            feats = [a_all, m_all]
            w_out = w_out_even[j]

            kp = k_p.reshape(bp, seq, FOX_HEADS, HEAD_DIM)
            vp = v_p.reshape(bp, seq, FOX_HEADS, HEAD_DIM)
            last = jnp.stack([xz[(b + 1) * seq - (CONV_W - 1):(b + 1) * seq, :CONV_CH] for b in range(bp)])
            conv_p = jnp.concatenate([jnp.zeros((bp, CONV_W - 1, CONV_CH), F32), last], axis=1)[:, -(CONV_W - 1):]
            even_p.append((kp, vp, jnp.moveaxis(logf_p, 0, 2),
                           st_p.reshape(bp, M_HEADS, M_HEADDIM, D_STATE), conv_p))
            conv_s = jnp.concatenate([state_conv[j], xz_s[:, None, :CONV_CH]], axis=1)[:, -(CONV_W - 1):]
            even_s.append((k_s[:, None], v_s[:, None], logf_s[:, None, :], st_s, conv_s))
        else:
            u = matmul_rope(x, w_in_odd[j].astype(BF16), cos_t, sin_t, tm, 256, ODD_MIX + KV_W)
            u_s = u[np_tok:nt]
            lw = cache_win_k.shape[2]
            kn = u_s[:, ODD_MIX:ODD_MIX + KV_W]
            vn = u_s[:, ODD_MIX + KV_W:]
            o_s = swa_sample(attn_sinks[j][:, None], u_s[:, :ODD_MIX].reshape(bd, SW_HEADS, HEAD_DIM), kn, vn,
                             jnp.transpose(cache_win_k[j], (0, 2, 3, 1)), jnp.transpose(cache_win_v[j], (0, 2, 3, 1)),
                             min(8, bd))
            feats = [swa_prompt(attn_sinks[j], u, past_prompt(o_s), bp, seq)]
            w_out = w_out_odd[j]

            rows = min(WINDOW, seq)
            tail = jnp.stack([u[(b + 1) * seq - rows:(b + 1) * seq, ODD_MIX:] for b in range(bp)])
            odd_p.append((tail[:, :, :KV_W].reshape(bp, rows, SW_KV_HEADS, HEAD_DIM),
                          tail[:, :, KV_W:].reshape(bp, rows, SW_KV_HEADS, HEAD_DIM)))
            ka = jnp.concatenate([cache_win_k[j], kn.reshape(bd, 1, SW_KV_HEADS, HEAD_DIM)], axis=1)[:, -lw:]
            va = jnp.concatenate([cache_win_v[j], vn.reshape(bd, 1, SW_KV_HEADS, HEAD_DIM)], axis=1)[:, -lw:]
            odd_s.append((ka, va))
        x = _channel_and_ple(x, feats, p_all, past_prompt(p_sample[li])[np_tok:], li, w_out, shared, tm)

    x_prompt_out, x_tail_out = x
    yp = x_prompt_out.reshape(bp, seq, d)
    ys = x_tail_out[:bd].reshape(bd, 1, d)
    return (yp, ys,
            jnp.stack([st[0] for st in even_p]), jnp.stack([st[1] for st in even_p]),
            jnp.stack([st[2] for st in even_p]), jnp.stack([st[3] for st in even_p]),
            jnp.stack([st[4] for st in even_p]),
            jnp.stack([st[0] for st in odd_p]), jnp.stack([st[1] for st in odd_p]),
            jnp.stack([st[0] for st in even_s]), jnp.stack([st[1] for st in even_s]),
            jnp.stack([st[2] for st in even_s]), jnp.stack([st[3] for st in even_s]),
            jnp.stack([st[4] for st in even_s]),
            jnp.stack([st[0] for st in odd_s]), jnp.stack([st[1] for st in odd_s]))
```

```python
import functools
import math

import jax
import jax.numpy as jnp
from jax import lax
from jax.experimental import pallas as pl
from jax.experimental.pallas import tpu as pltpu

F32 = jnp.float32
BF16 = jnp.bfloat16
HIGHEST = lax.Precision.HIGHEST

D_MODEL = 1024
HEAD_DIM = 64
FOX_HEADS = 8
FOX_WIDTH = FOX_HEADS * HEAD_DIM
M_HEADS = 8
M_HEADDIM = 64
M_INNER = M_HEADS * M_HEADDIM
M_GROUPS = 2
HPG = M_HEADS // M_GROUPS
D_STATE = 128
CONV_W = 4
BC_W = M_GROUPS * D_STATE
CONV_CH = M_INNER + 2 * BC_W
SSD_CHUNK = 128
RMS_EPS = 1e-5
SW_HEADS = 16
SW_KV_HEADS = 4
SW_GROUP = SW_HEADS // SW_KV_HEADS
WINDOW = 128
ROPE_THETA = 10000.0
ODD_MIX = SW_HEADS * HEAD_DIM
KV_W = SW_KV_HEADS * HEAD_DIM
N_EXPERT_GROUPS = 4
EXPERTS_PER_GROUP = 8
N_EXPERTS = N_EXPERT_GROUPS * EXPERTS_PER_GROUP
EXPERT_FF = 512
PLE_DIM = 256
LN_EPS = 1e-5
PAGE_SIZE = 128

LANES = 128
SUBLANES = 8
VMEM_LIMIT = 48 * 1024 * 1024

NEG_INF = float("-inf")
EXPERT_CHUNKS = 2
FOX_PAGES_PER_STEP = 32
FOX_TILE = 1024
FOX_Q_STRIP = 1024


def _params(sem, vmem=VMEM_LIMIT):
    return pltpu.CompilerParams(dimension_semantics=sem, vmem_limit_bytes=vmem)


def _nt_dot(a, b, precision=None):
    return lax.dot_general(a, b, (((1,), (1,)), ((), ())), precision=precision,
                           preferred_element_type=F32)


def _dot(a, b, precision=None):
    return jnp.dot(a, b, precision=precision, preferred_element_type=F32)


def _silu(x):
    return x * (1.0 / (1.0 + jnp.exp(-x)))


def _softplus(x):
    return jnp.maximum(x, 0.0) + jnp.log(1.0 + jnp.exp(-jnp.abs(x)))


def _iota(shape, dim):
    return lax.broadcasted_iota(jnp.int32, shape, dim)


def _mm_rope_kernel(x_ref, w_ref, cos_ref, sin_ref, o_ref, *, tn, rope_cols):
    xb = x_ref[...].astype(BF16)
    reps = tn // LANES
    cos = jnp.concatenate([cos_ref[...]] * reps, axis=1)
    sin = jnp.concatenate([sin_ref[...]] * reps, axis=1)
    half = HEAD_DIM // 2
    first = (_iota((xb.shape[0], tn), 1) % HEAD_DIM) < half
    for c in range(w_ref.shape[1] // tn):
        acc = _dot(xb, w_ref[:, c * tn:(c + 1) * tn])
        if c * tn < rope_cols:
            partner = jnp.where(first, pltpu.roll(acc, tn - half, 1), pltpu.roll(acc, half, 1))
            acc = acc * cos + partner * sin
        o_ref[:, c * tn:(c + 1) * tn] = acc


def _even_proj_kernel(x_ref, w_ref, wt_ref, bt_ref, k_ref, v_ref, xz_ref, qkv_ref, qt_ref, small_ref):
    x = x_ref[...]
    xb = x.astype(BF16)
    wt = wt_ref[...]
    wt_hi = wt.astype(BF16)
    wt_lo = (wt - wt_hi.astype(F32)).astype(BF16)
    x_lo = (x - xb.astype(F32)).astype(BF16)
    r = _nt_dot(wt_hi, xb) + _nt_dot(wt_lo, xb) + _nt_dot(wt_hi, x_lo) + bt_ref[...]
    small_ref[...] = jnp.where(_iota(r.shape, 0) < FOX_HEADS, -_softplus(-r), _softplus(r))
    w = FOX_WIDTH
    f32_dst = (None, k_ref, v_ref)
    for c in range(3):
        acc = _dot(xb, w_ref[:, c * w:(c + 1) * w])
        qkv_ref[:, c * w:(c + 1) * w] = acc.astype(BF16)
        if f32_dst[c] is not None:
            f32_dst[c][...] = acc
        else:
            @pl.when(pl.program_id(0) == pl.num_programs(0) - 1)
            def _():
                qt_ref[...] = acc
    for c in range(xz_ref.shape[1] // w):
        xz_ref[:, c * w:(c + 1) * w] = _dot(xb, w_ref[:, (3 + c) * w:(4 + c) * w])


def even_proj(x, w, wt, bt, tm, first_tile, n_tiles):
    k = x.shape[1]
    m = n_tiles * tm
    n = w.shape[1]
    wq = 3 * FOX_WIDTH
    row = lambda i: (i, 0)
    const = lambda i: (0, 0)
    return pl.pallas_call(
        _even_proj_kernel,
        grid=(n_tiles,),
        in_specs=[pl.BlockSpec((tm, k), lambda i: (first_tile + i, 0)), pl.BlockSpec((k, n), const),
                  pl.BlockSpec((16, k), const), pl.BlockSpec((16, 1), const)],
        out_specs=[pl.BlockSpec((tm, FOX_WIDTH), row), pl.BlockSpec((tm, FOX_WIDTH), row),
                   pl.BlockSpec((tm, n - wq), row), pl.BlockSpec((tm, wq), row),
                   pl.BlockSpec((tm, FOX_WIDTH), const), pl.BlockSpec((16, tm), lambda i: (0, i))],
        out_shape=[jax.ShapeDtypeStruct((m, FOX_WIDTH), F32), jax.ShapeDtypeStruct((m, FOX_WIDTH), F32),
                   jax.ShapeDtypeStruct((m, n - wq), F32), jax.ShapeDtypeStruct((m, wq), BF16),
                   jax.ShapeDtypeStruct((tm, FOX_WIDTH), F32), jax.ShapeDtypeStruct((16, m), F32)],
        compiler_params=_params(("arbitrary",)),
        name="even_proj",
    )(x, w, wt, bt)


def matmul_rope(x, w, cos, sin, tm, tn, rope_cols):
    m, k = x.shape
    n = w.shape[1]
    assert n % tn == 0 and rope_cols % tn == 0
    return pl.pallas_call(
        functools.partial(_mm_rope_kernel, tn=tn, rope_cols=rope_cols),
        grid=(m // tm,),
        in_specs=[pl.BlockSpec((tm, k), lambda i: (i, 0)),
                  pl.BlockSpec((k, n), lambda i: (0, 0)),
                  pl.BlockSpec((tm, LANES), lambda i: (i, 0)),
                  pl.BlockSpec((tm, LANES), lambda i: (i, 0))],
        out_specs=pl.BlockSpec((tm, n), lambda i: (i, 0)),
        out_shape=jax.ShapeDtypeStruct((m, n), F32),
        compiler_params=_params(("parallel",)),
        name="matmul_rope",
    )(x, w, cos, sin)


def _cumsum_kernel(x_ref, o_ref, carry_ref):
    @pl.when(pl.program_id(0) == 0)
    def _():
        carry_ref[...] = jnp.zeros_like(carry_ref)

    x = x_ref[...]
    w = x.shape[1]
    tri = (_iota((w, w), 0) <= _iota((w, w), 1)).astype(F32)
    c = _dot(x, tri, precision=HIGHEST) + carry_ref[...]
    o_ref[...] = c
    carry_ref[...] = c[:, w - 1:w]


def cumsum_lanes(x, chunk):
    r, l = x.shape
    return pl.pallas_call(
        _cumsum_kernel,
        grid=(l // chunk,),
        in_specs=[pl.BlockSpec((r, chunk), lambda i: (0, i))],
        out_specs=pl.BlockSpec((r, chunk), lambda i: (0, i)),
        out_shape=jax.ShapeDtypeStruct((r, l), F32),
        scratch_shapes=[pltpu.VMEM((r, 1), F32)],
        compiler_params=_params(("arbitrary",)),
        name="cumsum_lanes",
    )(x)


def _fox_prompt_kernel(q_ref, k_ref, v_ref, ck_ref, rest_ref, o_ref, *, tq):
    del rest_ref
    qi = pl.program_id(2)
    strip = min(FOX_Q_STRIP, tq)
    n_strips = tq // strip
    chains = [(h, r) for h in range(2) for r in range(n_strips)]
    lane = _iota((strip, LANES), 1)
    q_chain = []
    for h, r in chains:
        q = q_ref[r * strip:(r + 1) * strip, :] * (HEAD_DIM ** -0.5)
        keep = (lane < HEAD_DIM) if h == 0 else (lane >= HEAD_DIM)
        q_chain.append(jnp.where(keep, q, jnp.zeros_like(q)))

    lane_kv = _iota((tq, LANES), 1)
    den_lane = (HEAD_DIM, 0)
    own = (jnp.where(lane_kv < HEAD_DIM, 1.0, 0.0).astype(BF16), jnp.where(lane_kv >= HEAD_DIM, 1.0, 0.0).astype(BF16))
    den = tuple(jnp.where(lane_kv == d, 1.0, 0.0).astype(BF16) for d in den_lane)

    def step(j, carry, masked):
        start = pl.multiple_of(j * tq, tq)
        kb = k_ref[pl.ds(start, tq), :]
        vb = v_ref[pl.ds(start, tq), :]
        v_heads = (vb * own[0] + den[0], vb * own[1] + den[1])
        ck = ck_ref[0, 0, j]
        out = []
        for (h, r), qc, (m, acc) in zip(chains, q_chain, carry):
            s = _nt_dot(qc, kb) - ck[h:h + 1, :]
            if masked:
                s = jnp.where(_iota(s.shape, 1) <= _iota(s.shape, 0) + r * strip, s, NEG_INF)
            m_new = jnp.maximum(m, jnp.max(s, axis=1, keepdims=True))
            p = jnp.exp(s - m_new).astype(BF16)
            acc = jnp.exp(m - m_new) * acc + _dot(p, v_heads[h])
            out.append((m_new, acc))
        return tuple(out)

    init1 = (jnp.full((strip, 1), NEG_INF, F32), jnp.zeros((strip, LANES), F32))
    carry = lax.fori_loop(0, qi, lambda j, c: step(j, c, False), (init1,) * len(chains))
    final = step(qi, carry, True)
    for r in range(n_strips):
        a0, a1 = final[r][1], final[n_strips + r][1]
        o0 = a0 / a0[:, den_lane[0]:den_lane[0] + 1]
        o1 = a1 / a1[:, den_lane[1]:den_lane[1] + 1]
        o_ref[r * strip:(r + 1) * strip, :] = jnp.where(lane < HEAD_DIM, o0, o1).astype(o_ref.dtype)


def fox_prompt(qkv, ck, rest, n_batch, seq, tq):
    nq = seq // tq
    pairs = FOX_HEADS // 2
    return pl.pallas_call(
        functools.partial(_fox_prompt_kernel, tq=tq),
        grid=(n_batch, pairs, nq),
        in_specs=[pl.BlockSpec((tq, LANES), lambda b, h, i: (b * nq + i, h)),
                  pl.BlockSpec((seq, LANES), lambda b, h, i: (b, pairs + h)),
                  pl.BlockSpec((seq, LANES), lambda b, h, i: (b, 2 * pairs + h)),
                  pl.BlockSpec((1, 1, nq, 2, tq), lambda b, h, i: (b, h, 0, 0, 0)),
                  pl.BlockSpec(memory_space=pl.ANY)],
        out_specs=pl.BlockSpec((tq, LANES), lambda b, h, i: (b * nq + i, h)),
        out_shape=jax.ShapeDtypeStruct(rest.shape, rest.dtype),
        input_output_aliases={4: 0},
        compiler_params=_params(("parallel", "parallel", "arbitrary")),
        name="fox_prompt",
    )(qkv, qkv, qkv, ck, rest)


def _block_diag_rows(full):
    rowh = _iota((FOX_HEADS, HEAD_DIM), 0)
    out = jnp.zeros((FOX_HEADS, HEAD_DIM), F32)
    for h in range(FOX_HEADS):
        out = out + jnp.where(rowh == h, full[:, h * HEAD_DIM:(h + 1) * HEAD_DIM], 0.0)
    return out


def _fox_sample_kernel(pt_ref, qbd_ref, q_ref, kn_ref, vbd_ref, ln_ref, *refs, pages_per_step):
    del pt_ref
    pp = pages_per_step
    k_refs, v_refs, lf_refs = refs[:pp], refs[pp:2 * pp], refs[2 * pp:3 * pp]
    o_ref = refs[3 * pp]
    m_ref, l_ref, acc_ref, carry_ref = refs[3 * pp + 1:]
    t = pl.program_id(1)
    scale = HEAD_DIM ** -0.5

    @pl.when(t == 0)
    def _():
        m_ref[...] = jnp.sum(q_ref[0] * kn_ref[0], axis=1, keepdims=True) * scale
        l_ref[...] = jnp.ones_like(l_ref)
        acc_ref[...] = vbd_ref[0]
        carry_ref[...] = ln_ref[0]

    qb = (qbd_ref[0] * scale).astype(BF16)
    lane = _iota((FOX_HEADS, PAGE_SIZE), 1)
    width = FOX_HEADS * HEAD_DIM
    carry = carry_ref[...]
    scores = []
    for r in range(pp):
        lf = lf_refs[r][0, 0]
        x = lf
        for sh in (1, 2, 4, 8, 16, 32, 64):
            x = x + jnp.where(lane + sh < PAGE_SIZE, pltpu.roll(x, PAGE_SIZE - sh, 1), 0.0)
        kp = k_refs[r][0, 0].reshape(width, PAGE_SIZE).astype(BF16)
        scores.append(_dot(qb, kp) + ((x - lf) + carry))
        carry = carry + x[:, 0:1]
    carry_ref[...] = carry
    s = jnp.concatenate(scores, axis=1)
    m = m_ref[...]
    m_new = jnp.maximum(m, jnp.max(s, axis=1, keepdims=True))
    alpha = jnp.exp(m - m_new)
    p = jnp.exp(s - m_new)
    l_ref[...] = alpha * l_ref[...] + jnp.sum(p, axis=1, keepdims=True)
    m_ref[...] = m_new
    pb = p.astype(BF16)
    acc = alpha * acc_ref[...]
    for r in range(pp):
        vp = v_refs[r][0, 0].reshape(width, PAGE_SIZE).astype(BF16)
        acc = acc + _nt_dot(pb[:, r * PAGE_SIZE:(r + 1) * PAGE_SIZE], vp)
    acc_ref[...] = acc

    @pl.when(t == pl.num_programs(1) - 1)
    def _():
        o_ref[0] = _block_diag_rows(acc / l_ref[...])


def fox_sample(page_table, qbd, q, kn, vbd, ln, cache_kt, cache_vt, cache_lft, layer, pages_per_step):
    bd, n_pages = page_table.shape
    pp = pages_per_step
    steps = n_pages // pp
    width = FOX_HEADS * HEAD_DIM

    def page(b, t, pt, r):
        return pt[b, n_pages - 1 - (t * pp + r)]

    kv_specs = [pl.BlockSpec((1, 1, FOX_HEADS, HEAD_DIM, PAGE_SIZE),
                             functools.partial(lambda b, t, pt, r: (layer, page(b, t, pt, r), 0, 0, 0), r=r))
                for r in range(pp)]
    lf_specs = [pl.BlockSpec((1, 1, FOX_HEADS, PAGE_SIZE),
                             functools.partial(lambda b, t, pt, r: (layer, page(b, t, pt, r), 0, 0), r=r))
                for r in range(pp)]
    tok = pl.BlockSpec((1, FOX_HEADS, HEAD_DIM), lambda b, t, pt: (b, 0, 0))
    wide = pl.BlockSpec((1, FOX_HEADS, width), lambda b, t, pt: (b, 0, 0))
    grid_spec = pltpu.PrefetchScalarGridSpec(
        num_scalar_prefetch=1,
        grid=(bd, steps),
        in_specs=[wide, tok, tok, wide, pl.BlockSpec((1, FOX_HEADS, 1), lambda b, t, pt: (b, 0, 0))]
        + kv_specs + kv_specs + lf_specs,
        out_specs=tok,
        scratch_shapes=[pltpu.VMEM((FOX_HEADS, 1), F32), pltpu.VMEM((FOX_HEADS, 1), F32),
                        pltpu.VMEM((FOX_HEADS, width), F32), pltpu.VMEM((FOX_HEADS, 1), F32)],
    )
    return pl.pallas_call(
        functools.partial(_fox_sample_kernel, pages_per_step=pp),
        grid_spec=grid_spec,
        out_shape=jax.ShapeDtypeStruct((bd, FOX_HEADS, HEAD_DIM), F32),
        compiler_params=_params(("parallel", "arbitrary")),
        name="fox_sample",
    )(page_table, qbd, q, kn, vbd, ln, *([cache_kt] * pp), *([cache_vt] * pp), *([cache_lft] * pp))


def _ssd_epilogue(y, xs, z, dskip_e, normw):
    y = (y + dskip_e * xs) * _silu(z)
    half = M_INNER // M_GROUPS
    outs = []
    for g in range(M_GROUPS):
        yg = y[:, g * half:(g + 1) * half]
        ms = jnp.sum(yg * yg, axis=1, keepdims=True) * (1.0 / half)
        outs.append(yg * lax.rsqrt(ms + RMS_EPS))
    return jnp.concatenate(outs, axis=1) * normw


def _ssd_prompt_kernel(xbc_ref, z_ref, dt_ref, cw_ref, cb_ref, nega_ref, dskip_ref, normw_ref, rest_ref,
                       o_ref, st_ref, ext_ref, h_ref):
    del rest_ref
    c = pl.program_id(1)
    L = SSD_CHUNK
    pad = SUBLANES

    @pl.when(c == 0)
    def _():
        ext_ref[0:pad, :] = jnp.zeros((pad, CONV_CH), F32)
        h_ref[...] = jnp.zeros_like(h_ref)

    ext_ref[pad:pad + L, :] = xbc_ref[...]
    acc = ext_ref[pad:pad + L, :] * cw_ref[CONV_W - 1:CONV_W, :]
    for j in range(CONV_W - 1):
        off = pad - (CONV_W - 1) + j
        acc = acc + ext_ref[off:off + L, :] * cw_ref[j:j + 1, :]
    u = _silu(acc + cb_ref[...])
    ext_ref[0:pad, :] = ext_ref[L:L + pad, :]

    xs = u[:, :M_INNER]
    dt_t = dt_ref[...]
    cum_t = _dot(dt_t * nega_ref[...], (_iota((L, L), 0) <= _iota((L, L), 1)).astype(F32),
                 precision=HIGHEST)
    eye = (_iota((L, L), 0) == _iota((L, L), 1)).astype(F32)
    cols = _nt_dot(eye, jnp.concatenate([cum_t, dt_t], axis=0), precision=HIGHEST)
    cum_last = cum_t[:, L - 1:L]
    tail_t = jnp.exp(cum_last - cum_t) * dt_t
    tril = _iota((L, L), 0) >= _iota((L, L), 1)
    lane = _iota((L, LANES), 1)
    rowi = _iota((L, LANES), 0)

    y_pairs = []
    for g in range(M_GROUPS):
        bm = u[:, M_INNER + g * D_STATE:M_INNER + (g + 1) * D_STATE]
        cm = u[:, M_INNER + BC_W + g * D_STATE:M_INNER + BC_W + (g + 1) * D_STATE]
        bmb = bm.astype(BF16)
        cmb = cm.astype(BF16)
        cb = _nt_dot(cmb, bmb)
        for pr in range(HPG // 2):
            pidx = g * (HPG // 2) + pr
            xs_pair = xs[:, pidx * LANES:(pidx + 1) * LANES]
            xs_pair_b = xs_pair.astype(BF16)
            h0 = h_ref[pidx]
            ych = _nt_dot(cmb, h0.astype(BF16))
            yw = []
            for k in range(2):
                hd = 2 * pidx + k
                diff = cols[:, hd:hd + 1] - cum_t[hd:hd + 1, :]
                decay = jnp.exp(jnp.where(tril, diff, NEG_INF))
                w = cb * decay * dt_t[hd:hd + 1, :]
                yw.append(_dot(w.astype(BF16), xs_pair_b))
            e0 = jnp.exp(cols[:, 2 * pidx:2 * pidx + 1])
            e1 = jnp.exp(cols[:, 2 * pidx + 1:2 * pidx + 2])
            first = lane < M_HEADDIM
            y_pairs.append(jnp.where(first, yw[0], yw[1]) + ych * jnp.where(first, e0, e1))
            top = rowi < M_HEADDIM
            tail_m = jnp.where(top, jnp.broadcast_to(tail_t[2 * pidx:2 * pidx + 1, :], (L, L)),
                               jnp.broadcast_to(tail_t[2 * pidx + 1:2 * pidx + 2, :], (L, L)))
            dec_m = jnp.where(top, jnp.exp(cum_last[2 * pidx:2 * pidx + 1, :]),
                              jnp.exp(cum_last[2 * pidx + 1:2 * pidx + 2, :]))
            xt = xs_pair.T * tail_m
            h_ref[pidx] = h0 * dec_m + _dot(xt.astype(BF16), bmb)

    y = jnp.concatenate(y_pairs, axis=1)
    o_ref[...] = _ssd_epilogue(y, xs, z_ref[...], dskip_ref[...], normw_ref[...]).astype(o_ref.dtype)
    st_ref[0] = h_ref[...]


def ssd_prompt(xz, dt_rows, conv_w, conv_b, nega, dskip_e, normw, rest, n_batch, seq):
    L = SSD_CHUNK
    nc = seq // L
    pairs = M_HEADS // 2
    const = lambda b, c: (0, 0)
    return pl.pallas_call(
        _ssd_prompt_kernel,
        grid=(n_batch, nc),
        in_specs=[pl.BlockSpec((L, CONV_CH), lambda b, c: (b * nc + c, 0)),
                  pl.BlockSpec((L, M_INNER), lambda b, c: (b * nc + c, CONV_CH // M_INNER)),
                  pl.BlockSpec((M_HEADS, L), lambda b, c: (1, b * nc + c)),
                  pl.BlockSpec((CONV_W, CONV_CH), const),
                  pl.BlockSpec((1, CONV_CH), const),
                  pl.BlockSpec((M_HEADS, 1), const),
                  pl.BlockSpec((1, M_INNER), const),
                  pl.BlockSpec((1, M_INNER), const),
                  pl.BlockSpec(memory_space=pl.ANY)],
        out_specs=[pl.BlockSpec((L, M_INNER), lambda b, c: (b * nc + c, 0)),
                   pl.BlockSpec((1, pairs, LANES, D_STATE), lambda b, c: (b, 0, 0, 0))],
        out_shape=[jax.ShapeDtypeStruct(rest.shape, rest.dtype),
                   jax.ShapeDtypeStruct((n_batch, pairs, LANES, D_STATE), F32)],
        input_output_aliases={8: 0},
        scratch_shapes=[pltpu.VMEM((L + SUBLANES, CONV_CH), F32),
                        pltpu.VMEM((pairs, LANES, D_STATE), F32)],
        compiler_params=_params(("parallel", "arbitrary")),
        name="ssd_prompt",
    )(xz, xz, dt_rows, conv_w, conv_b, nega, dskip_e, normw, rest)


def _ssd_sample_kernel(x_ref, wdt_ref, dtb_ref, nega_ref, xbc_ref, ctx_ref, z_ref, cw_ref, cb_ref,
                       dskip_ref, normw_ref, h0_ref, o_ref, hn_ref,
                       u_ref, coef_t_ref, dec_t_ref, dec_ref, dtx_ref, *, seqs_per_step):
    sb = seqs_per_step
    i = pl.program_id(0)
    nseq = x_ref.shape[0]

    @pl.when(i == 0)
    def _():
        acc = xbc_ref[...] * cw_ref[CONV_W - 1:CONV_W, :]
        for j in range(CONV_W - 1):
            acc = acc + ctx_ref[j] * cw_ref[j:j + 1, :]
        u = _silu(acc + cb_ref[...])
        u_ref[...] = u
        dt = _softplus(_dot(x_ref[...], wdt_ref[...], precision=HIGHEST) + dtb_ref[...])
        dec = jnp.exp(dt * nega_ref[...])
        coef = dt * u[:, :M_INNER]
        dec_ref[...] = dec
        dtx_ref[...] = coef
        for blk in range(M_INNER // LANES):
            sl = slice(blk * LANES, (blk + 1) * LANES)
            coef_t_ref[sl, :] = coef[:, sl].T
            dec_t_ref[sl, :] = dec[:, sl].T

    base = pl.multiple_of(i * sb, sb)
    ub = u_ref[pl.ds(base, sb), :]
    lane_seq = _iota((M_INNER, nseq), 1)
    rows = _iota((sb, M_INNER // M_GROUPS), 0)
    ch = [jnp.zeros((sb, M_INNER // M_GROUPS), F32) for _ in range(M_GROUPS)]
    for s in range(sb):
        onehot = lane_seq == base + s
        cx = jnp.sum(jnp.where(onehot, coef_t_ref[...], 0.0), axis=1, keepdims=True)
        dc = jnp.sum(jnp.where(onehot, dec_t_ref[...], 0.0), axis=1, keepdims=True)
        for g in range(M_GROUPS):
            brow = ub[s:s + 1, M_INNER + g * D_STATE:M_INNER + (g + 1) * D_STATE]
            cblk = ub[:, M_INNER + BC_W + g * D_STATE:M_INNER + BC_W + (g + 1) * D_STATE]
            hg = h0_ref[s, g * HPG:(g + 1) * HPG].reshape(HPG * M_HEADDIM, D_STATE)
            r = _nt_dot(cblk.astype(BF16), hg.astype(BF16))
            ch[g] = ch[g] + jnp.where(rows == s, r, 0.0)
            lo = g * HPG * M_HEADDIM
            hn = hg * dc[lo:lo + HPG * M_HEADDIM] + cx[lo:lo + HPG * M_HEADDIM] * brow
            hn_ref[s, g * HPG:(g + 1) * HPG] = hn.reshape(HPG, M_HEADDIM, D_STATE)

    xs = ub[:, :M_INNER]
    dec = dec_ref[pl.ds(base, sb), :]
    coef = dtx_ref[pl.ds(base, sb), :]
    ys = []
    half = M_INNER // M_GROUPS
    for g in range(M_GROUPS):
        bm = ub[:, M_INNER + g * D_STATE:M_INNER + (g + 1) * D_STATE]
        cm = ub[:, M_INNER + BC_W + g * D_STATE:M_INNER + BC_W + (g + 1) * D_STATE]
        cb = jnp.sum(cm * bm, axis=1, keepdims=True)
        ys.append(cb * coef[:, g * half:(g + 1) * half] + ch[g] * dec[:, g * half:(g + 1) * half])
    y = jnp.concatenate(ys, axis=1)
    o_ref[...] = _ssd_epilogue(y, xs, z_ref[...], dskip_ref[...], normw_ref[...])


def ssd_sample(x_s, wdt_e, dtb_e, nega_e, xbc_s, ctx, z_s, conv_w, conv_b, dskip_e, normw, h0, seqs_per_step):
    bd = x_s.shape[0]
    sb = seqs_per_step
    const = lambda i: (0, 0)
    return pl.pallas_call(
        functools.partial(_ssd_sample_kernel, seqs_per_step=sb),
        grid=(bd // sb,),
        in_specs=[pl.BlockSpec((bd, D_MODEL), const),
                  pl.BlockSpec((D_MODEL, M_INNER), const),
                  pl.BlockSpec((1, M_INNER), const),
                  pl.BlockSpec((1, M_INNER), const),
                  pl.BlockSpec((bd, CONV_CH), const),
                  pl.BlockSpec((CONV_W - 1, bd, CONV_CH), lambda i: (0, 0, 0)),
                  pl.BlockSpec((sb, M_INNER), lambda i: (i, 0)),
                  pl.BlockSpec((CONV_W, CONV_CH), const),
                  pl.BlockSpec((1, CONV_CH), const),
                  pl.BlockSpec((1, M_INNER), const),
                  pl.BlockSpec((1, M_INNER), const),
                  pl.BlockSpec((sb, M_HEADS, M_HEADDIM, D_STATE), lambda i: (i, 0, 0, 0))],
        out_specs=[pl.BlockSpec((sb, M_INNER), lambda i: (i, 0)),
                   pl.BlockSpec((sb, M_HEADS, M_HEADDIM, D_STATE), lambda i: (i, 0, 0, 0))],
        out_shape=[jax.ShapeDtypeStruct((bd, M_INNER), F32),
                   jax.ShapeDtypeStruct((bd, M_HEADS, M_HEADDIM, D_STATE), F32)],
        scratch_shapes=[pltpu.VMEM((bd, CONV_CH), F32),
                        pltpu.VMEM((M_INNER, bd), F32),
                        pltpu.VMEM((M_INNER, bd), F32),
                        pltpu.VMEM((bd, M_INNER), F32),
                        pltpu.VMEM((bd, M_INNER), F32)],
        compiler_params=_params(("arbitrary",)),
        name="ssd_sample",
    )(x_s, wdt_e, dtb_e, nega_e, xbc_s, ctx, z_s, conv_w, conv_b, dskip_e, normw, h0)


def _to_half(x, src_half, dst_half):
    return x if src_half == dst_half else pltpu.roll(x, HEAD_DIM, 1)


def _swa_prompt_kernel(sink_ref, q_ref, kp_ref, kc_ref, vp_ref, vc_ref, rest_ref, o_ref):
    del rest_ref
    n = pl.program_id(1)
    W = WINDOW
    scale = HEAD_DIM ** -0.5
    lane = _iota((W, LANES), 1)
    rows4 = _iota((SW_GROUP * W, 2 * W), 0)
    t = rows4 % W
    col = _iota((SW_GROUP * W, 2 * W), 1)
    valid = (col >= t) & (col <= t + W) & ((n > 0) | (col >= W))
    rcol = _iota((SW_GROUP * W, 1), 0) // W
    outs = [None] * SW_HEADS
    for j in range(SW_KV_HEADS):
        ch, hf = j // 2, j % 2
        kk = jnp.concatenate([kp_ref[:, ch * LANES:(ch + 1) * LANES],
                              kc_ref[:, ch * LANES:(ch + 1) * LANES]], axis=0).astype(BF16)
        vv = jnp.concatenate([vp_ref[:, ch * LANES:(ch + 1) * LANES],
                              vc_ref[:, ch * LANES:(ch + 1) * LANES]], axis=0).astype(BF16)
        qs = []
        sink = jnp.zeros((SW_GROUP * W, 1), F32)
        for g in range(SW_GROUP):
            hq = j * SW_GROUP + g
            qc = q_ref[:, (hq // 2) * LANES:(hq // 2 + 1) * LANES] * scale
            qc = _to_half(qc, hq % 2, hf)
            keep = (lane < HEAD_DIM) if hf == 0 else (lane >= HEAD_DIM)
            qs.append(jnp.where(keep, qc, 0.0))
            sink = jnp.where(rcol == g, sink_ref[hq], sink)
        q4 = jnp.concatenate(qs, axis=0).astype(BF16)
        s = jnp.where(valid, _nt_dot(q4, kk), NEG_INF)
        m = jnp.maximum(jnp.max(s, axis=1, keepdims=True), sink)
        p = jnp.exp(s - m)
        den = jnp.sum(p, axis=1, keepdims=True) + jnp.exp(sink - m)
        o = _dot(p.astype(BF16), vv) / den
        for g in range(SW_GROUP):
            hq = j * SW_GROUP + g
            outs[hq] = _to_half(o[g * W:(g + 1) * W], hf, hq % 2)
    for c in range(SW_HEADS // 2):
        o_ref[:, c * LANES:(c + 1) * LANES] = jnp.where(lane < HEAD_DIM, outs[2 * c],
                                                        outs[2 * c + 1]).astype(o_ref.dtype)


def swa_prompt(sinks, u_odd, rest, n_batch, seq):
    nb = seq // WINDOW
    kcol = ODD_MIX // KV_W
    return pl.pallas_call(
        _swa_prompt_kernel,
        grid=(n_batch, nb),
        in_specs=[pl.BlockSpec(memory_space=pltpu.SMEM),
                  pl.BlockSpec((WINDOW, ODD_MIX), lambda b, n: (b * nb + n, 0)),
                  pl.BlockSpec((WINDOW, KV_W), lambda b, n: (b * nb + jnp.maximum(n - 1, 0), kcol)),
                  pl.BlockSpec((WINDOW, KV_W), lambda b, n: (b * nb + n, kcol)),
                  pl.BlockSpec((WINDOW, KV_W), lambda b, n: (b * nb + jnp.maximum(n - 1, 0), kcol + 1)),
                  pl.BlockSpec((WINDOW, KV_W), lambda b, n: (b * nb + n, kcol + 1)),
                  pl.BlockSpec(memory_space=pl.ANY)],
        out_specs=pl.BlockSpec((WINDOW, ODD_MIX), lambda b, n: (b * nb + n, 0)),
        out_shape=jax.ShapeDtypeStruct(rest.shape, rest.dtype),
        input_output_aliases={6: 0},
        compiler_params=_params(("parallel", "arbitrary")),
        name="swa_prompt",
    )(sinks, u_odd, u_odd, u_odd, u_odd, u_odd, rest)


def _swa_sample_kernel(sink_ref, q_ref, kn_ref, vn_ref, bk_ref, bv_ref, o_ref, *, seqs_per_step):
    scale = HEAD_DIM ** -0.5
    rowg = _iota((SW_HEADS, HEAD_DIM), 0) // SW_GROUP
    sink = sink_ref[...]
    for s in range(seqs_per_step):
        q16 = q_ref[s] * scale
        qbd = jnp.concatenate([jnp.where(rowg == j, q16, 0.0) for j in range(SW_KV_HEADS)], axis=1)
        kb = bk_ref[s].reshape(KV_W, -1).astype(BF16)
        vb = bv_ref[s].reshape(KV_W, -1).astype(BF16)
        sc = _dot(qbd.astype(BF16), kb)
        s_new = jnp.sum(qbd * kn_ref[s:s + 1, :], axis=1, keepdims=True)
        m = jnp.maximum(jnp.maximum(jnp.max(sc, axis=1, keepdims=True), s_new), sink)
        p = jnp.exp(sc - m)
        pn = jnp.exp(s_new - m)
        den = jnp.sum(p, axis=1, keepdims=True) + pn + jnp.exp(sink - m)
        full = (_nt_dot(p.astype(BF16), vb) + pn * vn_ref[s:s + 1, :]) / den
        o16 = jnp.zeros((SW_HEADS, HEAD_DIM), F32)
        for j in range(SW_KV_HEADS):
            o16 = o16 + jnp.where(rowg == j, full[:, j * HEAD_DIM:(j + 1) * HEAD_DIM], 0.0)
        o_ref[s] = o16


def swa_sample(sinks_col, q, kn, vn, buf_k, buf_v, seqs_per_step):
    bd, _, _, lw = buf_k.shape
    sb = seqs_per_step
    return pl.pallas_call(
        functools.partial(_swa_sample_kernel, seqs_per_step=sb),
        grid=(bd // sb,),
        in_specs=[pl.BlockSpec((SW_HEADS, 1), lambda i: (0, 0)),
                  pl.BlockSpec((sb, SW_HEADS, HEAD_DIM), lambda i: (i, 0, 0)),
                  pl.BlockSpec((sb, KV_W), lambda i: (i, 0)),
                  pl.BlockSpec((sb, KV_W), lambda i: (i, 0)),
                  pl.BlockSpec((sb, SW_KV_HEADS, HEAD_DIM, lw), lambda i: (i, 0, 0, 0)),
                  pl.BlockSpec((sb, SW_KV_HEADS, HEAD_DIM, lw), lambda i: (i, 0, 0, 0))],
        out_specs=pl.BlockSpec((sb, SW_HEADS, HEAD_DIM), lambda i: (i, 0, 0)),
        out_shape=jax.ShapeDtypeStruct((bd, SW_HEADS, HEAD_DIM), F32),
        compiler_params=_params(("parallel",)),
        name="swa_sample",
    )(sinks_col, q, kn, vn, buf_k, buf_v)


def _layer_norm(h, g, b):
    mu = jnp.mean(h, axis=1, keepdims=True)
    d = h - mu
    var = jnp.mean(d * d, axis=1, keepdims=True)
    return d * lax.rsqrt(var + LN_EPS) * g + b


def _mix_route_kernel(*refs, alpha, n_feat):
    am_refs, wo_refs = refs[:n_feat], refs[n_feat:2 * n_feat]
    x_ref, g_ref, b_ref, wr_ref, br_ref, x1_ref, rw_ref, re_ref = refs[2 * n_feat:]
    mix = _dot(am_refs[0][...].astype(BF16), wo_refs[0][...])
    for a_ref, w_ref in zip(am_refs[1:], wo_refs[1:]):
        mix = mix + _dot(a_ref[...].astype(BF16), w_ref[...])
    x1 = _layer_norm(alpha * x_ref[...] + mix, g_ref[...], b_ref[...])
    x1_ref[...] = x1
    wr = wr_ref[...]
    x_hi = x1.astype(BF16)
    w_hi = wr.astype(BF16)
    x_lo = (x1 - x_hi.astype(F32)).astype(BF16)
    w_lo = (wr - w_hi.astype(F32)).astype(BF16)
    logits = _dot(x_hi, w_hi) + _dot(x_lo, w_hi) + _dot(x_hi, w_lo) + br_ref[...]
    lane_i = _iota(logits.shape, 1)
    lane = lane_i.astype(F32)
    big = float(LANES)
    gl = jnp.where(lane_i < N_EXPERT_GROUPS, logits, NEG_INF)
    gmax = jnp.max(gl, axis=1, keepdims=True)
    grp = jnp.min(jnp.where(gl == gmax, lane, big), axis=1, keepdims=True)
    pg = 1.0 / jnp.sum(jnp.exp(gl - gmax), axis=1, keepdims=True)
    rel = lane - (N_EXPERT_GROUPS + grp * EXPERTS_PER_GROUP)
    el = jnp.where(rel >= 0.0, jnp.where(rel < EXPERTS_PER_GROUP, logits, NEG_INF), NEG_INF)
    v1 = jnp.max(el, axis=1, keepdims=True)
    i1 = jnp.min(jnp.where(el == v1, lane, big), axis=1, keepdims=True)
    el2 = jnp.where(lane == i1, NEG_INF, el)
    v2 = jnp.max(el2, axis=1, keepdims=True)
    i2 = jnp.min(jnp.where(el2 == v2, lane, big), axis=1, keepdims=True)
    e = jnp.exp(v2 - v1)
    w1 = pg / (1.0 + e)
    w2 = w1 * e
    rw_ref[...] = jnp.where(lane_i == 0, w1, jnp.where(lane_i == 1, w2, 0.0))
    e1 = (i1 - N_EXPERT_GROUPS).astype(jnp.int32)
    e2 = (i2 - N_EXPERT_GROUPS).astype(jnp.int32)
    re_ref[...] = jnp.where(lane_i == 0, e1, jnp.where(lane_i == 1, e2, 0))


def mix_route(feats, wos, x, g, b, wr, br, alpha, tm):
    m, d = x.shape
    const = lambda i: (0, 0)
    row = lambda i: (i, 0)
    return pl.pallas_call(
        functools.partial(_mix_route_kernel, alpha=alpha, n_feat=len(feats)),
        grid=(m // tm,),
        in_specs=[pl.BlockSpec((tm, a.shape[1]), row) for a in feats]
        + [pl.BlockSpec((w.shape[0], d), const) for w in wos]
        + [pl.BlockSpec((tm, d), row), pl.BlockSpec((1, d), const), pl.BlockSpec((1, d), const),
           pl.BlockSpec((d, LANES), const), pl.BlockSpec((1, LANES), const)],
        out_specs=[pl.BlockSpec((tm, d), row),
                   pl.BlockSpec((tm, LANES), row), pl.BlockSpec((tm, LANES), row)],
        out_shape=[jax.ShapeDtypeStruct((m, d), F32),
                   jax.ShapeDtypeStruct((m, LANES), F32), jax.ShapeDtypeStruct((m, LANES), jnp.int32)],
        compiler_params=_params(("parallel",)),
        name="mix_route",
    )(*feats, *wos, x, g, b, wr, br)


def _experts_kernel(te_ref, tv_ref, x_ref, wg_ref, wu_ref, wd_ref, *rest, first_tile):
    y_ref = rest[-1]
    t = first_tile + pl.program_id(0)

    @pl.when(tv_ref[t] > 0)
    def _():
        x = x_ref[...].astype(BF16)
        h = _silu(_dot(x, wg_ref[0, 0].astype(BF16))) * _dot(x, wu_ref[0, 0].astype(BF16))
        y_ref[...] = _dot(h.astype(BF16), wd_ref[0, 0].astype(BF16))

    @pl.when(tv_ref[t] == 0)
    def _():
        y_ref[...] = jnp.zeros_like(y_ref)


def experts(tile_expert, tile_valid, xs, wg, wu, wd, layer, te, first_tile, n_rows, y_prev):
    r, d = xs.shape
    ff = wg.shape[3]
    ex = lambda t, e, v: (layer, e[first_tile + t], 0, 0)
    in_specs = [pl.BlockSpec((te, d), lambda t, e, v: (t, 0)),
                pl.BlockSpec((1, 1, d, ff), ex), pl.BlockSpec((1, 1, d, ff), ex), pl.BlockSpec((1, 1, ff, d), ex)]
    args = [tile_expert, tile_valid, xs, wg, wu, wd]
    aliases = {}
    if y_prev is not None:
        in_specs.append(pl.BlockSpec(memory_space=pl.ANY))
        aliases = {len(args): 0}
        args.append(y_prev)
    grid_spec = pltpu.PrefetchScalarGridSpec(
        num_scalar_prefetch=2,
        grid=(r // te,),
        in_specs=in_specs,
        out_specs=pl.BlockSpec((te, d), lambda t, e, v: (first_tile + t, 0)),
    )
    return pl.pallas_call(
        functools.partial(_experts_kernel, first_tile=first_tile),
        grid_spec=grid_spec,
        out_shape=jax.ShapeDtypeStruct((n_rows, d), F32),
        input_output_aliases=aliases,
        compiler_params=_params(("arbitrary",)),
        name="experts",
    )(*args)


def _combine_ple_kernel(x1_ref, y0_ref, y1_ref, rw_ref, pp_ref, pt_ref, g_ref, b_ref, wg_ref, bg_ref, wp_ref,
                        o_ref, *, alpha, prompt_tiles, first_tile):
    rw = rw_ref[...]
    f = rw[:, 0:1] * y0_ref[...] + rw[:, 1:2] * y1_ref[...]
    x2 = _layer_norm(alpha * x1_ref[...] + f, g_ref[...], b_ref[...])
    gl = _dot(x2.astype(BF16), wg_ref[...]) + bg_ref[...]
    gate = 1.0 / (1.0 + jnp.exp(-gl))
    p = jnp.where(first_tile + pl.program_id(0) < prompt_tiles, pp_ref[0], pt_ref[...])
    o_ref[...] = x2 + gate * _dot(p.astype(BF16), wp_ref[...])


def combine_ple(x1, y0, y1, rw, p_prompt, p_tail, layer, g, b, wg, bg, wp, alpha, tm, first_tile, n_tiles):
    d = x1.shape[1]
    pd = p_prompt.shape[2]
    prompt_tiles = p_prompt.shape[1] // tm
    const = lambda i: (0, 0)
    row = lambda i: (first_tile + i, 0)
    return pl.pallas_call(
        functools.partial(_combine_ple_kernel, alpha=alpha, prompt_tiles=prompt_tiles, first_tile=first_tile),
        grid=(n_tiles,),
        in_specs=[pl.BlockSpec((tm, d), row), pl.BlockSpec((tm, d), row), pl.BlockSpec((tm, d), row),
                  pl.BlockSpec((tm, LANES), row),
                  pl.BlockSpec((1, tm, pd), lambda i: (layer, jnp.minimum(first_tile + i, prompt_tiles - 1), 0)),
                  pl.BlockSpec((tm, pd), lambda i: (jnp.maximum(first_tile + i - prompt_tiles, 0), 0)),
                  pl.BlockSpec((1, d), const), pl.BlockSpec((1, d), const),
                  pl.BlockSpec((d, d), const), pl.BlockSpec((1, d), const), pl.BlockSpec((pd, d), const)],
        out_specs=pl.BlockSpec((tm, d), lambda i: (i, 0)),
        out_shape=jax.ShapeDtypeStruct((n_tiles * tm, d), F32),
        compiler_params=_params(("parallel",)),
        name="combine_ple",
    )(x1, y0, y1, rw, p_prompt, p_tail, g, b, wg, bg, wp)


def _tiles(n_tokens):
    tm = 512 if n_tokens >= 4096 else 128
    return tm, ((n_tokens + tm - 1) // tm) * tm


def _expert_tile(n_tokens):
    return 256 if n_tokens >= 4096 else 32


def _channel_and_ple(x, feats, p_prompt, p_tail, li, w_out, w, tm):
    depth = w["ln_mix_g"].shape[0]
    alpha = (2 * depth) ** 0.25
    ntp, d = x.shape
    wr = jnp.concatenate([w["w_router_group"][li],
                          jnp.moveaxis(w["w_router_expert"][li], 0, 1).reshape(d, N_EXPERTS)], axis=1)
    wr = jnp.pad(wr, ((0, 0), (0, LANES - wr.shape[1])))
    br = jnp.concatenate([w["b_router_group"][li], w["b_router_expert"][li].reshape(-1)])
    br = jnp.pad(br, (0, LANES - br.shape[0]))[None, :]
    wo = w_out.astype(BF16)
    splits = [0]
    for a in feats:
        splits.append(splits[-1] + a.shape[1])
    x1, rw, re = mix_route(feats, [wo[lo:hi] for lo, hi in zip(splits[:-1], splits[1:])], x,
                           w["ln_mix_g"][li][None], w["ln_mix_b"][li][None], wr, br, alpha, tm)

    te = _expert_tile(ntp)
    n_flat = 2 * ntp
    flat = re[:, :2].reshape(-1)
    onehot = (flat[:, None] == jnp.arange(N_EXPERTS, dtype=jnp.int32)[None, :]).astype(jnp.int32)
    running = jnp.cumsum(onehot, axis=0)
    counts = running[-1]
    padded = ((counts + te - 1) // te) * te
    gend = jnp.cumsum(padded)
    gstart = gend - padded
    pos_flat = jnp.sum(onehot * (running - 1 + gstart[None, :]), axis=1)
    n_rows = ((n_flat + N_EXPERTS * (te - 1) + te - 1) // te) * te
    row_token = (jnp.arange(n_rows, dtype=jnp.int32) % ntp).at[pos_flat].set(
        jnp.arange(n_flat, dtype=jnp.int32) // 2, mode="promise_in_bounds", unique_indices=True)
    pos_flat = pos_flat.reshape(ntp, 2)
    tile_start = jnp.arange(n_rows // te, dtype=jnp.int32) * te
    tile_expert = jnp.minimum(jnp.sum((gend[None, :] <= tile_start[:, None]).astype(jnp.int32), axis=1),
                              N_EXPERTS - 1)
    tile_valid = (tile_start < gend[-1]).astype(jnp.int32)

    def rows(a, idx):
        return a.at[idx].get(mode="promise_in_bounds")

    n_tiles = n_rows // te
    bounds = [(c * n_tiles) // EXPERT_CHUNKS for c in range(EXPERT_CHUNKS + 1)]
    y = None
    for t0, t1 in zip(bounds[:-1], bounds[1:]):
        xs = rows(x1, row_token[t0 * te:t1 * te])
        y = experts(tile_expert, tile_valid, xs, w["w_exp_gate"], w["w_exp_up"], w["w_exp_down"], li, te,
                    t0, n_rows, y)
    y0 = rows(y, pos_flat[:, 0])
    y1 = rows(y, pos_flat[:, 1])
    prompt_tiles = p_prompt.shape[1] // tm
    parts = [(0, ntp // tm)] if li + 1 < depth else [(0, prompt_tiles), (prompt_tiles, ntp // tm - prompt_tiles)]
    outs = [combine_ple(x1, y0, y1, rw, p_prompt, p_tail, li, w["ln_ffn_g"][li][None], w["ln_ffn_b"][li][None],
                        w["w_ple_gate"][li].astype(BF16), w["b_ple_gate"][li][None],
                        w["w_ple_proj"][li].astype(BF16), alpha, tm, first, n) for first, n in parts]
    return outs[0] if len(outs) == 1 else tuple(outs)


def kernel(x_prompt, x_sample, p_prompt, p_sample, cache_fox_k, cache_fox_v, cache_fox_logf, state_ssm, state_conv, cache_win_k, cache_win_v, page_table, w_in_even, b_fgate, conv_w, conv_b, dt_bias, a_log, d_skip, ssm_norm_w, w_out_even, w_in_odd, attn_sinks, w_out_odd, ln_mix_g, ln_mix_b, ln_ffn_g, ln_ffn_b, w_router_group, b_router_group, w_router_expert, b_router_expert, w_exp_gate, w_exp_up, w_exp_down, w_ple_proj, w_ple_gate, b_ple_gate):
    bp, seq, d = x_prompt.shape
    bd, t_dec, _ = x_sample.shape
    assert t_dec == 1 and d == D_MODEL
    depth = p_prompt.shape[0]
    n_pages = page_table.shape[1]
    past_len = n_pages * PAGE_SIZE
    np_tok = bp * seq
    nt = np_tok + bd
    tm, ntp = _tiles(nt)
    pad = ntp - nt

    def tokens(a_p, a_s):
        parts = [a_p.reshape(np_tok, -1), a_s.reshape(bd, -1)]
        if pad:
            parts.append(jnp.zeros((pad, parts[0].shape[1]), parts[0].dtype))
        return jnp.concatenate(parts, axis=0)

    def past_prompt(a_s, dtype=BF16):
        a_s = a_s.reshape(bd, -1).astype(dtype)
        return jnp.zeros((ntp, a_s.shape[1]), dtype).at[np_tok:nt].set(a_s)

    assert np_tok % tm == 0 and seq >= CONV_W - 1
    x = tokens(x_prompt, x_sample)
    p_all = p_prompt.reshape(depth, np_tok, -1)
    shared = dict(ln_mix_g=ln_mix_g, ln_mix_b=ln_mix_b, ln_ffn_g=ln_ffn_g, ln_ffn_b=ln_ffn_b,
                  w_router_group=w_router_group, b_router_group=b_router_group,
                  w_router_expert=w_router_expert, b_router_expert=b_router_expert,
                  w_exp_gate=w_exp_gate, w_exp_up=w_exp_up, w_exp_down=w_exp_down,
                  w_ple_proj=w_ple_proj, w_ple_gate=w_ple_gate, b_ple_gate=b_ple_gate)

    half = HEAD_DIM // 2
    inv = jnp.exp(-math.log(ROPE_THETA) * jnp.arange(half, dtype=F32) / half)
    pos = jnp.concatenate([jnp.tile(jnp.arange(seq, dtype=jnp.int32), bp),
                           jnp.full((bd,), past_len, jnp.int32), jnp.zeros((pad,), jnp.int32)])
    ang = pos.astype(F32)[:, None] * inv[None, :]
    cos_t = jnp.tile(jnp.cos(ang), (1, LANES // half))
    sin_t = jnp.tile(jnp.concatenate([-jnp.sin(ang), jnp.sin(ang)], axis=1), (1, LANES // HEAD_DIM))

    even_p, even_s, odd_p, odd_s = [], [], [], []
    for li in range(depth):
        j = li // 2
        if li % 2 == 0:
            wi = w_in_even[j]
            c0 = 3 * FOX_WIDTH
            c1 = c0 + FOX_HEADS
            c2 = c1 + M_INNER
            c3 = c2 + CONV_CH
            w_main = jnp.concatenate([wi[:, :c0], wi[:, c2:c3], wi[:, c1:c2]], axis=1).astype(BF16)
            w_small_t = jnp.concatenate([wi[:, c0:c1], wi[:, c3:]], axis=1).T
            b_small = jnp.concatenate([b_fgate[j], dt_bias[j]])[:, None]
            prompt_tiles = np_tok // tm
            k_p, v_p, xz, qkv, _, small = even_proj(x, w_main, w_small_t, b_small, tm, 0, prompt_tiles)
            k_t, v_t, xz_t, _, q_t, small_t = even_proj(x, w_main, w_small_t, b_small, tm, prompt_tiles,
                                                        ntp // tm - prompt_tiles)

            tq = min(FOX_TILE, seq)
            logf_p = small[:FOX_HEADS].reshape(FOX_HEADS, bp, seq)
            cum = cumsum_lanes(jnp.moveaxis(logf_p, 1, 0).reshape(bp * FOX_HEADS, seq), min(512, seq))
            ck = cum.reshape(bp, FOX_HEADS // 2, 2, seq // tq, tq).transpose(0, 1, 3, 2, 4)
            logf_s = small_t[:FOX_HEADS, :bd].T
            q_s = q_t[:bd].reshape(bd, FOX_HEADS, HEAD_DIM)
            k_s = k_t[:bd].reshape(bd, FOX_HEADS, HEAD_DIM)
            v_s = v_t[:bd].reshape(bd, FOX_HEADS, HEAD_DIM)
            eye = jnp.eye(FOX_HEADS, dtype=F32)[None, :, :, None]

            def block_diag(a):
                return (a[:, :, None, :] * eye).reshape(bd, FOX_HEADS, FOX_WIDTH)

            a_s = fox_sample(page_table, block_diag(q_s), q_s, k_s, block_diag(v_s), logf_s[:, :, None],
                             jnp.transpose(cache_fox_k, (0, 1, 3, 4, 2)), jnp.transpose(cache_fox_v, (0, 1, 3, 4, 2)),
                             jnp.transpose(cache_fox_logf, (0, 1, 3, 2)), j, min(FOX_PAGES_PER_STEP, n_pages))
            a_all = fox_prompt(qkv, ck, past_prompt(a_s), bp, seq, tq)

            nega = -jnp.exp(a_log[j])
            dskip_e = jnp.repeat(d_skip[j], M_HEADDIM)[None, :]
            normw = ssm_norm_w[j][None, :]
            xz_s = xz_t[:bd]
            m_s, st_s = ssd_sample(x[np_tok:nt], jnp.repeat(wi[:, c3:], M_HEADDIM, axis=1),
                                   jnp.repeat(dt_bias[j], M_HEADDIM)[None, :], jnp.repeat(nega, M_HEADDIM)[None, :],
                                   xz_s[:, :CONV_CH], jnp.moveaxis(state_conv[j], 1, 0), xz_s[:, CONV_CH:],
                                   conv_w[j], conv_b[j][None, :], dskip_e, normw, state_ssm[j], min(8, bd))
            m_all, st_p = ssd_prompt(xz, small, conv_w[j], conv_b[j][None, :], nega[:, None], dskip_e, normw,
                                     past_prompt(m_s), bp, seq)
            feats = [a_all, m_all]
            w_out = w_out_even[j]

            kp = k_p.reshape(bp, seq, FOX_HEADS, HEAD_DIM)
            vp = v_p.reshape(bp, seq, FOX_HEADS, HEAD_DIM)
            last = jnp.stack([xz[(b + 1) * seq - (CONV_W - 1):(b + 1) * seq, :CONV_CH] for b in range(bp)])
            conv_p = jnp.concatenate([jnp.zeros((bp, CONV_W - 1, CONV_CH), F32), last], axis=1)[:, -(CONV_W - 1):]
            even_p.append((kp, vp, jnp.moveaxis(logf_p, 0, 2),
                           st_p.reshape(bp, M_HEADS, M_HEADDIM, D_STATE), conv_p))
            conv_s = jnp.concatenate([state_conv[j], xz_s[:, None, :CONV_CH]], axis=1)[:, -(CONV_W - 1):]
            even_s.append((k_s[:, None], v_s[:, None], logf_s[:, None, :], st_s, conv_s))
        else:
            u = matmul_rope(x, w_in_odd[j].astype(BF16), cos_t, sin_t, tm, 256, ODD_MIX + KV_W)
            u_s = u[np_tok:nt]
            lw = cache_win_k.shape[2]
            kn = u_s[:, ODD_MIX:ODD_MIX + KV_W]
            vn = u_s[:, ODD_MIX + KV_W:]
            o_s = swa_sample(attn_sinks[j][:, None], u_s[:, :ODD_MIX].reshape(bd, SW_HEADS, HEAD_DIM), kn, vn,
                             jnp.transpose(cache_win_k[j], (0, 2, 3, 1)), jnp.transpose(cache_win_v[j], (0, 2, 3, 1)),
                             min(8, bd))
            feats = [swa_prompt(attn_sinks[j], u, past_prompt(o_s), bp, seq)]
            w_out = w_out_odd[j]

            rows = min(WINDOW, seq)
            tail = jnp.stack([u[(b + 1) * seq - rows:(b + 1) * seq, ODD_MIX:] for b in range(bp)])
            odd_p.append((tail[:, :, :KV_W].reshape(bp, rows, SW_KV_HEADS, HEAD_DIM),
                          tail[:, :, KV_W:].reshape(bp, rows, SW_KV_HEADS, HEAD_DIM)))
            ka = jnp.concatenate([cache_win_k[j], kn.reshape(bd, 1, SW_KV_HEADS, HEAD_DIM)], axis=1)[:, -lw:]
            va = jnp.concatenate([cache_win_v[j], vn.reshape(bd, 1, SW_KV_HEADS, HEAD_DIM)], axis=1)[:, -lw:]
            odd_s.append((ka, va))
        x = _channel_and_ple(x, feats, p_all, past_prompt(p_sample[li], F32)[np_tok:], li, w_out, shared, tm)

    x_prompt_out, x_tail_out = x
    yp = x_prompt_out.reshape(bp, seq, d)
    ys = x_tail_out[:bd].reshape(bd, 1, d)
    return (yp, ys,
            jnp.stack([st[0] for st in even_p]), jnp.stack([st[1] for st in even_p]),
            jnp.stack([st[2] for st in even_p]), jnp.stack([st[3] for st in even_p]),
            jnp.stack([st[4] for st in even_p]),
            jnp.stack([st[0] for st in odd_p]), jnp.stack([st[1] for st in odd_p]),
            jnp.stack([st[0] for st in even_s]), jnp.stack([st[1] for st in even_s]),
            jnp.stack([st[2] for st in even_s]), jnp.stack([st[3] for st in even_s]),
            jnp.stack([st[4] for st in even_s]),
            jnp.stack([st[0] for st in odd_s]), jnp.stack([st[1] for st in odd_s]))
```

```python
import functools
import math

import jax
import jax.numpy as jnp
from jax import lax
from jax.experimental import pallas as pl
from jax.experimental.pallas import tpu as pltpu

F32 = jnp.float32
BF16 = jnp.bfloat16
HIGHEST = lax.Precision.HIGHEST

D_MODEL = 1024
HEAD_DIM = 64
FOX_HEADS = 8
FOX_WIDTH = FOX_HEADS * HEAD_DIM
M_HEADS = 8
M_HEADDIM = 64
M_INNER = M_HEADS * M_HEADDIM
M_GROUPS = 2
HPG = M_HEADS // M_GROUPS
D_STATE = 128
CONV_W = 4
BC_W = M_GROUPS * D_STATE
CONV_CH = M_INNER + 2 * BC_W
SSD_CHUNK = 128
RMS_EPS = 1e-5
SW_HEADS = 16
SW_KV_HEADS = 4
SW_GROUP = SW_HEADS // SW_KV_HEADS
WINDOW = 128
ROPE_THETA = 10000.0
ODD_MIX = SW_HEADS * HEAD_DIM
KV_W = SW_KV_HEADS * HEAD_DIM
N_EXPERT_GROUPS = 4
EXPERTS_PER_GROUP = 8
N_EXPERTS = N_EXPERT_GROUPS * EXPERTS_PER_GROUP
EXPERT_FF = 512
PLE_DIM = 256
LN_EPS = 1e-5
PAGE_SIZE = 128

LANES = 128
SUBLANES = 8
VMEM_LIMIT = 48 * 1024 * 1024

NEG_INF = float("-inf")
EXPERT_CHUNKS = 2
FOX_PAGES_PER_STEP = 32
FOX_TILE = 1024
FOX_DIAG_PARTS = 2


def _params(sem, vmem=VMEM_LIMIT):
    return pltpu.CompilerParams(dimension_semantics=sem, vmem_limit_bytes=vmem)


def _nt_dot(a, b, precision=None):
    return lax.dot_general(a, b, (((1,), (1,)), ((), ())), precision=precision,
                           preferred_element_type=F32)


def _dot(a, b, precision=None):
    return jnp.dot(a, b, precision=precision, preferred_element_type=F32)


def _silu(x):
    return x * (1.0 / (1.0 + jnp.exp(-x)))


def _softplus(x):
    return jnp.maximum(x, 0.0) + jnp.log(1.0 + jnp.exp(-jnp.abs(x)))


def _iota(shape, dim):
    return lax.broadcasted_iota(jnp.int32, shape, dim)


def _mm_rope_kernel(x_ref, w_ref, cos_ref, sin_ref, o_ref, *, tn, rope_cols):
    xb = x_ref[...].astype(BF16)
    reps = tn // LANES
    cos = jnp.concatenate([cos_ref[...]] * reps, axis=1)
    sin = jnp.concatenate([sin_ref[...]] * reps, axis=1)
    half = HEAD_DIM // 2
    first = (_iota((xb.shape[0], tn), 1) % HEAD_DIM) < half
    for c in range(w_ref.shape[1] // tn):
        acc = _dot(xb, w_ref[:, c * tn:(c + 1) * tn])
        if c * tn < rope_cols:
            partner = jnp.where(first, pltpu.roll(acc, tn - half, 1), pltpu.roll(acc, half, 1))
            acc = acc * cos + partner * sin
        o_ref[:, c * tn:(c + 1) * tn] = acc


def _even_proj_kernel(x_ref, w_ref, wt_ref, bt_ref, k_ref, v_ref, xz_ref, qkv_ref, qt_ref, small_ref):
    x = x_ref[...]
    xb = x.astype(BF16)
    wt = wt_ref[...]
    wt_hi = wt.astype(BF16)
    wt_lo = (wt - wt_hi.astype(F32)).astype(BF16)
    x_lo = (x - xb.astype(F32)).astype(BF16)
    r = _nt_dot(wt_hi, xb) + _nt_dot(wt_lo, xb) + _nt_dot(wt_hi, x_lo) + bt_ref[...]
    small_ref[...] = jnp.where(_iota(r.shape, 0) < FOX_HEADS, -_softplus(-r), _softplus(r))
    w = FOX_WIDTH
    f32_dst = (None, k_ref, v_ref)
    for c in range(3):
        acc = _dot(xb, w_ref[:, c * w:(c + 1) * w])
        qkv_ref[:, c * w:(c + 1) * w] = acc.astype(BF16)
        if f32_dst[c] is not None:
            f32_dst[c][...] = acc
        else:
            @pl.when(pl.program_id(0) == pl.num_programs(0) - 1)
            def _():
                qt_ref[...] = acc
    for c in range(xz_ref.shape[1] // w):
        xz_ref[:, c * w:(c + 1) * w] = _dot(xb, w_ref[:, (3 + c) * w:(4 + c) * w])


def even_proj(x, w, wt, bt, tm, first_tile, n_tiles):
    k = x.shape[1]
    m = n_tiles * tm
    n = w.shape[1]
    wq = 3 * FOX_WIDTH
    row = lambda i: (i, 0)
    const = lambda i: (0, 0)
    return pl.pallas_call(
        _even_proj_kernel,
        grid=(n_tiles,),
        in_specs=[pl.BlockSpec((tm, k), lambda i: (first_tile + i, 0)), pl.BlockSpec((k, n), const),
                  pl.BlockSpec((16, k), const), pl.BlockSpec((16, 1), const)],
        out_specs=[pl.BlockSpec((tm, FOX_WIDTH), row), pl.BlockSpec((tm, FOX_WIDTH), row),
                   pl.BlockSpec((tm, n - wq), row), pl.BlockSpec((tm, wq), row),
                   pl.BlockSpec((tm, FOX_WIDTH), const), pl.BlockSpec((16, tm), lambda i: (0, i))],
        out_shape=[jax.ShapeDtypeStruct((m, FOX_WIDTH), F32), jax.ShapeDtypeStruct((m, FOX_WIDTH), F32),
                   jax.ShapeDtypeStruct((m, n - wq), F32), jax.ShapeDtypeStruct((m, wq), BF16),
                   jax.ShapeDtypeStruct((tm, FOX_WIDTH), F32), jax.ShapeDtypeStruct((16, m), F32)],
        compiler_params=_params(("arbitrary",)),
        name="even_proj",
    )(x, w, wt, bt)


def matmul_rope(x, w, cos, sin, tm, tn, rope_cols):
    m, k = x.shape
    n = w.shape[1]
    assert n % tn == 0 and rope_cols % tn == 0
    return pl.pallas_call(
        functools.partial(_mm_rope_kernel, tn=tn, rope_cols=rope_cols),
        grid=(m // tm,),
        in_specs=[pl.BlockSpec((tm, k), lambda i: (i, 0)),
                  pl.BlockSpec((k, n), lambda i: (0, 0)),
                  pl.BlockSpec((tm, LANES), lambda i: (i, 0)),
                  pl.BlockSpec((tm, LANES), lambda i: (i, 0))],
        out_specs=pl.BlockSpec((tm, n), lambda i: (i, 0)),
        out_shape=jax.ShapeDtypeStruct((m, n), F32),
        compiler_params=_params(("parallel",)),
        name="matmul_rope",
    )(x, w, cos, sin)


def _cumsum_kernel(x_ref, o_ref, carry_ref):
    @pl.when(pl.program_id(0) == 0)
    def _():
        carry_ref[...] = jnp.zeros_like(carry_ref)

    x = x_ref[...]
    w = x.shape[1]
    tri = (_iota((w, w), 0) <= _iota((w, w), 1)).astype(F32)
    c = _dot(x, tri, precision=HIGHEST) + carry_ref[...]
    o_ref[...] = c
    carry_ref[...] = c[:, w - 1:w]


def cumsum_lanes(x, chunk):
    r, l = x.shape
    return pl.pallas_call(
        _cumsum_kernel,
        grid=(l // chunk,),
        in_specs=[pl.BlockSpec((r, chunk), lambda i: (0, i))],
        out_specs=pl.BlockSpec((r, chunk), lambda i: (0, i)),
        out_shape=jax.ShapeDtypeStruct((r, l), F32),
        scratch_shapes=[pltpu.VMEM((r, 1), F32)],
        compiler_params=_params(("arbitrary",)),
        name="cumsum_lanes",
    )(x)


def _fox_prompt_kernel(q_ref, k_ref, v_ref, ck_ref, rest_ref, o_ref, *, tq):
    del rest_ref
    qi = pl.program_id(2)
    den_lane = (HEAD_DIM, 0)

    def lane_masks(rows):
        lane = _iota((rows, LANES), 1)
        own = (jnp.where(lane < HEAD_DIM, 1.0, 0.0).astype(BF16), jnp.where(lane >= HEAD_DIM, 1.0, 0.0).astype(BF16))
        den = tuple(jnp.where(lane == d, 1.0, 0.0).astype(BF16) for d in den_lane)
        return lane, own, den

    def query_heads(row0, rows, lane):
        q = q_ref[row0:row0 + rows, :] * (HEAD_DIM ** -0.5)
        zero = jnp.zeros_like(q)
        return jnp.where(lane < HEAD_DIM, q, zero), jnp.where(lane >= HEAD_DIM, q, zero)

    def update(qh, kb, vh, ckh, m, acc, row_offset):
        s = _nt_dot(qh, kb) - ckh
        if row_offset is not None:
            s = jnp.where(_iota(s.shape, 1) <= _iota(s.shape, 0) + row_offset, s, NEG_INF)
        m_new = jnp.maximum(m, jnp.max(s, axis=1, keepdims=True))
        p = jnp.exp(s - m_new).astype(BF16)
        return m_new, jnp.exp(m - m_new) * acc + _dot(p, vh)

    lane_q, own, den = lane_masks(tq)
    q_heads = query_heads(0, tq, lane_q)

    def body(j, carry):
        start = pl.multiple_of(j * tq, tq)
        kb = k_ref[pl.ds(start, tq), :]
        vb = v_ref[pl.ds(start, tq), :]
        ck = ck_ref[0, 0, j]
        return tuple(update(q_heads[h], kb, vb * own[h] + den[h], ck[h:h + 1, :], *carry[h], None)
                     for h in range(2))

    init1 = (jnp.full((tq, 1), NEG_INF, F32), jnp.zeros((tq, LANES), F32))
    carry = lax.fori_loop(0, qi, body, (init1, init1))

    start = pl.multiple_of(qi * tq, tq)
    ck = ck_ref[0, 0, qi]
    rows = tq // FOX_DIAG_PARTS
    for part in range(FOX_DIAG_PARTS):
        row0, ncol = part * rows, (part + 1) * rows
        lane_r, _, _ = lane_masks(rows)
        _, own_n, den_n = lane_masks(ncol)
        qh = query_heads(row0, rows, lane_r)
        kb = k_ref[pl.ds(start, ncol), :]
        vb = v_ref[pl.ds(start, ncol), :]
        accs = [update(qh[h], kb, vb * own_n[h] + den_n[h], ck[h:h + 1, :ncol],
                       carry[h][0][row0:row0 + rows], carry[h][1][row0:row0 + rows], row0)[1] for h in range(2)]
        o0 = accs[0] / accs[0][:, den_lane[0]:den_lane[0] + 1]
        o1 = accs[1] / accs[1][:, den_lane[1]:den_lane[1] + 1]
        o_ref[row0:row0 + rows, :] = jnp.where(lane_r < HEAD_DIM, o0, o1).astype(o_ref.dtype)


def fox_prompt(qkv, ck, rest, n_batch, seq, tq):
    nq = seq // tq
    pairs = FOX_HEADS // 2
    return pl.pallas_call(
        functools.partial(_fox_prompt_kernel, tq=tq),
        grid=(n_batch, pairs, nq),
        in_specs=[pl.BlockSpec((tq, LANES), lambda b, h, i: (b * nq + i, h)),
                  pl.BlockSpec((seq, LANES), lambda b, h, i: (b, pairs + h)),
                  pl.BlockSpec((seq, LANES), lambda b, h, i: (b, 2 * pairs + h)),
                  pl.BlockSpec((1, 1, nq, 2, tq), lambda b, h, i: (b, h, 0, 0, 0)),
                  pl.BlockSpec(memory_space=pl.ANY)],
        out_specs=pl.BlockSpec((tq, LANES), lambda b, h, i: (b * nq + i, h)),
        out_shape=jax.ShapeDtypeStruct(rest.shape, rest.dtype),
        input_output_aliases={4: 0},
        compiler_params=_params(("parallel", "parallel", "arbitrary")),
        name="fox_prompt",
    )(qkv, qkv, qkv, ck, rest)


def _block_diag_rows(full):
    rowh = _iota((FOX_HEADS, HEAD_DIM), 0)
    out = jnp.zeros((FOX_HEADS, HEAD_DIM), F32)
    for h in range(FOX_HEADS):
        out = out + jnp.where(rowh == h, full[:, h * HEAD_DIM:(h + 1) * HEAD_DIM], 0.0)
    return out


def _fox_sample_kernel(pt_ref, qbd_ref, q_ref, kn_ref, vbd_ref, ln_ref, *refs, pages_per_step):
    del pt_ref
    pp = pages_per_step
    k_refs, v_refs, lf_refs = refs[:pp], refs[pp:2 * pp], refs[2 * pp:3 * pp]
    o_ref = refs[3 * pp]
    m_ref, l_ref, acc_ref, carry_ref = refs[3 * pp + 1:]
    t = pl.program_id(1)
    scale = HEAD_DIM ** -0.5

    @pl.when(t == 0)
    def _():
        m_ref[...] = jnp.sum(q_ref[0] * kn_ref[0], axis=1, keepdims=True) * scale
        l_ref[...] = jnp.ones_like(l_ref)
        acc_ref[...] = vbd_ref[0]
        carry_ref[...] = ln_ref[0]

    qb = (qbd_ref[0] * scale).astype(BF16)
    lane = _iota((FOX_HEADS, PAGE_SIZE), 1)
    width = FOX_HEADS * HEAD_DIM
    carry = carry_ref[...]
    scores = []
    for r in range(pp):
        lf = lf_refs[r][0, 0]
        x = lf
        for sh in (1, 2, 4, 8, 16, 32, 64):
            x = x + jnp.where(lane + sh < PAGE_SIZE, pltpu.roll(x, PAGE_SIZE - sh, 1), 0.0)
        kp = k_refs[r][0, 0].reshape(width, PAGE_SIZE).astype(BF16)
        scores.append(_dot(qb, kp) + ((x - lf) + carry))
        carry = carry + x[:, 0:1]
    carry_ref[...] = carry
    s = jnp.concatenate(scores, axis=1)
    m = m_ref[...]
    m_new = jnp.maximum(m, jnp.max(s, axis=1, keepdims=True))
    alpha = jnp.exp(m - m_new)
    p = jnp.exp(s - m_new)
    l_ref[...] = alpha * l_ref[...] + jnp.sum(p, axis=1, keepdims=True)
    m_ref[...] = m_new
    pb = p.astype(BF16)
    acc = alpha * acc_ref[...]
    for r in range(pp):
        vp = v_refs[r][0, 0].reshape(width, PAGE_SIZE).astype(BF16)
        acc = acc + _nt_dot(pb[:, r * PAGE_SIZE:(r + 1) * PAGE_SIZE], vp)
    acc_ref[...] = acc

    @pl.when(t == pl.num_programs(1) - 1)
    def _():
        o_ref[0] = _block_diag_rows(acc / l_ref[...])


def fox_sample(page_table, qbd, q, kn, vbd, ln, cache_kt, cache_vt, cache_lft, layer, pages_per_step):
    bd, n_pages = page_table.shape
    pp = pages_per_step
    steps = n_pages // pp
    width = FOX_HEADS * HEAD_DIM

    def page(b, t, pt, r):
        return pt[b, n_pages - 1 - (t * pp + r)]

    kv_specs = [pl.BlockSpec((1, 1, FOX_HEADS, HEAD_DIM, PAGE_SIZE),
                             functools.partial(lambda b, t, pt, r: (layer, page(b, t, pt, r), 0, 0, 0), r=r))
                for r in range(pp)]
    lf_specs = [pl.BlockSpec((1, 1, FOX_HEADS, PAGE_SIZE),
                             functools.partial(lambda b, t, pt, r: (layer, page(b, t, pt, r), 0, 0), r=r))
                for r in range(pp)]
    tok = pl.BlockSpec((1, FOX_HEADS, HEAD_DIM), lambda b, t, pt: (b, 0, 0))
    wide = pl.BlockSpec((1, FOX_HEADS, width), lambda b, t, pt: (b, 0, 0))
    grid_spec = pltpu.PrefetchScalarGridSpec(
        num_scalar_prefetch=1,
        grid=(bd, steps),
        in_specs=[wide, tok, tok, wide, pl.BlockSpec((1, FOX_HEADS, 1), lambda b, t, pt: (b, 0, 0))]
        + kv_specs + kv_specs + lf_specs,
        out_specs=tok,
        scratch_shapes=[pltpu.VMEM((FOX_HEADS, 1), F32), pltpu.VMEM((FOX_HEADS, 1), F32),
                        pltpu.VMEM((FOX_HEADS, width), F32), pltpu.VMEM((FOX_HEADS, 1), F32)],
    )
    return pl.pallas_call(
        functools.partial(_fox_sample_kernel, pages_per_step=pp),
        grid_spec=grid_spec,
        out_shape=jax.ShapeDtypeStruct((bd, FOX_HEADS, HEAD_DIM), F32),
        compiler_params=_params(("parallel", "arbitrary")),
        name="fox_sample",
    )(page_table, qbd, q, kn, vbd, ln, *([cache_kt] * pp), *([cache_vt] * pp), *([cache_lft] * pp))


def _ssd_epilogue(y, xs, z, dskip_e, normw):
    y = (y + dskip_e * xs) * _silu(z)
    half = M_INNER // M_GROUPS
    outs = []
    for g in range(M_GROUPS):
        yg = y[:, g * half:(g + 1) * half]
        ms = jnp.sum(yg * yg, axis=1, keepdims=True) * (1.0 / half)
        outs.append(yg * lax.rsqrt(ms + RMS_EPS))
    return jnp.concatenate(outs, axis=1) * normw


def _ssd_prompt_kernel(xbc_ref, z_ref, dt_ref, cw_ref, cb_ref, nega_ref, dskip_ref, normw_ref, rest_ref,
                       o_ref, st_ref, ext_ref, h_ref):
    del rest_ref
    c = pl.program_id(1)
    L = SSD_CHUNK
    pad = SUBLANES

    @pl.when(c == 0)
    def _():
        ext_ref[0:pad, :] = jnp.zeros((pad, CONV_CH), F32)
        h_ref[...] = jnp.zeros_like(h_ref)

    ext_ref[pad:pad + L, :] = xbc_ref[...]
    acc = ext_ref[pad:pad + L, :] * cw_ref[CONV_W - 1:CONV_W, :]
    for j in range(CONV_W - 1):
        off = pad - (CONV_W - 1) + j
        acc = acc + ext_ref[off:off + L, :] * cw_ref[j:j + 1, :]
    u = _silu(acc + cb_ref[...])
    ext_ref[0:pad, :] = ext_ref[L:L + pad, :]

    xs = u[:, :M_INNER]
    dt_t = dt_ref[...]
    cum_t = _dot(dt_t * nega_ref[...], (_iota((L, L), 0) <= _iota((L, L), 1)).astype(F32),
                 precision=HIGHEST)
    eye = (_iota((L, L), 0) == _iota((L, L), 1)).astype(F32)
    cols = _nt_dot(eye, jnp.concatenate([cum_t, dt_t], axis=0), precision=HIGHEST)
    cum_last = cum_t[:, L - 1:L]
    tail_t = jnp.exp(cum_last - cum_t) * dt_t
    tril = _iota((L, L), 0) >= _iota((L, L), 1)
    lane = _iota((L, LANES), 1)
    rowi = _iota((L, LANES), 0)

    y_pairs = []
    for g in range(M_GROUPS):
        bm = u[:, M_INNER + g * D_STATE:M_INNER + (g + 1) * D_STATE]
        cm = u[:, M_INNER + BC_W + g * D_STATE:M_INNER + BC_W + (g + 1) * D_STATE]
        bmb = bm.astype(BF16)
        cmb = cm.astype(BF16)
        cb = _nt_dot(cmb, bmb)
        for pr in range(HPG // 2):
            pidx = g * (HPG // 2) + pr
            xs_pair = xs[:, pidx * LANES:(pidx + 1) * LANES]
            xs_pair_b = xs_pair.astype(BF16)
            h0 = h_ref[pidx]
            ych = _nt_dot(cmb, h0.astype(BF16))
            yw = []
            for k in range(2):
                hd = 2 * pidx + k
                diff = cols[:, hd:hd + 1] - cum_t[hd:hd + 1, :]
                decay = jnp.exp(jnp.where(tril, diff, NEG_INF))
                w = cb * decay * dt_t[hd:hd + 1, :]
                yw.append(_dot(w.astype(BF16), xs_pair_b))
            e0 = jnp.exp(cols[:, 2 * pidx:2 * pidx + 1])
            e1 = jnp.exp(cols[:, 2 * pidx + 1:2 * pidx + 2])
            first = lane < M_HEADDIM
            y_pairs.append(jnp.where(first, yw[0], yw[1]) + ych * jnp.where(first, e0, e1))
            top = rowi < M_HEADDIM
            tail_m = jnp.where(top, jnp.broadcast_to(tail_t[2 * pidx:2 * pidx + 1, :], (L, L)),
                               jnp.broadcast_to(tail_t[2 * pidx + 1:2 * pidx + 2, :], (L, L)))
            dec_m = jnp.where(top, jnp.exp(cum_last[2 * pidx:2 * pidx + 1, :]),
                              jnp.exp(cum_last[2 * pidx + 1:2 * pidx + 2, :]))
            xt = xs_pair.T * tail_m
            h_ref[pidx] = h0 * dec_m + _dot(xt.astype(BF16), bmb)

    y = jnp.concatenate(y_pairs, axis=1)
    o_ref[...] = _ssd_epilogue(y, xs, z_ref[...], dskip_ref[...], normw_ref[...]).astype(o_ref.dtype)
    st_ref[0] = h_ref[...]


def ssd_prompt(xz, dt_rows, conv_w, conv_b, nega, dskip_e, normw, rest, n_batch, seq):
    L = SSD_CHUNK
    nc = seq // L
    pairs = M_HEADS // 2
    const = lambda b, c: (0, 0)
    return pl.pallas_call(
        _ssd_prompt_kernel,
        grid=(n_batch, nc),
        in_specs=[pl.BlockSpec((L, CONV_CH), lambda b, c: (b * nc + c, 0)),
                  pl.BlockSpec((L, M_INNER), lambda b, c: (b * nc + c, CONV_CH // M_INNER)),
                  pl.BlockSpec((M_HEADS, L), lambda b, c: (1, b * nc + c)),
                  pl.BlockSpec((CONV_W, CONV_CH), const),
                  pl.BlockSpec((1, CONV_CH), const),
                  pl.BlockSpec((M_HEADS, 1), const),
                  pl.BlockSpec((1, M_INNER), const),
                  pl.BlockSpec((1, M_INNER), const),
                  pl.BlockSpec(memory_space=pl.ANY)],
        out_specs=[pl.BlockSpec((L, M_INNER), lambda b, c: (b * nc + c, 0)),
                   pl.BlockSpec((1, pairs, LANES, D_STATE), lambda b, c: (b, 0, 0, 0))],
        out_shape=[jax.ShapeDtypeStruct(rest.shape, rest.dtype),
                   jax.ShapeDtypeStruct((n_batch, pairs, LANES, D_STATE), F32)],
        input_output_aliases={8: 0},
        scratch_shapes=[pltpu.VMEM((L + SUBLANES, CONV_CH), F32),
                        pltpu.VMEM((pairs, LANES, D_STATE), F32)],
        compiler_params=_params(("parallel", "arbitrary")),
        name="ssd_prompt",
    )(xz, xz, dt_rows, conv_w, conv_b, nega, dskip_e, normw, rest)


def _ssd_sample_kernel(x_ref, wdt_ref, dtb_ref, nega_ref, xbc_ref, ctx_ref, z_ref, cw_ref, cb_ref,
                       dskip_ref, normw_ref, h0_ref, o_ref, hn_ref,
                       u_ref, coef_t_ref, dec_t_ref, dec_ref, dtx_ref, *, seqs_per_step):
    sb = seqs_per_step
    i = pl.program_id(0)
    nseq = x_ref.shape[0]

    @pl.when(i == 0)
    def _():
        acc = xbc_ref[...] * cw_ref[CONV_W - 1:CONV_W, :]
        for j in range(CONV_W - 1):
            acc = acc + ctx_ref[j] * cw_ref[j:j + 1, :]
        u = _silu(acc + cb_ref[...])
        u_ref[...] = u
        dt = _softplus(_dot(x_ref[...], wdt_ref[...], precision=HIGHEST) + dtb_ref[...])
        dec = jnp.exp(dt * nega_ref[...])
        coef = dt * u[:, :M_INNER]
        dec_ref[...] = dec
        dtx_ref[...] = coef
        for blk in range(M_INNER // LANES):
            sl = slice(blk * LANES, (blk + 1) * LANES)
            coef_t_ref[sl, :] = coef[:, sl].T
            dec_t_ref[sl, :] = dec[:, sl].T

    base = pl.multiple_of(i * sb, sb)
    ub = u_ref[pl.ds(base, sb), :]
    lane_seq = _iota((M_INNER, nseq), 1)
    rows = _iota((sb, M_INNER // M_GROUPS), 0)
    ch = [jnp.zeros((sb, M_INNER // M_GROUPS), F32) for _ in range(M_GROUPS)]
    for s in range(sb):
        onehot = lane_seq == base + s
        cx = jnp.sum(jnp.where(onehot, coef_t_ref[...], 0.0), axis=1, keepdims=True)
        dc = jnp.sum(jnp.where(onehot, dec_t_ref[...], 0.0), axis=1, keepdims=True)
        for g in range(M_GROUPS):
            brow = ub[s:s + 1, M_INNER + g * D_STATE:M_INNER + (g + 1) * D_STATE]
            cblk = ub[:, M_INNER + BC_W + g * D_STATE:M_INNER + BC_W + (g + 1) * D_STATE]
            hg = h0_ref[s, g * HPG:(g + 1) * HPG].reshape(HPG * M_HEADDIM, D_STATE)
            r = _nt_dot(cblk.astype(BF16), hg.astype(BF16))
            ch[g] = ch[g] + jnp.where(rows == s, r, 0.0)
            lo = g * HPG * M_HEADDIM
            hn = hg * dc[lo:lo + HPG * M_HEADDIM] + cx[lo:lo + HPG * M_HEADDIM] * brow
            hn_ref[s, g * HPG:(g + 1) * HPG] = hn.reshape(HPG, M_HEADDIM, D_STATE)

    xs = ub[:, :M_INNER]
    dec = dec_ref[pl.ds(base, sb), :]
    coef = dtx_ref[pl.ds(base, sb), :]
    ys = []
    half = M_INNER // M_GROUPS
    for g in range(M_GROUPS):
        bm = ub[:, M_INNER + g * D_STATE:M_INNER + (g + 1) * D_STATE]
        cm = ub[:, M_INNER + BC_W + g * D_STATE:M_INNER + BC_W + (g + 1) * D_STATE]
        cb = jnp.sum(cm * bm, axis=1, keepdims=True)
        ys.append(cb * coef[:, g * half:(g + 1) * half] + ch[g] * dec[:, g * half:(g + 1) * half])
    y = jnp.concatenate(ys, axis=1)
    o_ref[...] = _ssd_epilogue(y, xs, z_ref[...], dskip_ref[...], normw_ref[...])


def ssd_sample(x_s, wdt_e, dtb_e, nega_e, xbc_s, ctx, z_s, conv_w, conv_b, dskip_e, normw, h0, seqs_per_step):
    bd = x_s.shape[0]
    sb = seqs_per_step
    const = lambda i: (0, 0)
    return pl.pallas_call(
        functools.partial(_ssd_sample_kernel, seqs_per_step=sb),
        grid=(bd // sb,),
        in_specs=[pl.BlockSpec((bd, D_MODEL), const),
                  pl.BlockSpec((D_MODEL, M_INNER), const),
                  pl.BlockSpec((1, M_INNER), const),
                  pl.BlockSpec((1, M_INNER), const),
                  pl.BlockSpec((bd, CONV_CH), const),
                  pl.BlockSpec((CONV_W - 1, bd, CONV_CH), lambda i: (0, 0, 0)),
                  pl.BlockSpec((sb, M_INNER), lambda i: (i, 0)),
                  pl.BlockSpec((CONV_W, CONV_CH), const),
                  pl.BlockSpec((1, CONV_CH), const),
                  pl.BlockSpec((1, M_INNER), const),
                  pl.BlockSpec((1, M_INNER), const),
                  pl.BlockSpec((sb, M_HEADS, M_HEADDIM, D_STATE), lambda i: (i, 0, 0, 0))],
        out_specs=[pl.BlockSpec((sb, M_INNER), lambda i: (i, 0)),
                   pl.BlockSpec((sb, M_HEADS, M_HEADDIM, D_STATE), lambda i: (i, 0, 0, 0))],
        out_shape=[jax.ShapeDtypeStruct((bd, M_INNER), F32),
                   jax.ShapeDtypeStruct((bd, M_HEADS, M_HEADDIM, D_STATE), F32)],
        scratch_shapes=[pltpu.VMEM((bd, CONV_CH), F32),
                        pltpu.VMEM((M_INNER, bd), F32),
                        pltpu.VMEM((M_INNER, bd), F32),
                        pltpu.VMEM((bd, M_INNER), F32),
                        pltpu.VMEM((bd, M_INNER), F32)],
        compiler_params=_params(("arbitrary",)),
        name="ssd_sample",
    )(x_s, wdt_e, dtb_e, nega_e, xbc_s, ctx, z_s, conv_w, conv_b, dskip_e, normw, h0)


def _to_half(x, src_half, dst_half):
    return x if src_half == dst_half else pltpu.roll(x, HEAD_DIM, 1)


def _swa_prompt_kernel(sink_ref, q_ref, kp_ref, kc_ref, vp_ref, vc_ref, rest_ref, o_ref):
    del rest_ref
    n = pl.program_id(1)
    W = WINDOW
    scale = HEAD_DIM ** -0.5
    lane = _iota((W, LANES), 1)
    rows4 = _iota((SW_GROUP * W, 2 * W), 0)
    t = rows4 % W
    col = _iota((SW_GROUP * W, 2 * W), 1)
    valid = (col >= t) & (col <= t + W) & ((n > 0) | (col >= W))
    rcol = _iota((SW_GROUP * W, 1), 0) // W
    outs = [None] * SW_HEADS
    for j in range(SW_KV_HEADS):
        ch, hf = j // 2, j % 2
        kk = jnp.concatenate([kp_ref[:, ch * LANES:(ch + 1) * LANES],
                              kc_ref[:, ch * LANES:(ch + 1) * LANES]], axis=0).astype(BF16)
        vv = jnp.concatenate([vp_ref[:, ch * LANES:(ch + 1) * LANES],
                              vc_ref[:, ch * LANES:(ch + 1) * LANES]], axis=0).astype(BF16)
        qs = []
        sink = jnp.zeros((SW_GROUP * W, 1), F32)
        for g in range(SW_GROUP):
            hq = j * SW_GROUP + g
            qc = q_ref[:, (hq // 2) * LANES:(hq // 2 + 1) * LANES] * scale
            qc = _to_half(qc, hq % 2, hf)
            keep = (lane < HEAD_DIM) if hf == 0 else (lane >= HEAD_DIM)
            qs.append(jnp.where(keep, qc, 0.0))
            sink = jnp.where(rcol == g, sink_ref[hq], sink)
        q4 = jnp.concatenate(qs, axis=0).astype(BF16)
        s = jnp.where(valid, _nt_dot(q4, kk), NEG_INF)
        m = jnp.maximum(jnp.max(s, axis=1, keepdims=True), sink)
        p = jnp.exp(s - m)
        den = jnp.sum(p, axis=1, keepdims=True) + jnp.exp(sink - m)
        o = _dot(p.astype(BF16), vv) / den
        for g in range(SW_GROUP):
            hq = j * SW_GROUP + g
            outs[hq] = _to_half(o[g * W:(g + 1) * W], hf, hq % 2)
    for c in range(SW_HEADS // 2):
        o_ref[:, c * LANES:(c + 1) * LANES] = jnp.where(lane < HEAD_DIM, outs[2 * c],
                                                        outs[2 * c + 1]).astype(o_ref.dtype)


def swa_prompt(sinks, u_odd, rest, n_batch, seq):
    nb = seq // WINDOW
    kcol = ODD_MIX // KV_W
    return pl.pallas_call(
        _swa_prompt_kernel,
        grid=(n_batch, nb),
        in_specs=[pl.BlockSpec(memory_space=pltpu.SMEM),
                  pl.BlockSpec((WINDOW, ODD_MIX), lambda b, n: (b * nb + n, 0)),
                  pl.BlockSpec((WINDOW, KV_W), lambda b, n: (b * nb + jnp.maximum(n - 1, 0), kcol)),
                  pl.BlockSpec((WINDOW, KV_W), lambda b, n: (b * nb + n, kcol)),
                  pl.BlockSpec((WINDOW, KV_W), lambda b, n: (b * nb + jnp.maximum(n - 1, 0), kcol + 1)),
                  pl.BlockSpec((WINDOW, KV_W), lambda b, n: (b * nb + n, kcol + 1)),
                  pl.BlockSpec(memory_space=pl.ANY)],
        out_specs=pl.BlockSpec((WINDOW, ODD_MIX), lambda b, n: (b * nb + n, 0)),
        out_shape=jax.ShapeDtypeStruct(rest.shape, rest.dtype),
        input_output_aliases={6: 0},
        compiler_params=_params(("parallel", "arbitrary")),
        name="swa_prompt",
    )(sinks, u_odd, u_odd, u_odd, u_odd, u_odd, rest)


def _swa_sample_kernel(sink_ref, q_ref, kn_ref, vn_ref, bk_ref, bv_ref, o_ref, *, seqs_per_step):
    scale = HEAD_DIM ** -0.5
    rowg = _iota((SW_HEADS, HEAD_DIM), 0) // SW_GROUP
    sink = sink_ref[...]
    for s in range(seqs_per_step):
        q16 = q_ref[s] * scale
        qbd = jnp.concatenate([jnp.where(rowg == j, q16, 0.0) for j in range(SW_KV_HEADS)], axis=1)
        kb = bk_ref[s].reshape(KV_W, -1).astype(BF16)
        vb = bv_ref[s].reshape(KV_W, -1).astype(BF16)
        sc = _dot(qbd.astype(BF16), kb)
        s_new = jnp.sum(qbd * kn_ref[s:s + 1, :], axis=1, keepdims=True)
        m = jnp.maximum(jnp.maximum(jnp.max(sc, axis=1, keepdims=True), s_new), sink)
        p = jnp.exp(sc - m)
        pn = jnp.exp(s_new - m)
        den = jnp.sum(p, axis=1, keepdims=True) + pn + jnp.exp(sink - m)
        full = (_nt_dot(p.astype(BF16), vb) + pn * vn_ref[s:s + 1, :]) / den
        o16 = jnp.zeros((SW_HEADS, HEAD_DIM), F32)
        for j in range(SW_KV_HEADS):
            o16 = o16 + jnp.where(rowg == j, full[:, j * HEAD_DIM:(j + 1) * HEAD_DIM], 0.0)
        o_ref[s] = o16


def swa_sample(sinks_col, q, kn, vn, buf_k, buf_v, seqs_per_step):
    bd, _, _, lw = buf_k.shape
    sb = seqs_per_step
    return pl.pallas_call(
        functools.partial(_swa_sample_kernel, seqs_per_step=sb),
        grid=(bd // sb,),
        in_specs=[pl.BlockSpec((SW_HEADS, 1), lambda i: (0, 0)),
                  pl.BlockSpec((sb, SW_HEADS, HEAD_DIM), lambda i: (i, 0, 0)),
                  pl.BlockSpec((sb, KV_W), lambda i: (i, 0)),
                  pl.BlockSpec((sb, KV_W), lambda i: (i, 0)),
                  pl.BlockSpec((sb, SW_KV_HEADS, HEAD_DIM, lw), lambda i: (i, 0, 0, 0)),
                  pl.BlockSpec((sb, SW_KV_HEADS, HEAD_DIM, lw), lambda i: (i, 0, 0, 0))],
        out_specs=pl.BlockSpec((sb, SW_HEADS, HEAD_DIM), lambda i: (i, 0, 0)),
        out_shape=jax.ShapeDtypeStruct((bd, SW_HEADS, HEAD_DIM), F32),
        compiler_params=_params(("parallel",)),
        name="swa_sample",
    )(sinks_col, q, kn, vn, buf_k, buf_v)


def _layer_norm(h, g, b):
    mu = jnp.mean(h, axis=1, keepdims=True)
    d = h - mu
    var = jnp.mean(d * d, axis=1, keepdims=True)
    return d * lax.rsqrt(var + LN_EPS) * g + b


def _mix_route_kernel(*refs, alpha, n_feat, prompt_tiles):
    am_refs, wo_refs = refs[:n_feat], refs[n_feat:2 * n_feat]
    xp_ref, xt_ref, g_ref, b_ref, wr_ref, br_ref, x1_ref, rw_ref, re_ref = refs[2 * n_feat:]
    mix = _dot(am_refs[0][...].astype(BF16), wo_refs[0][...])
    for a_ref, w_ref in zip(am_refs[1:], wo_refs[1:]):
        mix = mix + _dot(a_ref[...].astype(BF16), w_ref[...])
    x = jnp.where(pl.program_id(0) < prompt_tiles, xp_ref[...], xt_ref[...])
    x1 = _layer_norm(alpha * x + mix, g_ref[...], b_ref[...])
    x1_ref[...] = x1
    wr = wr_ref[...]
    x_hi = x1.astype(BF16)
    w_hi = wr.astype(BF16)
    x_lo = (x1 - x_hi.astype(F32)).astype(BF16)
    w_lo = (wr - w_hi.astype(F32)).astype(BF16)
    logits = _dot(x_hi, w_hi) + _dot(x_lo, w_hi) + _dot(x_hi, w_lo) + br_ref[...]
    lane_i = _iota(logits.shape, 1)
    lane = lane_i.astype(F32)
    big = float(LANES)
    gl = jnp.where(lane_i < N_EXPERT_GROUPS, logits, NEG_INF)
    gmax = jnp.max(gl, axis=1, keepdims=True)
    grp = jnp.min(jnp.where(gl == gmax, lane, big), axis=1, keepdims=True)
    pg = 1.0 / jnp.sum(jnp.exp(gl - gmax), axis=1, keepdims=True)
    rel = lane - (N_EXPERT_GROUPS + grp * EXPERTS_PER_GROUP)
    el = jnp.where(rel >= 0.0, jnp.where(rel < EXPERTS_PER_GROUP, logits, NEG_INF), NEG_INF)
    v1 = jnp.max(el, axis=1, keepdims=True)
    i1 = jnp.min(jnp.where(el == v1, lane, big), axis=1, keepdims=True)
    el2 = jnp.where(lane == i1, NEG_INF, el)
    v2 = jnp.max(el2, axis=1, keepdims=True)
    i2 = jnp.min(jnp.where(el2 == v2, lane, big), axis=1, keepdims=True)
    e = jnp.exp(v2 - v1)
    w1 = pg / (1.0 + e)
    w2 = w1 * e
    rw_ref[...] = jnp.where(lane_i == 0, w1, jnp.where(lane_i == 1, w2, 0.0))
    e1 = (i1 - N_EXPERT_GROUPS).astype(jnp.int32)
    e2 = (i2 - N_EXPERT_GROUPS).astype(jnp.int32)
    re_ref[...] = jnp.where(lane_i == 0, e1, jnp.where(lane_i == 1, e2, 0))


def mix_route(feats, wos, x_prompt, x_tail, g, b, wr, br, alpha, tm):
    d = x_tail.shape[1]
    prompt_tiles = (feats[0].shape[0] - x_tail.shape[0]) // tm
    m = feats[0].shape[0]
    const = lambda i: (0, 0)
    row = lambda i: (i, 0)
    return pl.pallas_call(
        functools.partial(_mix_route_kernel, alpha=alpha, n_feat=len(feats), prompt_tiles=prompt_tiles),
        grid=(m // tm,),
        in_specs=[pl.BlockSpec((tm, a.shape[1]), row) for a in feats]
        + [pl.BlockSpec((w.shape[0], d), const) for w in wos]
        + [pl.BlockSpec((tm, d), lambda i: (jnp.minimum(i, prompt_tiles - 1), 0)),
           pl.BlockSpec((tm, d), lambda i: (jnp.maximum(i - prompt_tiles, 0), 0)),
           pl.BlockSpec((1, d), const), pl.BlockSpec((1, d), const),
           pl.BlockSpec((d, LANES), const), pl.BlockSpec((1, LANES), const)],
        out_specs=[pl.BlockSpec((tm, d), row),
                   pl.BlockSpec((tm, LANES), row), pl.BlockSpec((tm, LANES), row)],
        out_shape=[jax.ShapeDtypeStruct((m, d), F32),
                   jax.ShapeDtypeStruct((m, LANES), F32), jax.ShapeDtypeStruct((m, LANES), jnp.int32)],
        compiler_params=_params(("parallel",)),
        name="mix_route",
    )(*feats, *wos, x_prompt, x_tail, g, b, wr, br)


def _experts_kernel(te_ref, tv_ref, x_ref, wg_ref, wu_ref, wd_ref, *rest, first_tile):
    y_ref = rest[-1]
    t = first_tile + pl.program_id(0)

    @pl.when(tv_ref[t] > 0)
    def _():
        x = x_ref[...].astype(BF16)
        h = _silu(_dot(x, wg_ref[0, 0].astype(BF16))) * _dot(x, wu_ref[0, 0].astype(BF16))
        y_ref[...] = _dot(h.astype(BF16), wd_ref[0, 0].astype(BF16))

    @pl.when(tv_ref[t] == 0)
    def _():
        y_ref[...] = jnp.zeros_like(y_ref)


def experts(tile_expert, tile_valid, xs, wg, wu, wd, layer, te, first_tile, n_rows, y_prev):
    r, d = xs.shape
    ff = wg.shape[3]
    ex = lambda t, e, v: (layer, e[first_tile + t], 0, 0)
    in_specs = [pl.BlockSpec((te, d), lambda t, e, v: (t, 0)),
                pl.BlockSpec((1, 1, d, ff), ex), pl.BlockSpec((1, 1, d, ff), ex), pl.BlockSpec((1, 1, ff, d), ex)]
    args = [tile_expert, tile_valid, xs, wg, wu, wd]
    aliases = {}
    if y_prev is not None:
        in_specs.append(pl.BlockSpec(memory_space=pl.ANY))
        aliases = {len(args): 0}
        args.append(y_prev)
    grid_spec = pltpu.PrefetchScalarGridSpec(
        num_scalar_prefetch=2,
        grid=(r // te,),
        in_specs=in_specs,
        out_specs=pl.BlockSpec((te, d), lambda t, e, v: (first_tile + t, 0)),
    )
    return pl.pallas_call(
        functools.partial(_experts_kernel, first_tile=first_tile),
        grid_spec=grid_spec,
        out_shape=jax.ShapeDtypeStruct((n_rows, d), F32),
        input_output_aliases=aliases,
        compiler_params=_params(("arbitrary",)),
        name="experts",
    )(*args)


def _combine_ple_kernel(x1_ref, y0_ref, y1_ref, rw_ref, pp_ref, pt_ref, g_ref, b_ref, wg_ref, bg_ref, wp_ref,
                        o_ref, *, alpha, prompt_tiles, first_tile):
    rw = rw_ref[...]
    f = rw[:, 0:1] * y0_ref[...] + rw[:, 1:2] * y1_ref[...]
    x2 = _layer_norm(alpha * x1_ref[...] + f, g_ref[...], b_ref[...])
    gl = _dot(x2.astype(BF16), wg_ref[...]) + bg_ref[...]
    gate = 1.0 / (1.0 + jnp.exp(-gl))
    p = jnp.where(first_tile + pl.program_id(0) < prompt_tiles, pp_ref[0], pt_ref[...])
    o_ref[...] = x2 + gate * _dot(p.astype(BF16), wp_ref[...])


def combine_ple(x1, y0, y1, rw, p_prompt, p_tail, layer, g, b, wg, bg, wp, alpha, tm, first_tile, n_tiles):
    d = x1.shape[1]
    pd = p_prompt.shape[2]
    prompt_tiles = p_prompt.shape[1] // tm
    const = lambda i: (0, 0)
    row = lambda i: (first_tile + i, 0)
    return pl.pallas_call(
        functools.partial(_combine_ple_kernel, alpha=alpha, prompt_tiles=prompt_tiles, first_tile=first_tile),
        grid=(n_tiles,),
        in_specs=[pl.BlockSpec((tm, d), row), pl.BlockSpec((tm, d), row), pl.BlockSpec((tm, d), row),
                  pl.BlockSpec((tm, LANES), row),
                  pl.BlockSpec((1, tm, pd), lambda i: (layer, jnp.minimum(first_tile + i, prompt_tiles - 1), 0)),
                  pl.BlockSpec((tm, pd), lambda i: (jnp.maximum(first_tile + i - prompt_tiles, 0), 0)),
                  pl.BlockSpec((1, d), const), pl.BlockSpec((1, d), const),
                  pl.BlockSpec((d, d), const), pl.BlockSpec((1, d), const), pl.BlockSpec((pd, d), const)],
        out_specs=pl.BlockSpec((tm, d), lambda i: (i, 0)),
        out_shape=jax.ShapeDtypeStruct((n_tiles * tm, d), F32),
        compiler_params=_params(("parallel",)),
        name="combine_ple",
    )(x1, y0, y1, rw, p_prompt, p_tail, g, b, wg, bg, wp)


def _tiles(n_tokens):
    tm = 512 if n_tokens >= 4096 else 128
    return tm, ((n_tokens + tm - 1) // tm) * tm


def _expert_tile(n_tokens):
    return 256 if n_tokens >= 4096 else 32


def _channel_and_ple(x_main, x_tail, feats, p_prompt, p_tail, li, w_out, w, tm):
    depth = w["ln_mix_g"].shape[0]
    alpha = (2 * depth) ** 0.25
    ntp, d = feats[0].shape[0], x_tail.shape[1]
    wr = jnp.concatenate([w["w_router_group"][li],
                          jnp.moveaxis(w["w_router_expert"][li], 0, 1).reshape(d, N_EXPERTS)], axis=1)
    wr = jnp.pad(wr, ((0, 0), (0, LANES - wr.shape[1])))
    br = jnp.concatenate([w["b_router_group"][li], w["b_router_expert"][li].reshape(-1)])
    br = jnp.pad(br, (0, LANES - br.shape[0]))[None, :]
    wo = w_out.astype(BF16)
    splits = [0]
    for a in feats:
        splits.append(splits[-1] + a.shape[1])
    x1, rw, re = mix_route(feats, [wo[lo:hi] for lo, hi in zip(splits[:-1], splits[1:])], x_main, x_tail,
                           w["ln_mix_g"][li][None], w["ln_mix_b"][li][None], wr, br, alpha, tm)

    te = _expert_tile(ntp)
    n_flat = 2 * ntp
    flat = re[:, :2].reshape(-1)
    onehot = (flat[:, None] == jnp.arange(N_EXPERTS, dtype=jnp.int32)[None, :]).astype(jnp.int32)
    running = jnp.cumsum(onehot, axis=0)
    counts = running[-1]
    padded = ((counts + te - 1) // te) * te
    gend = jnp.cumsum(padded)
    gstart = gend - padded
    pos_flat = jnp.sum(onehot * (running - 1 + gstart[None, :]), axis=1)
    n_rows = ((n_flat + N_EXPERTS * (te - 1) + te - 1) // te) * te
    row_token = (jnp.arange(n_rows, dtype=jnp.int32) % ntp).at[pos_flat].set(
        jnp.arange(n_flat, dtype=jnp.int32) // 2, mode="promise_in_bounds", unique_indices=True)
    pos_flat = pos_flat.reshape(ntp, 2)
    tile_start = jnp.arange(n_rows // te, dtype=jnp.int32) * te
    tile_expert = jnp.minimum(jnp.sum((gend[None, :] <= tile_start[:, None]).astype(jnp.int32), axis=1),
                              N_EXPERTS - 1)
    tile_valid = (tile_start < gend[-1]).astype(jnp.int32)

    def rows(a, idx):
        return a.at[idx].get(mode="promise_in_bounds")

    n_tiles = n_rows // te
    bounds = [(c * n_tiles) // EXPERT_CHUNKS for c in range(EXPERT_CHUNKS + 1)]
    y = None
    for t0, t1 in zip(bounds[:-1], bounds[1:]):
        xs = rows(x1, row_token[t0 * te:t1 * te])
        y = experts(tile_expert, tile_valid, xs, w["w_exp_gate"], w["w_exp_up"], w["w_exp_down"], li, te,
                    t0, n_rows, y)
    y0 = rows(y, pos_flat[:, 0])
    y1 = rows(y, pos_flat[:, 1])
    prompt_tiles = p_prompt.shape[1] // tm
    parts = [(0, ntp // tm)] if li + 1 < depth else [(0, prompt_tiles), (prompt_tiles, ntp // tm - prompt_tiles)]
    outs = [combine_ple(x1, y0, y1, rw, p_prompt, p_tail, li, w["ln_ffn_g"][li][None], w["ln_ffn_b"][li][None],
                        w["w_ple_gate"][li].astype(BF16), w["b_ple_gate"][li][None],
                        w["w_ple_proj"][li].astype(BF16), alpha, tm, first, n) for first, n in parts]
    return outs[0] if len(outs) == 1 else tuple(outs)


def kernel(x_prompt, x_sample, p_prompt, p_sample, cache_fox_k, cache_fox_v, cache_fox_logf, state_ssm, state_conv, cache_win_k, cache_win_v, page_table, w_in_even, b_fgate, conv_w, conv_b, dt_bias, a_log, d_skip, ssm_norm_w, w_out_even, w_in_odd, attn_sinks, w_out_odd, ln_mix_g, ln_mix_b, ln_ffn_g, ln_ffn_b, w_router_group, b_router_group, w_router_expert, b_router_expert, w_exp_gate, w_exp_up, w_exp_down, w_ple_proj, w_ple_gate, b_ple_gate):
    bp, seq, d = x_prompt.shape
    bd, t_dec, _ = x_sample.shape
    assert t_dec == 1 and d == D_MODEL
    depth = p_prompt.shape[0]
    n_pages = page_table.shape[1]
    past_len = n_pages * PAGE_SIZE
    np_tok = bp * seq
    nt = np_tok + bd
    tm, ntp = _tiles(nt)
    pad = ntp - nt

    def past_prompt(a_s):
        a_s = a_s.reshape(bd, -1).astype(BF16)
        return jnp.zeros((ntp, a_s.shape[1]), BF16).at[np_tok:nt].set(a_s)

    def after_prompt(a_s):
        a_s = a_s.reshape(bd, -1)
        return jnp.zeros((ntp - np_tok, a_s.shape[1]), a_s.dtype).at[:bd].set(a_s)

    assert np_tok % tm == 0 and seq >= CONV_W - 1
    x_full = None
    x_main = x_prompt.reshape(np_tok, d)
    x_tail = after_prompt(x_sample)
    p_all = p_prompt.reshape(depth, np_tok, -1)
    shared = dict(ln_mix_g=ln_mix_g, ln_mix_b=ln_mix_b, ln_ffn_g=ln_ffn_g, ln_ffn_b=ln_ffn_b,
                  w_router_group=w_router_group, b_router_group=b_router_group,
                  w_router_expert=w_router_expert, b_router_expert=b_router_expert,
                  w_exp_gate=w_exp_gate, w_exp_up=w_exp_up, w_exp_down=w_exp_down,
                  w_ple_proj=w_ple_proj, w_ple_gate=w_ple_gate, b_ple_gate=b_ple_gate)

    half = HEAD_DIM // 2
    inv = jnp.exp(-math.log(ROPE_THETA) * jnp.arange(half, dtype=F32) / half)
    pos = jnp.concatenate([jnp.tile(jnp.arange(seq, dtype=jnp.int32), bp),
                           jnp.full((bd,), past_len, jnp.int32), jnp.zeros((pad,), jnp.int32)])
    ang = pos.astype(F32)[:, None] * inv[None, :]
    cos_t = jnp.tile(jnp.cos(ang), (1, LANES // half))
    sin_t = jnp.tile(jnp.concatenate([-jnp.sin(ang), jnp.sin(ang)], axis=1), (1, LANES // HEAD_DIM))

    even_p, even_s, odd_p, odd_s = [], [], [], []
    for li in range(depth):
        j = li // 2
        if li % 2 == 0:
            wi = w_in_even[j]
            c0 = 3 * FOX_WIDTH
            c1 = c0 + FOX_HEADS
            c2 = c1 + M_INNER
            c3 = c2 + CONV_CH
            w_main = jnp.concatenate([wi[:, :c0], wi[:, c2:c3], wi[:, c1:c2]], axis=1).astype(BF16)
            w_small_t = jnp.concatenate([wi[:, c0:c1], wi[:, c3:]], axis=1).T
            b_small = jnp.concatenate([b_fgate[j], dt_bias[j]])[:, None]
            prompt_tiles = np_tok // tm
            k_p, v_p, xz, qkv, _, small = even_proj(x_main, w_main, w_small_t, b_small, tm, 0, prompt_tiles)
            k_t, v_t, xz_t, _, q_t, small_t = even_proj(x_tail, w_main, w_small_t, b_small, tm, 0,
                                                        ntp // tm - prompt_tiles)

            tq = min(FOX_TILE, seq)
            logf_p = small[:FOX_HEADS].reshape(FOX_HEADS, bp, seq)
            cum = cumsum_lanes(jnp.moveaxis(logf_p, 1, 0).reshape(bp * FOX_HEADS, seq), min(512, seq))
            ck = cum.reshape(bp, FOX_HEADS // 2, 2, seq // tq, tq).transpose(0, 1, 3, 2, 4)
            logf_s = small_t[:FOX_HEADS, :bd].T
            q_s = q_t[:bd].reshape(bd, FOX_HEADS, HEAD_DIM)
            k_s = k_t[:bd].reshape(bd, FOX_HEADS, HEAD_DIM)
            v_s = v_t[:bd].reshape(bd, FOX_HEADS, HEAD_DIM)
            eye = jnp.eye(FOX_HEADS, dtype=F32)[None, :, :, None]

            def block_diag(a):
                return (a[:, :, None, :] * eye).reshape(bd, FOX_HEADS, FOX_WIDTH)

            a_s = fox_sample(page_table, block_diag(q_s), q_s, k_s, block_diag(v_s), logf_s[:, :, None],
                             jnp.transpose(cache_fox_k, (0, 1, 3, 4, 2)), jnp.transpose(cache_fox_v, (0, 1, 3, 4, 2)),
                             jnp.transpose(cache_fox_logf, (0, 1, 3, 2)), j, min(FOX_PAGES_PER_STEP, n_pages))
            a_all = fox_prompt(qkv, ck, past_prompt(a_s), bp, seq, tq)

            nega = -jnp.exp(a_log[j])
            dskip_e = jnp.repeat(d_skip[j], M_HEADDIM)[None, :]
            normw = ssm_norm_w[j][None, :]
            xz_s = xz_t[:bd]
            m_s, st_s = ssd_sample(x_tail[:bd], jnp.repeat(wi[:, c3:], M_HEADDIM, axis=1),
                                   jnp.repeat(dt_bias[j], M_HEADDIM)[None, :], jnp.repeat(nega, M_HEADDIM)[None, :],
                                   xz_s[:, :CONV_CH], jnp.moveaxis(state_conv[j], 1, 0), xz_s[:, CONV_CH:],
                                   conv_w[j], conv_b[j][None, :], dskip_e, normw, state_ssm[j], min(8, bd))
            m_all, st_p = ssd_prompt(xz, small, conv_w[j], conv_b[j][None, :], nega[:, None], dskip_e, normw,
                                     past_prompt(m_s), bp, seq)
            feats = [a_all, m_all]
            w_out = w_out_even[j]

            kp = k_p.reshape(bp, seq, FOX_HEADS, HEAD_DIM)
            vp = v_p.reshape(bp, seq, FOX_HEADS, HEAD_DIM)
            last = jnp.stack([xz[(b + 1) * seq - (CONV_W - 1):(b + 1) * seq, :CONV_CH] for b in range(bp)])
            conv_p = jnp.concatenate([jnp.zeros((bp, CONV_W - 1, CONV_CH), F32), last], axis=1)[:, -(CONV_W - 1):]
            even_p.append((kp, vp, jnp.moveaxis(logf_p, 0, 2),
                           st_p.reshape(bp, M_HEADS, M_HEADDIM, D_STATE), conv_p))
            conv_s = jnp.concatenate([state_conv[j], xz_s[:, None, :CONV_CH]], axis=1)[:, -(CONV_W - 1):]
            even_s.append((k_s[:, None], v_s[:, None], logf_s[:, None, :], st_s, conv_s))
        else:
            assert x_full is not None
            u = matmul_rope(x_full, w_in_odd[j].astype(BF16), cos_t, sin_t, tm, 256, ODD_MIX + KV_W)
            u_s = u[np_tok:nt]
            lw = cache_win_k.shape[2]
            kn = u_s[:, ODD_MIX:ODD_MIX + KV_W]
            vn = u_s[:, ODD_MIX + KV_W:]
            o_s = swa_sample(attn_sinks[j][:, None], u_s[:, :ODD_MIX].reshape(bd, SW_HEADS, HEAD_DIM), kn, vn,
                             jnp.transpose(cache_win_k[j], (0, 2, 3, 1)), jnp.transpose(cache_win_v[j], (0, 2, 3, 1)),
                             min(8, bd))
            feats = [swa_prompt(attn_sinks[j], u, past_prompt(o_s), bp, seq)]
            w_out = w_out_odd[j]

            rows = min(WINDOW, seq)
            tail = jnp.stack([u[(b + 1) * seq - rows:(b + 1) * seq, ODD_MIX:] for b in range(bp)])
            odd_p.append((tail[:, :, :KV_W].reshape(bp, rows, SW_KV_HEADS, HEAD_DIM),
                          tail[:, :, KV_W:].reshape(bp, rows, SW_KV_HEADS, HEAD_DIM)))
            ka = jnp.concatenate([cache_win_k[j], kn.reshape(bd, 1, SW_KV_HEADS, HEAD_DIM)], axis=1)[:, -lw:]
            va = jnp.concatenate([cache_win_v[j], vn.reshape(bd, 1, SW_KV_HEADS, HEAD_DIM)], axis=1)[:, -lw:]
            odd_s.append((ka, va))
        x_full = _channel_and_ple(x_main, x_tail, feats, p_all, after_prompt(p_sample[li]), li, w_out, shared, tm)
        if li + 1 < depth:
            x_main, x_tail = x_full, x_full[np_tok:]

    x_prompt_out, x_tail_out = x_full
    yp = x_prompt_out.reshape(bp, seq, d)
    ys = x_tail_out[:bd].reshape(bd, 1, d)
    return (yp, ys,
            jnp.stack([st[0] for st in even_p]), jnp.stack([st[1] for st in even_p]),
            jnp.stack([st[2] for st in even_p]), jnp.stack([st[3] for st in even_p]),
            jnp.stack([st[4] for st in even_p]),
            jnp.stack([st[0] for st in odd_p]), jnp.stack([st[1] for st in odd_p]),
            jnp.stack([st[0] for st in even_s]), jnp.stack([st[1] for st in even_s]),
            jnp.stack([st[2] for st in even_s]), jnp.stack([st[3] for st in even_s]),
            jnp.stack([st[4] for st in even_s]),
            jnp.stack([st[0] for st in odd_s]), jnp.stack([st[1] for st in odd_s]))
```

```python
import functools
import math

import jax
import jax.numpy as jnp
from jax import lax
from jax.experimental import pallas as pl
from jax.experimental.pallas import tpu as pltpu

F32 = jnp.float32
BF16 = jnp.bfloat16
HIGHEST = lax.Precision.HIGHEST

D_MODEL = 1024
HEAD_DIM = 64
FOX_HEADS = 8
FOX_WIDTH = FOX_HEADS * HEAD_DIM
M_HEADS = 8
M_HEADDIM = 64
M_INNER = M_HEADS * M_HEADDIM
M_GROUPS = 2
HPG = M_HEADS // M_GROUPS
D_STATE = 128
CONV_W = 4
BC_W = M_GROUPS * D_STATE
CONV_CH = M_INNER + 2 * BC_W
SSD_CHUNK = 128
RMS_EPS = 1e-5
SW_HEADS = 16
SW_KV_HEADS = 4
SW_GROUP = SW_HEADS // SW_KV_HEADS
WINDOW = 128
ROPE_THETA = 10000.0
ODD_MIX = SW_HEADS * HEAD_DIM
KV_W = SW_KV_HEADS * HEAD_DIM
N_EXPERT_GROUPS = 4
EXPERTS_PER_GROUP = 8
N_EXPERTS = N_EXPERT_GROUPS * EXPERTS_PER_GROUP
LN_EPS = 1e-5
PAGE_SIZE = 128

LANES = 128
SUBLANES = 8
VMEM_LIMIT = 48 * 1024 * 1024

NEG_INF = float("-inf")

EXPERT_CHUNKS = 2
FOX_PAGES_PER_STEP = 32
FOX_TILE = 1024
FOX_DIAG_PARTS = 2
ROPE_COL_TILE = 256
CUMSUM_CHUNK = 512
DECODE_SEQS_PER_STEP = 8


def _params(sem, vmem=VMEM_LIMIT):
    return pltpu.CompilerParams(dimension_semantics=sem, vmem_limit_bytes=vmem)


def _nt_dot(a, b, precision=None):
    return lax.dot_general(a, b, (((1,), (1,)), ((), ())), precision=precision,
                           preferred_element_type=F32)


def _dot(a, b, precision=None):
    return jnp.dot(a, b, precision=precision, preferred_element_type=F32)


def _silu(x):
    return x * (1.0 / (1.0 + jnp.exp(-x)))


def _softplus(x):
    return jnp.maximum(x, 0.0) + jnp.log(1.0 + jnp.exp(-jnp.abs(x)))


def _iota(shape, dim):
    return lax.broadcasted_iota(jnp.int32, shape, dim)


def _mm_rope_kernel(x_ref, w_ref, cos_ref, sin_ref, o_ref, *, tn, rope_cols):
    xb = x_ref[...].astype(BF16)
    reps = tn // LANES
    cos = jnp.concatenate([cos_ref[...]] * reps, axis=1)
    sin = jnp.concatenate([sin_ref[...]] * reps, axis=1)
    half = HEAD_DIM // 2
    first = (_iota((xb.shape[0], tn), 1) % HEAD_DIM) < half
    for c in range(w_ref.shape[1] // tn):
        acc = _dot(xb, w_ref[:, c * tn:(c + 1) * tn])
        if c * tn < rope_cols:
            partner = jnp.where(first, pltpu.roll(acc, tn - half, 1), pltpu.roll(acc, half, 1))
            acc = acc * cos + partner * sin
        o_ref[:, c * tn:(c + 1) * tn] = acc


def _even_proj_kernel(x_ref, w_ref, wt_ref, bt_ref, k_ref, v_ref, xz_ref, qkv_ref, qt_ref, small_ref):
    x = x_ref[...]
    xb = x.astype(BF16)
    wt = wt_ref[...]
    wt_hi = wt.astype(BF16)
    wt_lo = (wt - wt_hi.astype(F32)).astype(BF16)
    x_lo = (x - xb.astype(F32)).astype(BF16)
    r = _nt_dot(wt_hi, xb) + _nt_dot(wt_lo, xb) + _nt_dot(wt_hi, x_lo) + bt_ref[...]
    small_ref[...] = jnp.where(_iota(r.shape, 0) < FOX_HEADS, -_softplus(-r), _softplus(r))
    w = FOX_WIDTH
    f32_dst = (None, k_ref, v_ref)
    for c in range(3):
        acc = _dot(xb, w_ref[:, c * w:(c + 1) * w])
        qkv_ref[:, c * w:(c + 1) * w] = acc.astype(BF16)
        if f32_dst[c] is not None:
            f32_dst[c][...] = acc
        else:
            @pl.when(pl.program_id(0) == pl.num_programs(0) - 1)
            def _():
                qt_ref[...] = acc
    for c in range(xz_ref.shape[1] // w):
        xz_ref[:, c * w:(c + 1) * w] = _dot(xb, w_ref[:, (3 + c) * w:(4 + c) * w])


def even_proj(x, w, wt, bt, tm, first_tile, n_tiles):
    k = x.shape[1]
    m = n_tiles * tm
    n = w.shape[1]
    wq = 3 * FOX_WIDTH
    row = lambda i: (i, 0)
    const = lambda i: (0, 0)
    return pl.pallas_call(
        _even_proj_kernel,
        grid=(n_tiles,),
        in_specs=[pl.BlockSpec((tm, k), lambda i: (first_tile + i, 0)), pl.BlockSpec((k, n), const),
                  pl.BlockSpec((16, k), const), pl.BlockSpec((16, 1), const)],
        out_specs=[pl.BlockSpec((tm, FOX_WIDTH), row), pl.BlockSpec((tm, FOX_WIDTH), row),
                   pl.BlockSpec((tm, n - wq), row), pl.BlockSpec((tm, wq), row),
                   pl.BlockSpec((tm, FOX_WIDTH), const), pl.BlockSpec((16, tm), lambda i: (0, i))],
        out_shape=[jax.ShapeDtypeStruct((m, FOX_WIDTH), F32), jax.ShapeDtypeStruct((m, FOX_WIDTH), F32),
                   jax.ShapeDtypeStruct((m, n - wq), F32), jax.ShapeDtypeStruct((m, wq), BF16),
                   jax.ShapeDtypeStruct((tm, FOX_WIDTH), F32), jax.ShapeDtypeStruct((16, m), F32)],
        compiler_params=_params(("arbitrary",)),
        name="even_proj",
    )(x, w, wt, bt)


def matmul_rope(x, w, cos, sin, tm, tn, rope_cols):
    m, k = x.shape
    n = w.shape[1]
    assert n % tn == 0 and rope_cols % tn == 0
    return pl.pallas_call(
        functools.partial(_mm_rope_kernel, tn=tn, rope_cols=rope_cols),
        grid=(m // tm,),
        in_specs=[pl.BlockSpec((tm, k), lambda i: (i, 0)),
                  pl.BlockSpec((k, n), lambda i: (0, 0)),
                  pl.BlockSpec((tm, LANES), lambda i: (i, 0)),
                  pl.BlockSpec((tm, LANES), lambda i: (i, 0))],
        out_specs=pl.BlockSpec((tm, n), lambda i: (i, 0)),
        out_shape=jax.ShapeDtypeStruct((m, n), F32),
        compiler_params=_params(("parallel",)),
        name="matmul_rope",
    )(x, w, cos, sin)


def _cumsum_kernel(x_ref, o_ref, carry_ref):
    @pl.when(pl.program_id(0) == 0)
    def _():
        carry_ref[...] = jnp.zeros_like(carry_ref)

    x = x_ref[...]
    w = x.shape[1]
    tri = (_iota((w, w), 0) <= _iota((w, w), 1)).astype(F32)
    c = _dot(x, tri, precision=HIGHEST) + carry_ref[...]
    o_ref[...] = c
    carry_ref[...] = c[:, w - 1:w]


def cumsum_lanes(x, chunk):
    r, l = x.shape
    return pl.pallas_call(
        _cumsum_kernel,
        grid=(l // chunk,),
        in_specs=[pl.BlockSpec((r, chunk), lambda i: (0, i))],
        out_specs=pl.BlockSpec((r, chunk), lambda i: (0, i)),
        out_shape=jax.ShapeDtypeStruct((r, l), F32),
        scratch_shapes=[pltpu.VMEM((r, 1), F32)],
        compiler_params=_params(("arbitrary",)),
        name="cumsum_lanes",
    )(x)


def _fox_prompt_kernel(q_ref, k_ref, v_ref, ck_ref, rest_ref, o_ref, *, tq):
    del rest_ref
    qi = pl.program_id(2)
    den_lane = (HEAD_DIM, 0)

    def lane_masks(rows):
        lane = _iota((rows, LANES), 1)
        own = (jnp.where(lane < HEAD_DIM, 1.0, 0.0).astype(BF16), jnp.where(lane >= HEAD_DIM, 1.0, 0.0).astype(BF16))
        den = tuple(jnp.where(lane == d, 1.0, 0.0).astype(BF16) for d in den_lane)
        return lane, own, den

    def query_heads(row0, rows, lane):
        q = q_ref[row0:row0 + rows, :] * (HEAD_DIM ** -0.5)
        zero = jnp.zeros_like(q)
        return jnp.where(lane < HEAD_DIM, q, zero), jnp.where(lane >= HEAD_DIM, q, zero)

    def update(qh, kb, vh, ckh, m, acc, row_offset):
        s = _nt_dot(qh, kb) - ckh
        if row_offset is not None:
            s = jnp.where(_iota(s.shape, 1) <= _iota(s.shape, 0) + row_offset, s, NEG_INF)
        m_new = jnp.maximum(m, jnp.max(s, axis=1, keepdims=True))
        p = jnp.exp(s - m_new).astype(BF16)
        return m_new, jnp.exp(m - m_new) * acc + _dot(p, vh)

    lane_q, own, den = lane_masks(tq)
    q_heads = query_heads(0, tq, lane_q)

    def body(j, carry):
        start = pl.multiple_of(j * tq, tq)
        kb = k_ref[pl.ds(start, tq), :]
        vb = v_ref[pl.ds(start, tq), :]
        ck = ck_ref[0, 0, j]
        return tuple(update(q_heads[h], kb, vb * own[h] + den[h], ck[h:h + 1, :], *carry[h], None)
                     for h in range(2))

    init1 = (jnp.full((tq, 1), NEG_INF, F32), jnp.zeros((tq, LANES), F32))
    carry = lax.fori_loop(0, qi, body, (init1, init1))

    start = pl.multiple_of(qi * tq, tq)
    ck = ck_ref[0, 0, qi]
    rows = tq // FOX_DIAG_PARTS
    for part in range(FOX_DIAG_PARTS):
        row0, ncol = part * rows, (part + 1) * rows
        lane_r, _, _ = lane_masks(rows)
        _, own_n, den_n = lane_masks(ncol)
        qh = query_heads(row0, rows, lane_r)
        kb = k_ref[pl.ds(start, ncol), :]
        vb = v_ref[pl.ds(start, ncol), :]
        accs = [update(qh[h], kb, vb * own_n[h] + den_n[h], ck[h:h + 1, :ncol],
                       carry[h][0][row0:row0 + rows], carry[h][1][row0:row0 + rows], row0)[1] for h in range(2)]
        o0 = accs[0] / accs[0][:, den_lane[0]:den_lane[0] + 1]
        o1 = accs[1] / accs[1][:, den_lane[1]:den_lane[1] + 1]
        o_ref[row0:row0 + rows, :] = jnp.where(lane_r < HEAD_DIM, o0, o1).astype(o_ref.dtype)


def fox_prompt(qkv, ck, rest, n_batch, seq, tq):
    nq = seq // tq
    pairs = FOX_HEADS // 2
    return pl.pallas_call(
        functools.partial(_fox_prompt_kernel, tq=tq),
        grid=(n_batch, pairs, nq),
        in_specs=[pl.BlockSpec((tq, LANES), lambda b, h, i: (b * nq + i, h)),
                  pl.BlockSpec((seq, LANES), lambda b, h, i: (b, pairs + h)),
                  pl.BlockSpec((seq, LANES), lambda b, h, i: (b, 2 * pairs + h)),
                  pl.BlockSpec((1, 1, nq, 2, tq), lambda b, h, i: (b, h, 0, 0, 0)),
                  pl.BlockSpec(memory_space=pl.ANY)],
        out_specs=pl.BlockSpec((tq, LANES), lambda b, h, i: (b * nq + i, h)),
        out_shape=jax.ShapeDtypeStruct(rest.shape, rest.dtype),
        input_output_aliases={4: 0},
        compiler_params=_params(("parallel", "parallel", "arbitrary")),
        name="fox_prompt",
    )(qkv, qkv, qkv, ck, rest)


def _block_diag_rows(full):
    rowh = _iota((FOX_HEADS, HEAD_DIM), 0)
    out = jnp.zeros((FOX_HEADS, HEAD_DIM), F32)
    for h in range(FOX_HEADS):
        out = out + jnp.where(rowh == h, full[:, h * HEAD_DIM:(h + 1) * HEAD_DIM], 0.0)
    return out


def _fox_sample_kernel(pt_ref, qbd_ref, q_ref, kn_ref, vbd_ref, ln_ref, *refs, pages_per_step):
    del pt_ref
    pp = pages_per_step
    k_refs, v_refs, lf_refs = refs[:pp], refs[pp:2 * pp], refs[2 * pp:3 * pp]
    o_ref = refs[3 * pp]
    m_ref, l_ref, acc_ref, carry_ref = refs[3 * pp + 1:]
    t = pl.program_id(1)
    scale = HEAD_DIM ** -0.5

    @pl.when(t == 0)
    def _():
        m_ref[...] = jnp.sum(q_ref[0] * kn_ref[0], axis=1, keepdims=True) * scale
        l_ref[...] = jnp.ones_like(l_ref)
        acc_ref[...] = vbd_ref[0]
        carry_ref[...] = ln_ref[0]

    qb = (qbd_ref[0] * scale).astype(BF16)
    lane = _iota((FOX_HEADS, PAGE_SIZE), 1)
    width = FOX_HEADS * HEAD_DIM
    carry = carry_ref[...]
    scores = []
    for r in range(pp):
        lf = lf_refs[r][0, 0]
        x = lf
        for sh in (1, 2, 4, 8, 16, 32, 64):
            x = x + jnp.where(lane + sh < PAGE_SIZE, pltpu.roll(x, PAGE_SIZE - sh, 1), 0.0)
        kp = k_refs[r][0, 0].reshape(width, PAGE_SIZE).astype(BF16)
        scores.append(_dot(qb, kp) + ((x - lf) + carry))
        carry = carry + x[:, 0:1]
    carry_ref[...] = carry
    s = jnp.concatenate(scores, axis=1)
    m = m_ref[...]
    m_new = jnp.maximum(m, jnp.max(s, axis=1, keepdims=True))
    alpha = jnp.exp(m - m_new)
    p = jnp.exp(s - m_new)
    l_ref[...] = alpha * l_ref[...] + jnp.sum(p, axis=1, keepdims=True)
    m_ref[...] = m_new
    pb = p.astype(BF16)
    acc = alpha * acc_ref[...]
    for r in range(pp):
        vp = v_refs[r][0, 0].reshape(width, PAGE_SIZE).astype(BF16)
        acc = acc + _nt_dot(pb[:, r * PAGE_SIZE:(r + 1) * PAGE_SIZE], vp)
    acc_ref[...] = acc

    @pl.when(t == pl.num_programs(1) - 1)
    def _():
        o_ref[0] = _block_diag_rows(acc / l_ref[...])


def fox_sample(page_table, qbd, q, kn, vbd, ln, cache_kt, cache_vt, cache_lft, layer, pages_per_step):
    bd, n_pages = page_table.shape
    pp = pages_per_step
    steps = n_pages // pp
    width = FOX_HEADS * HEAD_DIM

    def page(b, t, pt, r):
        return pt[b, n_pages - 1 - (t * pp + r)]

    kv_specs = [pl.BlockSpec((1, 1, FOX_HEADS, HEAD_DIM, PAGE_SIZE),
                             functools.partial(lambda b, t, pt, r: (layer, page(b, t, pt, r), 0, 0, 0), r=r))
                for r in range(pp)]
    lf_specs = [pl.BlockSpec((1, 1, FOX_HEADS, PAGE_SIZE),
                             functools.partial(lambda b, t, pt, r: (layer, page(b, t, pt, r), 0, 0), r=r))
                for r in range(pp)]
    tok = pl.BlockSpec((1, FOX_HEADS, HEAD_DIM), lambda b, t, pt: (b, 0, 0))
    wide = pl.BlockSpec((1, FOX_HEADS, width), lambda b, t, pt: (b, 0, 0))
    grid_spec = pltpu.PrefetchScalarGridSpec(
        num_scalar_prefetch=1,
        grid=(bd, steps),
        in_specs=[wide, tok, tok, wide, pl.BlockSpec((1, FOX_HEADS, 1), lambda b, t, pt: (b, 0, 0))]
        + kv_specs + kv_specs + lf_specs,
        out_specs=tok,
        scratch_shapes=[pltpu.VMEM((FOX_HEADS, 1), F32), pltpu.VMEM((FOX_HEADS, 1), F32),
                        pltpu.VMEM((FOX_HEADS, width), F32), pltpu.VMEM((FOX_HEADS, 1), F32)],
    )
    return pl.pallas_call(
        functools.partial(_fox_sample_kernel, pages_per_step=pp),
        grid_spec=grid_spec,
        out_shape=jax.ShapeDtypeStruct((bd, FOX_HEADS, HEAD_DIM), F32),
        compiler_params=_params(("parallel", "arbitrary")),
        name="fox_sample",
    )(page_table, qbd, q, kn, vbd, ln, *([cache_kt] * pp), *([cache_vt] * pp), *([cache_lft] * pp))


def _ssd_epilogue(y, xs, z, dskip_e, normw):
    y = (y + dskip_e * xs) * _silu(z)
    half = M_INNER // M_GROUPS
    outs = []
    for g in range(M_GROUPS):
        yg = y[:, g * half:(g + 1) * half]
        ms = jnp.sum(yg * yg, axis=1, keepdims=True) * (1.0 / half)
        outs.append(yg * lax.rsqrt(ms + RMS_EPS))
    return jnp.concatenate(outs, axis=1) * normw


def _ssd_prompt_kernel(xbc_ref, z_ref, dt_ref, cw_ref, cb_ref, nega_ref, dskip_ref, normw_ref, rest_ref,
                       o_ref, st_ref, ext_ref, h_ref):
    del rest_ref
    c = pl.program_id(1)
    L = SSD_CHUNK
    pad = SUBLANES

    @pl.when(c == 0)
    def _():
        ext_ref[0:pad, :] = jnp.zeros((pad, CONV_CH), F32)
        h_ref[...] = jnp.zeros_like(h_ref)

    ext_ref[pad:pad + L, :] = xbc_ref[...]
    acc = ext_ref[pad:pad + L, :] * cw_ref[CONV_W - 1:CONV_W, :]
    for j in range(CONV_W - 1):
        off = pad - (CONV_W - 1) + j
        acc = acc + ext_ref[off:off + L, :] * cw_ref[j:j + 1, :]
    u = _silu(acc + cb_ref[...])
    ext_ref[0:pad, :] = ext_ref[L:L + pad, :]

    xs = u[:, :M_INNER]
    dt_t = dt_ref[...]
    cum_t = _dot(dt_t * nega_ref[...], (_iota((L, L), 0) <= _iota((L, L), 1)).astype(F32),
                 precision=HIGHEST)
    eye = (_iota((L, L), 0) == _iota((L, L), 1)).astype(F32)
    cols = _nt_dot(eye, jnp.concatenate([cum_t, dt_t], axis=0), precision=HIGHEST)
    cum_last = cum_t[:, L - 1:L]
    tail_t = jnp.exp(cum_last - cum_t) * dt_t
    tril = _iota((L, L), 0) >= _iota((L, L), 1)
    lane = _iota((L, LANES), 1)
    rowi = _iota((L, LANES), 0)

    y_pairs = []
    for g in range(M_GROUPS):
        bm = u[:, M_INNER + g * D_STATE:M_INNER + (g + 1) * D_STATE]
        cm = u[:, M_INNER + BC_W + g * D_STATE:M_INNER + BC_W + (g + 1) * D_STATE]
        bmb = bm.astype(BF16)
        cmb = cm.astype(BF16)
        cb = _nt_dot(cmb, bmb)
        for pr in range(HPG // 2):
            pidx = g * (HPG // 2) + pr
            xs_pair = xs[:, pidx * LANES:(pidx + 1) * LANES]
            xs_pair_b = xs_pair.astype(BF16)
            h0 = h_ref[pidx]
            ych = _nt_dot(cmb, h0.astype(BF16))
            yw = []
            for k in range(2):
                hd = 2 * pidx + k
                diff = cols[:, hd:hd + 1] - cum_t[hd:hd + 1, :]
                decay = jnp.exp(jnp.where(tril, diff, NEG_INF))
                w = cb * decay * dt_t[hd:hd + 1, :]
                yw.append(_dot(w.astype(BF16), xs_pair_b))
            e0 = jnp.exp(cols[:, 2 * pidx:2 * pidx + 1])
            e1 = jnp.exp(cols[:, 2 * pidx + 1:2 * pidx + 2])
            first = lane < M_HEADDIM
            y_pairs.append(jnp.where(first, yw[0], yw[1]) + ych * jnp.where(first, e0, e1))
            top = rowi < M_HEADDIM
            tail_m = jnp.where(top, jnp.broadcast_to(tail_t[2 * pidx:2 * pidx + 1, :], (L, L)),
                               jnp.broadcast_to(tail_t[2 * pidx + 1:2 * pidx + 2, :], (L, L)))
            dec_m = jnp.where(top, jnp.exp(cum_last[2 * pidx:2 * pidx + 1, :]),
                              jnp.exp(cum_last[2 * pidx + 1:2 * pidx + 2, :]))
            xt = xs_pair.T * tail_m
            h_ref[pidx] = h0 * dec_m + _dot(xt.astype(BF16), bmb)

    y = jnp.concatenate(y_pairs, axis=1)
    o_ref[...] = _ssd_epilogue(y, xs, z_ref[...], dskip_ref[...], normw_ref[...]).astype(o_ref.dtype)
    st_ref[0] = h_ref[...]


def ssd_prompt(xz, dt_rows, conv_w, conv_b, nega, dskip_e, normw, rest, n_batch, seq):
    L = SSD_CHUNK
    nc = seq // L
    pairs = M_HEADS // 2
    const = lambda b, c: (0, 0)
    return pl.pallas_call(
        _ssd_prompt_kernel,
        grid=(n_batch, nc),
        in_specs=[pl.BlockSpec((L, CONV_CH), lambda b, c: (b * nc + c, 0)),
                  pl.BlockSpec((L, M_INNER), lambda b, c: (b * nc + c, CONV_CH // M_INNER)),
                  pl.BlockSpec((M_HEADS, L), lambda b, c: (1, b * nc + c)),
                  pl.BlockSpec((CONV_W, CONV_CH), const),
                  pl.BlockSpec((1, CONV_CH), const),
                  pl.BlockSpec((M_HEADS, 1), const),
                  pl.BlockSpec((1, M_INNER), const),
                  pl.BlockSpec((1, M_INNER), const),
                  pl.BlockSpec(memory_space=pl.ANY)],
        out_specs=[pl.BlockSpec((L, M_INNER), lambda b, c: (b * nc + c, 0)),
                   pl.BlockSpec((1, pairs, LANES, D_STATE), lambda b, c: (b, 0, 0, 0))],
        out_shape=[jax.ShapeDtypeStruct(rest.shape, rest.dtype),
                   jax.ShapeDtypeStruct((n_batch, pairs, LANES, D_STATE), F32)],
        input_output_aliases={8: 0},
        scratch_shapes=[pltpu.VMEM((L + SUBLANES, CONV_CH), F32),
                        pltpu.VMEM((pairs, LANES, D_STATE), F32)],
        compiler_params=_params(("parallel", "arbitrary")),
        name="ssd_prompt",
    )(xz, xz, dt_rows, conv_w, conv_b, nega, dskip_e, normw, rest)


def _ssd_sample_kernel(x_ref, wdt_ref, dtb_ref, nega_ref, xbc_ref, ctx_ref, z_ref, cw_ref, cb_ref,
                       dskip_ref, normw_ref, h0_ref, o_ref, hn_ref,
                       u_ref, coef_t_ref, dec_t_ref, dec_ref, dtx_ref, *, seqs_per_step):
    sb = seqs_per_step
    i = pl.program_id(0)
    nseq = x_ref.shape[0]

    @pl.when(i == 0)
    def _():
        acc = xbc_ref[...] * cw_ref[CONV_W - 1:CONV_W, :]
        for j in range(CONV_W - 1):
            acc = acc + ctx_ref[j] * cw_ref[j:j + 1, :]
        u = _silu(acc + cb_ref[...])
        u_ref[...] = u
        dt = _softplus(_dot(x_ref[...], wdt_ref[...], precision=HIGHEST) + dtb_ref[...])
        dec = jnp.exp(dt * nega_ref[...])
        coef = dt * u[:, :M_INNER]
        dec_ref[...] = dec
        dtx_ref[...] = coef
        for blk in range(M_INNER // LANES):
            sl = slice(blk * LANES, (blk + 1) * LANES)
            coef_t_ref[sl, :] = coef[:, sl].T
            dec_t_ref[sl, :] = dec[:, sl].T

    base = pl.multiple_of(i * sb, sb)
    ub = u_ref[pl.ds(base, sb), :]
    lane_seq = _iota((M_INNER, nseq), 1)
    rows = _iota((sb, M_INNER // M_GROUPS), 0)
    ch = [jnp.zeros((sb, M_INNER // M_GROUPS), F32) for _ in range(M_GROUPS)]
    for s in range(sb):
        onehot = lane_seq == base + s
        cx = jnp.sum(jnp.where(onehot, coef_t_ref[...], 0.0), axis=1, keepdims=True)
        dc = jnp.sum(jnp.where(onehot, dec_t_ref[...], 0.0), axis=1, keepdims=True)
        for g in range(M_GROUPS):
            brow = ub[s:s + 1, M_INNER + g * D_STATE:M_INNER + (g + 1) * D_STATE]
            cblk = ub[:, M_INNER + BC_W + g * D_STATE:M_INNER + BC_W + (g + 1) * D_STATE]
            hg = h0_ref[s, g * HPG:(g + 1) * HPG].reshape(HPG * M_HEADDIM, D_STATE)
            r = _nt_dot(cblk.astype(BF16), hg.astype(BF16))
            ch[g] = ch[g] + jnp.where(rows == s, r, 0.0)
            lo = g * HPG * M_HEADDIM
            hn = hg * dc[lo:lo + HPG * M_HEADDIM] + cx[lo:lo + HPG * M_HEADDIM] * brow
            hn_ref[s, g * HPG:(g + 1) * HPG] = hn.reshape(HPG, M_HEADDIM, D_STATE)

    xs = ub[:, :M_INNER]
    dec = dec_ref[pl.ds(base, sb), :]
    coef = dtx_ref[pl.ds(base, sb), :]
    ys = []
    half = M_INNER // M_GROUPS
    for g in range(M_GROUPS):
        bm = ub[:, M_INNER + g * D_STATE:M_INNER + (g + 1) * D_STATE]
        cm = ub[:, M_INNER + BC_W + g * D_STATE:M_INNER + BC_W + (g + 1) * D_STATE]
        cb = jnp.sum(cm * bm, axis=1, keepdims=True)
        ys.append(cb * coef[:, g * half:(g + 1) * half] + ch[g] * dec[:, g * half:(g + 1) * half])
    y = jnp.concatenate(ys, axis=1)
    o_ref[...] = _ssd_epilogue(y, xs, z_ref[...], dskip_ref[...], normw_ref[...])


def ssd_sample(x_s, wdt_e, dtb_e, nega_e, xbc_s, ctx, z_s, conv_w, conv_b, dskip_e, normw, h0, seqs_per_step):
    bd = x_s.shape[0]
    sb = seqs_per_step
    const = lambda i: (0, 0)
    return pl.pallas_call(
        functools.partial(_ssd_sample_kernel, seqs_per_step=sb),
        grid=(bd // sb,),
        in_specs=[pl.BlockSpec((bd, D_MODEL), const),
                  pl.BlockSpec((D_MODEL, M_INNER), const),
                  pl.BlockSpec((1, M_INNER), const),
                  pl.BlockSpec((1, M_INNER), const),
                  pl.BlockSpec((bd, CONV_CH), const),
                  pl.BlockSpec((CONV_W - 1, bd, CONV_CH), lambda i: (0, 0, 0)),
                  pl.BlockSpec((sb, M_INNER), lambda i: (i, 0)),
                  pl.BlockSpec((CONV_W, CONV_CH), const),
                  pl.BlockSpec((1, CONV_CH), const),
                  pl.BlockSpec((1, M_INNER), const),
                  pl.BlockSpec((1, M_INNER), const),
                  pl.BlockSpec((sb, M_HEADS, M_HEADDIM, D_STATE), lambda i: (i, 0, 0, 0))],
        out_specs=[pl.BlockSpec((sb, M_INNER), lambda i: (i, 0)),
                   pl.BlockSpec((sb, M_HEADS, M_HEADDIM, D_STATE), lambda i: (i, 0, 0, 0))],
        out_shape=[jax.ShapeDtypeStruct((bd, M_INNER), F32),
                   jax.ShapeDtypeStruct((bd, M_HEADS, M_HEADDIM, D_STATE), F32)],
        scratch_shapes=[pltpu.VMEM((bd, CONV_CH), F32),
                        pltpu.VMEM((M_INNER, bd), F32),
                        pltpu.VMEM((M_INNER, bd), F32),
                        pltpu.VMEM((bd, M_INNER), F32),
                        pltpu.VMEM((bd, M_INNER), F32)],
        compiler_params=_params(("arbitrary",)),
        name="ssd_sample",
    )(x_s, wdt_e, dtb_e, nega_e, xbc_s, ctx, z_s, conv_w, conv_b, dskip_e, normw, h0)


def _to_half(x, src_half, dst_half):
    return x if src_half == dst_half else pltpu.roll(x, HEAD_DIM, 1)


def _swa_prompt_kernel(sink_ref, q_ref, kp_ref, kc_ref, vp_ref, vc_ref, rest_ref, o_ref):
    del rest_ref
    n = pl.program_id(1)
    W = WINDOW
    scale = HEAD_DIM ** -0.5
    lane = _iota((W, LANES), 1)
    rows4 = _iota((SW_GROUP * W, 2 * W), 0)
    t = rows4 % W
    col = _iota((SW_GROUP * W, 2 * W), 1)
    valid = (col >= t) & (col <= t + W) & ((n > 0) | (col >= W))
    rcol = _iota((SW_GROUP * W, 1), 0) // W
    outs = [None] * SW_HEADS
    for j in range(SW_KV_HEADS):
        ch, hf = j // 2, j % 2
        kk = jnp.concatenate([kp_ref[:, ch * LANES:(ch + 1) * LANES],
                              kc_ref[:, ch * LANES:(ch + 1) * LANES]], axis=0).astype(BF16)
        vv = jnp.concatenate([vp_ref[:, ch * LANES:(ch + 1) * LANES],
                              vc_ref[:, ch * LANES:(ch + 1) * LANES]], axis=0).astype(BF16)
        qs = []
        sink = jnp.zeros((SW_GROUP * W, 1), F32)
        for g in range(SW_GROUP):
            hq = j * SW_GROUP + g
            qc = q_ref[:, (hq // 2) * LANES:(hq // 2 + 1) * LANES] * scale
            qc = _to_half(qc, hq % 2, hf)
            keep = (lane < HEAD_DIM) if hf == 0 else (lane >= HEAD_DIM)
            qs.append(jnp.where(keep, qc, 0.0))
            sink = jnp.where(rcol == g, sink_ref[hq], sink)
        q4 = jnp.concatenate(qs, axis=0).astype(BF16)
        s = jnp.where(valid, _nt_dot(q4, kk), NEG_INF)
        m = jnp.maximum(jnp.max(s, axis=1, keepdims=True), sink)
        p = jnp.exp(s - m)
        den = jnp.sum(p, axis=1, keepdims=True) + jnp.exp(sink - m)
        o = _dot(p.astype(BF16), vv) / den
        for g in range(SW_GROUP):
            hq = j * SW_GROUP + g
            outs[hq] = _to_half(o[g * W:(g + 1) * W], hf, hq % 2)
    for c in range(SW_HEADS // 2):
        o_ref[:, c * LANES:(c + 1) * LANES] = jnp.where(lane < HEAD_DIM, outs[2 * c],
                                                        outs[2 * c + 1]).astype(o_ref.dtype)


def swa_prompt(sinks, u_odd, rest, n_batch, seq):
    nb = seq // WINDOW
    kcol = ODD_MIX // KV_W
    return pl.pallas_call(
        _swa_prompt_kernel,
        grid=(n_batch, nb),
        in_specs=[pl.BlockSpec(memory_space=pltpu.SMEM),
                  pl.BlockSpec((WINDOW, ODD_MIX), lambda b, n: (b * nb + n, 0)),
                  pl.BlockSpec((WINDOW, KV_W), lambda b, n: (b * nb + jnp.maximum(n - 1, 0), kcol)),
                  pl.BlockSpec((WINDOW, KV_W), lambda b, n: (b * nb + n, kcol)),
                  pl.BlockSpec((WINDOW, KV_W), lambda b, n: (b * nb + jnp.maximum(n - 1, 0), kcol + 1)),
                  pl.BlockSpec((WINDOW, KV_W), lambda b, n: (b * nb + n, kcol + 1)),
                  pl.BlockSpec(memory_space=pl.ANY)],
        out_specs=pl.BlockSpec((WINDOW, ODD_MIX), lambda b, n: (b * nb + n, 0)),
        out_shape=jax.ShapeDtypeStruct(rest.shape, rest.dtype),
        input_output_aliases={6: 0},
        compiler_params=_params(("parallel", "arbitrary")),
        name="swa_prompt",
    )(sinks, u_odd, u_odd, u_odd, u_odd, u_odd, rest)


def _swa_sample_kernel(sink_ref, q_ref, kn_ref, vn_ref, bk_ref, bv_ref, o_ref, *, seqs_per_step):
    scale = HEAD_DIM ** -0.5
    rowg = _iota((SW_HEADS, HEAD_DIM), 0) // SW_GROUP
    sink = sink_ref[...]
    for s in range(seqs_per_step):
        q16 = q_ref[s] * scale
        qbd = jnp.concatenate([jnp.where(rowg == j, q16, 0.0) for j in range(SW_KV_HEADS)], axis=1)
        kb = bk_ref[s].reshape(KV_W, -1).astype(BF16)
        vb = bv_ref[s].reshape(KV_W, -1).astype(BF16)
        sc = _dot(qbd.astype(BF16), kb)
        s_new = jnp.sum(qbd * kn_ref[s:s + 1, :], axis=1, keepdims=True)
        m = jnp.maximum(jnp.maximum(jnp.max(sc, axis=1, keepdims=True), s_new), sink)
        p = jnp.exp(sc - m)
        pn = jnp.exp(s_new - m)
        den = jnp.sum(p, axis=1, keepdims=True) + pn + jnp.exp(sink - m)
        full = (_nt_dot(p.astype(BF16), vb) + pn * vn_ref[s:s + 1, :]) / den
        o16 = jnp.zeros((SW_HEADS, HEAD_DIM), F32)
        for j in range(SW_KV_HEADS):
            o16 = o16 + jnp.where(rowg == j, full[:, j * HEAD_DIM:(j + 1) * HEAD_DIM], 0.0)
        o_ref[s] = o16


def swa_sample(sinks_col, q, kn, vn, buf_k, buf_v, seqs_per_step):
    bd, _, _, lw = buf_k.shape
    sb = seqs_per_step
    return pl.pallas_call(
        functools.partial(_swa_sample_kernel, seqs_per_step=sb),
        grid=(bd // sb,),
        in_specs=[pl.BlockSpec((SW_HEADS, 1), lambda i: (0, 0)),
                  pl.BlockSpec((sb, SW_HEADS, HEAD_DIM), lambda i: (i, 0, 0)),
                  pl.BlockSpec((sb, KV_W), lambda i: (i, 0)),
                  pl.BlockSpec((sb, KV_W), lambda i: (i, 0)),
                  pl.BlockSpec((sb, SW_KV_HEADS, HEAD_DIM, lw), lambda i: (i, 0, 0, 0)),
                  pl.BlockSpec((sb, SW_KV_HEADS, HEAD_DIM, lw), lambda i: (i, 0, 0, 0))],
        out_specs=pl.BlockSpec((sb, SW_HEADS, HEAD_DIM), lambda i: (i, 0, 0)),
        out_shape=jax.ShapeDtypeStruct((bd, SW_HEADS, HEAD_DIM), F32),
        compiler_params=_params(("parallel",)),
        name="swa_sample",
    )(sinks_col, q, kn, vn, buf_k, buf_v)


def _layer_norm(h, g, b):
    mu = jnp.mean(h, axis=1, keepdims=True)
    d = h - mu
    var = jnp.mean(d * d, axis=1, keepdims=True)
    return d * lax.rsqrt(var + LN_EPS) * g + b


def _mix_route_kernel(*refs, alpha, n_feat, prompt_tiles):
    am_refs, wo_refs = refs[:n_feat], refs[n_feat:2 * n_feat]
    xp_ref, xt_ref, g_ref, b_ref, wr_ref, br_ref, x1_ref, rw_ref, re_ref = refs[2 * n_feat:]
    mix = _dot(am_refs[0][...].astype(BF16), wo_refs[0][...])
    for a_ref, w_ref in zip(am_refs[1:], wo_refs[1:]):
        mix = mix + _dot(a_ref[...].astype(BF16), w_ref[...])
    x = jnp.where(pl.program_id(0) < prompt_tiles, xp_ref[...], xt_ref[...])
    x1 = _layer_norm(alpha * x + mix, g_ref[...], b_ref[...])
    x1_ref[...] = x1
    wr = wr_ref[...]
    x_hi = x1.astype(BF16)
    w_hi = wr.astype(BF16)
    x_lo = (x1 - x_hi.astype(F32)).astype(BF16)
    w_lo = (wr - w_hi.astype(F32)).astype(BF16)
    logits = _dot(x_hi, w_hi) + _dot(x_lo, w_hi) + _dot(x_hi, w_lo) + br_ref[...]
    lane_i = _iota(logits.shape, 1)
    lane = lane_i.astype(F32)
    big = float(LANES)
    gl = jnp.where(lane_i < N_EXPERT_GROUPS, logits, NEG_INF)
    gmax = jnp.max(gl, axis=1, keepdims=True)
    grp = jnp.min(jnp.where(gl == gmax, lane, big), axis=1, keepdims=True)
    pg = 1.0 / jnp.sum(jnp.exp(gl - gmax), axis=1, keepdims=True)
    rel = lane - (N_EXPERT_GROUPS + grp * EXPERTS_PER_GROUP)
    el = jnp.where(rel >= 0.0, jnp.where(rel < EXPERTS_PER_GROUP, logits, NEG_INF), NEG_INF)
    v1 = jnp.max(el, axis=1, keepdims=True)
    i1 = jnp.min(jnp.where(el == v1, lane, big), axis=1, keepdims=True)
    el2 = jnp.where(lane == i1, NEG_INF, el)
    v2 = jnp.max(el2, axis=1, keepdims=True)
    i2 = jnp.min(jnp.where(el2 == v2, lane, big), axis=1, keepdims=True)
    e = jnp.exp(v2 - v1)
    w1 = pg / (1.0 + e)
    w2 = w1 * e
    rw_ref[...] = jnp.where(lane_i == 0, w1, jnp.where(lane_i == 1, w2, 0.0))
    e1 = (i1 - N_EXPERT_GROUPS).astype(jnp.int32)
    e2 = (i2 - N_EXPERT_GROUPS).astype(jnp.int32)
    re_ref[...] = jnp.where(lane_i == 0, e1, jnp.where(lane_i == 1, e2, 0))


def mix_route(feats, wos, x_prompt, x_tail, g, b, wr, br, alpha, tm):
    d = x_tail.shape[1]
    prompt_tiles = (feats[0].shape[0] - x_tail.shape[0]) // tm
    m = feats[0].shape[0]
    const = lambda i: (0, 0)
    row = lambda i: (i, 0)
    return pl.pallas_call(
        functools.partial(_mix_route_kernel, alpha=alpha, n_feat=len(feats), prompt_tiles=prompt_tiles),
        grid=(m // tm,),
        in_specs=[pl.BlockSpec((tm, a.shape[1]), row) for a in feats]
        + [pl.BlockSpec((w.shape[0], d), const) for w in wos]
        + [pl.BlockSpec((tm, d), lambda i: (jnp.minimum(i, prompt_tiles - 1), 0)),
           pl.BlockSpec((tm, d), lambda i: (jnp.maximum(i - prompt_tiles, 0), 0)),
           pl.BlockSpec((1, d), const), pl.BlockSpec((1, d), const),
           pl.BlockSpec((d, LANES), const), pl.BlockSpec((1, LANES), const)],
        out_specs=[pl.BlockSpec((tm, d), row),
                   pl.BlockSpec((tm, LANES), row), pl.BlockSpec((tm, LANES), row)],
        out_shape=[jax.ShapeDtypeStruct((m, d), F32),
                   jax.ShapeDtypeStruct((m, LANES), F32), jax.ShapeDtypeStruct((m, LANES), jnp.int32)],
        compiler_params=_params(("parallel",)),
        name="mix_route",
    )(*feats, *wos, x_prompt, x_tail, g, b, wr, br)


def _experts_kernel(te_ref, tv_ref, x_ref, wg_ref, wu_ref, wd_ref, *rest, first_tile):
    y_ref = rest[-1]
    t = first_tile + pl.program_id(0)

    @pl.when(tv_ref[t] > 0)
    def _():
        x = x_ref[...].astype(BF16)
        h = _silu(_dot(x, wg_ref[0, 0].astype(BF16))) * _dot(x, wu_ref[0, 0].astype(BF16))
        y_ref[...] = _dot(h.astype(BF16), wd_ref[0, 0].astype(BF16))

    @pl.when(tv_ref[t] == 0)
    def _():
        y_ref[...] = jnp.zeros_like(y_ref)


def experts(tile_expert, tile_valid, xs, wg, wu, wd, layer, te, first_tile, n_rows, y_prev):
    r, d = xs.shape
    ff = wg.shape[3]
    ex = lambda t, e, v: (layer, e[first_tile + t], 0, 0)
    in_specs = [pl.BlockSpec((te, d), lambda t, e, v: (t, 0)),
                pl.BlockSpec((1, 1, d, ff), ex), pl.BlockSpec((1, 1, d, ff), ex), pl.BlockSpec((1, 1, ff, d), ex)]
    args = [tile_expert, tile_valid, xs, wg, wu, wd]
    aliases = {}
    if y_prev is not None:
        in_specs.append(pl.BlockSpec(memory_space=pl.ANY))
        aliases = {len(args): 0}
        args.append(y_prev)
    grid_spec = pltpu.PrefetchScalarGridSpec(
        num_scalar_prefetch=2,
        grid=(r // te,),
        in_specs=in_specs,
        out_specs=pl.BlockSpec((te, d), lambda t, e, v: (first_tile + t, 0)),
    )
    return pl.pallas_call(
        functools.partial(_experts_kernel, first_tile=first_tile),
        grid_spec=grid_spec,
        out_shape=jax.ShapeDtypeStruct((n_rows, d), F32),
        input_output_aliases=aliases,
        compiler_params=_params(("arbitrary",)),
        name="experts",
    )(*args)


def _combine_ple_kernel(x1_ref, y0_ref, y1_ref, rw_ref, pp_ref, pt_ref, g_ref, b_ref, wg_ref, bg_ref, wp_ref,
                        o_ref, *, alpha, prompt_tiles, first_tile):
    rw = rw_ref[...]
    f = rw[:, 0:1] * y0_ref[...] + rw[:, 1:2] * y1_ref[...]
    x2 = _layer_norm(alpha * x1_ref[...] + f, g_ref[...], b_ref[...])
    gl = _dot(x2.astype(BF16), wg_ref[...]) + bg_ref[...]
    gate = 1.0 / (1.0 + jnp.exp(-gl))
    p = jnp.where(first_tile + pl.program_id(0) < prompt_tiles, pp_ref[0], pt_ref[...])
    o_ref[...] = x2 + gate * _dot(p.astype(BF16), wp_ref[...])


def combine_ple(x1, y0, y1, rw, p_prompt, p_tail, layer, g, b, wg, bg, wp, alpha, tm, first_tile, n_tiles):
    d = x1.shape[1]
    pd = p_prompt.shape[2]
    prompt_tiles = p_prompt.shape[1] // tm
    const = lambda i: (0, 0)
    row = lambda i: (first_tile + i, 0)
    return pl.pallas_call(
        functools.partial(_combine_ple_kernel, alpha=alpha, prompt_tiles=prompt_tiles, first_tile=first_tile),
        grid=(n_tiles,),
        in_specs=[pl.BlockSpec((tm, d), row), pl.BlockSpec((tm, d), row), pl.BlockSpec((tm, d), row),
                  pl.BlockSpec((tm, LANES), row),
                  pl.BlockSpec((1, tm, pd), lambda i: (layer, jnp.minimum(first_tile + i, prompt_tiles - 1), 0)),
                  pl.BlockSpec((tm, pd), lambda i: (jnp.maximum(first_tile + i - prompt_tiles, 0), 0)),
                  pl.BlockSpec((1, d), const), pl.BlockSpec((1, d), const),
                  pl.BlockSpec((d, d), const), pl.BlockSpec((1, d), const), pl.BlockSpec((pd, d), const)],
        out_specs=pl.BlockSpec((tm, d), lambda i: (i, 0)),
        out_shape=jax.ShapeDtypeStruct((n_tiles * tm, d), F32),
        compiler_params=_params(("parallel",)),
        name="combine_ple",
    )(x1, y0, y1, rw, p_prompt, p_tail, g, b, wg, bg, wp)


def _tiles(n_tokens):
    tm = 512 if n_tokens >= 4096 else 128
    return tm, ((n_tokens + tm - 1) // tm) * tm


def _expert_tile(n_tokens):
    return 256 if n_tokens >= 4096 else 32


def _channel_and_ple(x_main, x_tail, feats, p_prompt, p_tail, li, w_out, w, tm):
    depth = w["ln_mix_g"].shape[0]
    alpha = (2 * depth) ** 0.25
    ntp, d = feats[0].shape[0], x_tail.shape[1]
    wr = jnp.concatenate([w["w_router_group"][li],
                          jnp.moveaxis(w["w_router_expert"][li], 0, 1).reshape(d, N_EXPERTS)], axis=1)
    wr = jnp.pad(wr, ((0, 0), (0, LANES - wr.shape[1])))
    br = jnp.concatenate([w["b_router_group"][li], w["b_router_expert"][li].reshape(-1)])
    br = jnp.pad(br, (0, LANES - br.shape[0]))[None, :]
    wo = w_out.astype(BF16)
    splits = [0]
    for a in feats:
        splits.append(splits[-1] + a.shape[1])
    x1, rw, re = mix_route(feats, [wo[lo:hi] for lo, hi in zip(splits[:-1], splits[1:])], x_main, x_tail,
                           w["ln_mix_g"][li][None], w["ln_mix_b"][li][None], wr, br, alpha, tm)

    te = _expert_tile(ntp)
    n_flat = 2 * ntp
    flat = re[:, :2].reshape(-1)
    onehot = (flat[:, None] == jnp.arange(N_EXPERTS, dtype=jnp.int32)[None, :]).astype(jnp.int32)
    running = jnp.cumsum(onehot, axis=0)
    counts = running[-1]
    padded = ((counts + te - 1) // te) * te
    gend = jnp.cumsum(padded)
    gstart = gend - padded
    pos_flat = jnp.sum(onehot * (running - 1 + gstart[None, :]), axis=1)
    n_rows = ((n_flat + N_EXPERTS * (te - 1) + te - 1) // te) * te
    row_token = (jnp.arange(n_rows, dtype=jnp.int32) % ntp).at[pos_flat].set(
        jnp.arange(n_flat, dtype=jnp.int32) // 2, mode="promise_in_bounds", unique_indices=True)
    pos_flat = pos_flat.reshape(ntp, 2)
    tile_start = jnp.arange(n_rows // te, dtype=jnp.int32) * te
    tile_expert = jnp.minimum(jnp.sum((gend[None, :] <= tile_start[:, None]).astype(jnp.int32), axis=1),
                              N_EXPERTS - 1)
    tile_valid = (tile_start < gend[-1]).astype(jnp.int32)

    def rows(a, idx):
        return a.at[idx].get(mode="promise_in_bounds")

    n_tiles = n_rows // te
    bounds = [(c * n_tiles) // EXPERT_CHUNKS for c in range(EXPERT_CHUNKS + 1)]
    y = None
    for t0, t1 in zip(bounds[:-1], bounds[1:]):
        xs = rows(x1, row_token[t0 * te:t1 * te])
        y = experts(tile_expert, tile_valid, xs, w["w_exp_gate"], w["w_exp_up"], w["w_exp_down"], li, te,
                    t0, n_rows, y)
    y0 = rows(y, pos_flat[:, 0])
    y1 = rows(y, pos_flat[:, 1])
    prompt_tiles = p_prompt.shape[1] // tm
    parts = [(0, ntp // tm)] if li + 1 < depth else [(0, prompt_tiles), (prompt_tiles, ntp // tm - prompt_tiles)]
    outs = [combine_ple(x1, y0, y1, rw, p_prompt, p_tail, li, w["ln_ffn_g"][li][None], w["ln_ffn_b"][li][None],
                        w["w_ple_gate"][li].astype(BF16), w["b_ple_gate"][li][None],
                        w["w_ple_proj"][li].astype(BF16), alpha, tm, first, n) for first, n in parts]
    return outs[0] if len(outs) == 1 else tuple(outs)


def kernel(x_prompt, x_sample, p_prompt, p_sample, cache_fox_k, cache_fox_v, cache_fox_logf, state_ssm, state_conv, cache_win_k, cache_win_v, page_table, w_in_even, b_fgate, conv_w, conv_b, dt_bias, a_log, d_skip, ssm_norm_w, w_out_even, w_in_odd, attn_sinks, w_out_odd, ln_mix_g, ln_mix_b, ln_ffn_g, ln_ffn_b, w_router_group, b_router_group, w_router_expert, b_router_expert, w_exp_gate, w_exp_up, w_exp_down, w_ple_proj, w_ple_gate, b_ple_gate):
    bp, seq, d = x_prompt.shape
    bd, t_dec, _ = x_sample.shape
    assert t_dec == 1 and d == D_MODEL
    depth = p_prompt.shape[0]
    n_pages = page_table.shape[1]
    past_len = n_pages * PAGE_SIZE
    np_tok = bp * seq
    nt = np_tok + bd
    tm, ntp = _tiles(nt)
    pad = ntp - nt

    def past_prompt(a_s):
        a_s = a_s.reshape(bd, -1).astype(BF16)
        return jnp.zeros((ntp, a_s.shape[1]), BF16).at[np_tok:nt].set(a_s)

    def after_prompt(a_s):
        a_s = a_s.reshape(bd, -1)
        return jnp.zeros((ntp - np_tok, a_s.shape[1]), a_s.dtype).at[:bd].set(a_s)

    assert np_tok % tm == 0 and seq >= CONV_W - 1
    x_full = None
    x_main = x_prompt.reshape(np_tok, d)
    x_tail = after_prompt(x_sample)
    p_all = p_prompt.reshape(depth, np_tok, -1)
    shared = dict(ln_mix_g=ln_mix_g, ln_mix_b=ln_mix_b, ln_ffn_g=ln_ffn_g, ln_ffn_b=ln_ffn_b,
                  w_router_group=w_router_group, b_router_group=b_router_group,
                  w_router_expert=w_router_expert, b_router_expert=b_router_expert,
                  w_exp_gate=w_exp_gate, w_exp_up=w_exp_up, w_exp_down=w_exp_down,
                  w_ple_proj=w_ple_proj, w_ple_gate=w_ple_gate, b_ple_gate=b_ple_gate)

    half = HEAD_DIM // 2
    inv = jnp.exp(-math.log(ROPE_THETA) * jnp.arange(half, dtype=F32) / half)
    pos = jnp.concatenate([jnp.tile(jnp.arange(seq, dtype=jnp.int32), bp),
                           jnp.full((bd,), past_len, jnp.int32), jnp.zeros((pad,), jnp.int32)])
    ang = pos.astype(F32)[:, None] * inv[None, :]
    cos_t = jnp.tile(jnp.cos(ang), (1, LANES // half))
    sin_t = jnp.tile(jnp.concatenate([-jnp.sin(ang), jnp.sin(ang)], axis=1), (1, LANES // HEAD_DIM))

    even_p, even_s, odd_p, odd_s = [], [], [], []
    for li in range(depth):
        j = li // 2
        if li % 2 == 0:
            wi = w_in_even[j]
            c0 = 3 * FOX_WIDTH
            c1 = c0 + FOX_HEADS
            c2 = c1 + M_INNER
            c3 = c2 + CONV_CH
            w_main = jnp.concatenate([wi[:, :c0], wi[:, c2:c3], wi[:, c1:c2]], axis=1).astype(BF16)
            w_small_t = jnp.concatenate([wi[:, c0:c1], wi[:, c3:]], axis=1).T
            b_small = jnp.concatenate([b_fgate[j], dt_bias[j]])[:, None]
            prompt_tiles = np_tok // tm
            k_p, v_p, xz, qkv, _, small = even_proj(x_main, w_main, w_small_t, b_small, tm, 0, prompt_tiles)
            k_t, v_t, xz_t, _, q_t, small_t = even_proj(x_tail, w_main, w_small_t, b_small, tm, 0,
                                                        ntp // tm - prompt_tiles)

            tq = min(FOX_TILE, seq)
            logf_p = small[:FOX_HEADS].reshape(FOX_HEADS, bp, seq)
            cum = cumsum_lanes(jnp.moveaxis(logf_p, 1, 0).reshape(bp * FOX_HEADS, seq), min(CUMSUM_CHUNK, seq))
            ck = cum.reshape(bp, FOX_HEADS // 2, 2, seq // tq, tq).transpose(0, 1, 3, 2, 4)
            logf_s = small_t[:FOX_HEADS, :bd].T
            q_s = q_t[:bd].reshape(bd, FOX_HEADS, HEAD_DIM)
            k_s = k_t[:bd].reshape(bd, FOX_HEADS, HEAD_DIM)
            v_s = v_t[:bd].reshape(bd, FOX_HEADS, HEAD_DIM)
            eye = jnp.eye(FOX_HEADS, dtype=F32)[None, :, :, None]

            def block_diag(a):
                return (a[:, :, None, :] * eye).reshape(bd, FOX_HEADS, FOX_WIDTH)

            a_s = fox_sample(page_table, block_diag(q_s), q_s, k_s, block_diag(v_s), logf_s[:, :, None],
                             jnp.transpose(cache_fox_k, (0, 1, 3, 4, 2)), jnp.transpose(cache_fox_v, (0, 1, 3, 4, 2)),
                             jnp.transpose(cache_fox_logf, (0, 1, 3, 2)), j, min(FOX_PAGES_PER_STEP, n_pages))
            a_all = fox_prompt(qkv, ck, past_prompt(a_s), bp, seq, tq)

            nega = -jnp.exp(a_log[j])
            dskip_e = jnp.repeat(d_skip[j], M_HEADDIM)[None, :]
            normw = ssm_norm_w[j][None, :]
            xz_s = xz_t[:bd]
            m_s, st_s = ssd_sample(x_tail[:bd], jnp.repeat(wi[:, c3:], M_HEADDIM, axis=1),
                                   jnp.repeat(dt_bias[j], M_HEADDIM)[None, :], jnp.repeat(nega, M_HEADDIM)[None, :],
                                   xz_s[:, :CONV_CH], jnp.moveaxis(state_conv[j], 1, 0), xz_s[:, CONV_CH:],
                                   conv_w[j], conv_b[j][None, :], dskip_e, normw, state_ssm[j],
                                   min(DECODE_SEQS_PER_STEP, bd))
            m_all, st_p = ssd_prompt(xz, small, conv_w[j], conv_b[j][None, :], nega[:, None], dskip_e, normw,
                                     past_prompt(m_s), bp, seq)
            feats = [a_all, m_all]
            w_out = w_out_even[j]

            kp = k_p.reshape(bp, seq, FOX_HEADS, HEAD_DIM)
            vp = v_p.reshape(bp, seq, FOX_HEADS, HEAD_DIM)
            last = jnp.stack([xz[(b + 1) * seq - (CONV_W - 1):(b + 1) * seq, :CONV_CH] for b in range(bp)])
            conv_p = jnp.concatenate([jnp.zeros((bp, CONV_W - 1, CONV_CH), F32), last], axis=1)[:, -(CONV_W - 1):]
            even_p.append((kp, vp, jnp.moveaxis(logf_p, 0, 2),
                           st_p.reshape(bp, M_HEADS, M_HEADDIM, D_STATE), conv_p))
            conv_s = jnp.concatenate([state_conv[j], xz_s[:, None, :CONV_CH]], axis=1)[:, -(CONV_W - 1):]
            even_s.append((k_s[:, None], v_s[:, None], logf_s[:, None, :], st_s, conv_s))
        else:
            assert x_full is not None
            u = matmul_rope(x_full, w_in_odd[j].astype(BF16), cos_t, sin_t, tm, ROPE_COL_TILE, ODD_MIX + KV_W)
            u_s = u[np_tok:nt]
            lw = cache_win_k.shape[2]
            kn = u_s[:, ODD_MIX:ODD_MIX + KV_W]
            vn = u_s[:, ODD_MIX + KV_W:]
            o_s = swa_sample(attn_sinks[j][:, None], u_s[:, :ODD_MIX].reshape(bd, SW_HEADS, HEAD_DIM), kn, vn,
                             jnp.transpose(cache_win_k[j], (0, 2, 3, 1)), jnp.transpose(cache_win_v[j], (0, 2, 3, 1)),
                             min(DECODE_SEQS_PER_STEP, bd))
            feats = [swa_prompt(attn_sinks[j], u, past_prompt(o_s), bp, seq)]
            w_out = w_out_odd[j]

            rows = min(WINDOW, seq)
            tail = jnp.stack([u[(b + 1) * seq - rows:(b + 1) * seq, ODD_MIX:] for b in range(bp)])
            odd_p.append((tail[:, :, :KV_W].reshape(bp, rows, SW_KV_HEADS, HEAD_DIM),
                          tail[:, :, KV_W:].reshape(bp, rows, SW_KV_HEADS, HEAD_DIM)))
            ka = jnp.concatenate([cache_win_k[j], kn.reshape(bd, 1, SW_KV_HEADS, HEAD_DIM)], axis=1)[:, -lw:]
            va = jnp.concatenate([cache_win_v[j], vn.reshape(bd, 1, SW_KV_HEADS, HEAD_DIM)], axis=1)[:, -lw:]
            odd_s.append((ka, va))
        x_full = _channel_and_ple(x_main, x_tail, feats, p_all, after_prompt(p_sample[li]), li, w_out, shared, tm)
        if li + 1 < depth:
            x_main, x_tail = x_full, x_full[np_tok:]

    x_prompt_out, x_tail_out = x_full
    yp = x_prompt_out.reshape(bp, seq, d)
    ys = x_tail_out[:bd].reshape(bd, 1, d)
    return (yp, ys,
            jnp.stack([st[0] for st in even_p]), jnp.stack([st[1] for st in even_p]),
            jnp.stack([st[2] for st in even_p]), jnp.stack([st[3] for st in even_p]),
            jnp.stack([st[4] for st in even_p]),
            jnp.stack([st[0] for st in odd_p]), jnp.stack([st[1] for st in odd_p]),
            jnp.stack([st[0] for st in even_s]), jnp.stack([st[1] for st in even_s]),
            jnp.stack([st[2] for st in even_s]), jnp.stack([st[3] for st in even_s]),
            jnp.stack([st[4] for st in even_s]),
            jnp.stack([st[0] for st in odd_s]), jnp.stack([st[1] for st in odd_s]))
```

```python
import functools
import math

import jax
import jax.numpy as jnp
from jax import lax
from jax.experimental import pallas as pl
from jax.experimental.pallas import tpu as pltpu

F32 = jnp.float32
BF16 = jnp.bfloat16
HIGHEST = lax.Precision.HIGHEST

D_MODEL = 1024
HEAD_DIM = 64
FOX_HEADS = 8
FOX_WIDTH = FOX_HEADS * HEAD_DIM
M_HEADS = 8
M_HEADDIM = 64
M_INNER = M_HEADS * M_HEADDIM
M_GROUPS = 2
HPG = M_HEADS // M_GROUPS
D_STATE = 128
CONV_W = 4
BC_W = M_GROUPS * D_STATE
CONV_CH = M_INNER + 2 * BC_W
SSD_CHUNK = 128
RMS_EPS = 1e-5
SW_HEADS = 16
SW_KV_HEADS = 4
SW_GROUP = SW_HEADS // SW_KV_HEADS
WINDOW = 128
ROPE_THETA = 10000.0
ODD_MIX = SW_HEADS * HEAD_DIM
KV_W = SW_KV_HEADS * HEAD_DIM
N_EXPERT_GROUPS = 4
EXPERTS_PER_GROUP = 8
N_EXPERTS = N_EXPERT_GROUPS * EXPERTS_PER_GROUP
LN_EPS = 1e-5
PAGE_SIZE = 128

LANES = 128
SUBLANES = 8
VMEM_LIMIT = 48 * 1024 * 1024

NEG_INF = float("-inf")

EXPERT_CHUNK_ENDS = (0.25, 1.0)
FOX_PAGES_PER_STEP = 32
FOX_TILE = 1024
FOX_DIAG_PARTS = 2
ROPE_COL_TILE = 256
CUMSUM_CHUNK = 512
DECODE_SEQS_PER_STEP = 8


def _params(sem, vmem=VMEM_LIMIT):
    return pltpu.CompilerParams(dimension_semantics=sem, vmem_limit_bytes=vmem)


def _nt_dot(a, b, precision=None):
    return lax.dot_general(a, b, (((1,), (1,)), ((), ())), precision=precision,
                           preferred_element_type=F32)


def _dot(a, b, precision=None):
    return jnp.dot(a, b, precision=precision, preferred_element_type=F32)


def _silu(x):
    return x * (1.0 / (1.0 + jnp.exp(-x)))


def _softplus(x):
    return jnp.maximum(x, 0.0) + jnp.log(1.0 + jnp.exp(-jnp.abs(x)))


def _iota(shape, dim):
    return lax.broadcasted_iota(jnp.int32, shape, dim)


def _mm_rope_kernel(x_ref, w_ref, cos_ref, sin_ref, o_ref, *, tn, rope_cols):
    xb = x_ref[...].astype(BF16)
    reps = tn // LANES
    cos = jnp.concatenate([cos_ref[...]] * reps, axis=1)
    sin = jnp.concatenate([sin_ref[...]] * reps, axis=1)
    half = HEAD_DIM // 2
    first = (_iota((xb.shape[0], tn), 1) % HEAD_DIM) < half
    for c in range(w_ref.shape[1] // tn):
        acc = _dot(xb, w_ref[:, c * tn:(c + 1) * tn])
        if c * tn < rope_cols:
            partner = jnp.where(first, pltpu.roll(acc, tn - half, 1), pltpu.roll(acc, half, 1))
            acc = acc * cos + partner * sin
        o_ref[:, c * tn:(c + 1) * tn] = acc


def _even_proj_kernel(x_ref, w_ref, wt_ref, bt_ref, k_ref, v_ref, xz_ref, qkv_ref, qt_ref, small_ref):
    x = x_ref[...]
    xb = x.astype(BF16)
    wt = wt_ref[...]
    wt_hi = wt.astype(BF16)
    wt_lo = (wt - wt_hi.astype(F32)).astype(BF16)
    x_lo = (x - xb.astype(F32)).astype(BF16)
    r = _nt_dot(wt_hi, xb) + _nt_dot(wt_lo, xb) + _nt_dot(wt_hi, x_lo) + bt_ref[...]
    small_ref[...] = jnp.where(_iota(r.shape, 0) < FOX_HEADS, -_softplus(-r), _softplus(r))
    w = FOX_WIDTH
    f32_dst = (None, k_ref, v_ref)
    for c in range(3):
        acc = _dot(xb, w_ref[:, c * w:(c + 1) * w])
        qkv_ref[:, c * w:(c + 1) * w] = acc.astype(BF16)
        if f32_dst[c] is not None:
            f32_dst[c][...] = acc
        else:
            @pl.when(pl.program_id(0) == pl.num_programs(0) - 1)
            def _():
                qt_ref[...] = acc
    for c in range(xz_ref.shape[1] // w):
        xz_ref[:, c * w:(c + 1) * w] = _dot(xb, w_ref[:, (3 + c) * w:(4 + c) * w])


def even_proj(x, w, wt, bt, tm, first_tile, n_tiles):
    k = x.shape[1]
    m = n_tiles * tm
    n = w.shape[1]
    wq = 3 * FOX_WIDTH
    row = lambda i: (i, 0)
    const = lambda i: (0, 0)
    return pl.pallas_call(
        _even_proj_kernel,
        grid=(n_tiles,),
        in_specs=[pl.BlockSpec((tm, k), lambda i: (first_tile + i, 0)), pl.BlockSpec((k, n), const),
                  pl.BlockSpec((16, k), const), pl.BlockSpec((16, 1), const)],
        out_specs=[pl.BlockSpec((tm, FOX_WIDTH), row), pl.BlockSpec((tm, FOX_WIDTH), row),
                   pl.BlockSpec((tm, n - wq), row), pl.BlockSpec((tm, wq), row),
                   pl.BlockSpec((tm, FOX_WIDTH), const), pl.BlockSpec((16, tm), lambda i: (0, i))],
        out_shape=[jax.ShapeDtypeStruct((m, FOX_WIDTH), F32), jax.ShapeDtypeStruct((m, FOX_WIDTH), F32),
                   jax.ShapeDtypeStruct((m, n - wq), F32), jax.ShapeDtypeStruct((m, wq), BF16),
                   jax.ShapeDtypeStruct((tm, FOX_WIDTH), F32), jax.ShapeDtypeStruct((16, m), F32)],
        compiler_params=_params(("arbitrary",)),
        name="even_proj",
    )(x, w, wt, bt)


def matmul_rope(x, w, cos, sin, tm, tn, rope_cols):
    m, k = x.shape
    n = w.shape[1]
    assert n % tn == 0 and rope_cols % tn == 0
    return pl.pallas_call(
        functools.partial(_mm_rope_kernel, tn=tn, rope_cols=rope_cols),
        grid=(m // tm,),
        in_specs=[pl.BlockSpec((tm, k), lambda i: (i, 0)),
                  pl.BlockSpec((k, n), lambda i: (0, 0)),
                  pl.BlockSpec((tm, LANES), lambda i: (i, 0)),
                  pl.BlockSpec((tm, LANES), lambda i: (i, 0))],
        out_specs=pl.BlockSpec((tm, n), lambda i: (i, 0)),
        out_shape=jax.ShapeDtypeStruct((m, n), F32),
        compiler_params=_params(("parallel",)),
        name="matmul_rope",
    )(x, w, cos, sin)


def _cumsum_kernel(x_ref, o_ref, carry_ref):
    @pl.when(pl.program_id(0) == 0)
    def _():
        carry_ref[...] = jnp.zeros_like(carry_ref)

    x = x_ref[...]
    w = x.shape[1]
    tri = (_iota((w, w), 0) <= _iota((w, w), 1)).astype(F32)
    c = _dot(x, tri, precision=HIGHEST) + carry_ref[...]
    o_ref[...] = c
    carry_ref[...] = c[:, w - 1:w]


def cumsum_lanes(x, chunk):
    r, l = x.shape
    return pl.pallas_call(
        _cumsum_kernel,
        grid=(l // chunk,),
        in_specs=[pl.BlockSpec((r, chunk), lambda i: (0, i))],
        out_specs=pl.BlockSpec((r, chunk), lambda i: (0, i)),
        out_shape=jax.ShapeDtypeStruct((r, l), F32),
        scratch_shapes=[pltpu.VMEM((r, 1), F32)],
        compiler_params=_params(("arbitrary",)),
        name="cumsum_lanes",
    )(x)


def _fox_prompt_kernel(q_ref, k_ref, v_ref, ck_ref, rest_ref, o_ref, *, tq):
    del rest_ref
    qi = pl.program_id(2)
    den_lane = (HEAD_DIM, 0)

    def lane_masks(rows):
        lane = _iota((rows, LANES), 1)
        own = (jnp.where(lane < HEAD_DIM, 1.0, 0.0).astype(BF16), jnp.where(lane >= HEAD_DIM, 1.0, 0.0).astype(BF16))
        den = tuple(jnp.where(lane == d, 1.0, 0.0).astype(BF16) for d in den_lane)
        return lane, own, den

    def query_heads(row0, rows, lane):
        q = q_ref[row0:row0 + rows, :] * (HEAD_DIM ** -0.5)
        zero = jnp.zeros_like(q)
        return jnp.where(lane < HEAD_DIM, q, zero), jnp.where(lane >= HEAD_DIM, q, zero)

    def update(qh, kb, vh, ckh, m, acc, row_offset):
        s = _nt_dot(qh, kb) - ckh
        if row_offset is not None:
            s = jnp.where(_iota(s.shape, 1) <= _iota(s.shape, 0) + row_offset, s, NEG_INF)
        m_new = jnp.maximum(m, jnp.max(s, axis=1, keepdims=True))
        p = jnp.exp(s - m_new).astype(BF16)
        return m_new, jnp.exp(m - m_new) * acc + _dot(p, vh)

    lane_q, own, den = lane_masks(tq)
    q_heads = query_heads(0, tq, lane_q)

    def body(j, carry):
        start = pl.multiple_of(j * tq, tq)
        kb = k_ref[pl.ds(start, tq), :]
        vb = v_ref[pl.ds(start, tq), :]
        ck = ck_ref[0, 0, j]
        return tuple(update(q_heads[h], kb, vb * own[h] + den[h], ck[h:h + 1, :], *carry[h], None)
                     for h in range(2))

    init1 = (jnp.full((tq, 1), NEG_INF, F32), jnp.zeros((tq, LANES), F32))
    carry = lax.fori_loop(0, qi, body, (init1, init1))

    start = pl.multiple_of(qi * tq, tq)
    ck = ck_ref[0, 0, qi]
    rows = tq // FOX_DIAG_PARTS
    for part in range(FOX_DIAG_PARTS):
        row0, ncol = part * rows, (part + 1) * rows
        lane_r, _, _ = lane_masks(rows)
        _, own_n, den_n = lane_masks(ncol)
        qh = query_heads(row0, rows, lane_r)
        kb = k_ref[pl.ds(start, ncol), :]
        vb = v_ref[pl.ds(start, ncol), :]
        accs = [update(qh[h], kb, vb * own_n[h] + den_n[h], ck[h:h + 1, :ncol],
                       carry[h][0][row0:row0 + rows], carry[h][1][row0:row0 + rows], row0)[1] for h in range(2)]
        o0 = accs[0] / accs[0][:, den_lane[0]:den_lane[0] + 1]
        o1 = accs[1] / accs[1][:, den_lane[1]:den_lane[1] + 1]
        o_ref[row0:row0 + rows, :] = jnp.where(lane_r < HEAD_DIM, o0, o1).astype(o_ref.dtype)


def fox_prompt(qkv, ck, rest, n_batch, seq, tq):
    nq = seq // tq
    pairs = FOX_HEADS // 2
    return pl.pallas_call(
        functools.partial(_fox_prompt_kernel, tq=tq),
        grid=(n_batch, pairs, nq),
        in_specs=[pl.BlockSpec((tq, LANES), lambda b, h, i: (b * nq + i, h)),
                  pl.BlockSpec((seq, LANES), lambda b, h, i: (b, pairs + h)),
                  pl.BlockSpec((seq, LANES), lambda b, h, i: (b, 2 * pairs + h)),
                  pl.BlockSpec((1, 1, nq, 2, tq), lambda b, h, i: (b, h, 0, 0, 0)),
                  pl.BlockSpec(memory_space=pl.ANY)],
        out_specs=pl.BlockSpec((tq, LANES), lambda b, h, i: (b * nq + i, h)),
        out_shape=jax.ShapeDtypeStruct(rest.shape, rest.dtype),
        input_output_aliases={4: 0},
        compiler_params=_params(("parallel", "parallel", "arbitrary")),
        name="fox_prompt",
    )(qkv, qkv, qkv, ck, rest)


def _block_diag_rows(full):
    rowh = _iota((FOX_HEADS, HEAD_DIM), 0)
    out = jnp.zeros((FOX_HEADS, HEAD_DIM), F32)
    for h in range(FOX_HEADS):
        out = out + jnp.where(rowh == h, full[:, h * HEAD_DIM:(h + 1) * HEAD_DIM], 0.0)
    return out


def _fox_sample_kernel(pt_ref, qbd_ref, q_ref, kn_ref, vbd_ref, ln_ref, *refs, pages_per_step):
    del pt_ref
    pp = pages_per_step
    k_refs, v_refs, lf_refs = refs[:pp], refs[pp:2 * pp], refs[2 * pp:3 * pp]
    o_ref = refs[3 * pp]
    m_ref, l_ref, acc_ref, carry_ref = refs[3 * pp + 1:]
    t = pl.program_id(1)
    scale = HEAD_DIM ** -0.5

    @pl.when(t == 0)
    def _():
        m_ref[...] = jnp.sum(q_ref[0] * kn_ref[0], axis=1, keepdims=True) * scale
        l_ref[...] = jnp.ones_like(l_ref)
        acc_ref[...] = vbd_ref[0]
        carry_ref[...] = ln_ref[0]

    qb = (qbd_ref[0] * scale).astype(BF16)
    lane = _iota((FOX_HEADS, PAGE_SIZE), 1)
    width = FOX_HEADS * HEAD_DIM
    carry = carry_ref[...]
    scores = []
    for r in range(pp):
        lf = lf_refs[r][0, 0]
        x = lf
        for sh in (1, 2, 4, 8, 16, 32, 64):
            x = x + jnp.where(lane + sh < PAGE_SIZE, pltpu.roll(x, PAGE_SIZE - sh, 1), 0.0)
        kp = k_refs[r][0, 0].reshape(width, PAGE_SIZE).astype(BF16)
        scores.append(_dot(qb, kp) + ((x - lf) + carry))
        carry = carry + x[:, 0:1]
    carry_ref[...] = carry
    s = jnp.concatenate(scores, axis=1)
    m = m_ref[...]
    m_new = jnp.maximum(m, jnp.max(s, axis=1, keepdims=True))
    alpha = jnp.exp(m - m_new)
    p = jnp.exp(s - m_new)
    l_ref[...] = alpha * l_ref[...] + jnp.sum(p, axis=1, keepdims=True)
    m_ref[...] = m_new
    pb = p.astype(BF16)
    acc = alpha * acc_ref[...]
    for r in range(pp):
        vp = v_refs[r][0, 0].reshape(width, PAGE_SIZE).astype(BF16)
        acc = acc + _nt_dot(pb[:, r * PAGE_SIZE:(r + 1) * PAGE_SIZE], vp)
    acc_ref[...] = acc

    @pl.when(t == pl.num_programs(1) - 1)
    def _():
        o_ref[0] = _block_diag_rows(acc / l_ref[...])


def fox_sample(page_table, qbd, q, kn, vbd, ln, cache_kt, cache_vt, cache_lft, layer, pages_per_step):
    bd, n_pages = page_table.shape
    pp = pages_per_step
    steps = n_pages // pp
    width = FOX_HEADS * HEAD_DIM

    def page(b, t, pt, r):
        return pt[b, n_pages - 1 - (t * pp + r)]

    kv_specs = [pl.BlockSpec((1, 1, FOX_HEADS, HEAD_DIM, PAGE_SIZE),
                             functools.partial(lambda b, t, pt, r: (layer, page(b, t, pt, r), 0, 0, 0), r=r))
                for r in range(pp)]
    lf_specs = [pl.BlockSpec((1, 1, FOX_HEADS, PAGE_SIZE),
                             functools.partial(lambda b, t, pt, r: (layer, page(b, t, pt, r), 0, 0), r=r))
                for r in range(pp)]
    tok = pl.BlockSpec((1, FOX_HEADS, HEAD_DIM), lambda b, t, pt: (b, 0, 0))
    wide = pl.BlockSpec((1, FOX_HEADS, width), lambda b, t, pt: (b, 0, 0))
    grid_spec = pltpu.PrefetchScalarGridSpec(
        num_scalar_prefetch=1,
        grid=(bd, steps),
        in_specs=[wide, tok, tok, wide, pl.BlockSpec((1, FOX_HEADS, 1), lambda b, t, pt: (b, 0, 0))]
        + kv_specs + kv_specs + lf_specs,
        out_specs=tok,
        scratch_shapes=[pltpu.VMEM((FOX_HEADS, 1), F32), pltpu.VMEM((FOX_HEADS, 1), F32),
                        pltpu.VMEM((FOX_HEADS, width), F32), pltpu.VMEM((FOX_HEADS, 1), F32)],
    )
    return pl.pallas_call(
        functools.partial(_fox_sample_kernel, pages_per_step=pp),
        grid_spec=grid_spec,
        out_shape=jax.ShapeDtypeStruct((bd, FOX_HEADS, HEAD_DIM), F32),
        compiler_params=_params(("parallel", "arbitrary")),
        name="fox_sample",
    )(page_table, qbd, q, kn, vbd, ln, *([cache_kt] * pp), *([cache_vt] * pp), *([cache_lft] * pp))


def _ssd_epilogue(y, xs, z, dskip_e, normw):
    y = (y + dskip_e * xs) * _silu(z)
    half = M_INNER // M_GROUPS
    outs = []
    for g in range(M_GROUPS):
        yg = y[:, g * half:(g + 1) * half]
        ms = jnp.sum(yg * yg, axis=1, keepdims=True) * (1.0 / half)
        outs.append(yg * lax.rsqrt(ms + RMS_EPS))
    return jnp.concatenate(outs, axis=1) * normw


def _ssd_prompt_kernel(xbc_ref, z_ref, dt_ref, cw_ref, cb_ref, nega_ref, dskip_ref, normw_ref, rest_ref,
                       o_ref, st_ref, ext_ref, h_ref):
    del rest_ref
    c = pl.program_id(1)
    L = SSD_CHUNK
    pad = SUBLANES

    @pl.when(c == 0)
    def _():
        ext_ref[0:pad, :] = jnp.zeros((pad, CONV_CH), F32)
        h_ref[...] = jnp.zeros_like(h_ref)

    ext_ref[pad:pad + L, :] = xbc_ref[...]
    acc = ext_ref[pad:pad + L, :] * cw_ref[CONV_W - 1:CONV_W, :]
    for j in range(CONV_W - 1):
        off = pad - (CONV_W - 1) + j
        acc = acc + ext_ref[off:off + L, :] * cw_ref[j:j + 1, :]
    u = _silu(acc + cb_ref[...])
    ext_ref[0:pad, :] = ext_ref[L:L + pad, :]

    xs = u[:, :M_INNER]
    dt_t = dt_ref[...]
    cum_t = _dot(dt_t * nega_ref[...], (_iota((L, L), 0) <= _iota((L, L), 1)).astype(F32),
                 precision=HIGHEST)
    eye = (_iota((L, L), 0) == _iota((L, L), 1)).astype(F32)
    cols = _nt_dot(eye, jnp.concatenate([cum_t, dt_t], axis=0), precision=HIGHEST)
    cum_last = cum_t[:, L - 1:L]
    tail_t = jnp.exp(cum_last - cum_t) * dt_t
    tril = _iota((L, L), 0) >= _iota((L, L), 1)
    lane = _iota((L, LANES), 1)
    rowi = _iota((L, LANES), 0)

    y_pairs = []
    for g in range(M_GROUPS):
        bm = u[:, M_INNER + g * D_STATE:M_INNER + (g + 1) * D_STATE]
        cm = u[:, M_INNER + BC_W + g * D_STATE:M_INNER + BC_W + (g + 1) * D_STATE]
        bmb = bm.astype(BF16)
        cmb = cm.astype(BF16)
        cb = _nt_dot(cmb, bmb)
        for pr in range(HPG // 2):
            pidx = g * (HPG // 2) + pr
            xs_pair = xs[:, pidx * LANES:(pidx + 1) * LANES]
            xs_pair_b = xs_pair.astype(BF16)
            h0 = h_ref[pidx]
            ych = _nt_dot(cmb, h0.astype(BF16))
            yw = []
            for k in range(2):
                hd = 2 * pidx + k
                diff = cols[:, hd:hd + 1] - cum_t[hd:hd + 1, :]
                decay = jnp.exp(jnp.where(tril, diff, NEG_INF))
                w = cb * decay * dt_t[hd:hd + 1, :]
                yw.append(_dot(w.astype(BF16), xs_pair_b))
            e0 = jnp.exp(cols[:, 2 * pidx:2 * pidx + 1])
            e1 = jnp.exp(cols[:, 2 * pidx + 1:2 * pidx + 2])
            first = lane < M_HEADDIM
            y_pairs.append(jnp.where(first, yw[0], yw[1]) + ych * jnp.where(first, e0, e1))
            top = rowi < M_HEADDIM
            tail_m = jnp.where(top, jnp.broadcast_to(tail_t[2 * pidx:2 * pidx + 1, :], (L, L)),
                               jnp.broadcast_to(tail_t[2 * pidx + 1:2 * pidx + 2, :], (L, L)))
            dec_m = jnp.where(top, jnp.exp(cum_last[2 * pidx:2 * pidx + 1, :]),
                              jnp.exp(cum_last[2 * pidx + 1:2 * pidx + 2, :]))
            xt = xs_pair.T * tail_m
            h_ref[pidx] = h0 * dec_m + _dot(xt.astype(BF16), bmb)

    y = jnp.concatenate(y_pairs, axis=1)
    o_ref[...] = _ssd_epilogue(y, xs, z_ref[...], dskip_ref[...], normw_ref[...]).astype(o_ref.dtype)
    st_ref[0] = h_ref[...]


def ssd_prompt(xz, dt_rows, conv_w, conv_b, nega, dskip_e, normw, rest, n_batch, seq):
    L = SSD_CHUNK
    nc = seq // L
    pairs = M_HEADS // 2
    const = lambda b, c: (0, 0)
    return pl.pallas_call(
        _ssd_prompt_kernel,
        grid=(n_batch, nc),
        in_specs=[pl.BlockSpec((L, CONV_CH), lambda b, c: (b * nc + c, 0)),
                  pl.BlockSpec((L, M_INNER), lambda b, c: (b * nc + c, CONV_CH // M_INNER)),
                  pl.BlockSpec((M_HEADS, L), lambda b, c: (1, b * nc + c)),
                  pl.BlockSpec((CONV_W, CONV_CH), const),
                  pl.BlockSpec((1, CONV_CH), const),
                  pl.BlockSpec((M_HEADS, 1), const),
                  pl.BlockSpec((1, M_INNER), const),
                  pl.BlockSpec((1, M_INNER), const),
                  pl.BlockSpec(memory_space=pl.ANY)],
        out_specs=[pl.BlockSpec((L, M_INNER), lambda b, c: (b * nc + c, 0)),
                   pl.BlockSpec((1, pairs, LANES, D_STATE), lambda b, c: (b, 0, 0, 0))],
        out_shape=[jax.ShapeDtypeStruct(rest.shape, rest.dtype),
                   jax.ShapeDtypeStruct((n_batch, pairs, LANES, D_STATE), F32)],
        input_output_aliases={8: 0},
        scratch_shapes=[pltpu.VMEM((L + SUBLANES, CONV_CH), F32),
                        pltpu.VMEM((pairs, LANES, D_STATE), F32)],
        compiler_params=_params(("parallel", "arbitrary")),
        name="ssd_prompt",
    )(xz, xz, dt_rows, conv_w, conv_b, nega, dskip_e, normw, rest)


def _ssd_sample_kernel(x_ref, wdt_ref, dtb_ref, nega_ref, xbc_ref, ctx_ref, z_ref, cw_ref, cb_ref,
                       dskip_ref, normw_ref, h0_ref, o_ref, hn_ref,
                       u_ref, coef_t_ref, dec_t_ref, dec_ref, dtx_ref, *, seqs_per_step):
    sb = seqs_per_step
    i = pl.program_id(0)
    nseq = x_ref.shape[0]

    @pl.when(i == 0)
    def _():
        acc = xbc_ref[...] * cw_ref[CONV_W - 1:CONV_W, :]
        for j in range(CONV_W - 1):
            acc = acc + ctx_ref[j] * cw_ref[j:j + 1, :]
        u = _silu(acc + cb_ref[...])
        u_ref[...] = u
        dt = _softplus(_dot(x_ref[...], wdt_ref[...], precision=HIGHEST) + dtb_ref[...])
        dec = jnp.exp(dt * nega_ref[...])
        coef = dt * u[:, :M_INNER]
        dec_ref[...] = dec
        dtx_ref[...] = coef
        for blk in range(M_INNER // LANES):
            sl = slice(blk * LANES, (blk + 1) * LANES)
            coef_t_ref[sl, :] = coef[:, sl].T
            dec_t_ref[sl, :] = dec[:, sl].T

    base = pl.multiple_of(i * sb, sb)
    ub = u_ref[pl.ds(base, sb), :]
    lane_seq = _iota((M_INNER, nseq), 1)
    rows = _iota((sb, M_INNER // M_GROUPS), 0)
    ch = [jnp.zeros((sb, M_INNER // M_GROUPS), F32) for _ in range(M_GROUPS)]
    for s in range(sb):
        onehot = lane_seq == base + s
        cx = jnp.sum(jnp.where(onehot, coef_t_ref[...], 0.0), axis=1, keepdims=True)
        dc = jnp.sum(jnp.where(onehot, dec_t_ref[...], 0.0), axis=1, keepdims=True)
        for g in range(M_GROUPS):
            brow = ub[s:s + 1, M_INNER + g * D_STATE:M_INNER + (g + 1) * D_STATE]
            cblk = ub[:, M_INNER + BC_W + g * D_STATE:M_INNER + BC_W + (g + 1) * D_STATE]
            hg = h0_ref[s, g * HPG:(g + 1) * HPG].reshape(HPG * M_HEADDIM, D_STATE)
            r = _nt_dot(cblk.astype(BF16), hg.astype(BF16))
            ch[g] = ch[g] + jnp.where(rows == s, r, 0.0)
            lo = g * HPG * M_HEADDIM
            hn = hg * dc[lo:lo + HPG * M_HEADDIM] + cx[lo:lo + HPG * M_HEADDIM] * brow
            hn_ref[s, g * HPG:(g + 1) * HPG] = hn.reshape(HPG, M_HEADDIM, D_STATE)

    xs = ub[:, :M_INNER]
    dec = dec_ref[pl.ds(base, sb), :]
    coef = dtx_ref[pl.ds(base, sb), :]
    ys = []
    half = M_INNER // M_GROUPS
    for g in range(M_GROUPS):
        bm = ub[:, M_INNER + g * D_STATE:M_INNER + (g + 1) * D_STATE]
        cm = ub[:, M_INNER + BC_W + g * D_STATE:M_INNER + BC_W + (g + 1) * D_STATE]
        cb = jnp.sum(cm * bm, axis=1, keepdims=True)
        ys.append(cb * coef[:, g * half:(g + 1) * half] + ch[g] * dec[:, g * half:(g + 1) * half])
    y = jnp.concatenate(ys, axis=1)
    o_ref[...] = _ssd_epilogue(y, xs, z_ref[...], dskip_ref[...], normw_ref[...])


def ssd_sample(x_s, wdt_e, dtb_e, nega_e, xbc_s, ctx, z_s, conv_w, conv_b, dskip_e, normw, h0, seqs_per_step):
    bd = x_s.shape[0]
    sb = seqs_per_step
    const = lambda i: (0, 0)
    return pl.pallas_call(
        functools.partial(_ssd_sample_kernel, seqs_per_step=sb),
        grid=(bd // sb,),
        in_specs=[pl.BlockSpec((bd, D_MODEL), const),
                  pl.BlockSpec((D_MODEL, M_INNER), const),
                  pl.BlockSpec((1, M_INNER), const),
                  pl.BlockSpec((1, M_INNER), const),
                  pl.BlockSpec((bd, CONV_CH), const),
                  pl.BlockSpec((CONV_W - 1, bd, CONV_CH), lambda i: (0, 0, 0)),
                  pl.BlockSpec((sb, M_INNER), lambda i: (i, 0)),
                  pl.BlockSpec((CONV_W, CONV_CH), const),
                  pl.BlockSpec((1, CONV_CH), const),
                  pl.BlockSpec((1, M_INNER), const),
                  pl.BlockSpec((1, M_INNER), const),
                  pl.BlockSpec((sb, M_HEADS, M_HEADDIM, D_STATE), lambda i: (i, 0, 0, 0))],
        out_specs=[pl.BlockSpec((sb, M_INNER), lambda i: (i, 0)),
                   pl.BlockSpec((sb, M_HEADS, M_HEADDIM, D_STATE), lambda i: (i, 0, 0, 0))],
        out_shape=[jax.ShapeDtypeStruct((bd, M_INNER), F32),
                   jax.ShapeDtypeStruct((bd, M_HEADS, M_HEADDIM, D_STATE), F32)],
        scratch_shapes=[pltpu.VMEM((bd, CONV_CH), F32),
                        pltpu.VMEM((M_INNER, bd), F32),
                        pltpu.VMEM((M_INNER, bd), F32),
                        pltpu.VMEM((bd, M_INNER), F32),
                        pltpu.VMEM((bd, M_INNER), F32)],
        compiler_params=_params(("arbitrary",)),
        name="ssd_sample",
    )(x_s, wdt_e, dtb_e, nega_e, xbc_s, ctx, z_s, conv_w, conv_b, dskip_e, normw, h0)


def _to_half(x, src_half, dst_half):
    return x if src_half == dst_half else pltpu.roll(x, HEAD_DIM, 1)


def _swa_prompt_kernel(sink_ref, q_ref, kp_ref, kc_ref, vp_ref, vc_ref, rest_ref, o_ref):
    del rest_ref
    n = pl.program_id(1)
    W = WINDOW
    scale = HEAD_DIM ** -0.5
    lane = _iota((W, LANES), 1)
    rows4 = _iota((SW_GROUP * W, 2 * W), 0)
    t = rows4 % W
    col = _iota((SW_GROUP * W, 2 * W), 1)
    valid = (col >= t) & (col <= t + W) & ((n > 0) | (col >= W))
    rcol = _iota((SW_GROUP * W, 1), 0) // W
    outs = [None] * SW_HEADS
    for j in range(SW_KV_HEADS):
        ch, hf = j // 2, j % 2
        kk = jnp.concatenate([kp_ref[:, ch * LANES:(ch + 1) * LANES],
                              kc_ref[:, ch * LANES:(ch + 1) * LANES]], axis=0).astype(BF16)
        vv = jnp.concatenate([vp_ref[:, ch * LANES:(ch + 1) * LANES],
                              vc_ref[:, ch * LANES:(ch + 1) * LANES]], axis=0).astype(BF16)
        qs = []
        sink = jnp.zeros((SW_GROUP * W, 1), F32)
        for g in range(SW_GROUP):
            hq = j * SW_GROUP + g
            qc = q_ref[:, (hq // 2) * LANES:(hq // 2 + 1) * LANES] * scale
            qc = _to_half(qc, hq % 2, hf)
            keep = (lane < HEAD_DIM) if hf == 0 else (lane >= HEAD_DIM)
            qs.append(jnp.where(keep, qc, 0.0))
            sink = jnp.where(rcol == g, sink_ref[hq], sink)
        q4 = jnp.concatenate(qs, axis=0).astype(BF16)
        s = jnp.where(valid, _nt_dot(q4, kk), NEG_INF)
        m = jnp.maximum(jnp.max(s, axis=1, keepdims=True), sink)
        p = jnp.exp(s - m)
        den = jnp.sum(p, axis=1, keepdims=True) + jnp.exp(sink - m)
        o = _dot(p.astype(BF16), vv) / den
        for g in range(SW_GROUP):
            hq = j * SW_GROUP + g
            outs[hq] = _to_half(o[g * W:(g + 1) * W], hf, hq % 2)
    for c in range(SW_HEADS // 2):
        o_ref[:, c * LANES:(c + 1) * LANES] = jnp.where(lane < HEAD_DIM, outs[2 * c],
                                                        outs[2 * c + 1]).astype(o_ref.dtype)


def swa_prompt(sinks, u_odd, rest, n_batch, seq):
    nb = seq // WINDOW
    kcol = ODD_MIX // KV_W
    return pl.pallas_call(
        _swa_prompt_kernel,
        grid=(n_batch, nb),
        in_specs=[pl.BlockSpec(memory_space=pltpu.SMEM),
                  pl.BlockSpec((WINDOW, ODD_MIX), lambda b, n: (b * nb + n, 0)),
                  pl.BlockSpec((WINDOW, KV_W), lambda b, n: (b * nb + jnp.maximum(n - 1, 0), kcol)),
                  pl.BlockSpec((WINDOW, KV_W), lambda b, n: (b * nb + n, kcol)),
                  pl.BlockSpec((WINDOW, KV_W), lambda b, n: (b * nb + jnp.maximum(n - 1, 0), kcol + 1)),
                  pl.BlockSpec((WINDOW, KV_W), lambda b, n: (b * nb + n, kcol + 1)),
                  pl.BlockSpec(memory_space=pl.ANY)],
        out_specs=pl.BlockSpec((WINDOW, ODD_MIX), lambda b, n: (b * nb + n, 0)),
        out_shape=jax.ShapeDtypeStruct(rest.shape, rest.dtype),
        input_output_aliases={6: 0},
        compiler_params=_params(("parallel", "arbitrary")),
        name="swa_prompt",
    )(sinks, u_odd, u_odd, u_odd, u_odd, u_odd, rest)


def _swa_sample_kernel(sink_ref, q_ref, kn_ref, vn_ref, bk_ref, bv_ref, o_ref, *, seqs_per_step):
    scale = HEAD_DIM ** -0.5
    rowg = _iota((SW_HEADS, HEAD_DIM), 0) // SW_GROUP
    sink = sink_ref[...]
    for s in range(seqs_per_step):
        q16 = q_ref[s] * scale
        qbd = jnp.concatenate([jnp.where(rowg == j, q16, 0.0) for j in range(SW_KV_HEADS)], axis=1)
        kb = bk_ref[s].reshape(KV_W, -1).astype(BF16)
        vb = bv_ref[s].reshape(KV_W, -1).astype(BF16)
        sc = _dot(qbd.astype(BF16), kb)
        s_new = jnp.sum(qbd * kn_ref[s:s + 1, :], axis=1, keepdims=True)
        m = jnp.maximum(jnp.maximum(jnp.max(sc, axis=1, keepdims=True), s_new), sink)
        p = jnp.exp(sc - m)
        pn = jnp.exp(s_new - m)
        den = jnp.sum(p, axis=1, keepdims=True) + pn + jnp.exp(sink - m)
        full = (_nt_dot(p.astype(BF16), vb) + pn * vn_ref[s:s + 1, :]) / den
        o16 = jnp.zeros((SW_HEADS, HEAD_DIM), F32)
        for j in range(SW_KV_HEADS):
            o16 = o16 + jnp.where(rowg == j, full[:, j * HEAD_DIM:(j + 1) * HEAD_DIM], 0.0)
        o_ref[s] = o16


def swa_sample(sinks_col, q, kn, vn, buf_k, buf_v, seqs_per_step):
    bd, _, _, lw = buf_k.shape
    sb = seqs_per_step
    return pl.pallas_call(
        functools.partial(_swa_sample_kernel, seqs_per_step=sb),
        grid=(bd // sb,),
        in_specs=[pl.BlockSpec((SW_HEADS, 1), lambda i: (0, 0)),
                  pl.BlockSpec((sb, SW_HEADS, HEAD_DIM), lambda i: (i, 0, 0)),
                  pl.BlockSpec((sb, KV_W), lambda i: (i, 0)),
                  pl.BlockSpec((sb, KV_W), lambda i: (i, 0)),
                  pl.BlockSpec((sb, SW_KV_HEADS, HEAD_DIM, lw), lambda i: (i, 0, 0, 0)),
                  pl.BlockSpec((sb, SW_KV_HEADS, HEAD_DIM, lw), lambda i: (i, 0, 0, 0))],
        out_specs=pl.BlockSpec((sb, SW_HEADS, HEAD_DIM), lambda i: (i, 0, 0)),
        out_shape=jax.ShapeDtypeStruct((bd, SW_HEADS, HEAD_DIM), F32),
        compiler_params=_params(("parallel",)),
        name="swa_sample",
    )(sinks_col, q, kn, vn, buf_k, buf_v)


def _layer_norm(h, g, b):
    mu = jnp.mean(h, axis=1, keepdims=True)
    d = h - mu
    var = jnp.mean(d * d, axis=1, keepdims=True)
    return d * lax.rsqrt(var + LN_EPS) * g + b


def _mix_route_kernel(*refs, alpha, n_feat, prompt_tiles):
    am_refs, wo_refs = refs[:n_feat], refs[n_feat:2 * n_feat]
    xp_ref, xt_ref, g_ref, b_ref, wr_ref, br_ref, x1_ref, rw_ref, re_ref = refs[2 * n_feat:]
    mix = _dot(am_refs[0][...].astype(BF16), wo_refs[0][...])
    for a_ref, w_ref in zip(am_refs[1:], wo_refs[1:]):
        mix = mix + _dot(a_ref[...].astype(BF16), w_ref[...])
    x = jnp.where(pl.program_id(0) < prompt_tiles, xp_ref[...], xt_ref[...])
    x1 = _layer_norm(alpha * x + mix, g_ref[...], b_ref[...])
    x1_ref[...] = x1
    wr = wr_ref[...]
    x_hi = x1.astype(BF16)
    w_hi = wr.astype(BF16)
    x_lo = (x1 - x_hi.astype(F32)).astype(BF16)
    w_lo = (wr - w_hi.astype(F32)).astype(BF16)
    logits = _dot(x_hi, w_hi) + _dot(x_lo, w_hi) + _dot(x_hi, w_lo) + br_ref[...]
    lane_i = _iota(logits.shape, 1)
    lane = lane_i.astype(F32)
    big = float(LANES)
    gl = jnp.where(lane_i < N_EXPERT_GROUPS, logits, NEG_INF)
    gmax = jnp.max(gl, axis=1, keepdims=True)
    grp = jnp.min(jnp.where(gl == gmax, lane, big), axis=1, keepdims=True)
    pg = 1.0 / jnp.sum(jnp.exp(gl - gmax), axis=1, keepdims=True)
    rel = lane - (N_EXPERT_GROUPS + grp * EXPERTS_PER_GROUP)
    el = jnp.where(rel >= 0.0, jnp.where(rel < EXPERTS_PER_GROUP, logits, NEG_INF), NEG_INF)
    v1 = jnp.max(el, axis=1, keepdims=True)
    i1 = jnp.min(jnp.where(el == v1, lane, big), axis=1, keepdims=True)
    el2 = jnp.where(lane == i1, NEG_INF, el)
    v2 = jnp.max(el2, axis=1, keepdims=True)
    i2 = jnp.min(jnp.where(el2 == v2, lane, big), axis=1, keepdims=True)
    e = jnp.exp(v2 - v1)
    w1 = pg / (1.0 + e)
    w2 = w1 * e
    rw_ref[...] = jnp.where(lane_i == 0, w1, jnp.where(lane_i == 1, w2, 0.0))
    e1 = (i1 - N_EXPERT_GROUPS).astype(jnp.int32)
    e2 = (i2 - N_EXPERT_GROUPS).astype(jnp.int32)
    re_ref[...] = jnp.where(lane_i == 0, e1, jnp.where(lane_i == 1, e2, 0))


def mix_route(feats, wos, x_prompt, x_tail, g, b, wr, br, alpha, tm):
    d = x_tail.shape[1]
    prompt_tiles = (feats[0].shape[0] - x_tail.shape[0]) // tm
    m = feats[0].shape[0]
    const = lambda i: (0, 0)
    row = lambda i: (i, 0)
    return pl.pallas_call(
        functools.partial(_mix_route_kernel, alpha=alpha, n_feat=len(feats), prompt_tiles=prompt_tiles),
        grid=(m // tm,),
        in_specs=[pl.BlockSpec((tm, a.shape[1]), row) for a in feats]
        + [pl.BlockSpec((w.shape[0], d), const) for w in wos]
        + [pl.BlockSpec((tm, d), lambda i: (jnp.minimum(i, prompt_tiles - 1), 0)),
           pl.BlockSpec((tm, d), lambda i: (jnp.maximum(i - prompt_tiles, 0), 0)),
           pl.BlockSpec((1, d), const), pl.BlockSpec((1, d), const),
           pl.BlockSpec((d, LANES), const), pl.BlockSpec((1, LANES), const)],
        out_specs=[pl.BlockSpec((tm, d), row),
                   pl.BlockSpec((tm, LANES), row), pl.BlockSpec((tm, LANES), row)],
        out_shape=[jax.ShapeDtypeStruct((m, d), F32),
                   jax.ShapeDtypeStruct((m, LANES), F32), jax.ShapeDtypeStruct((m, LANES), jnp.int32)],
        compiler_params=_params(("parallel",)),
        name="mix_route",
    )(*feats, *wos, x_prompt, x_tail, g, b, wr, br)


def _experts_kernel(te_ref, tv_ref, x_ref, wg_ref, wu_ref, wd_ref, *rest, first_tile):
    y_ref = rest[-1]
    t = first_tile + pl.program_id(0)

    @pl.when(tv_ref[t] > 0)
    def _():
        x = x_ref[...].astype(BF16)
        h = _silu(_dot(x, wg_ref[0, 0].astype(BF16))) * _dot(x, wu_ref[0, 0].astype(BF16))
        y_ref[...] = _dot(h.astype(BF16), wd_ref[0, 0].astype(BF16))

    @pl.when(tv_ref[t] == 0)
    def _():
        y_ref[...] = jnp.zeros_like(y_ref)


def experts(tile_expert, tile_valid, xs, wg, wu, wd, layer, te, first_tile, n_rows, y_prev):
    r, d = xs.shape
    ff = wg.shape[3]
    ex = lambda t, e, v: (layer, e[first_tile + t], 0, 0)
    in_specs = [pl.BlockSpec((te, d), lambda t, e, v: (t, 0)),
                pl.BlockSpec((1, 1, d, ff), ex), pl.BlockSpec((1, 1, d, ff), ex), pl.BlockSpec((1, 1, ff, d), ex)]
    args = [tile_expert, tile_valid, xs, wg, wu, wd]
    aliases = {}
    if y_prev is not None:
        in_specs.append(pl.BlockSpec(memory_space=pl.ANY))
        aliases = {len(args): 0}
        args.append(y_prev)
    grid_spec = pltpu.PrefetchScalarGridSpec(
        num_scalar_prefetch=2,
        grid=(r // te,),
        in_specs=in_specs,
        out_specs=pl.BlockSpec((te, d), lambda t, e, v: (first_tile + t, 0)),
    )
    return pl.pallas_call(
        functools.partial(_experts_kernel, first_tile=first_tile),
        grid_spec=grid_spec,
        out_shape=jax.ShapeDtypeStruct((n_rows, d), F32),
        input_output_aliases=aliases,
        compiler_params=_params(("arbitrary",)),
        name="experts",
    )(*args)


def _combine_ple_kernel(x1_ref, y0_ref, y1_ref, rw_ref, pp_ref, pt_ref, g_ref, b_ref, wg_ref, bg_ref, wp_ref,
                        o_ref, *, alpha, prompt_tiles, first_tile):
    rw = rw_ref[...]
    f = rw[:, 0:1] * y0_ref[...] + rw[:, 1:2] * y1_ref[...]
    x2 = _layer_norm(alpha * x1_ref[...] + f, g_ref[...], b_ref[...])
    gl = _dot(x2.astype(BF16), wg_ref[...]) + bg_ref[...]
    gate = 1.0 / (1.0 + jnp.exp(-gl))
    p = jnp.where(first_tile + pl.program_id(0) < prompt_tiles, pp_ref[0], pt_ref[...])
    o_ref[...] = x2 + gate * _dot(p.astype(BF16), wp_ref[...])


def combine_ple(x1, y0, y1, rw, p_prompt, p_tail, layer, g, b, wg, bg, wp, alpha, tm, first_tile, n_tiles):
    d = x1.shape[1]
    pd = p_prompt.shape[2]
    prompt_tiles = p_prompt.shape[1] // tm
    const = lambda i: (0, 0)
    row = lambda i: (first_tile + i, 0)
    return pl.pallas_call(
        functools.partial(_combine_ple_kernel, alpha=alpha, prompt_tiles=prompt_tiles, first_tile=first_tile),
        grid=(n_tiles,),
        in_specs=[pl.BlockSpec((tm, d), row), pl.BlockSpec((tm, d), row), pl.BlockSpec((tm, d), row),
                  pl.BlockSpec((tm, LANES), row),
                  pl.BlockSpec((1, tm, pd), lambda i: (layer, jnp.minimum(first_tile + i, prompt_tiles - 1), 0)),
                  pl.BlockSpec((tm, pd), lambda i: (jnp.maximum(first_tile + i - prompt_tiles, 0), 0)),
                  pl.BlockSpec((1, d), const), pl.BlockSpec((1, d), const),
                  pl.BlockSpec((d, d), const), pl.BlockSpec((1, d), const), pl.BlockSpec((pd, d), const)],
        out_specs=pl.BlockSpec((tm, d), lambda i: (i, 0)),
        out_shape=jax.ShapeDtypeStruct((n_tiles * tm, d), F32),
        compiler_params=_params(("parallel",)),
        name="combine_ple",
    )(x1, y0, y1, rw, p_prompt, p_tail, g, b, wg, bg, wp)


def _tiles(n_tokens):
    tm = 512 if n_tokens >= 4096 else 128
    return tm, ((n_tokens + tm - 1) // tm) * tm


def _expert_tile(n_tokens):
    return 256 if n_tokens >= 4096 else 32


def _channel_and_ple(x_main, x_tail, feats, p_prompt, p_tail, li, w_out, w, tm):
    depth = w["ln_mix_g"].shape[0]
    alpha = (2 * depth) ** 0.25
    ntp, d = feats[0].shape[0], x_tail.shape[1]
    wr = jnp.concatenate([w["w_router_group"][li],
                          jnp.moveaxis(w["w_router_expert"][li], 0, 1).reshape(d, N_EXPERTS)], axis=1)
    wr = jnp.pad(wr, ((0, 0), (0, LANES - wr.shape[1])))
    br = jnp.concatenate([w["b_router_group"][li], w["b_router_expert"][li].reshape(-1)])
    br = jnp.pad(br, (0, LANES - br.shape[0]))[None, :]
    wo = w_out.astype(BF16)
    splits = [0]
    for a in feats:
        splits.append(splits[-1] + a.shape[1])
    x1, rw, re = mix_route(feats, [wo[lo:hi] for lo, hi in zip(splits[:-1], splits[1:])], x_main, x_tail,
                           w["ln_mix_g"][li][None], w["ln_mix_b"][li][None], wr, br, alpha, tm)

    te = _expert_tile(ntp)
    n_flat = 2 * ntp
    flat = re[:, :2].reshape(-1)
    onehot = (flat[:, None] == jnp.arange(N_EXPERTS, dtype=jnp.int32)[None, :]).astype(jnp.int32)
    running = jnp.cumsum(onehot, axis=0)
    counts = running[-1]
    padded = ((counts + te - 1) // te) * te
    gend = jnp.cumsum(padded)
    gstart = gend - padded
    pos_flat = jnp.sum(onehot * (running - 1 + gstart[None, :]), axis=1)
    n_rows = ((n_flat + N_EXPERTS * (te - 1) + te - 1) // te) * te
    row_token = (jnp.arange(n_rows, dtype=jnp.int32) % ntp).at[pos_flat].set(
        jnp.arange(n_flat, dtype=jnp.int32) // 2, mode="promise_in_bounds", unique_indices=True)
    pos_flat = pos_flat.reshape(ntp, 2)
    tile_start = jnp.arange(n_rows // te, dtype=jnp.int32) * te
    tile_expert = jnp.minimum(jnp.sum((gend[None, :] <= tile_start[:, None]).astype(jnp.int32), axis=1),
                              N_EXPERTS - 1)
    tile_valid = (tile_start < gend[-1]).astype(jnp.int32)

    def rows(a, idx):
        return a.at[idx].get(mode="promise_in_bounds")

    n_tiles = n_rows // te
    bounds = [0] + [max(1, int(f * n_tiles)) for f in EXPERT_CHUNK_ENDS]
    y = None
    for t0, t1 in zip(bounds[:-1], bounds[1:]):
        xs = rows(x1, row_token[t0 * te:t1 * te])
        y = experts(tile_expert, tile_valid, xs, w["w_exp_gate"], w["w_exp_up"], w["w_exp_down"], li, te,
                    t0, n_rows, y)
    y0 = rows(y, pos_flat[:, 0])
    y1 = rows(y, pos_flat[:, 1])
    prompt_tiles = p_prompt.shape[1] // tm
    parts = [(0, ntp // tm)] if li + 1 < depth else [(0, prompt_tiles), (prompt_tiles, ntp // tm - prompt_tiles)]
    outs = [combine_ple(x1, y0, y1, rw, p_prompt, p_tail, li, w["ln_ffn_g"][li][None], w["ln_ffn_b"][li][None],
                        w["w_ple_gate"][li].astype(BF16), w["b_ple_gate"][li][None],
                        w["w_ple_proj"][li].astype(BF16), alpha, tm, first, n) for first, n in parts]
    return outs[0] if len(outs) == 1 else tuple(outs)


def kernel(x_prompt, x_sample, p_prompt, p_sample, cache_fox_k, cache_fox_v, cache_fox_logf, state_ssm, state_conv, cache_win_k, cache_win_v, page_table, w_in_even, b_fgate, conv_w, conv_b, dt_bias, a_log, d_skip, ssm_norm_w, w_out_even, w_in_odd, attn_sinks, w_out_odd, ln_mix_g, ln_mix_b, ln_ffn_g, ln_ffn_b, w_router_group, b_router_group, w_router_expert, b_router_expert, w_exp_gate, w_exp_up, w_exp_down, w_ple_proj, w_ple_gate, b_ple_gate):
    bp, seq, d = x_prompt.shape
    bd, t_dec, _ = x_sample.shape
    assert t_dec == 1 and d == D_MODEL
    depth = p_prompt.shape[0]
    n_pages = page_table.shape[1]
    past_len = n_pages * PAGE_SIZE
    np_tok = bp * seq
    nt = np_tok + bd
    tm, ntp = _tiles(nt)
    pad = ntp - nt

    def past_prompt(a_s):
        a_s = a_s.reshape(bd, -1).astype(BF16)
        return jnp.zeros((ntp, a_s.shape[1]), BF16).at[np_tok:nt].set(a_s)

    def after_prompt(a_s):
        a_s = a_s.reshape(bd, -1)
        return jnp.zeros((ntp - np_tok, a_s.shape[1]), a_s.dtype).at[:bd].set(a_s)

    assert np_tok % tm == 0 and seq >= CONV_W - 1
    x_full = None
    x_main = x_prompt.reshape(np_tok, d)
    x_tail = after_prompt(x_sample)
    p_all = p_prompt.reshape(depth, np_tok, -1)
    shared = dict(ln_mix_g=ln_mix_g, ln_mix_b=ln_mix_b, ln_ffn_g=ln_ffn_g, ln_ffn_b=ln_ffn_b,
                  w_router_group=w_router_group, b_router_group=b_router_group,
                  w_router_expert=w_router_expert, b_router_expert=b_router_expert,
                  w_exp_gate=w_exp_gate, w_exp_up=w_exp_up, w_exp_down=w_exp_down,
                  w_ple_proj=w_ple_proj, w_ple_gate=w_ple_gate, b_ple_gate=b_ple_gate)

    half = HEAD_DIM // 2
    inv = jnp.exp(-math.log(ROPE_THETA) * jnp.arange(half, dtype=F32) / half)
    pos = jnp.concatenate([jnp.tile(jnp.arange(seq, dtype=jnp.int32), bp),
                           jnp.full((bd,), past_len, jnp.int32), jnp.zeros((pad,), jnp.int32)])
    ang = pos.astype(F32)[:, None] * inv[None, :]
    cos_t = jnp.tile(jnp.cos(ang), (1, LANES // half))
    sin_t = jnp.tile(jnp.concatenate([-jnp.sin(ang), jnp.sin(ang)], axis=1), (1, LANES // HEAD_DIM))

    even_p, even_s, odd_p, odd_s = [], [], [], []
    for li in range(depth):
        j = li // 2
        if li % 2 == 0:
            wi = w_in_even[j]
            c0 = 3 * FOX_WIDTH
            c1 = c0 + FOX_HEADS
            c2 = c1 + M_INNER
            c3 = c2 + CONV_CH
            w_main = jnp.concatenate([wi[:, :c0], wi[:, c2:c3], wi[:, c1:c2]], axis=1).astype(BF16)
            w_small_t = jnp.concatenate([wi[:, c0:c1], wi[:, c3:]], axis=1).T
            b_small = jnp.concatenate([b_fgate[j], dt_bias[j]])[:, None]
            prompt_tiles = np_tok // tm
            k_p, v_p, xz, qkv, _, small = even_proj(x_main, w_main, w_small_t, b_small, tm, 0, prompt_tiles)
            k_t, v_t, xz_t, _, q_t, small_t = even_proj(x_tail, w_main, w_small_t, b_small, tm, 0,
                                                        ntp // tm - prompt_tiles)

            tq = min(FOX_TILE, seq)
            logf_p = small[:FOX_HEADS].reshape(FOX_HEADS, bp, seq)
            cum = cumsum_lanes(jnp.moveaxis(logf_p, 1, 0).reshape(bp * FOX_HEADS, seq), min(CUMSUM_CHUNK, seq))
            ck = cum.reshape(bp, FOX_HEADS // 2, 2, seq // tq, tq).transpose(0, 1, 3, 2, 4)
            logf_s = small_t[:FOX_HEADS, :bd].T
            q_s = q_t[:bd].reshape(bd, FOX_HEADS, HEAD_DIM)
            k_s = k_t[:bd].reshape(bd, FOX_HEADS, HEAD_DIM)
            v_s = v_t[:bd].reshape(bd, FOX_HEADS, HEAD_DIM)
            eye = jnp.eye(FOX_HEADS, dtype=F32)[None, :, :, None]

            def block_diag(a):
                return (a[:, :, None, :] * eye).reshape(bd, FOX_HEADS, FOX_WIDTH)

            a_s = fox_sample(page_table, block_diag(q_s), q_s, k_s, block_diag(v_s), logf_s[:, :, None],
                             jnp.transpose(cache_fox_k, (0, 1, 3, 4, 2)), jnp.transpose(cache_fox_v, (0, 1, 3, 4, 2)),
                             jnp.transpose(cache_fox_logf, (0, 1, 3, 2)), j, min(FOX_PAGES_PER_STEP, n_pages))
            a_all = fox_prompt(qkv, ck, past_prompt(a_s), bp, seq, tq)

            nega = -jnp.exp(a_log[j])
            dskip_e = jnp.repeat(d_skip[j], M_HEADDIM)[None, :]
            normw = ssm_norm_w[j][None, :]
            xz_s = xz_t[:bd]
            m_s, st_s = ssd_sample(x_tail[:bd], jnp.repeat(wi[:, c3:], M_HEADDIM, axis=1),
                                   jnp.repeat(dt_bias[j], M_HEADDIM)[None, :], jnp.repeat(nega, M_HEADDIM)[None, :],
                                   xz_s[:, :CONV_CH], jnp.moveaxis(state_conv[j], 1, 0), xz_s[:, CONV_CH:],
                                   conv_w[j], conv_b[j][None, :], dskip_e, normw, state_ssm[j],
                                   min(DECODE_SEQS_PER_STEP, bd))
            m_all, st_p = ssd_prompt(xz, small, conv_w[j], conv_b[j][None, :], nega[:, None], dskip_e, normw,
                                     past_prompt(m_s), bp, seq)
            feats = [a_all, m_all]
            w_out = w_out_even[j]

            kp = k_p.reshape(bp, seq, FOX_HEADS, HEAD_DIM)
            vp = v_p.reshape(bp, seq, FOX_HEADS, HEAD_DIM)
            last = jnp.stack([xz[(b + 1) * seq - (CONV_W - 1):(b + 1) * seq, :CONV_CH] for b in range(bp)])
            conv_p = jnp.concatenate([jnp.zeros((bp, CONV_W - 1, CONV_CH), F32), last], axis=1)[:, -(CONV_W - 1):]
            even_p.append((kp, vp, jnp.moveaxis(logf_p, 0, 2),
                           st_p.reshape(bp, M_HEADS, M_HEADDIM, D_STATE), conv_p))
            conv_s = jnp.concatenate([state_conv[j], xz_s[:, None, :CONV_CH]], axis=1)[:, -(CONV_W - 1):]
            even_s.append((k_s[:, None], v_s[:, None], logf_s[:, None, :], st_s, conv_s))
        else:
            assert x_full is not None
            u = matmul_rope(x_full, w_in_odd[j].astype(BF16), cos_t, sin_t, tm, ROPE_COL_TILE, ODD_MIX + KV_W)
            u_s = u[np_tok:nt]
            lw = cache_win_k.shape[2]
            kn = u_s[:, ODD_MIX:ODD_MIX + KV_W]
            vn = u_s[:, ODD_MIX + KV_W:]
            o_s = swa_sample(attn_sinks[j][:, None], u_s[:, :ODD_MIX].reshape(bd, SW_HEADS, HEAD_DIM), kn, vn,
                             jnp.transpose(cache_win_k[j], (0, 2, 3, 1)), jnp.transpose(cache_win_v[j], (0, 2, 3, 1)),
                             min(DECODE_SEQS_PER_STEP, bd))
            feats = [swa_prompt(attn_sinks[j], u, past_prompt(o_s), bp, seq)]
            w_out = w_out_odd[j]

            rows = min(WINDOW, seq)
            tail = jnp.stack([u[(b + 1) * seq - rows:(b + 1) * seq, ODD_MIX:] for b in range(bp)])
            odd_p.append((tail[:, :, :KV_W].reshape(bp, rows, SW_KV_HEADS, HEAD_DIM),
                          tail[:, :, KV_W:].reshape(bp, rows, SW_KV_HEADS, HEAD_DIM)))
            ka = jnp.concatenate([cache_win_k[j], kn.reshape(bd, 1, SW_KV_HEADS, HEAD_DIM)], axis=1)[:, -lw:]
            va = jnp.concatenate([cache_win_v[j], vn.reshape(bd, 1, SW_KV_HEADS, HEAD_DIM)], axis=1)[:, -lw:]
            odd_s.append((ka, va))
        x_full = _channel_and_ple(x_main, x_tail, feats, p_all, after_prompt(p_sample[li]), li, w_out, shared, tm)
        if li + 1 < depth:
            x_main, x_tail = x_full, x_full[np_tok:]

    x_prompt_out, x_tail_out = x_full
    yp = x_prompt_out.reshape(bp, seq, d)
    ys = x_tail_out[:bd].reshape(bd, 1, d)
    return (yp, ys,
            jnp.stack([st[0] for st in even_p]), jnp.stack([st[1] for st in even_p]),
            jnp.stack([st[2] for st in even_p]), jnp.stack([st[3] for st in even_p]),
            jnp.stack([st[4] for st in even_p]),
            jnp.stack([st[0] for st in odd_p]), jnp.stack([st[1] for st in odd_p]),
            jnp.stack([st[0] for st in even_s]), jnp.stack([st[1] for st in even_s]),
            jnp.stack([st[2] for st in even_s]), jnp.stack([st[3] for st in even_s]),
            jnp.stack([st[4] for st in even_s]),
            jnp.stack([st[0] for st in odd_s]), jnp.stack([st[1] for st in odd_s]))
```

```python
import functools
import math

import jax
import jax.numpy as jnp
from jax import lax
from jax.experimental import pallas as pl
from jax.experimental.pallas import tpu as pltpu

F32 = jnp.float32
BF16 = jnp.bfloat16
HIGHEST = lax.Precision.HIGHEST

D_MODEL = 1024
HEAD_DIM = 64
FOX_HEADS = 8
FOX_WIDTH = FOX_HEADS * HEAD_DIM
M_HEADS = 8
M_HEADDIM = 64
M_INNER = M_HEADS * M_HEADDIM
M_GROUPS = 2
HPG = M_HEADS // M_GROUPS
D_STATE = 128
CONV_W = 4
BC_W = M_GROUPS * D_STATE
CONV_CH = M_INNER + 2 * BC_W
SSD_CHUNK = 128
RMS_EPS = 1e-5
SW_HEADS = 16
SW_KV_HEADS = 4
SW_GROUP = SW_HEADS // SW_KV_HEADS
WINDOW = 128
ROPE_THETA = 10000.0
ODD_MIX = SW_HEADS * HEAD_DIM
KV_W = SW_KV_HEADS * HEAD_DIM
N_EXPERT_GROUPS = 4
EXPERTS_PER_GROUP = 8
N_EXPERTS = N_EXPERT_GROUPS * EXPERTS_PER_GROUP
LN_EPS = 1e-5
PAGE_SIZE = 128

LANES = 128
SUBLANES = 8
VMEM_LIMIT = 48 * 1024 * 1024

NEG_INF = float("-inf")

EXPERT_CHUNK_ENDS = (0.25, 1.0)
FOX_PAGES_PER_STEP = 32
FOX_TILE = 1024
FOX_DIAG_PARTS = 2
ROPE_COL_TILE = 256
CUMSUM_CHUNK = 512
DECODE_SEQS_PER_STEP = 8


def _params(sem, vmem=VMEM_LIMIT):
    return pltpu.CompilerParams(dimension_semantics=sem, vmem_limit_bytes=vmem)


def _nt_dot(a, b, precision=None):
    return lax.dot_general(a, b, (((1,), (1,)), ((), ())), precision=precision,
                           preferred_element_type=F32)


def _dot(a, b, precision=None):
    return jnp.dot(a, b, precision=precision, preferred_element_type=F32)


def _silu(x):
    return x * (1.0 / (1.0 + jnp.exp(-x)))


def _softplus(x):
    return jnp.maximum(x, 0.0) + jnp.log(1.0 + jnp.exp(-jnp.abs(x)))


def _iota(shape, dim):
    return lax.broadcasted_iota(jnp.int32, shape, dim)


def _mm_rope_kernel(x_ref, w_ref, cos_ref, sin_ref, o_ref, *, tn, rope_cols):
    xb = x_ref[...].astype(BF16)
    reps = tn // LANES
    cos = jnp.concatenate([cos_ref[...]] * reps, axis=1)
    sin = jnp.concatenate([sin_ref[...]] * reps, axis=1)
    half = HEAD_DIM // 2
    first = (_iota((xb.shape[0], tn), 1) % HEAD_DIM) < half
    for c in range(w_ref.shape[1] // tn):
        acc = _dot(xb, w_ref[:, c * tn:(c + 1) * tn])
        if c * tn < rope_cols:
            partner = jnp.where(first, pltpu.roll(acc, tn - half, 1), pltpu.roll(acc, half, 1))
            acc = acc * cos + partner * sin
        o_ref[:, c * tn:(c + 1) * tn] = acc


def _even_proj_kernel(x_ref, w_ref, wt_ref, bt_ref, k_ref, v_ref, xz_ref, qkv_ref, qt_ref, small_ref):
    x = x_ref[...]
    xb = x.astype(BF16)
    wt = wt_ref[...]
    wt_hi = wt.astype(BF16)
    wt_lo = (wt - wt_hi.astype(F32)).astype(BF16)
    x_lo = (x - xb.astype(F32)).astype(BF16)
    r = _nt_dot(wt_hi, xb) + _nt_dot(wt_lo, xb) + _nt_dot(wt_hi, x_lo) + bt_ref[...]
    small_ref[...] = jnp.where(_iota(r.shape, 0) < FOX_HEADS, -_softplus(-r), _softplus(r))
    w = FOX_WIDTH
    f32_dst = (None, k_ref, v_ref)
    for c in range(3):
        acc = _dot(xb, w_ref[:, c * w:(c + 1) * w])
        qkv_ref[:, c * w:(c + 1) * w] = acc.astype(BF16)
        if f32_dst[c] is not None:
            f32_dst[c][...] = acc
        else:
            @pl.when(pl.program_id(0) == pl.num_programs(0) - 1)
            def _():
                qt_ref[...] = acc
    for c in range(xz_ref.shape[1] // w):
        xz_ref[:, c * w:(c + 1) * w] = _dot(xb, w_ref[:, (3 + c) * w:(4 + c) * w])


def even_proj(x, w, wt, bt, tm, first_tile, n_tiles):
    k = x.shape[1]
    m = n_tiles * tm
    n = w.shape[1]
    wq = 3 * FOX_WIDTH
    row = lambda i: (i, 0)
    const = lambda i: (0, 0)
    return pl.pallas_call(
        _even_proj_kernel,
        grid=(n_tiles,),
        in_specs=[pl.BlockSpec((tm, k), lambda i: (first_tile + i, 0)), pl.BlockSpec((k, n), const),
                  pl.BlockSpec((16, k), const), pl.BlockSpec((16, 1), const)],
        out_specs=[pl.BlockSpec((tm, FOX_WIDTH), row), pl.BlockSpec((tm, FOX_WIDTH), row),
                   pl.BlockSpec((tm, n - wq), row), pl.BlockSpec((tm, wq), row),
                   pl.BlockSpec((tm, FOX_WIDTH), const), pl.BlockSpec((16, tm), lambda i: (0, i))],
        out_shape=[jax.ShapeDtypeStruct((m, FOX_WIDTH), F32), jax.ShapeDtypeStruct((m, FOX_WIDTH), F32),
                   jax.ShapeDtypeStruct((m, n - wq), F32), jax.ShapeDtypeStruct((m, wq), BF16),
                   jax.ShapeDtypeStruct((tm, FOX_WIDTH), F32), jax.ShapeDtypeStruct((16, m), F32)],
        compiler_params=_params(("arbitrary",)),
        name="even_proj",
    )(x, w, wt, bt)


def matmul_rope(x, w, cos, sin, tm, tn, rope_cols):
    m, k = x.shape
    n = w.shape[1]
    assert n % tn == 0 and rope_cols % tn == 0
    return pl.pallas_call(
        functools.partial(_mm_rope_kernel, tn=tn, rope_cols=rope_cols),
        grid=(m // tm,),
        in_specs=[pl.BlockSpec((tm, k), lambda i: (i, 0)),
                  pl.BlockSpec((k, n), lambda i: (0, 0)),
                  pl.BlockSpec((tm, LANES), lambda i: (i, 0)),
                  pl.BlockSpec((tm, LANES), lambda i: (i, 0))],
        out_specs=pl.BlockSpec((tm, n), lambda i: (i, 0)),
        out_shape=jax.ShapeDtypeStruct((m, n), F32),
        compiler_params=_params(("parallel",)),
        name="matmul_rope",
    )(x, w, cos, sin)


def _cumsum_kernel(x_ref, o_ref, carry_ref):
    @pl.when(pl.program_id(0) == 0)
    def _():
        carry_ref[...] = jnp.zeros_like(carry_ref)

    x = x_ref[...]
    w = x.shape[1]
    tri = (_iota((w, w), 0) <= _iota((w, w), 1)).astype(F32)
    c = _dot(x, tri, precision=HIGHEST) + carry_ref[...]
    o_ref[...] = c
    carry_ref[...] = c[:, w - 1:w]


def cumsum_lanes(x, chunk):
    r, l = x.shape
    return pl.pallas_call(
        _cumsum_kernel,
        grid=(l // chunk,),
        in_specs=[pl.BlockSpec((r, chunk), lambda i: (0, i))],
        out_specs=pl.BlockSpec((r, chunk), lambda i: (0, i)),
        out_shape=jax.ShapeDtypeStruct((r, l), F32),
        scratch_shapes=[pltpu.VMEM((r, 1), F32)],
        compiler_params=_params(("arbitrary",)),
        name="cumsum_lanes",
    )(x)


def _fox_prompt_kernel(q_ref, k_ref, v_ref, ck_ref, rest_ref, o_ref, *, tq):
    del rest_ref
    qi = pl.program_id(2)
    den_lane = (HEAD_DIM, 0)

    def lane_masks(rows):
        lane = _iota((rows, LANES), 1)
        own = (jnp.where(lane < HEAD_DIM, 1.0, 0.0).astype(BF16), jnp.where(lane >= HEAD_DIM, 1.0, 0.0).astype(BF16))
        den = tuple(jnp.where(lane == d, 1.0, 0.0).astype(BF16) for d in den_lane)
        return lane, own, den

    def query_heads(row0, rows, lane):
        q = q_ref[row0:row0 + rows, :] * (HEAD_DIM ** -0.5)
        zero = jnp.zeros_like(q)
        return jnp.where(lane < HEAD_DIM, q, zero), jnp.where(lane >= HEAD_DIM, q, zero)

    def update(qh, kb, vh, ckh, m, acc, row_offset):
        s = _nt_dot(qh, kb) - ckh
        if row_offset is not None:
            s = jnp.where(_iota(s.shape, 1) <= _iota(s.shape, 0) + row_offset, s, NEG_INF)
        m_new = jnp.maximum(m, jnp.max(s, axis=1, keepdims=True))
        p = jnp.exp(s - m_new).astype(BF16)
        return m_new, jnp.exp(m - m_new) * acc + _dot(p, vh)

    lane_q, own, den = lane_masks(tq)
    q_heads = query_heads(0, tq, lane_q)

    def body(j, carry):
        start = pl.multiple_of(j * tq, tq)
        kb = k_ref[pl.ds(start, tq), :]
        vb = v_ref[pl.ds(start, tq), :]
        ck = ck_ref[0, 0, j]
        return tuple(update(q_heads[h], kb, vb * own[h] + den[h], ck[h:h + 1, :], *carry[h], None)
                     for h in range(2))

    init1 = (jnp.full((tq, 1), NEG_INF, F32), jnp.zeros((tq, LANES), F32))
    carry = lax.fori_loop(0, qi, body, (init1, init1))

    start = pl.multiple_of(qi * tq, tq)
    ck = ck_ref[0, 0, qi]
    rows = tq // FOX_DIAG_PARTS
    for part in range(FOX_DIAG_PARTS):
        row0, ncol = part * rows, (part + 1) * rows
        lane_r, _, _ = lane_masks(rows)
        _, own_n, den_n = lane_masks(ncol)
        qh = query_heads(row0, rows, lane_r)
        kb = k_ref[pl.ds(start, ncol), :]
        vb = v_ref[pl.ds(start, ncol), :]
        accs = [update(qh[h], kb, vb * own_n[h] + den_n[h], ck[h:h + 1, :ncol],
                       carry[h][0][row0:row0 + rows], carry[h][1][row0:row0 + rows], row0)[1] for h in range(2)]
        o0 = accs[0] / accs[0][:, den_lane[0]:den_lane[0] + 1]
        o1 = accs[1] / accs[1][:, den_lane[1]:den_lane[1] + 1]
        o_ref[row0:row0 + rows, :] = jnp.where(lane_r < HEAD_DIM, o0, o1).astype(o_ref.dtype)


def fox_prompt(qkv, ck, rest, n_batch, seq, tq):
    nq = seq // tq
    pairs = FOX_HEADS // 2
    return pl.pallas_call(
        functools.partial(_fox_prompt_kernel, tq=tq),
        grid=(n_batch, pairs, nq),
        in_specs=[pl.BlockSpec((tq, LANES), lambda b, h, i: (b * nq + i, h)),
                  pl.BlockSpec((seq, LANES), lambda b, h, i: (b, pairs + h)),
                  pl.BlockSpec((seq, LANES), lambda b, h, i: (b, 2 * pairs + h)),
                  pl.BlockSpec((1, 1, nq, 2, tq), lambda b, h, i: (b, h, 0, 0, 0)),
                  pl.BlockSpec(memory_space=pl.ANY)],
        out_specs=pl.BlockSpec((tq, LANES), lambda b, h, i: (b * nq + i, h)),
        out_shape=jax.ShapeDtypeStruct(rest.shape, rest.dtype),
        input_output_aliases={4: 0},
        compiler_params=_params(("parallel", "parallel", "arbitrary")),
        name="fox_prompt",
    )(qkv, qkv, qkv, ck, rest)


def _block_diag_rows(full):
    rowh = _iota((FOX_HEADS, HEAD_DIM), 0)
    out = jnp.zeros((FOX_HEADS, HEAD_DIM), F32)
    for h in range(FOX_HEADS):
        out = out + jnp.where(rowh == h, full[:, h * HEAD_DIM:(h + 1) * HEAD_DIM], 0.0)
    return out


def _fox_sample_kernel(pt_ref, qbd_ref, q_ref, kn_ref, vbd_ref, ln_ref, *refs, pages_per_step):
    del pt_ref
    pp = pages_per_step
    k_refs, v_refs, lf_refs = refs[:pp], refs[pp:2 * pp], refs[2 * pp:3 * pp]
    o_ref = refs[3 * pp]
    m_ref, l_ref, acc_ref, carry_ref = refs[3 * pp + 1:]
    t = pl.program_id(1)
    scale = HEAD_DIM ** -0.5

    @pl.when(t == 0)
    def _():
        m_ref[...] = jnp.sum(q_ref[0] * kn_ref[0], axis=1, keepdims=True) * scale
        l_ref[...] = jnp.ones_like(l_ref)
        acc_ref[...] = vbd_ref[0]
        carry_ref[...] = ln_ref[0]

    qb = (qbd_ref[0] * scale).astype(BF16)
    lane = _iota((FOX_HEADS, PAGE_SIZE), 1)
    width = FOX_HEADS * HEAD_DIM
    carry = carry_ref[...]
    scores = []
    for r in range(pp):
        lf = lf_refs[r][0, 0]
        x = lf
        for sh in (1, 2, 4, 8, 16, 32, 64):
            x = x + jnp.where(lane + sh < PAGE_SIZE, pltpu.roll(x, PAGE_SIZE - sh, 1), 0.0)
        kp = k_refs[r][0, 0].reshape(width, PAGE_SIZE).astype(BF16)
        scores.append(_dot(qb, kp) + ((x - lf) + carry))
        carry = carry + x[:, 0:1]
    carry_ref[...] = carry
    s = jnp.concatenate(scores, axis=1)
    m = m_ref[...]
    m_new = jnp.maximum(m, jnp.max(s, axis=1, keepdims=True))
    alpha = jnp.exp(m - m_new)
    p = jnp.exp(s - m_new)
    l_ref[...] = alpha * l_ref[...] + jnp.sum(p, axis=1, keepdims=True)
    m_ref[...] = m_new
    pb = p.astype(BF16)
    acc = alpha * acc_ref[...]
    for r in range(pp):
        vp = v_refs[r][0, 0].reshape(width, PAGE_SIZE).astype(BF16)
        acc = acc + _nt_dot(pb[:, r * PAGE_SIZE:(r + 1) * PAGE_SIZE], vp)
    acc_ref[...] = acc

    @pl.when(t == pl.num_programs(1) - 1)
    def _():
        o_ref[0] = _block_diag_rows(acc / l_ref[...])


def fox_sample(page_table, qbd, q, kn, vbd, ln, cache_kt, cache_vt, cache_lft, layer, pages_per_step):
    bd, n_pages = page_table.shape
    pp = pages_per_step
    steps = n_pages // pp
    width = FOX_HEADS * HEAD_DIM

    def page(b, t, pt, r):
        return pt[b, n_pages - 1 - (t * pp + r)]

    kv_specs = [pl.BlockSpec((1, 1, FOX_HEADS, HEAD_DIM, PAGE_SIZE),
                             functools.partial(lambda b, t, pt, r: (layer, page(b, t, pt, r), 0, 0, 0), r=r))
                for r in range(pp)]
    lf_specs = [pl.BlockSpec((1, 1, FOX_HEADS, PAGE_SIZE),
                             functools.partial(lambda b, t, pt, r: (layer, page(b, t, pt, r), 0, 0), r=r))
                for r in range(pp)]
    tok = pl.BlockSpec((1, FOX_HEADS, HEAD_DIM), lambda b, t, pt: (b, 0, 0))
    wide = pl.BlockSpec((1, FOX_HEADS, width), lambda b, t, pt: (b, 0, 0))
    grid_spec = pltpu.PrefetchScalarGridSpec(
        num_scalar_prefetch=1,
        grid=(bd, steps),
        in_specs=[wide, tok, tok, wide, pl.BlockSpec((1, FOX_HEADS, 1), lambda b, t, pt: (b, 0, 0))]
        + kv_specs + kv_specs + lf_specs,
        out_specs=tok,
        scratch_shapes=[pltpu.VMEM((FOX_HEADS, 1), F32), pltpu.VMEM((FOX_HEADS, 1), F32),
                        pltpu.VMEM((FOX_HEADS, width), F32), pltpu.VMEM((FOX_HEADS, 1), F32)],
    )
    return pl.pallas_call(
        functools.partial(_fox_sample_kernel, pages_per_step=pp),
        grid_spec=grid_spec,
        out_shape=jax.ShapeDtypeStruct((bd, FOX_HEADS, HEAD_DIM), F32),
        compiler_params=_params(("parallel", "arbitrary")),
        name="fox_sample",
    )(page_table, qbd, q, kn, vbd, ln, *([cache_kt] * pp), *([cache_vt] * pp), *([cache_lft] * pp))


def _ssd_epilogue(y, xs, z, dskip_e, normw):
    y = (y + dskip_e * xs) * _silu(z)
    half = M_INNER // M_GROUPS
    outs = []
    for g in range(M_GROUPS):
        yg = y[:, g * half:(g + 1) * half]
        ms = jnp.sum(yg * yg, axis=1, keepdims=True) * (1.0 / half)
        outs.append(yg * lax.rsqrt(ms + RMS_EPS))
    return jnp.concatenate(outs, axis=1) * normw


def _ssd_prompt_kernel(xbc_ref, z_ref, dt_ref, cw_ref, cb_ref, nega_ref, dskip_ref, normw_ref, rest_ref,
                       o_ref, st_ref, ext_ref, h_ref):
    del rest_ref
    c = pl.program_id(1)
    L = SSD_CHUNK
    pad = SUBLANES

    @pl.when(c == 0)
    def _():
        ext_ref[0:pad, :] = jnp.zeros((pad, CONV_CH), F32)
        h_ref[...] = jnp.zeros_like(h_ref)

    ext_ref[pad:pad + L, :] = xbc_ref[...]
    acc = ext_ref[pad:pad + L, :] * cw_ref[CONV_W - 1:CONV_W, :]
    for j in range(CONV_W - 1):
        off = pad - (CONV_W - 1) + j
        acc = acc + ext_ref[off:off + L, :] * cw_ref[j:j + 1, :]
    u = _silu(acc + cb_ref[...])
    ext_ref[0:pad, :] = ext_ref[L:L + pad, :]

    xs = u[:, :M_INNER]
    dt_t = dt_ref[...]
    cum_t = _dot(dt_t * nega_ref[...], (_iota((L, L), 0) <= _iota((L, L), 1)).astype(F32),
                 precision=HIGHEST)
    eye = (_iota((L, L), 0) == _iota((L, L), 1)).astype(F32)
    cols = _nt_dot(eye, jnp.concatenate([cum_t, dt_t], axis=0), precision=HIGHEST)
    cum_last = cum_t[:, L - 1:L]
    tail_t = jnp.exp(cum_last - cum_t) * dt_t
    tril = _iota((L, L), 0) >= _iota((L, L), 1)
    lane = _iota((L, LANES), 1)
    rowi = _iota((L, LANES), 0)

    y_pairs = []
    for g in range(M_GROUPS):
        bm = u[:, M_INNER + g * D_STATE:M_INNER + (g + 1) * D_STATE]
        cm = u[:, M_INNER + BC_W + g * D_STATE:M_INNER + BC_W + (g + 1) * D_STATE]
        bmb = bm.astype(BF16)
        cmb = cm.astype(BF16)
        cb = _nt_dot(cmb, bmb)
        for pr in range(HPG // 2):
            pidx = g * (HPG // 2) + pr
            xs_pair = xs[:, pidx * LANES:(pidx + 1) * LANES]
            xs_pair_b = xs_pair.astype(BF16)
            h0 = h_ref[pidx]
            ych = _nt_dot(cmb, h0.astype(BF16))
            yw = []
            for k in range(2):
                hd = 2 * pidx + k
                diff = cols[:, hd:hd + 1] - cum_t[hd:hd + 1, :]
                decay = jnp.exp(jnp.where(tril, diff, NEG_INF))
                w = cb * decay * dt_t[hd:hd + 1, :]
                yw.append(_dot(w.astype(BF16), xs_pair_b))
            e0 = jnp.exp(cols[:, 2 * pidx:2 * pidx + 1])
            e1 = jnp.exp(cols[:, 2 * pidx + 1:2 * pidx + 2])
            first = lane < M_HEADDIM
            y_pairs.append(jnp.where(first, yw[0], yw[1]) + ych * jnp.where(first, e0, e1))
            top = rowi < M_HEADDIM
            tail_m = jnp.where(top, jnp.broadcast_to(tail_t[2 * pidx:2 * pidx + 1, :], (L, L)),
                               jnp.broadcast_to(tail_t[2 * pidx + 1:2 * pidx + 2, :], (L, L)))
            dec_m = jnp.where(top, jnp.exp(cum_last[2 * pidx:2 * pidx + 1, :]),
                              jnp.exp(cum_last[2 * pidx + 1:2 * pidx + 2, :]))
            xt = xs_pair.T * tail_m
            h_ref[pidx] = h0 * dec_m + _dot(xt.astype(BF16), bmb)

    y = jnp.concatenate(y_pairs, axis=1)
    o_ref[...] = _ssd_epilogue(y, xs, z_ref[...], dskip_ref[...], normw_ref[...]).astype(o_ref.dtype)
    st_ref[0] = h_ref[...]


def ssd_prompt(xz, dt_rows, conv_w, conv_b, nega, dskip_e, normw, rest, n_batch, seq):
    L = SSD_CHUNK
    nc = seq // L
    pairs = M_HEADS // 2
    const = lambda b, c: (0, 0)
    return pl.pallas_call(
        _ssd_prompt_kernel,
        grid=(n_batch, nc),
        in_specs=[pl.BlockSpec((L, CONV_CH), lambda b, c: (b * nc + c, 0)),
                  pl.BlockSpec((L, M_INNER), lambda b, c: (b * nc + c, CONV_CH // M_INNER)),
                  pl.BlockSpec((M_HEADS, L), lambda b, c: (1, b * nc + c)),
                  pl.BlockSpec((CONV_W, CONV_CH), const),
                  pl.BlockSpec((1, CONV_CH), const),
                  pl.BlockSpec((M_HEADS, 1), const),
                  pl.BlockSpec((1, M_INNER), const),
                  pl.BlockSpec((1, M_INNER), const),
                  pl.BlockSpec(memory_space=pl.ANY)],
        out_specs=[pl.BlockSpec((L, M_INNER), lambda b, c: (b * nc + c, 0)),
                   pl.BlockSpec((1, pairs, LANES, D_STATE), lambda b, c: (b, 0, 0, 0))],
        out_shape=[jax.ShapeDtypeStruct(rest.shape, rest.dtype),
                   jax.ShapeDtypeStruct((n_batch, pairs, LANES, D_STATE), F32)],
        input_output_aliases={8: 0},
        scratch_shapes=[pltpu.VMEM((L + SUBLANES, CONV_CH), F32),
                        pltpu.VMEM((pairs, LANES, D_STATE), F32)],
        compiler_params=_params(("parallel", "arbitrary")),
        name="ssd_prompt",
    )(xz, xz, dt_rows, conv_w, conv_b, nega, dskip_e, normw, rest)


def _ssd_sample_kernel(x_ref, wdt_ref, dtb_ref, nega_ref, xbc_ref, ctx_ref, z_ref, cw_ref, cb_ref,
                       dskip_ref, normw_ref, h0_ref, o_ref, hn_ref,
                       u_ref, coef_t_ref, dec_t_ref, dec_ref, dtx_ref, *, seqs_per_step):
    sb = seqs_per_step
    i = pl.program_id(0)
    nseq = x_ref.shape[0]

    @pl.when(i == 0)
    def _():
        acc = xbc_ref[...] * cw_ref[CONV_W - 1:CONV_W, :]
        for j in range(CONV_W - 1):
            acc = acc + ctx_ref[j] * cw_ref[j:j + 1, :]
        u = _silu(acc + cb_ref[...])
        u_ref[...] = u
        dt = _softplus(_dot(x_ref[...], wdt_ref[...], precision=HIGHEST) + dtb_ref[...])
        dec = jnp.exp(dt * nega_ref[...])
        coef = dt * u[:, :M_INNER]
        dec_ref[...] = dec
        dtx_ref[...] = coef
        for blk in range(M_INNER // LANES):
            sl = slice(blk * LANES, (blk + 1) * LANES)
            coef_t_ref[sl, :] = coef[:, sl].T
            dec_t_ref[sl, :] = dec[:, sl].T

    base = pl.multiple_of(i * sb, sb)
    ub = u_ref[pl.ds(base, sb), :]
    lane_seq = _iota((M_INNER, nseq), 1)
    rows = _iota((sb, M_INNER // M_GROUPS), 0)
    ch = [jnp.zeros((sb, M_INNER // M_GROUPS), F32) for _ in range(M_GROUPS)]
    for s in range(sb):
        onehot = lane_seq == base + s
        cx = jnp.sum(jnp.where(onehot, coef_t_ref[...], 0.0), axis=1, keepdims=True)
        dc = jnp.sum(jnp.where(onehot, dec_t_ref[...], 0.0), axis=1, keepdims=True)
        for g in range(M_GROUPS):
            brow = ub[s:s + 1, M_INNER + g * D_STATE:M_INNER + (g + 1) * D_STATE]
            cblk = ub[:, M_INNER + BC_W + g * D_STATE:M_INNER + BC_W + (g + 1) * D_STATE]
            hg = h0_ref[s, g * HPG:(g + 1) * HPG].reshape(HPG * M_HEADDIM, D_STATE)
            r = _nt_dot(cblk.astype(BF16), hg.astype(BF16))
            ch[g] = ch[g] + jnp.where(rows == s, r, 0.0)
            lo = g * HPG * M_HEADDIM
            hn = hg * dc[lo:lo + HPG * M_HEADDIM] + cx[lo:lo + HPG * M_HEADDIM] * brow
            hn_ref[s, g * HPG:(g + 1) * HPG] = hn.reshape(HPG, M_HEADDIM, D_STATE)

    xs = ub[:, :M_INNER]
    dec = dec_ref[pl.ds(base, sb), :]
    coef = dtx_ref[pl.ds(base, sb), :]
    ys = []
    half = M_INNER // M_GROUPS
    for g in range(M_GROUPS):
        bm = ub[:, M_INNER + g * D_STATE:M_INNER + (g + 1) * D_STATE]
        cm = ub[:, M_INNER + BC_W + g * D_STATE:M_INNER + BC_W + (g + 1) * D_STATE]
        cb = jnp.sum(cm * bm, axis=1, keepdims=True)
        ys.append(cb * coef[:, g * half:(g + 1) * half] + ch[g] * dec[:, g * half:(g + 1) * half])
    y = jnp.concatenate(ys, axis=1)
    o_ref[...] = _ssd_epilogue(y, xs, z_ref[...], dskip_ref[...], normw_ref[...])


def ssd_sample(x_s, wdt_e, dtb_e, nega_e, xbc_s, ctx, z_s, conv_w, conv_b, dskip_e, normw, h0, seqs_per_step):
    bd = x_s.shape[0]
    sb = seqs_per_step
    const = lambda i: (0, 0)
    return pl.pallas_call(
        functools.partial(_ssd_sample_kernel, seqs_per_step=sb),
        grid=(bd // sb,),
        in_specs=[pl.BlockSpec((bd, D_MODEL), const),
                  pl.BlockSpec((D_MODEL, M_INNER), const),
                  pl.BlockSpec((1, M_INNER), const),
                  pl.BlockSpec((1, M_INNER), const),
                  pl.BlockSpec((bd, CONV_CH), const),
                  pl.BlockSpec((CONV_W - 1, bd, CONV_CH), lambda i: (0, 0, 0)),
                  pl.BlockSpec((sb, M_INNER), lambda i: (i, 0)),
                  pl.BlockSpec((CONV_W, CONV_CH), const),
                  pl.BlockSpec((1, CONV_CH), const),
                  pl.BlockSpec((1, M_INNER), const),
                  pl.BlockSpec((1, M_INNER), const),
                  pl.BlockSpec((sb, M_HEADS, M_HEADDIM, D_STATE), lambda i: (i, 0, 0, 0))],
        out_specs=[pl.BlockSpec((sb, M_INNER), lambda i: (i, 0)),
                   pl.BlockSpec((sb, M_HEADS, M_HEADDIM, D_STATE), lambda i: (i, 0, 0, 0))],
        out_shape=[jax.ShapeDtypeStruct((bd, M_INNER), F32),
                   jax.ShapeDtypeStruct((bd, M_HEADS, M_HEADDIM, D_STATE), F32)],
        scratch_shapes=[pltpu.VMEM((bd, CONV_CH), F32),
                        pltpu.VMEM((M_INNER, bd), F32),
                        pltpu.VMEM((M_INNER, bd), F32),
                        pltpu.VMEM((bd, M_INNER), F32),
                        pltpu.VMEM((bd, M_INNER), F32)],
        compiler_params=_params(("arbitrary",)),
        name="ssd_sample",
    )(x_s, wdt_e, dtb_e, nega_e, xbc_s, ctx, z_s, conv_w, conv_b, dskip_e, normw, h0)


def _to_half(x, src_half, dst_half):
    return x if src_half == dst_half else pltpu.roll(x, HEAD_DIM, 1)


def _swa_prompt_kernel(sink_ref, q_ref, kp_ref, kc_ref, vp_ref, vc_ref, rest_ref, o_ref):
    del rest_ref
    n = pl.program_id(1)
    W = WINDOW
    scale = HEAD_DIM ** -0.5
    lane = _iota((W, LANES), 1)
    rows4 = _iota((SW_GROUP * W, 2 * W), 0)
    t = rows4 % W
    col = _iota((SW_GROUP * W, 2 * W), 1)
    valid = (col >= t) & (col <= t + W) & ((n > 0) | (col >= W))
    rcol = _iota((SW_GROUP * W, 1), 0) // W
    outs = [None] * SW_HEADS
    for j in range(SW_KV_HEADS):
        ch, hf = j // 2, j % 2
        kk = jnp.concatenate([kp_ref[:, ch * LANES:(ch + 1) * LANES],
                              kc_ref[:, ch * LANES:(ch + 1) * LANES]], axis=0).astype(BF16)
        vv = jnp.concatenate([vp_ref[:, ch * LANES:(ch + 1) * LANES],
                              vc_ref[:, ch * LANES:(ch + 1) * LANES]], axis=0).astype(BF16)
        qs = []
        sink = jnp.zeros((SW_GROUP * W, 1), F32)
        for g in range(SW_GROUP):
            hq = j * SW_GROUP + g
            qc = q_ref[:, (hq // 2) * LANES:(hq // 2 + 1) * LANES] * scale
            qc = _to_half(qc, hq % 2, hf)
            keep = (lane < HEAD_DIM) if hf == 0 else (lane >= HEAD_DIM)
            qs.append(jnp.where(keep, qc, 0.0))
            sink = jnp.where(rcol == g, sink_ref[hq], sink)
        q4 = jnp.concatenate(qs, axis=0).astype(BF16)
        s = jnp.where(valid, _nt_dot(q4, kk), NEG_INF)
        m = jnp.maximum(jnp.max(s, axis=1, keepdims=True), sink)
        p = jnp.exp(s - m)
        den = jnp.sum(p, axis=1, keepdims=True) + jnp.exp(sink - m)
        o = _dot(p.astype(BF16), vv) / den
        for g in range(SW_GROUP):
            hq = j * SW_GROUP + g
            outs[hq] = _to_half(o[g * W:(g + 1) * W], hf, hq % 2)
    for c in range(SW_HEADS // 2):
        o_ref[:, c * LANES:(c + 1) * LANES] = jnp.where(lane < HEAD_DIM, outs[2 * c],
                                                        outs[2 * c + 1]).astype(o_ref.dtype)


def swa_prompt(sinks, u_odd, rest, n_batch, seq):
    nb = seq // WINDOW
    kcol = ODD_MIX // KV_W
    return pl.pallas_call(
        _swa_prompt_kernel,
        grid=(n_batch, nb),
        in_specs=[pl.BlockSpec(memory_space=pltpu.SMEM),
                  pl.BlockSpec((WINDOW, ODD_MIX), lambda b, n: (b * nb + n, 0)),
                  pl.BlockSpec((WINDOW, KV_W), lambda b, n: (b * nb + jnp.maximum(n - 1, 0), kcol)),
                  pl.BlockSpec((WINDOW, KV_W), lambda b, n: (b * nb + n, kcol)),
                  pl.BlockSpec((WINDOW, KV_W), lambda b, n: (b * nb + jnp.maximum(n - 1, 0), kcol + 1)),
                  pl.BlockSpec((WINDOW, KV_W), lambda b, n: (b * nb + n, kcol + 1)),
                  pl.BlockSpec(memory_space=pl.ANY)],
        out_specs=pl.BlockSpec((WINDOW, ODD_MIX), lambda b, n: (b * nb + n, 0)),
        out_shape=jax.ShapeDtypeStruct(rest.shape, rest.dtype),
        input_output_aliases={6: 0},
        compiler_params=_params(("parallel", "arbitrary")),
        name="swa_prompt",
    )(sinks, u_odd, u_odd, u_odd, u_odd, u_odd, rest)


def _swa_sample_kernel(sink_ref, q_ref, kn_ref, vn_ref, bk_ref, bv_ref, o_ref, *, seqs_per_step):
    scale = HEAD_DIM ** -0.5
    rowg = _iota((SW_HEADS, HEAD_DIM), 0) // SW_GROUP
    sink = sink_ref[...]
    for s in range(seqs_per_step):
        q16 = q_ref[s] * scale
        qbd = jnp.concatenate([jnp.where(rowg == j, q16, 0.0) for j in range(SW_KV_HEADS)], axis=1)
        kb = bk_ref[s].reshape(KV_W, -1).astype(BF16)
        vb = bv_ref[s].reshape(KV_W, -1).astype(BF16)
        sc = _dot(qbd.astype(BF16), kb)
        s_new = jnp.sum(qbd * kn_ref[s:s + 1, :], axis=1, keepdims=True)
        m = jnp.maximum(jnp.maximum(jnp.max(sc, axis=1, keepdims=True), s_new), sink)
        p = jnp.exp(sc - m)
        pn = jnp.exp(s_new - m)
        den = jnp.sum(p, axis=1, keepdims=True) + pn + jnp.exp(sink - m)
        full = (_nt_dot(p.astype(BF16), vb) + pn * vn_ref[s:s + 1, :]) / den
        o16 = jnp.zeros((SW_HEADS, HEAD_DIM), F32)
        for j in range(SW_KV_HEADS):
            o16 = o16 + jnp.where(rowg == j, full[:, j * HEAD_DIM:(j + 1) * HEAD_DIM], 0.0)
        o_ref[s] = o16


def swa_sample(sinks_col, q, kn, vn, buf_k, buf_v, seqs_per_step):
    bd, _, _, lw = buf_k.shape
    sb = seqs_per_step
    return pl.pallas_call(
        functools.partial(_swa_sample_kernel, seqs_per_step=sb),
        grid=(bd // sb,),
        in_specs=[pl.BlockSpec((SW_HEADS, 1), lambda i: (0, 0)),
                  pl.BlockSpec((sb, SW_HEADS, HEAD_DIM), lambda i: (i, 0, 0)),
                  pl.BlockSpec((sb, KV_W), lambda i: (i, 0)),
                  pl.BlockSpec((sb, KV_W), lambda i: (i, 0)),
                  pl.BlockSpec((sb, SW_KV_HEADS, HEAD_DIM, lw), lambda i: (i, 0, 0, 0)),
                  pl.BlockSpec((sb, SW_KV_HEADS, HEAD_DIM, lw), lambda i: (i, 0, 0, 0))],
        out_specs=pl.BlockSpec((sb, SW_HEADS, HEAD_DIM), lambda i: (i, 0, 0)),
        out_shape=jax.ShapeDtypeStruct((bd, SW_HEADS, HEAD_DIM), F32),
        compiler_params=_params(("parallel",)),
        name="swa_sample",
    )(sinks_col, q, kn, vn, buf_k, buf_v)


def _layer_norm(h, g, b):
    mu = jnp.mean(h, axis=1, keepdims=True)
    d = h - mu
    var = jnp.mean(d * d, axis=1, keepdims=True)
    return d * lax.rsqrt(var + LN_EPS) * g + b


def _mix_route_kernel(*refs, alpha, n_feat, prompt_tiles):
    am_refs, wo_refs = refs[:n_feat], refs[n_feat:2 * n_feat]
    xp_ref, xt_ref, g_ref, b_ref, wr_ref, br_ref, x1_ref, rw_ref, re_ref = refs[2 * n_feat:]
    mix = _dot(am_refs[0][...].astype(BF16), wo_refs[0][...])
    for a_ref, w_ref in zip(am_refs[1:], wo_refs[1:]):
        mix = mix + _dot(a_ref[...].astype(BF16), w_ref[...])
    x = jnp.where(pl.program_id(0) < prompt_tiles, xp_ref[...], xt_ref[...])
    x1 = _layer_norm(alpha * x + mix, g_ref[...], b_ref[...])
    x1_ref[...] = x1
    wr = wr_ref[...]
    x_hi = x1.astype(BF16)
    w_hi = wr.astype(BF16)
    x_lo = (x1 - x_hi.astype(F32)).astype(BF16)
    w_lo = (wr - w_hi.astype(F32)).astype(BF16)
    logits = _dot(x_hi, w_hi) + _dot(x_lo, w_hi) + _dot(x_hi, w_lo) + br_ref[...]
    lane_i = _iota(logits.shape, 1)
    lane = lane_i.astype(F32)
    big = float(LANES)
    gl = jnp.where(lane_i < N_EXPERT_GROUPS, logits, NEG_INF)
    gmax = jnp.max(gl, axis=1, keepdims=True)
    grp = jnp.min(jnp.where(gl == gmax, lane, big), axis=1, keepdims=True)
    pg = 1.0 / jnp.sum(jnp.exp(gl - gmax), axis=1, keepdims=True)
    rel = lane - (N_EXPERT_GROUPS + grp * EXPERTS_PER_GROUP)
    el = jnp.where(rel >= 0.0, jnp.where(rel < EXPERTS_PER_GROUP, logits, NEG_INF), NEG_INF)
    v1 = jnp.max(el, axis=1, keepdims=True)
    i1 = jnp.min(jnp.where(el == v1, lane, big), axis=1, keepdims=True)
    el2 = jnp.where(lane == i1, NEG_INF, el)
    v2 = jnp.max(el2, axis=1, keepdims=True)
    i2 = jnp.min(jnp.where(el2 == v2, lane, big), axis=1, keepdims=True)
    e = jnp.exp(v2 - v1)
    w1 = pg / (1.0 + e)
    w2 = w1 * e
    rw_ref[...] = jnp.where(lane_i == 0, w1, jnp.where(lane_i == 1, w2, 0.0))
    e1 = (i1 - N_EXPERT_GROUPS).astype(jnp.int32)
    e2 = (i2 - N_EXPERT_GROUPS).astype(jnp.int32)
    re_ref[...] = jnp.where(lane_i == 0, e1, jnp.where(lane_i == 1, e2, 0))


def mix_route(feats, wos, x_prompt, x_tail, g, b, wr, br, alpha, tm):
    d = x_tail.shape[1]
    prompt_tiles = (feats[0].shape[0] - x_tail.shape[0]) // tm
    m = feats[0].shape[0]
    const = lambda i: (0, 0)
    row = lambda i: (i, 0)
    return pl.pallas_call(
        functools.partial(_mix_route_kernel, alpha=alpha, n_feat=len(feats), prompt_tiles=prompt_tiles),
        grid=(m // tm,),
        in_specs=[pl.BlockSpec((tm, a.shape[1]), row) for a in feats]
        + [pl.BlockSpec((w.shape[0], d), const) for w in wos]
        + [pl.BlockSpec((tm, d), lambda i: (jnp.minimum(i, prompt_tiles - 1), 0)),
           pl.BlockSpec((tm, d), lambda i: (jnp.maximum(i - prompt_tiles, 0), 0)),
           pl.BlockSpec((1, d), const), pl.BlockSpec((1, d), const),
           pl.BlockSpec((d, LANES), const), pl.BlockSpec((1, LANES), const)],
        out_specs=[pl.BlockSpec((tm, d), row),
                   pl.BlockSpec((tm, LANES), row), pl.BlockSpec((tm, LANES), row)],
        out_shape=[jax.ShapeDtypeStruct((m, d), F32),
                   jax.ShapeDtypeStruct((m, LANES), F32), jax.ShapeDtypeStruct((m, LANES), jnp.int32)],
        compiler_params=_params(("parallel",)),
        name="mix_route",
    )(*feats, *wos, x_prompt, x_tail, g, b, wr, br)


def _experts_kernel(te_ref, tv_ref, x_ref, wg_ref, wu_ref, wd_ref, *rest, first_tile):
    y_ref = rest[-1]
    t = first_tile + pl.program_id(0)

    @pl.when(tv_ref[t] > 0)
    def _():
        x = x_ref[...].astype(BF16)
        h = _silu(_dot(x, wg_ref[0, 0].astype(BF16))) * _dot(x, wu_ref[0, 0].astype(BF16))
        y_ref[...] = _dot(h.astype(BF16), wd_ref[0, 0].astype(BF16))

    @pl.when(tv_ref[t] == 0)
    def _():
        y_ref[...] = jnp.zeros_like(y_ref)


def experts(tile_expert, tile_valid, xs, wg, wu, wd, layer, te, first_tile, n_rows, y_prev):
    r, d = xs.shape
    ff = wg.shape[3]
    ex = lambda t, e, v: (layer, e[first_tile + t], 0, 0)
    in_specs = [pl.BlockSpec((te, d), lambda t, e, v: (t, 0)),
                pl.BlockSpec((1, 1, d, ff), ex), pl.BlockSpec((1, 1, d, ff), ex), pl.BlockSpec((1, 1, ff, d), ex)]
    args = [tile_expert, tile_valid, xs, wg, wu, wd]
    aliases = {}
    if y_prev is not None:
        in_specs.append(pl.BlockSpec(memory_space=pl.ANY))
        aliases = {len(args): 0}
        args.append(y_prev)
    grid_spec = pltpu.PrefetchScalarGridSpec(
        num_scalar_prefetch=2,
        grid=(r // te,),
        in_specs=in_specs,
        out_specs=pl.BlockSpec((te, d), lambda t, e, v: (first_tile + t, 0)),
    )
    return pl.pallas_call(
        functools.partial(_experts_kernel, first_tile=first_tile),
        grid_spec=grid_spec,
        out_shape=jax.ShapeDtypeStruct((n_rows, d), F32),
        input_output_aliases=aliases,
        compiler_params=_params(("arbitrary",)),
        name="experts",
    )(*args)


def _combine_ple_kernel(x1_ref, y0_ref, y1_ref, rw_ref, pp_ref, pt_ref, g_ref, b_ref, wg_ref, bg_ref, wp_ref,
                        o_ref, *, alpha, prompt_tiles, first_tile):
    rw = rw_ref[...]
    f = rw[:, 0:1] * y0_ref[...] + rw[:, 1:2] * y1_ref[...]
    x2 = _layer_norm(alpha * x1_ref[...] + f, g_ref[...], b_ref[...])
    gl = _dot(x2.astype(BF16), wg_ref[...]) + bg_ref[...]
    gate = 1.0 / (1.0 + jnp.exp(-gl))
    p = jnp.where(first_tile + pl.program_id(0) < prompt_tiles, pp_ref[0], pt_ref[...])
    o_ref[...] = x2 + gate * _dot(p.astype(BF16), wp_ref[...])


def combine_ple(x1, y0, y1, rw, p_prompt, p_tail, layer, g, b, wg, bg, wp, alpha, tm, first_tile, n_tiles):
    d = x1.shape[1]
    pd = p_prompt.shape[2]
    prompt_tiles = p_prompt.shape[1] // tm
    const = lambda i: (0, 0)
    row = lambda i: (first_tile + i, 0)
    return pl.pallas_call(
        functools.partial(_combine_ple_kernel, alpha=alpha, prompt_tiles=prompt_tiles, first_tile=first_tile),
        grid=(n_tiles,),
        in_specs=[pl.BlockSpec((tm, d), row), pl.BlockSpec((tm, d), row), pl.BlockSpec((tm, d), row),
                  pl.BlockSpec((tm, LANES), row),
                  pl.BlockSpec((1, tm, pd), lambda i: (layer, jnp.minimum(first_tile + i, prompt_tiles - 1), 0)),
                  pl.BlockSpec((tm, pd), lambda i: (jnp.maximum(first_tile + i - prompt_tiles, 0), 0)),
                  pl.BlockSpec((1, d), const), pl.BlockSpec((1, d), const),
                  pl.BlockSpec((d, d), const), pl.BlockSpec((1, d), const), pl.BlockSpec((pd, d), const)],
        out_specs=pl.BlockSpec((tm, d), lambda i: (i, 0)),
        out_shape=jax.ShapeDtypeStruct((n_tiles * tm, d), F32),
        compiler_params=_params(("parallel",)),
        name="combine_ple",
    )(x1, y0, y1, rw, p_prompt, p_tail, g, b, wg, bg, wp)


def _heads_t_kernel(x_ref, o_ref):
    rows = x_ref.shape[0]
    o_ref[0] = x_ref[...].T.reshape(FOX_HEADS, HEAD_DIM, rows)


def heads_transposed(x, n_batch, seq, tm):
    nt = seq // tm
    return pl.pallas_call(
        _heads_t_kernel,
        grid=(n_batch, nt),
        in_specs=[pl.BlockSpec((tm, FOX_WIDTH), lambda b, i: (b * nt + i, 0))],
        out_specs=pl.BlockSpec((1, FOX_HEADS, HEAD_DIM, tm), lambda b, i: (b, 0, 0, i)),
        out_shape=jax.ShapeDtypeStruct((n_batch, FOX_HEADS, HEAD_DIM, seq), F32),
        compiler_params=_params(("parallel", "parallel")),
        name="heads_transposed",
    )(x)


def _tiles(n_tokens):
    tm = 512 if n_tokens >= 4096 else 128
    return tm, ((n_tokens + tm - 1) // tm) * tm


def _expert_tile(n_tokens):
    return 256 if n_tokens >= 4096 else 32


def _channel_and_ple(x_main, x_tail, feats, p_prompt, p_tail, li, w_out, w, tm):
    depth = w["ln_mix_g"].shape[0]
    alpha = (2 * depth) ** 0.25
    ntp, d = feats[0].shape[0], x_tail.shape[1]
    wr = jnp.concatenate([w["w_router_group"][li],
                          jnp.moveaxis(w["w_router_expert"][li], 0, 1).reshape(d, N_EXPERTS)], axis=1)
    wr = jnp.pad(wr, ((0, 0), (0, LANES - wr.shape[1])))
    br = jnp.concatenate([w["b_router_group"][li], w["b_router_expert"][li].reshape(-1)])
    br = jnp.pad(br, (0, LANES - br.shape[0]))[None, :]
    wo = w_out.astype(BF16)
    splits = [0]
    for a in feats:
        splits.append(splits[-1] + a.shape[1])
    x1, rw, re = mix_route(feats, [wo[lo:hi] for lo, hi in zip(splits[:-1], splits[1:])], x_main, x_tail,
                           w["ln_mix_g"][li][None], w["ln_mix_b"][li][None], wr, br, alpha, tm)

    te = _expert_tile(ntp)
    n_flat = 2 * ntp
    flat = re[:, :2].reshape(-1)
    onehot = (flat[:, None] == jnp.arange(N_EXPERTS, dtype=jnp.int32)[None, :]).astype(jnp.int32)
    running = jnp.cumsum(onehot, axis=0)
    counts = running[-1]
    padded = ((counts + te - 1) // te) * te
    gend = jnp.cumsum(padded)
    gstart = gend - padded
    pos_flat = jnp.sum(onehot * (running - 1 + gstart[None, :]), axis=1)
    n_rows = ((n_flat + N_EXPERTS * (te - 1) + te - 1) // te) * te
    row_token = (jnp.arange(n_rows, dtype=jnp.int32) % ntp).at[pos_flat].set(
        jnp.arange(n_flat, dtype=jnp.int32) // 2, mode="promise_in_bounds", unique_indices=True)
    pos_flat = pos_flat.reshape(ntp, 2)
    tile_start = jnp.arange(n_rows // te, dtype=jnp.int32) * te
    tile_expert = jnp.minimum(jnp.sum((gend[None, :] <= tile_start[:, None]).astype(jnp.int32), axis=1),
                              N_EXPERTS - 1)
    tile_valid = (tile_start < gend[-1]).astype(jnp.int32)

    def rows(a, idx):
        return a.at[idx].get(mode="promise_in_bounds")

    n_tiles = n_rows // te
    bounds = [0] + [max(1, int(f * n_tiles)) for f in EXPERT_CHUNK_ENDS]
    y = None
    for t0, t1 in zip(bounds[:-1], bounds[1:]):
        xs = rows(x1, row_token[t0 * te:t1 * te])
        y = experts(tile_expert, tile_valid, xs, w["w_exp_gate"], w["w_exp_up"], w["w_exp_down"], li, te,
                    t0, n_rows, y)
    y0 = rows(y, pos_flat[:, 0])
    y1 = rows(y, pos_flat[:, 1])
    prompt_tiles = p_prompt.shape[1] // tm
    parts = [(0, ntp // tm)] if li + 1 < depth else [(0, prompt_tiles), (prompt_tiles, ntp // tm - prompt_tiles)]
    outs = [combine_ple(x1, y0, y1, rw, p_prompt, p_tail, li, w["ln_ffn_g"][li][None], w["ln_ffn_b"][li][None],
                        w["w_ple_gate"][li].astype(BF16), w["b_ple_gate"][li][None],
                        w["w_ple_proj"][li].astype(BF16), alpha, tm, first, n) for first, n in parts]
    return outs[0] if len(outs) == 1 else tuple(outs)


def kernel(x_prompt, x_sample, p_prompt, p_sample, cache_fox_k, cache_fox_v, cache_fox_logf, state_ssm, state_conv, cache_win_k, cache_win_v, page_table, w_in_even, b_fgate, conv_w, conv_b, dt_bias, a_log, d_skip, ssm_norm_w, w_out_even, w_in_odd, attn_sinks, w_out_odd, ln_mix_g, ln_mix_b, ln_ffn_g, ln_ffn_b, w_router_group, b_router_group, w_router_expert, b_router_expert, w_exp_gate, w_exp_up, w_exp_down, w_ple_proj, w_ple_gate, b_ple_gate):
    bp, seq, d = x_prompt.shape
    bd, t_dec, _ = x_sample.shape
    assert t_dec == 1 and d == D_MODEL
    depth = p_prompt.shape[0]
    n_pages = page_table.shape[1]
    past_len = n_pages * PAGE_SIZE
    np_tok = bp * seq
    nt = np_tok + bd
    tm, ntp = _tiles(nt)
    pad = ntp - nt

    def past_prompt(a_s):
        a_s = a_s.reshape(bd, -1).astype(BF16)
        return jnp.zeros((ntp, a_s.shape[1]), BF16).at[np_tok:nt].set(a_s)

    def after_prompt(a_s):
        a_s = a_s.reshape(bd, -1)
        return jnp.zeros((ntp - np_tok, a_s.shape[1]), a_s.dtype).at[:bd].set(a_s)

    assert np_tok % tm == 0 and seq >= CONV_W - 1
    x_full = None
    x_main = x_prompt.reshape(np_tok, d)
    x_tail = after_prompt(x_sample)
    p_all = p_prompt.reshape(depth, np_tok, -1)
    shared = dict(ln_mix_g=ln_mix_g, ln_mix_b=ln_mix_b, ln_ffn_g=ln_ffn_g, ln_ffn_b=ln_ffn_b,
                  w_router_group=w_router_group, b_router_group=b_router_group,
                  w_router_expert=w_router_expert, b_router_expert=b_router_expert,
                  w_exp_gate=w_exp_gate, w_exp_up=w_exp_up, w_exp_down=w_exp_down,
                  w_ple_proj=w_ple_proj, w_ple_gate=w_ple_gate, b_ple_gate=b_ple_gate)

    half = HEAD_DIM // 2
    inv = jnp.exp(-math.log(ROPE_THETA) * jnp.arange(half, dtype=F32) / half)
    pos = jnp.concatenate([jnp.tile(jnp.arange(seq, dtype=jnp.int32), bp),
                           jnp.full((bd,), past_len, jnp.int32), jnp.zeros((pad,), jnp.int32)])
    ang = pos.astype(F32)[:, None] * inv[None, :]
    cos_t = jnp.tile(jnp.cos(ang), (1, LANES // half))
    sin_t = jnp.tile(jnp.concatenate([-jnp.sin(ang), jnp.sin(ang)], axis=1), (1, LANES // HEAD_DIM))

    even_p, even_s, odd_p, odd_s = [], [], [], []
    for li in range(depth):
        j = li // 2
        if li % 2 == 0:
            wi = w_in_even[j]
            c0 = 3 * FOX_WIDTH
            c1 = c0 + FOX_HEADS
            c2 = c1 + M_INNER
            c3 = c2 + CONV_CH
            w_main = jnp.concatenate([wi[:, :c0], wi[:, c2:c3], wi[:, c1:c2]], axis=1).astype(BF16)
            w_small_t = jnp.concatenate([wi[:, c0:c1], wi[:, c3:]], axis=1).T
            b_small = jnp.concatenate([b_fgate[j], dt_bias[j]])[:, None]
            prompt_tiles = np_tok // tm
            k_p, v_p, xz, qkv, _, small = even_proj(x_main, w_main, w_small_t, b_small, tm, 0, prompt_tiles)
            k_t, v_t, xz_t, _, q_t, small_t = even_proj(x_tail, w_main, w_small_t, b_small, tm, 0,
                                                        ntp // tm - prompt_tiles)

            tq = min(FOX_TILE, seq)
            logf_p = small[:FOX_HEADS].reshape(FOX_HEADS, bp, seq)
            cum = cumsum_lanes(jnp.moveaxis(logf_p, 1, 0).reshape(bp * FOX_HEADS, seq), min(CUMSUM_CHUNK, seq))
            ck = cum.reshape(bp, FOX_HEADS // 2, 2, seq // tq, tq).transpose(0, 1, 3, 2, 4)
            logf_s = small_t[:FOX_HEADS, :bd].T
            q_s = q_t[:bd].reshape(bd, FOX_HEADS, HEAD_DIM)
            k_s = k_t[:bd].reshape(bd, FOX_HEADS, HEAD_DIM)
            v_s = v_t[:bd].reshape(bd, FOX_HEADS, HEAD_DIM)
            eye = jnp.eye(FOX_HEADS, dtype=F32)[None, :, :, None]

            def block_diag(a):
                return (a[:, :, None, :] * eye).reshape(bd, FOX_HEADS, FOX_WIDTH)

            a_s = fox_sample(page_table, block_diag(q_s), q_s, k_s, block_diag(v_s), logf_s[:, :, None],
                             jnp.transpose(cache_fox_k, (0, 1, 3, 4, 2)), jnp.transpose(cache_fox_v, (0, 1, 3, 4, 2)),
                             jnp.transpose(cache_fox_logf, (0, 1, 3, 2)), j, min(FOX_PAGES_PER_STEP, n_pages))
            a_all = fox_prompt(qkv, ck, past_prompt(a_s), bp, seq, tq)

            nega = -jnp.exp(a_log[j])
            dskip_e = jnp.repeat(d_skip[j], M_HEADDIM)[None, :]
            normw = ssm_norm_w[j][None, :]
            xz_s = xz_t[:bd]
            m_s, st_s = ssd_sample(x_tail[:bd], jnp.repeat(wi[:, c3:], M_HEADDIM, axis=1),
                                   jnp.repeat(dt_bias[j], M_HEADDIM)[None, :], jnp.repeat(nega, M_HEADDIM)[None, :],
                                   xz_s[:, :CONV_CH], jnp.moveaxis(state_conv[j], 1, 0), xz_s[:, CONV_CH:],
                                   conv_w[j], conv_b[j][None, :], dskip_e, normw, state_ssm[j],
                                   min(DECODE_SEQS_PER_STEP, bd))
            m_all, st_p = ssd_prompt(xz, small, conv_w[j], conv_b[j][None, :], nega[:, None], dskip_e, normw,
                                     past_prompt(m_s), bp, seq)
            feats = [a_all, m_all]
            w_out = w_out_even[j]

            t_rows = min(tm, seq)
            kp = jnp.transpose(heads_transposed(k_p, bp, seq, t_rows), (0, 3, 1, 2))
            vp = jnp.transpose(heads_transposed(v_p, bp, seq, t_rows), (0, 3, 1, 2))
            last = jnp.stack([xz[(b + 1) * seq - (CONV_W - 1):(b + 1) * seq, :CONV_CH] for b in range(bp)])
            conv_p = jnp.concatenate([jnp.zeros((bp, CONV_W - 1, CONV_CH), F32), last], axis=1)[:, -(CONV_W - 1):]
            even_p.append((kp, vp, jnp.moveaxis(logf_p, 0, 2),
                           st_p.reshape(bp, M_HEADS, M_HEADDIM, D_STATE), conv_p))
            conv_s = jnp.concatenate([state_conv[j], xz_s[:, None, :CONV_CH]], axis=1)[:, -(CONV_W - 1):]
            even_s.append((k_s[:, None], v_s[:, None], logf_s[:, None, :], st_s, conv_s))
        else:
            assert x_full is not None
            u = matmul_rope(x_full, w_in_odd[j].astype(BF16), cos_t, sin_t, tm, ROPE_COL_TILE, ODD_MIX + KV_W)
            u_s = u[np_tok:nt]
            lw = cache_win_k.shape[2]
            kn = u_s[:, ODD_MIX:ODD_MIX + KV_W]
            vn = u_s[:, ODD_MIX + KV_W:]
            o_s = swa_sample(attn_sinks[j][:, None], u_s[:, :ODD_MIX].reshape(bd, SW_HEADS, HEAD_DIM), kn, vn,
                             jnp.transpose(cache_win_k[j], (0, 2, 3, 1)), jnp.transpose(cache_win_v[j], (0, 2, 3, 1)),
                             min(DECODE_SEQS_PER_STEP, bd))
            feats = [swa_prompt(attn_sinks[j], u, past_prompt(o_s), bp, seq)]
            w_out = w_out_odd[j]

            rows = min(WINDOW, seq)
            tail = jnp.stack([u[(b + 1) * seq - rows:(b + 1) * seq, ODD_MIX:] for b in range(bp)])
            odd_p.append((tail[:, :, :KV_W].reshape(bp, rows, SW_KV_HEADS, HEAD_DIM),
                          tail[:, :, KV_W:].reshape(bp, rows, SW_KV_HEADS, HEAD_DIM)))
            ka = jnp.concatenate([cache_win_k[j], kn.reshape(bd, 1, SW_KV_HEADS, HEAD_DIM)], axis=1)[:, -lw:]
            va = jnp.concatenate([cache_win_v[j], vn.reshape(bd, 1, SW_KV_HEADS, HEAD_DIM)], axis=1)[:, -lw:]
            odd_s.append((ka, va))
        x_full = _channel_and_ple(x_main, x_tail, feats, p_all, after_prompt(p_sample[li]), li, w_out, shared, tm)
        if li + 1 < depth:
            x_main, x_tail = x_full, x_full[np_tok:]

    x_prompt_out, x_tail_out = x_full
    yp = x_prompt_out.reshape(bp, seq, d)
    ys = x_tail_out[:bd].reshape(bd, 1, d)
    return (yp, ys,
            jnp.stack([st[0] for st in even_p]), jnp.stack([st[1] for st in even_p]),
            jnp.stack([st[2] for st in even_p]), jnp.stack([st[3] for st in even_p]),
            jnp.stack([st[4] for st in even_p]),
            jnp.stack([st[0] for st in odd_p]), jnp.stack([st[1] for st in odd_p]),
            jnp.stack([st[0] for st in even_s]), jnp.stack([st[1] for st in even_s]),
            jnp.stack([st[2] for st in even_s]), jnp.stack([st[3] for st in even_s]),
            jnp.stack([st[4] for st in even_s]),
            jnp.stack([st[0] for st in odd_s]), jnp.stack([st[1] for st in odd_s]))
```
